```python
import jax, jax.numpy as jnp
from jax import lax
import numpy as np

D_MODEL = 1024
BATCH = 2
SEQ = 8192
DEPTH = 1

CHUNK = 64
N_PAST_CHUNKS = 8
BAND = (N_PAST_CHUNKS + 1) * CHUNK
ATT_HEADS = 8
ATT_HEAD_DIM = 64
ATT_WIDTH = ATT_HEADS * ATT_HEAD_DIM
MAX_REL = 256
N_REL = 2 * MAX_REL + 1
DN_HEADS = 4
DN_HEAD_DIM = 128
DN_WIDTH = DN_HEADS * DN_HEAD_DIM
CONV_K = 4
D_FF = 4 * D_MODEL
EPS = 1e-6
SPLITS = [ATT_WIDTH, 2 * ATT_WIDTH, 3 * ATT_WIDTH,
          3 * ATT_WIDTH + 3 * DN_WIDTH,
          3 * ATT_WIDTH + 4 * DN_WIDTH,
          3 * ATT_WIDTH + 4 * DN_WIDTH + DN_HEADS]
IN_COLS = 3 * ATT_WIDTH + 4 * DN_WIDTH + 2 * DN_HEADS

kernel_name = "hymba_chunkattn_gdn_hybrid"


def rms_norm(x, g):
    xf = x.astype(jnp.float32)
    y = xf * lax.rsqrt(jnp.mean(xf * xf, axis=-1, keepdims=True) + EPS)
    return (y * g.astype(jnp.float32)).astype(x.dtype)


def l2_norm(x):
    xf = x.astype(jnp.float32)
    return xf * lax.rsqrt(jnp.sum(xf * xf, axis=-1, keepdims=True) + EPS)


def chunked_rel_attention(q, k, v, q_gain, k_gain, rel_bias):
    B, L, H, Dh = q.shape
    nc = L // CHUNK
    q = rms_norm(q, q_gain)
    k = rms_norm(k, k_gain)
    qc = q.reshape(B, nc, CHUNK, H, Dh)
    pad = ((0, 0), (N_PAST_CHUNKS * CHUNK, 0), (0, 0), (0, 0))
    kp = jnp.pad(k, pad).reshape(B, nc + N_PAST_CHUNKS, CHUNK, H, Dh)
    vp = jnp.pad(v, pad).reshape(B, nc + N_PAST_CHUNKS, CHUNK, H, Dh)
    kb = jnp.concatenate([kp[:, j:j + nc] for j in range(N_PAST_CHUNKS + 1)], axis=2)
    vb = jnp.concatenate([vp[:, j:j + nc] for j in range(N_PAST_CHUNKS + 1)], axis=2)
    scores = jnp.einsum('bcqhd,bckhd->bchqk', qc, kb,
                        preferred_element_type=jnp.float32) * (Dh ** -0.5)
    q_pos = N_PAST_CHUNKS * CHUNK + np.arange(CHUNK)
    k_pos = np.arange(BAND)
    rel_idx = np.clip(q_pos[:, None] - k_pos[None, :], -MAX_REL, MAX_REL) + MAX_REL
    bias = rel_bias.astype(jnp.float32)[:, rel_idx]
    scores = scores + bias[None, None]
    key_chunk = jnp.arange(nc)[:, None] + jnp.asarray(k_pos // CHUNK)[None, :] - N_PAST_CHUNKS
    valid = key_chunk >= 0
    scores = jnp.where(valid[None, :, None, None, :], scores, jnp.finfo(jnp.float32).min)
    p = jax.nn.softmax(scores, axis=-1).astype(v.dtype)
    o = jnp.einsum('bchqk,bckhd->bcqhd', p, vb)
    return o.reshape(B, L, H * Dh)


def causal_short_conv(x, w):
    L = x.shape[1]
    xp = jnp.pad(x, ((0, 0), (CONV_K - 1, 0), (0, 0)))
    y = xp[:, 0:L] * w[0]
    for j in range(1, CONV_K):
        y = y + xp[:, j:j + L] * w[j]
    return jax.nn.silu(y)


def gated_delta_rule(q, k, v, g, beta):
    B, L, H, Dk = q.shape
    Dv = v.shape[-1]
    nc = L // CHUNK
    f32 = jnp.float32
    q = l2_norm(q) * (Dk ** -0.5)
    k = l2_norm(k)
    v = v.astype(f32)

    def to_chunks(t):
        return jnp.transpose(t.reshape(B, nc, CHUNK, H, -1), (0, 3, 1, 2, 4))

    q, k, v = to_chunks(q), to_chunks(k), to_chunks(v)
    g = jnp.transpose(g.astype(f32).reshape(B, nc, CHUNK, H), (0, 3, 1, 2))
    beta = jnp.transpose(beta.astype(f32).reshape(B, nc, CHUNK, H), (0, 3, 1, 2))
    decay = jnp.cumsum(g, axis=-1)
    tril = jnp.asarray(np.tril(np.ones((CHUNK, CHUNK), dtype=bool)))
    strict = jnp.asarray(np.tril(np.ones((CHUNK, CHUNK), dtype=bool), -1))
    diff = decay[..., :, None] - decay[..., None, :]
    l_mask = jnp.where(tril, jnp.exp(jnp.where(tril, diff, 0.0)), 0.0)
    k_beta = k * beta[..., None]
    v_beta = v * beta[..., None]
    a_strict = jnp.where(strict, jnp.einsum('bhnid,bhnjd->bhnij', k_beta, k) * l_mask, 0.0)
    t_mat = a_strict + jnp.eye(CHUNK, dtype=f32)
    value = lax.linalg.triangular_solve(t_mat, v_beta, left_side=True, lower=True,
                                        unit_diagonal=True)
    k_cumdecay = lax.linalg.triangular_solve(t_mat, k_beta * jnp.exp(decay)[..., None],
                                             left_side=True, lower=True, unit_diagonal=True)
    attn_intra = jnp.where(tril, jnp.einsum('bhnid,bhnjd->bhnij', q, k) * l_mask, 0.0)
    q_decay = q * jnp.exp(decay)[..., None]
    k_tail = k * jnp.exp(decay[..., -1:] - decay)[..., None]
    chunk_decay = jnp.exp(decay[..., -1])

    def step(S, inp):
        qd, kc, val, att, kt, cd = inp
        v_new = val - jnp.einsum('bhcd,bhde->bhce', kc, S)
        o = jnp.einsum('bhcd,bhde->bhce', qd, S) + jnp.einsum('bhij,bhje->bhie', att, v_new)
        S = S * cd[..., None, None] + jnp.einsum('bhcd,bhce->bhde', kt, v_new)
        return S, o

    xs = tuple(jnp.moveaxis(t, 2, 0) for t in
               (q_decay, k_cumdecay, value, attn_intra, k_tail, chunk_decay))
    S0 = jnp.zeros((B, H, Dk, Dv), f32)
    _, o = lax.scan(step, S0, xs)
    return jnp.transpose(o, (1, 0, 3, 2, 4)).reshape(B, L, H, Dv)


def setup_inputs(seed: int = 0) -> dict:
    key = jax.random.key(seed)
    ks = jax.random.split(key, 16)
    f32 = jnp.float32

    def nrm(k, shape, s):
        return jax.random.normal(k, shape, f32) * s

    x = nrm(ks[0], (BATCH, SEQ, D_MODEL), 1.0)
    mix_norm_gain = 1.0 + nrm(ks[1], (DEPTH, D_MODEL), 0.05)
    w_in = nrm(ks[2], (DEPTH, D_MODEL, IN_COLS), D_MODEL ** -0.5)
    att_q_gain = 1.0 + nrm(ks[3], (DEPTH, ATT_HEAD_DIM), 0.05)
    att_k_gain = 1.0 + nrm(ks[4], (DEPTH, ATT_HEAD_DIM), 0.05)
    rel_bias = nrm(ks[5], (DEPTH, ATT_HEADS, N_REL), 0.2)
    att_out_gain = 1.0 + nrm(ks[6], (DEPTH, ATT_WIDTH), 0.05)
    dn_conv_w = nrm(ks[7], (DEPTH, CONV_K, 3 * DN_WIDTH), CONV_K ** -0.5)
    dn_a_log = jnp.log(jax.random.uniform(ks[8], (DEPTH, DN_HEADS), f32, 1.0, 16.0))
    dt = jnp.exp(jax.random.uniform(ks[9], (DEPTH, DN_HEADS), f32,
                                    float(np.log(1e-3)), float(np.log(1e-1))))
    dn_dt_bias = dt + jnp.log(-jnp.expm1(-dt))
    dn_out_gain = 1.0 + nrm(ks[10], (DEPTH, DN_HEAD_DIM), 0.05)
    w_out = nrm(ks[11], (DEPTH, D_MODEL, D_MODEL), D_MODEL ** -0.5)
    ffn_norm_gain = 1.0 + nrm(ks[12], (DEPTH, D_MODEL), 0.05)
    w_ff1 = nrm(ks[13], (DEPTH, D_MODEL, D_FF), D_MODEL ** -0.5)
    w_ff2 = nrm(ks[14], (DEPTH, D_FF, D_MODEL), D_FF ** -0.5)
    return {"x": x, "mix_norm_gain": mix_norm_gain, "w_in": w_in,
            "att_q_gain": att_q_gain, "att_k_gain": att_k_gain, "rel_bias": rel_bias,
            "att_out_gain": att_out_gain, "dn_conv_w": dn_conv_w, "dn_a_log": dn_a_log,
            "dn_dt_bias": dn_dt_bias, "dn_out_gain": dn_out_gain, "w_out": w_out,
            "ffn_norm_gain": ffn_norm_gain, "w_ff1": w_ff1, "w_ff2": w_ff2}


def reference(x, mix_norm_gain, w_in, att_q_gain, att_k_gain, rel_bias, att_out_gain,
              dn_conv_w, dn_a_log, dn_dt_bias, dn_out_gain, w_out, ffn_norm_gain,
              w_ff1, w_ff2):
    B, L, _ = x.shape
    h = x
    for layer in range(DEPTH):
        u = rms_norm(h, mix_norm_gain[layer])
        proj = u @ w_in[layer]
        aq, ak, av, dqkv, dgate, da, db = jnp.split(proj, SPLITS, axis=-1)
        shp_a = (B, L, ATT_HEADS, ATT_HEAD_DIM)
        o_att = chunked_rel_attention(aq.reshape(shp_a), ak.reshape(shp_a), av.reshape(shp_a),
                                      att_q_gain[layer], att_k_gain[layer], rel_bias[layer])
        o_att = rms_norm(o_att, att_out_gain[layer])
        dqkv = causal_short_conv(dqkv, dn_conv_w[layer])
        dq, dk, dv = jnp.split(dqkv, 3, axis=-1)
        shp_d = (B, L, DN_HEADS, DN_HEAD_DIM)
        log_decay = -jnp.exp(dn_a_log[layer].astype(jnp.float32)) * jax.nn.softplus(
            da.astype(jnp.float32) + dn_dt_bias[layer].astype(jnp.float32))
        beta = jax.nn.sigmoid(db.astype(jnp.float32))
        o_dn = gated_delta_rule(dq.reshape(shp_d), dk.reshape(shp_d), dv.reshape(shp_d),
                                log_decay, beta)
        o_dn = rms_norm(o_dn, dn_out_gain[layer]) * jax.nn.silu(
            dgate.reshape(shp_d).astype(jnp.float32))
        o_dn = o_dn.reshape(B, L, DN_WIDTH).astype(h.dtype)
        mixed = jnp.concatenate([o_att, o_dn], axis=-1) @ w_out[layer]
        h = h + mixed
        z = rms_norm(h, ffn_norm_gain[layer]) @ w_ff1[layer]
        h = h + jnp.square(jax.nn.relu(z)) @ w_ff2[layer]
    return h
```

```python
import functools

import numpy as np
import jax
import jax.numpy as jnp
from jax import lax
from jax.experimental import pallas as pl
from jax.experimental.pallas import tpu as pltpu

F32 = jnp.float32
BF16 = jnp.bfloat16

CHUNK = 64
N_PAST_CHUNKS = 8
ATT_HEADS = 8
ATT_HEAD_DIM = 64
ATT_WIDTH = ATT_HEADS * ATT_HEAD_DIM
MAX_REL = 256
DN_HEADS = 4
DN_HEAD_DIM = 128
DN_WIDTH = DN_HEADS * DN_HEAD_DIM
CONV_K = 4
EPS = 1e-6

MAIN_COLS = 3 * ATT_WIDTH + 4 * DN_WIDTH
AB_PAD = 128

LANES = 128
VMEM_LIMIT = 56 * 1024 * 1024

IN_TM = 512
IN_TN = 512
ATT_TQ = 4 * CHUNK
ATT_WIN = ATT_TQ + N_PAST_CHUNKS * CHUNK
DN_C = 128
DN_TB = 2 * DN_C
DN_TAIL = 8
FFN_TM = 512
FFN_TF = 1024
NEG_BIG = -1e30


def _dot(a, b):
    return jnp.dot(a, b, preferred_element_type=F32)


def _dot_nt(a, b):
    return lax.dot_general(a, b, (((1,), (1,)), ((), ())), preferred_element_type=F32)


def _dot_tn(a, b):
    return lax.dot_general(a, b, (((0,), (0,)), ((), ())), preferred_element_type=F32)


def _split2(a):
    hi = a.astype(BF16)
    lo = (a - hi.astype(F32)).astype(BF16)
    return hi, lo


def _mm3(a, b):
    ah, al = _split2(a)
    bh, bl = _split2(b)
    return _dot(ah, bh) + (_dot(ah, bl) + _dot(al, bh))


def _split3(a):
    p1 = a.astype(BF16)
    r1 = a - p1.astype(F32)
    p2 = r1.astype(BF16)
    p3 = (r1 - p2.astype(F32)).astype(BF16)
    return p1, p2, p3


def _sigmoid(x):
    return 1.0 / (1.0 + jnp.exp(-x))


def _softplus(x):
    return jnp.maximum(x, 0.0) + jnp.log(1.0 + jnp.exp(-jnp.abs(x)))


def _inproj_kernel(x_ref, g_ref, wm_ref, wab_ref, wabt_ref, qg_ref, kg_ref, seg_ref,
                   main_ref, ab_ref, abt_ref):
    x = x_ref[...]
    ms = jnp.mean(x * x, axis=-1, keepdims=True)
    u = (x * lax.rsqrt(ms + EPS) * g_ref[...]).astype(BF16)
    for c in range(MAIN_COLS // IN_TN):
        cols = slice(c * IN_TN, (c + 1) * IN_TN)
        y = _dot(u, wm_ref[:, cols])
        if c < 2:
            gain = qg_ref if c == 0 else kg_ref
            m = _dot((y * y).astype(BF16), seg_ref[...])
            y = y * lax.rsqrt(m + EPS) * gain[...]
        main_ref[:, cols] = y.astype(BF16)
    ab_ref[...] = _dot(u, wab_ref[...])
    abt_ref[...] = _dot_nt(wabt_ref[...], u)


def _inproj(xf, gain, w_main, w_ab, w_abt, qg, kg, seg):
    t, d = xf.shape
    const = lambda i: (0, 0)
    return pl.pallas_call(
        _inproj_kernel,
        grid=(t // IN_TM,),
        in_specs=[
            pl.BlockSpec((IN_TM, d), lambda i: (i, 0)),
            pl.BlockSpec((1, d), const),
            pl.BlockSpec((d, MAIN_COLS), const),
            pl.BlockSpec((d, AB_PAD), const),
            pl.BlockSpec((AB_PAD, d), const),
            pl.BlockSpec((1, ATT_WIDTH), const),
            pl.BlockSpec((1, ATT_WIDTH), const),
            pl.BlockSpec((ATT_WIDTH, ATT_WIDTH), const),
        ],
        out_specs=[
            pl.BlockSpec((IN_TM, MAIN_COLS), lambda i: (i, 0)),
            pl.BlockSpec((IN_TM, AB_PAD), lambda i: (i, 0)),
            pl.BlockSpec((AB_PAD, IN_TM), lambda i: (0, i)),
        ],
        out_shape=[
            jax.ShapeDtypeStruct((t, MAIN_COLS), BF16),
            jax.ShapeDtypeStruct((t, AB_PAD), F32),
            jax.ShapeDtypeStruct((AB_PAD, t), F32),
        ],
        compiler_params=pltpu.CompilerParams(
            dimension_semantics=("parallel",), vmem_limit_bytes=VMEM_LIMIT),
        name="inproj",
    )(xf, gain, w_main, w_ab, w_abt, qg, kg, seg)


def _attn_kernel(q_ref, k0_ref, k1_ref, k2_ref, v0_ref, v1_ref, v2_ref, bias_ref, og_ref,
                 o_ref):
    i = pl.program_id(1)
    jchunk = lax.broadcasted_iota(jnp.int32, (1, ATT_WIN), 1) >> 6
    smask = jnp.where(jchunk + (ATT_TQ // CHUNK) * i - N_PAST_CHUNKS >= 0, 0.0, NEG_BIG)
    lane = lax.broadcasted_iota(jnp.int32, (1, LANES), 1)
    outs = []
    for p in range(ATT_WIDTH // LANES):
        sl = slice(p * LANES, (p + 1) * LANES)
        qp = q_ref[:, sl]
        kw = jnp.concatenate([k0_ref[:, sl], k1_ref[:, sl], k2_ref[:, sl]], axis=0)
        vw = jnp.concatenate([v0_ref[:, sl], v1_ref[:, sl], v2_ref[:, sl]], axis=0)
        o_pair = None
        for e in range(LANES // ATT_HEAD_DIM):
            mine = (lane >= e * ATT_HEAD_DIM) & (lane < (e + 1) * ATT_HEAD_DIM)
            q_e = jnp.where(mine, qp, jnp.zeros_like(qp))
            s = _dot_nt(q_e, kw) + bias_ref[2 * p + e] + smask
            mx = jnp.max(s, axis=-1, keepdims=True)
            pe = jnp.exp(s - mx)
            den = jnp.sum(pe, axis=-1, keepdims=True)
            v_e = jnp.where(mine, vw, jnp.zeros_like(vw))
            o_e = _dot(pe.astype(BF16), v_e) / den
            o_pair = o_e if o_pair is None else o_pair + o_e
        outs.append(o_pair)
    o = jnp.concatenate(outs, axis=1)
    ms = jnp.mean(o * o, axis=-1, keepdims=True)
    o_ref[...] = (o * lax.rsqrt(ms + EPS) * og_ref[...]).astype(BF16)


def _attention(main, bias, og, batch, seq):
    nb = seq // ATT_TQ
    blk = (ATT_TQ, ATT_WIDTH)

    def kv_spec(col, back):
        return pl.BlockSpec(blk, lambda b, i: (b * nb + jnp.maximum(i - back, 0), col))

    return pl.pallas_call(
        _attn_kernel,
        grid=(batch, nb),
        in_specs=[
            pl.BlockSpec(blk, lambda b, i: (b * nb + i, 0)),
            kv_spec(1, 2), kv_spec(1, 1), kv_spec(1, 0),
            kv_spec(2, 2), kv_spec(2, 1), kv_spec(2, 0),
            pl.BlockSpec((ATT_HEADS, ATT_TQ, ATT_WIN), lambda b, i: (0, 0, 0)),
            pl.BlockSpec((1, ATT_WIDTH), lambda b, i: (0, 0)),
        ],
        out_specs=pl.BlockSpec(blk, lambda b, i: (b * nb + i, 0)),
        out_shape=jax.ShapeDtypeStruct((batch * seq, ATT_WIDTH), BF16),
        compiler_params=pltpu.CompilerParams(
            dimension_semantics=("parallel", "parallel"), vmem_limit_bytes=VMEM_LIMIT),
        name="band_attention",
    )(main, main, main, main, main, main, main, bias, og)


def _attention_bias(rel_bias):
    r = np.arange(ATT_TQ)[:, None]
    s = np.arange(ATT_WIN)[None, :]
    dist = N_PAST_CHUNKS * CHUNK + r - s
    rel_idx = np.clip(dist, -MAX_REL, MAX_REL) + MAX_REL
    back = s // CHUNK - r // CHUNK
    in_band = (back >= 0) & (back <= N_PAST_CHUNKS)
    table = rel_bias.astype(F32)[:, rel_idx]
    return jnp.where(jnp.asarray(in_band)[None], table, NEG_BIG)


def _inv_unit_lower(a, ri, ci):
    eye = jnp.where(ri == ci, 1.0, 0.0)
    n1 = jnp.where((ri >> 3) == (ci >> 3), -a, 0.0)
    n2 = _mm3(n1, n1)
    n4 = _mm3(n2, n2)
    t = eye + n1
    t = t + _mm3(t, n2)
    t = t + _mm3(t, n4)
    for sh in range(3, 7):
        off = ((ri >> (sh + 1)) == (ci >> (sh + 1))) & ((ri >> sh) != (ci >> sh))
        lb = jnp.where(off, a, 0.0)
        t = t - _mm3(t, _mm3(lb, t))
    return t


def _gdn_kernel(x_ref, gate_ref, ab_ref, abt_ref, cw_ref, alog_ref, dtb_ref, alogt_ref, dtbt_ref,
                og_ref, o_ref, s_ref, xbuf_ref):
    i = pl.program_id(1)

    @pl.when(i == 0)
    def _():
        s_ref[...] = jnp.zeros_like(s_ref)
        xbuf_ref[0:DN_TAIL, :] = jnp.zeros((DN_TAIL, 3 * DN_WIDTH), F32)

    x = x_ref[...].astype(F32)
    xbuf_ref[DN_TAIL:DN_TAIL + DN_TB, :] = x
    y = cw_ref[CONV_K - 1:CONV_K, :] * x
    for j in range(CONV_K - 1):
        start = DN_TAIL - (CONV_K - 1) + j
        y = y + cw_ref[j:j + 1, :] * xbuf_ref[start:start + DN_TB, :]
    xbuf_ref[0:DN_TAIL, :] = x[DN_TB - DN_TAIL:DN_TB, :]
    y = y * _sigmoid(y)

    ab = ab_ref[...]
    g_col = -jnp.exp(alog_ref[...]) * _softplus(ab + dtb_ref[...])
    beta_col = _sigmoid(ab)
    abt = abt_ref[0:8, :]
    g_row = -jnp.exp(alogt_ref[...]) * _softplus(abt + dtbt_ref[...])

    rr = lax.broadcasted_iota(jnp.int32, (DN_TB, DN_TB), 0)
    cc = lax.broadcasted_iota(jnp.int32, (DN_TB, DN_TB), 1)
    same = (rr >> 7) == (cc >> 7)
    ltri = jnp.where(same & (cc <= rr), 1.0, 0.0).astype(BF16)
    utri = jnp.where(same & (rr <= cc), 1.0, 0.0).astype(BF16)
    gc1, gc2, gc3 = _split3(g_col)
    d_col = _dot(ltri, gc1) + (_dot(ltri, gc2) + _dot(ltri, gc3))
    gr1, gr2, gr3 = _split3(g_row)
    d_row = _dot(gr1, utri) + (_dot(gr2, utri) + _dot(gr3, utri))

    ri = lax.broadcasted_iota(jnp.int32, (DN_C, DN_C), 0)
    ci = lax.broadcasted_iota(jnp.int32, (DN_C, DN_C), 1)
    og = og_ref[...]

    for c in range(DN_TB // DN_C):
        rows = slice(c * DN_C, (c + 1) * DN_C)
        for h in range(DN_HEADS):
            hs = slice(h * DN_HEAD_DIM, (h + 1) * DN_HEAD_DIM)
            q = y[rows, h * DN_HEAD_DIM:(h + 1) * DN_HEAD_DIM]
            k = y[rows, DN_WIDTH + h * DN_HEAD_DIM:DN_WIDTH + (h + 1) * DN_HEAD_DIM]
            v = y[rows, 2 * DN_WIDTH + h * DN_HEAD_DIM:2 * DN_WIDTH + (h + 1) * DN_HEAD_DIM]
            q = q * lax.rsqrt(jnp.sum(q * q, axis=-1, keepdims=True) + EPS) * (DN_HEAD_DIM ** -0.5)
            k = k * lax.rsqrt(jnp.sum(k * k, axis=-1, keepdims=True) + EPS)

            dc = d_col[rows, h:h + 1]
            dr = d_row[h:h + 1, rows]
            dl = dr[:, DN_C - 1:DN_C]
            bc = beta_col[rows, DN_HEADS + h:DN_HEADS + h + 1]
            lmask = jnp.exp(jnp.where(ci <= ri, dc - dr, -jnp.inf))

            kb = k * bc
            kbf = k.astype(BF16)
            p = _dot_nt(jnp.concatenate([kb, q], axis=0).astype(BF16), kbf)
            a = jnp.where(ci < ri, p[0:DN_C] * lmask, 0.0)
            att = p[DN_C:2 * DN_C] * lmask

            tinv = _inv_unit_lower(a, ri, ci)
            edc = jnp.exp(dc)
            w = _mm3(tinv, jnp.concatenate([v * bc, kb * edc], axis=1))
            value = w[:, 0:DN_HEAD_DIM]
            kcd = w[:, DN_HEAD_DIM:2 * DN_HEAD_DIM]

            qd = q * edc
            kt = k * jnp.exp(dl - dc)
            s_old = s_ref[h]
            r = _dot(jnp.concatenate([kcd, qd], axis=0).astype(BF16), s_old.astype(BF16))
            v_new = value - r[0:DN_C]
            vnb = v_new.astype(BF16)
            o = r[DN_C:2 * DN_C] + _dot(att.astype(BF16), vnb)
            s_ref[h] = s_old * jnp.exp(dl) + _dot_tn(kt.astype(BF16), vnb)

            ms = jnp.mean(o * o, axis=-1, keepdims=True)
            gt = gate_ref[rows, hs].astype(F32)
            o_ref[rows, hs] = (o * lax.rsqrt(ms + EPS) * og * (gt * _sigmoid(gt))).astype(BF16)


def _deltanet(main, ab, abt, conv_w, alog, dtb, alogt, dtbt, og, batch, seq):
    nb = seq // DN_TB
    const = lambda b, i: (0, 0)
    return pl.pallas_call(
        _gdn_kernel,
        grid=(batch, nb),
        in_specs=[
            pl.BlockSpec((DN_TB, 3 * DN_WIDTH), lambda b, i: (b * nb + i, 1)),
            pl.BlockSpec((DN_TB, DN_WIDTH), lambda b, i: (b * nb + i, 6)),
            pl.BlockSpec((DN_TB, AB_PAD), lambda b, i: (b * nb + i, 0)),
            pl.BlockSpec((AB_PAD, DN_TB), lambda b, i: (0, b * nb + i)),
            pl.BlockSpec((CONV_K, 3 * DN_WIDTH), const),
            pl.BlockSpec((1, AB_PAD), const),
            pl.BlockSpec((1, AB_PAD), const),
            pl.BlockSpec((8, 1), const),
            pl.BlockSpec((8, 1), const),
            pl.BlockSpec((1, DN_HEAD_DIM), const),
        ],
        out_specs=pl.BlockSpec((DN_TB, DN_WIDTH), lambda b, i: (b * nb + i, 0)),
        out_shape=jax.ShapeDtypeStruct((batch * seq, DN_WIDTH), BF16),
        scratch_shapes=[
            pltpu.VMEM((DN_HEADS, DN_HEAD_DIM, DN_HEAD_DIM), F32),
            pltpu.VMEM((DN_TAIL + DN_TB, 3 * DN_WIDTH), F32),
        ],
        compiler_params=pltpu.CompilerParams(
            dimension_semantics=("arbitrary", "arbitrary"), vmem_limit_bytes=VMEM_LIMIT),
        name="gated_deltanet",
    )(main, main, ab, abt, conv_w, alog, dtb, alogt, dtbt, og)


def _out_ffn_kernel(x_ref, oa_ref, od_ref, wo_ref, fg_ref, w1_ref, w2_ref, o_ref):
    mixed = _dot(oa_ref[...], wo_ref[0:ATT_WIDTH, :]) + _dot(od_ref[...], wo_ref[ATT_WIDTH:, :])
    h1 = x_ref[...] + mixed
    ms = jnp.mean(h1 * h1, axis=-1, keepdims=True)
    n = (h1 * lax.rsqrt(ms + EPS) * fg_ref[...]).astype(BF16)
    acc = h1
    for c in range(w1_ref.shape[1] // FFN_TF):
        z = _dot(n, w1_ref[:, c * FFN_TF:(c + 1) * FFN_TF])
        act = jnp.square(jnp.maximum(z, 0.0)).astype(BF16)
        acc = acc + _dot(act, w2_ref[c * FFN_TF:(c + 1) * FFN_TF, :])
    o_ref[...] = acc


def _out_ffn(xf, o_att, o_dn, w_out, fg, w1, w2):
    t, d = xf.shape
    dff = w1.shape[1]
    const = lambda i: (0, 0)
    resident = dict(pipeline_mode=pl.Buffered(1))
    return pl.pallas_call(
        _out_ffn_kernel,
        grid=(t // FFN_TM,),
        in_specs=[
            pl.BlockSpec((FFN_TM, d), lambda i: (i, 0)),
            pl.BlockSpec((FFN_TM, ATT_WIDTH), lambda i: (i, 0)),
            pl.BlockSpec((FFN_TM, DN_WIDTH), lambda i: (i, 0)),
            pl.BlockSpec((d, d), const, **resident),
            pl.BlockSpec((1, d), const),
            pl.BlockSpec((d, dff), const, **resident),
            pl.BlockSpec((dff, d), const, **resident),
        ],
        out_specs=pl.BlockSpec((FFN_TM, d), lambda i: (i, 0)),
        out_shape=jax.ShapeDtypeStruct((t, d), F32),
        compiler_params=pltpu.CompilerParams(
            dimension_semantics=("parallel",), vmem_limit_bytes=VMEM_LIMIT),
        name="out_ffn",
    )(xf, o_att, o_dn, w_out, fg, w1, w2)


def _layer(h, p, batch, seq):
    d = h.shape[1]
    w_in = p["w_in"]
    w_main = w_in[:, :MAIN_COLS].astype(BF16)
    w_ab = jnp.pad(w_in[:, MAIN_COLS:], ((0, 0), (0, AB_PAD - 2 * DN_HEADS))).astype(BF16)
    head_of = np.arange(ATT_WIDTH) // ATT_HEAD_DIM
    seg = jnp.asarray((head_of[:, None] == head_of[None, :]) / ATT_HEAD_DIM, BF16)
    qg = (jnp.tile(p["att_q_gain"].astype(F32), ATT_HEADS) * ATT_HEAD_DIM ** -0.5)[None]
    kg = jnp.tile(p["att_k_gain"].astype(F32), ATT_HEADS)[None]
    main, ab, abt = _inproj(h, p["mix_norm_gain"].astype(F32)[None], w_main, w_ab, w_ab.T,
                            qg, kg, seg)

    o_att = _attention(main, _attention_bias(p["rel_bias"]),
                       p["att_out_gain"].astype(F32)[None], batch, seq)

    pad_row = lambda v: jnp.pad(v.astype(F32), (0, AB_PAD - DN_HEADS))[None]
    pad_col = lambda v: jnp.pad(v.astype(F32), (0, 8 - DN_HEADS))[:, None]
    o_dn = _deltanet(main, ab, abt, p["dn_conv_w"].astype(F32),
                     pad_row(p["dn_a_log"]), pad_row(p["dn_dt_bias"]),
                     pad_col(p["dn_a_log"]), pad_col(p["dn_dt_bias"]),
                     p["dn_out_gain"].astype(F32)[None], batch, seq)

    return _out_ffn(h, o_att, o_dn, p["w_out"].astype(BF16), p["ffn_norm_gain"].astype(F32)[None],
                    p["w_ff1"].astype(BF16), p["w_ff2"].astype(BF16))


def kernel(x, mix_norm_gain, w_in, att_q_gain, att_k_gain, rel_bias, att_out_gain, dn_conv_w,
           dn_a_log, dn_dt_bias, dn_out_gain, w_out, ffn_norm_gain, w_ff1, w_ff2):
    batch, seq, d = x.shape
    assert seq % max(ATT_TQ, DN_TB) == 0 and (batch * seq) % max(IN_TM, FFN_TM) == 0
    params = dict(mix_norm_gain=mix_norm_gain, w_in=w_in, att_q_gain=att_q_gain,
                  att_k_gain=att_k_gain, rel_bias=rel_bias, att_out_gain=att_out_gain,
                  dn_conv_w=dn_conv_w, dn_a_log=dn_a_log, dn_dt_bias=dn_dt_bias,
                  dn_out_gain=dn_out_gain, w_out=w_out, ffn_norm_gain=ffn_norm_gain,
                  w_ff1=w_ff1, w_ff2=w_ff2)
    h = x.reshape(batch * seq, d)
    for layer in range(w_in.shape[0]):
        h = _layer(h, {name: v[layer] for name, v in params.items()}, batch, seq)
    return h.reshape(batch, seq, d)
```

```python
import functools

import numpy as np
import jax
import jax.numpy as jnp
from jax import lax
from jax.experimental import pallas as pl
from jax.experimental.pallas import tpu as pltpu

F32 = jnp.float32
BF16 = jnp.bfloat16

CHUNK = 64
N_PAST_CHUNKS = 8
ATT_HEADS = 8
ATT_HEAD_DIM = 64
ATT_WIDTH = ATT_HEADS * ATT_HEAD_DIM
MAX_REL = 256
DN_HEADS = 4
DN_HEAD_DIM = 128
DN_WIDTH = DN_HEADS * DN_HEAD_DIM
CONV_K = 4
EPS = 1e-6

MAIN_COLS = 3 * ATT_WIDTH + 4 * DN_WIDTH
AB_PAD = 128

LANES = 128
VMEM_LIMIT = 56 * 1024 * 1024

IN_TM = 512
IN_TN = 512
ATT_TQ = 4 * CHUNK
ATT_WIN = ATT_TQ + N_PAST_CHUNKS * CHUNK
DN_C = 128
DN_TB = 2 * DN_C
DN_TAIL = 8
FFN_TM = 512
FFN_TF = 1024
NEG_BIG = -1e30


def _dot(a, b):
    return jnp.dot(a, b, preferred_element_type=F32)


def _dot_nt(a, b):
    return lax.dot_general(a, b, (((1,), (1,)), ((), ())), preferred_element_type=F32)


def _dot_tn(a, b):
    return lax.dot_general(a, b, (((0,), (0,)), ((), ())), preferred_element_type=F32)


def _split3(a):
    p1 = a.astype(BF16)
    r1 = a - p1.astype(F32)
    p2 = r1.astype(BF16)
    p3 = (r1 - p2.astype(F32)).astype(BF16)
    return p1, p2, p3


def _sigmoid(x):
    return 1.0 / (1.0 + jnp.exp(-x))


def _softplus(x):
    return jnp.maximum(x, 0.0) + jnp.log(1.0 + jnp.exp(-jnp.abs(x)))


def _inproj_kernel(x_ref, g_ref, wm_ref, wab_ref, wabt_ref, qg_ref, kg_ref, seg_ref,
                   main_ref, ab_ref, abt_ref):
    x = x_ref[...]
    ms = jnp.mean(x * x, axis=-1, keepdims=True)
    u = (x * lax.rsqrt(ms + EPS) * g_ref[...]).astype(BF16)
    for c in range(MAIN_COLS // IN_TN):
        cols = slice(c * IN_TN, (c + 1) * IN_TN)
        y = _dot(u, wm_ref[:, cols])
        if c < 2:
            gain = qg_ref if c == 0 else kg_ref
            m = _dot((y * y).astype(BF16), seg_ref[...])
            y = y * lax.rsqrt(m + EPS) * gain[...]
        main_ref[:, cols] = y.astype(BF16)
    ab_ref[...] = _dot(u, wab_ref[...])
    abt_ref[...] = _dot_nt(wabt_ref[...], u)


def _inproj(xf, gain, w_main, w_ab, w_abt, qg, kg, seg):
    t, d = xf.shape
    const = lambda i: (0, 0)
    return pl.pallas_call(
        _inproj_kernel,
        grid=(t // IN_TM,),
        in_specs=[
            pl.BlockSpec((IN_TM, d), lambda i: (i, 0)),
            pl.BlockSpec((1, d), const),
            pl.BlockSpec((d, MAIN_COLS), const),
            pl.BlockSpec((d, AB_PAD), const),
            pl.BlockSpec((AB_PAD, d), const),
            pl.BlockSpec((1, ATT_WIDTH), const),
            pl.BlockSpec((1, ATT_WIDTH), const),
            pl.BlockSpec((ATT_WIDTH, ATT_WIDTH), const),
        ],
        out_specs=[
            pl.BlockSpec((IN_TM, MAIN_COLS), lambda i: (i, 0)),
            pl.BlockSpec((IN_TM, AB_PAD), lambda i: (i, 0)),
            pl.BlockSpec((AB_PAD, IN_TM), lambda i: (0, i)),
        ],
        out_shape=[
            jax.ShapeDtypeStruct((t, MAIN_COLS), BF16),
            jax.ShapeDtypeStruct((t, AB_PAD), F32),
            jax.ShapeDtypeStruct((AB_PAD, t), F32),
        ],
        compiler_params=pltpu.CompilerParams(
            dimension_semantics=("parallel",), vmem_limit_bytes=VMEM_LIMIT),
        name="inproj",
    )(xf, gain, w_main, w_ab, w_abt, qg, kg, seg)


def _attn_kernel(q_ref, k0_ref, k1_ref, k2_ref, v0_ref, v1_ref, v2_ref, bias_ref, og_ref,
                 o_ref):
    i = pl.program_id(1)
    jchunk = lax.broadcasted_iota(jnp.int32, (1, ATT_WIN), 1) >> 6
    smask = jnp.where(jchunk + (ATT_TQ // CHUNK) * i - N_PAST_CHUNKS >= 0, 0.0, NEG_BIG)
    lane = lax.broadcasted_iota(jnp.int32, (1, LANES), 1)
    outs = []
    for p in range(ATT_WIDTH // LANES):
        sl = slice(p * LANES, (p + 1) * LANES)
        qp = q_ref[:, sl]
        kw = jnp.concatenate([k0_ref[:, sl], k1_ref[:, sl], k2_ref[:, sl]], axis=0)
        vw = jnp.concatenate([v0_ref[:, sl], v1_ref[:, sl], v2_ref[:, sl]], axis=0)
        o_pair = None
        for e in range(LANES // ATT_HEAD_DIM):
            mine = (lane >= e * ATT_HEAD_DIM) & (lane < (e + 1) * ATT_HEAD_DIM)
            q_e = jnp.where(mine, qp, jnp.zeros_like(qp))
            s = _dot_nt(q_e, kw) + bias_ref[2 * p + e] + smask
            mx = jnp.max(s, axis=-1, keepdims=True)
            pe = jnp.exp(s - mx)
            den = jnp.sum(pe, axis=-1, keepdims=True)
            v_e = jnp.where(mine, vw, jnp.zeros_like(vw))
            o_e = _dot(pe.astype(BF16), v_e) / den
            o_pair = o_e if o_pair is None else o_pair + o_e
        outs.append(o_pair)
    o = jnp.concatenate(outs, axis=1)
    ms = jnp.mean(o * o, axis=-1, keepdims=True)
    o_ref[...] = (o * lax.rsqrt(ms + EPS) * og_ref[...]).astype(BF16)


def _attention(main, bias, og, batch, seq):
    nb = seq // ATT_TQ
    blk = (ATT_TQ, ATT_WIDTH)

    def kv_spec(col, back):
        return pl.BlockSpec(blk, lambda b, i: (b * nb + jnp.maximum(i - back, 0), col))

    return pl.pallas_call(
        _attn_kernel,
        grid=(batch, nb),
        in_specs=[
            pl.BlockSpec(blk, lambda b, i: (b * nb + i, 0)),
            kv_spec(1, 2), kv_spec(1, 1), kv_spec(1, 0),
            kv_spec(2, 2), kv_spec(2, 1), kv_spec(2, 0),
            pl.BlockSpec((ATT_HEADS, ATT_TQ, ATT_WIN), lambda b, i: (0, 0, 0)),
            pl.BlockSpec((1, ATT_WIDTH), lambda b, i: (0, 0)),
        ],
        out_specs=pl.BlockSpec(blk, lambda b, i: (b * nb + i, 0)),
        out_shape=jax.ShapeDtypeStruct((batch * seq, ATT_WIDTH), BF16),
        compiler_params=pltpu.CompilerParams(
            dimension_semantics=("parallel", "parallel"), vmem_limit_bytes=VMEM_LIMIT),
        name="band_attention",
    )(main, main, main, main, main, main, main, bias, og)


def _attention_bias(rel_bias):
    nh = rel_bias.shape[0]
    lo = N_PAST_CHUNKS * CHUNK - (ATT_WIN - 1) + MAX_REL
    n_tail = ATT_TQ + ATT_WIN - 1 - (2 * MAX_REL + 1 - lo)
    rb = rel_bias.astype(F32)
    by_dist = jnp.concatenate([rb[:, lo:], jnp.broadcast_to(rb[:, -1:], (nh, n_tail))], axis=1)
    period = ATT_TQ + ATT_WIN
    padded = jnp.pad(by_dist, ((0, 0), (0, period - by_dist.shape[1])))
    shifted = jnp.tile(padded, (1, ATT_TQ + 1))[:, :ATT_TQ * (period + 1)]
    shifted = shifted.reshape(nh, ATT_TQ, period + 1)[:, :, :ATT_WIN]
    table = shifted[:, :, ::-1]
    r = np.arange(ATT_TQ)[:, None]
    s = np.arange(ATT_WIN)[None, :]
    back = s // CHUNK - r // CHUNK
    in_band = (back >= 0) & (back <= N_PAST_CHUNKS)
    return jnp.where(jnp.asarray(in_band)[None], table, NEG_BIG)


def _inv_unit_lower(mats, ri, ci):
    eye = jnp.where(ri == ci, 1.0, 0.0)
    diag8 = (ri >> 3) == (ci >> 3)
    n1 = [jnp.where(diag8, -a, 0.0) for a in mats]
    n1b = [n.astype(BF16) for n in n1]
    n2b = [_dot(n, n).astype(BF16) for n in n1b]
    n4b = [_dot(n, n).astype(BF16) for n in n2b]
    ts = [eye + n for n in n1]
    ts = [t + _dot(t.astype(BF16), n) for t, n in zip(ts, n2b)]
    ts = [t + _dot(t.astype(BF16), n) for t, n in zip(ts, n4b)]
    for sh in range(3, 7):
        off = ((ri >> (sh + 1)) == (ci >> (sh + 1))) & ((ri >> sh) != (ci >> sh))
        tb = [t.astype(BF16) for t in ts]
        lt = [_dot(jnp.where(off, a, 0.0).astype(BF16), t).astype(BF16) for a, t in zip(mats, tb)]
        ts = [t - _dot(t_b, x) for t, t_b, x in zip(ts, tb, lt)]
    return ts


def _gdn_kernel(x_ref, gate_ref, ab_ref, abt_ref, cw_ref, alog_ref, dtb_ref, alogt_ref, dtbt_ref,
                og_ref, o_ref, s_ref, xbuf_ref):
    i = pl.program_id(1)

    @pl.when(i == 0)
    def _():
        s_ref[...] = jnp.zeros_like(s_ref)
        xbuf_ref[0:DN_TAIL, :] = jnp.zeros((DN_TAIL, 3 * DN_WIDTH), F32)

    x = x_ref[...].astype(F32)
    xbuf_ref[DN_TAIL:DN_TAIL + DN_TB, :] = x
    y = cw_ref[CONV_K - 1:CONV_K, :] * x
    for j in range(CONV_K - 1):
        start = DN_TAIL - (CONV_K - 1) + j
        y = y + cw_ref[j:j + 1, :] * xbuf_ref[start:start + DN_TB, :]
    xbuf_ref[0:DN_TAIL, :] = x[DN_TB - DN_TAIL:DN_TB, :]
    y = y * _sigmoid(y)

    ab = ab_ref[...]
    g_col = -jnp.exp(alog_ref[...]) * _softplus(ab + dtb_ref[...])
    beta_col = _sigmoid(ab)
    abt = abt_ref[0:8, :]
    g_row = -jnp.exp(alogt_ref[...]) * _softplus(abt + dtbt_ref[...])

    rr = lax.broadcasted_iota(jnp.int32, (DN_TB, DN_TB), 0)
    cc = lax.broadcasted_iota(jnp.int32, (DN_TB, DN_TB), 1)
    same = (rr >> 7) == (cc >> 7)
    ltri = jnp.where(same & (cc <= rr), 1.0, 0.0).astype(BF16)
    utri = jnp.where(same & (rr <= cc), 1.0, 0.0).astype(BF16)
    gc1, gc2, gc3 = _split3(g_col)
    d_col = _dot(ltri, gc1) + (_dot(ltri, gc2) + _dot(ltri, gc3))
    gr1, gr2, gr3 = _split3(g_row)
    d_row = _dot(gr1, utri) + (_dot(gr2, utri) + _dot(gr3, utri))

    ri = lax.broadcasted_iota(jnp.int32, (DN_C, DN_C), 0)
    ci = lax.broadcasted_iota(jnp.int32, (DN_C, DN_C), 1)
    og = og_ref[...]
    n_chunks = DN_TB // DN_C
    items = [(c, h) for c in range(n_chunks) for h in range(DN_HEADS)]

    loc = []
    for c, h in items:
        rows = slice(c * DN_C, (c + 1) * DN_C)
        q = y[rows, h * DN_HEAD_DIM:(h + 1) * DN_HEAD_DIM]
        k = y[rows, DN_WIDTH + h * DN_HEAD_DIM:DN_WIDTH + (h + 1) * DN_HEAD_DIM]
        v = y[rows, 2 * DN_WIDTH + h * DN_HEAD_DIM:2 * DN_WIDTH + (h + 1) * DN_HEAD_DIM]
        q = q * lax.rsqrt(jnp.sum(q * q, axis=-1, keepdims=True) + EPS) * (DN_HEAD_DIM ** -0.5)
        k = k * lax.rsqrt(jnp.sum(k * k, axis=-1, keepdims=True) + EPS)
        dc = d_col[rows, h:h + 1]
        dr = d_row[h:h + 1, rows]
        dl = dr[:, DN_C - 1:DN_C]
        bc = beta_col[rows, DN_HEADS + h:DN_HEADS + h + 1]
        lmask = jnp.exp(jnp.where(ci <= ri, dc - dr, -jnp.inf))
        kb = k * bc
        edc = jnp.exp(dc)
        loc.append(dict(
            lmask=lmask, dl=dl, kbf=k.astype(BF16),
            kbq=jnp.concatenate([kb, q], axis=0).astype(BF16),
            rhs=jnp.concatenate([v * bc, kb * edc], axis=1).astype(BF16),
            qd=(q * edc).astype(BF16), kt=(k * jnp.exp(dl - dc)).astype(BF16)))
    ps = [_dot_nt(m["kbq"], m["kbf"]) for m in loc]
    a_mats = [jnp.where(ci < ri, p[0:DN_C] * m["lmask"], 0.0) for p, m in zip(ps, loc)]
    atts = [(p[DN_C:2 * DN_C] * m["lmask"]).astype(BF16) for p, m in zip(ps, loc)]
    tinvs = _inv_unit_lower(a_mats, ri, ci)
    ws = [_dot(t.astype(BF16), m["rhs"]) for t, m in zip(tinvs, loc)]

    for c in range(n_chunks):
        idx = [c * DN_HEADS + h for h in range(DN_HEADS)]
        s_old = [s_ref[h] for h in range(DN_HEADS)]
        rs = [_dot(jnp.concatenate([ws[n][:, DN_HEAD_DIM:].astype(BF16), loc[n]["qd"]], axis=0),
                   s.astype(BF16)) for n, s in zip(idx, s_old)]
        vnb = [(ws[n][:, 0:DN_HEAD_DIM] - r[0:DN_C]).astype(BF16) for n, r in zip(idx, rs)]
        outs = [r[DN_C:2 * DN_C] + _dot(atts[n], vn) for n, r, vn in zip(idx, rs, vnb)]
        for h, (n, vn) in enumerate(zip(idx, vnb)):
            s_ref[h] = s_old[h] * jnp.exp(loc[n]["dl"]) + _dot_tn(loc[n]["kt"], vn)
        rows = slice(c * DN_C, (c + 1) * DN_C)
        for h, o in enumerate(outs):
            hs = slice(h * DN_HEAD_DIM, (h + 1) * DN_HEAD_DIM)
            ms = jnp.mean(o * o, axis=-1, keepdims=True)
            gt = gate_ref[rows, hs].astype(F32)
            o_ref[rows, hs] = (o * lax.rsqrt(ms + EPS) * og * (gt * _sigmoid(gt))).astype(BF16)


def _deltanet(main, ab, abt, conv_w, alog, dtb, alogt, dtbt, og, batch, seq):
    nb = seq // DN_TB
    const = lambda b, i: (0, 0)
    return pl.pallas_call(
        _gdn_kernel,
        grid=(batch, nb),
        in_specs=[
            pl.BlockSpec((DN_TB, 3 * DN_WIDTH), lambda b, i: (b * nb + i, 1)),
            pl.BlockSpec((DN_TB, DN_WIDTH), lambda b, i: (b * nb + i, 6)),
            pl.BlockSpec((DN_TB, AB_PAD), lambda b, i: (b * nb + i, 0)),
            pl.BlockSpec((AB_PAD, DN_TB), lambda b, i: (0, b * nb + i)),
            pl.BlockSpec((CONV_K, 3 * DN_WIDTH), const),
            pl.BlockSpec((1, AB_PAD), const),
            pl.BlockSpec((1, AB_PAD), const),
            pl.BlockSpec((8, 1), const),
            pl.BlockSpec((8, 1), const),
            pl.BlockSpec((1, DN_HEAD_DIM), const),
        ],
        out_specs=pl.BlockSpec((DN_TB, DN_WIDTH), lambda b, i: (b * nb + i, 0)),
        out_shape=jax.ShapeDtypeStruct((batch * seq, DN_WIDTH), BF16),
        scratch_shapes=[
            pltpu.VMEM((DN_HEADS, DN_HEAD_DIM, DN_HEAD_DIM), F32),
            pltpu.VMEM((DN_TAIL + DN_TB, 3 * DN_WIDTH), F32),
        ],
        compiler_params=pltpu.CompilerParams(
            dimension_semantics=("arbitrary", "arbitrary"), vmem_limit_bytes=VMEM_LIMIT),
        name="gated_deltanet",
    )(main, main, ab, abt, conv_w, alog, dtb, alogt, dtbt, og)


def _out_ffn_kernel(x_ref, oa_ref, od_ref, wo_ref, fg_ref, w1_ref, w2_ref, o_ref):
    mixed = _dot(oa_ref[...], wo_ref[0:ATT_WIDTH, :]) + _dot(od_ref[...], wo_ref[ATT_WIDTH:, :])
    h1 = x_ref[...] + mixed
    ms = jnp.mean(h1 * h1, axis=-1, keepdims=True)
    n = (h1 * lax.rsqrt(ms + EPS) * fg_ref[...]).astype(BF16)
    acc = h1
    for c in range(w1_ref.shape[1] // FFN_TF):
        z = _dot(n, w1_ref[:, c * FFN_TF:(c + 1) * FFN_TF])
        act = jnp.square(jnp.maximum(z, 0.0)).astype(BF16)
        acc = acc + _dot(act, w2_ref[c * FFN_TF:(c + 1) * FFN_TF, :])
    o_ref[...] = acc


def _out_ffn(xf, o_att, o_dn, w_out, fg, w1, w2):
    t, d = xf.shape
    dff = w1.shape[1]
    const = lambda i: (0, 0)
    resident = dict(pipeline_mode=pl.Buffered(1))
    return pl.pallas_call(
        _out_ffn_kernel,
        grid=(t // FFN_TM,),
        in_specs=[
            pl.BlockSpec((FFN_TM, d), lambda i: (i, 0)),
            pl.BlockSpec((FFN_TM, ATT_WIDTH), lambda i: (i, 0)),
            pl.BlockSpec((FFN_TM, DN_WIDTH), lambda i: (i, 0)),
            pl.BlockSpec((d, d), const, **resident),
            pl.BlockSpec((1, d), const),
            pl.BlockSpec((d, dff), const, **resident),
            pl.BlockSpec((dff, d), const, **resident),
        ],
        out_specs=pl.BlockSpec((FFN_TM, d), lambda i: (i, 0)),
        out_shape=jax.ShapeDtypeStruct((t, d), F32),
        compiler_params=pltpu.CompilerParams(
            dimension_semantics=("parallel",), vmem_limit_bytes=VMEM_LIMIT),
        name="out_ffn",
    )(xf, o_att, o_dn, w_out, fg, w1, w2)


def _branches(h, p, batch, seq):
    w_in = p["w_in"]
    w_main = w_in[:, :MAIN_COLS].astype(BF16)
    w_ab = jnp.pad(w_in[:, MAIN_COLS:], ((0, 0), (0, AB_PAD - 2 * DN_HEADS))).astype(BF16)
    head_of = np.arange(ATT_WIDTH) // ATT_HEAD_DIM
    seg = jnp.asarray((head_of[:, None] == head_of[None, :]) / ATT_HEAD_DIM, BF16)
    qg = (jnp.tile(p["att_q_gain"].astype(F32), ATT_HEADS) * ATT_HEAD_DIM ** -0.5)[None]
    kg = jnp.tile(p["att_k_gain"].astype(F32), ATT_HEADS)[None]
    main, ab, abt = _inproj(h, p["mix_norm_gain"].astype(F32)[None], w_main, w_ab, w_ab.T,
                            qg, kg, seg)

    o_att = _attention(main, _attention_bias(p["rel_bias"]),
                       p["att_out_gain"].astype(F32)[None], batch, seq)

    pad_row = lambda v: jnp.pad(v.astype(F32), (0, AB_PAD - DN_HEADS))[None]
    pad_col = lambda v: jnp.pad(v.astype(F32), (0, 8 - DN_HEADS))[:, None]
    o_dn = _deltanet(main, ab, abt, p["dn_conv_w"].astype(F32),
                     pad_row(p["dn_a_log"]), pad_row(p["dn_dt_bias"]),
                     pad_col(p["dn_a_log"]), pad_col(p["dn_dt_bias"]),
                     p["dn_out_gain"].astype(F32)[None], batch, seq)
    return o_att, o_dn


def _layer(h, p, batch, seq):
    o_att, o_dn = _branches(h, p, batch, seq)
    return _out_ffn(h, o_att, o_dn, p["w_out"].astype(BF16), p["ffn_norm_gain"].astype(F32)[None],
                    p["w_ff1"].astype(BF16), p["w_ff2"].astype(BF16))


def kernel(x, mix_norm_gain, w_in, att_q_gain, att_k_gain, rel_bias, att_out_gain, dn_conv_w,
           dn_a_log, dn_dt_bias, dn_out_gain, w_out, ffn_norm_gain, w_ff1, w_ff2):
    batch, seq, d = x.shape
    assert seq % max(ATT_TQ, DN_TB) == 0 and (batch * seq) % max(IN_TM, FFN_TM) == 0
    params = dict(mix_norm_gain=mix_norm_gain, w_in=w_in, att_q_gain=att_q_gain,
                  att_k_gain=att_k_gain, rel_bias=rel_bias, att_out_gain=att_out_gain,
                  dn_conv_w=dn_conv_w, dn_a_log=dn_a_log, dn_dt_bias=dn_dt_bias,
                  dn_out_gain=dn_out_gain, w_out=w_out, ffn_norm_gain=ffn_norm_gain,
                  w_ff1=w_ff1, w_ff2=w_ff2)
    h = x.reshape(batch * seq, d)
    for layer in range(w_in.shape[0]):
        h = _layer(h, {name: v[layer] for name, v in params.items()}, batch, seq)
    return h.reshape(batch, seq, d)
```

```python
import functools

import numpy as np
import jax
import jax.numpy as jnp
from jax import lax
from jax.experimental import pallas as pl
from jax.experimental.pallas import tpu as pltpu

F32 = jnp.float32
BF16 = jnp.bfloat16

CHUNK = 64
N_PAST_CHUNKS = 8
ATT_HEADS = 8
ATT_HEAD_DIM = 64
ATT_WIDTH = ATT_HEADS * ATT_HEAD_DIM
MAX_REL = 256
DN_HEADS = 4
DN_HEAD_DIM = 128
DN_WIDTH = DN_HEADS * DN_HEAD_DIM
CONV_K = 4
EPS = 1e-6

MAIN_COLS = 3 * ATT_WIDTH + 4 * DN_WIDTH
AB_PAD = 128

LANES = 128
VMEM_LIMIT = 56 * 1024 * 1024

IN_TM = 512
IN_TN = 512
ATT_TQ = 4 * CHUNK
ATT_WIN = ATT_TQ + N_PAST_CHUNKS * CHUNK
DN_C = 128
DN_TB = 2 * DN_C
DN_TAIL = 8
FFN_TM = 512
FFN_TF = 1024
NEG_BIG = -1e30
LOG2_E = 1.4426950408889634


def _dot(a, b):
    return jnp.dot(a, b, preferred_element_type=F32)


def _dot_nt(a, b):
    return lax.dot_general(a, b, (((1,), (1,)), ((), ())), preferred_element_type=F32)


def _dot_tn(a, b):
    return lax.dot_general(a, b, (((0,), (0,)), ((), ())), preferred_element_type=F32)


def _split3(a):
    p1 = a.astype(BF16)
    r1 = a - p1.astype(F32)
    p2 = r1.astype(BF16)
    p3 = (r1 - p2.astype(F32)).astype(BF16)
    return p1, p2, p3


def _sigmoid(x):
    return 1.0 / (1.0 + jnp.exp(-x))


def _softplus(x):
    return jnp.maximum(x, 0.0) + jnp.log(1.0 + jnp.exp(-jnp.abs(x)))


def _inproj_kernel(x_ref, g_ref, wm_ref, wab_ref, wabt_ref, qg_ref, kg_ref, seg_ref,
                   main_ref, ab_ref, abt_ref):
    x = x_ref[...]
    ms = jnp.mean(x * x, axis=-1, keepdims=True)
    u = (x * lax.rsqrt(ms + EPS) * g_ref[...]).astype(BF16)
    for c in range(MAIN_COLS // IN_TN):
        cols = slice(c * IN_TN, (c + 1) * IN_TN)
        y = _dot(u, wm_ref[:, cols])
        if c < 2:
            gain = qg_ref if c == 0 else kg_ref
            m = _dot((y * y).astype(BF16), seg_ref[...])
            y = y * lax.rsqrt(m + EPS) * gain[...]
        main_ref[:, cols] = y.astype(BF16)
    ab_ref[...] = _dot(u, wab_ref[...])
    abt_ref[...] = _dot_nt(wabt_ref[...], u)


def _inproj(xf, gain, w_main, w_ab, w_abt, qg, kg, seg):
    t, d = xf.shape
    const = lambda i: (0, 0)
    return pl.pallas_call(
        _inproj_kernel,
        grid=(t // IN_TM,),
        in_specs=[
            pl.BlockSpec((IN_TM, d), lambda i: (i, 0)),
            pl.BlockSpec((1, d), const),
            pl.BlockSpec((d, MAIN_COLS), const),
            pl.BlockSpec((d, AB_PAD), const),
            pl.BlockSpec((AB_PAD, d), const),
            pl.BlockSpec((1, ATT_WIDTH), const),
            pl.BlockSpec((1, ATT_WIDTH), const),
            pl.BlockSpec((ATT_WIDTH, ATT_WIDTH), const),
        ],
        out_specs=[
            pl.BlockSpec((IN_TM, MAIN_COLS), lambda i: (i, 0)),
            pl.BlockSpec((IN_TM, AB_PAD), lambda i: (i, 0)),
            pl.BlockSpec((AB_PAD, IN_TM), lambda i: (0, i)),
        ],
        out_shape=[
            jax.ShapeDtypeStruct((t, MAIN_COLS), BF16),
            jax.ShapeDtypeStruct((t, AB_PAD), F32),
            jax.ShapeDtypeStruct((AB_PAD, t), F32),
        ],
        compiler_params=pltpu.CompilerParams(
            dimension_semantics=("parallel",), vmem_limit_bytes=VMEM_LIMIT),
        name="inproj",
    )(xf, gain, w_main, w_ab, w_abt, qg, kg, seg)


def _attn_kernel(q_ref, k0_ref, k1_ref, k2_ref, v0_ref, v1_ref, v2_ref, bias_ref, og_ref,
                 o_ref):
    lane = lax.broadcasted_iota(jnp.int32, (1, LANES), 1)
    outs = []
    for p in range(ATT_WIDTH // LANES):
        sl = slice(p * LANES, (p + 1) * LANES)
        qp = q_ref[:, sl]
        kw = jnp.concatenate([k0_ref[:, sl], k1_ref[:, sl], k2_ref[:, sl]], axis=0)
        vw = jnp.concatenate([v0_ref[:, sl], v1_ref[:, sl], v2_ref[:, sl]], axis=0)
        normed = []
        for e in range(LANES // ATT_HEAD_DIM):
            mine = (lane >= e * ATT_HEAD_DIM) & (lane < (e + 1) * ATT_HEAD_DIM)
            q_e = jnp.where(mine, qp, jnp.zeros_like(qp))
            s = _dot_nt(q_e, kw) + bias_ref[0, 2 * p + e]
            mx = jnp.max(s, axis=-1, keepdims=True)
            pe = jnp.exp2(s - mx).astype(BF16)
            v_e = jnp.where(mine, vw, jnp.ones_like(vw))
            o_e = _dot(pe, v_e)
            normed.append(o_e / pltpu.roll(o_e, ATT_HEAD_DIM, axis=1))
        outs.append(jnp.where(lane < ATT_HEAD_DIM, normed[0], normed[1]))
    o = jnp.concatenate(outs, axis=1)
    ms = jnp.mean(o * o, axis=-1, keepdims=True)
    o_ref[...] = (o * lax.rsqrt(ms + EPS) * og_ref[...]).astype(BF16)


def _attention(main, bias, og, batch, seq):
    nb = seq // ATT_TQ
    blk = (ATT_TQ, ATT_WIDTH)
    n_tables = bias.shape[0]

    def kv_spec(col, back):
        return pl.BlockSpec(blk, lambda b, i: (b * nb + jnp.maximum(i - back, 0), col))

    return pl.pallas_call(
        _attn_kernel,
        grid=(batch, nb),
        in_specs=[
            pl.BlockSpec(blk, lambda b, i: (b * nb + i, 0)),
            kv_spec(1, 2), kv_spec(1, 1), kv_spec(1, 0),
            kv_spec(2, 2), kv_spec(2, 1), kv_spec(2, 0),
            pl.BlockSpec((1, ATT_HEADS, ATT_TQ, ATT_WIN),
                         lambda b, i: (jnp.minimum(i, n_tables - 1), 0, 0, 0)),
            pl.BlockSpec((1, ATT_WIDTH), lambda b, i: (0, 0)),
        ],
        out_specs=pl.BlockSpec(blk, lambda b, i: (b * nb + i, 0)),
        out_shape=jax.ShapeDtypeStruct((batch * seq, ATT_WIDTH), BF16),
        compiler_params=pltpu.CompilerParams(
            dimension_semantics=("parallel", "parallel"), vmem_limit_bytes=VMEM_LIMIT),
        name="band_attention",
    )(main, main, main, main, main, main, main, bias, og)


def _attention_bias(rel_bias):
    nh = rel_bias.shape[0]
    lo = N_PAST_CHUNKS * CHUNK - (ATT_WIN - 1) + MAX_REL
    n_dist = ATT_TQ + ATT_WIN - 1
    n_tail = n_dist - (2 * MAX_REL + 1 - lo)
    rb = rel_bias.astype(F32) * LOG2_E
    by_dist = jnp.concatenate([rb[:, lo:], jnp.broadcast_to(rb[:, -1:], (nh, n_tail))], axis=1)
    period = ATT_TQ + ATT_WIN
    signal = jnp.roll(jnp.pad(by_dist[:, ::-1], ((0, 0), (0, period - n_dist))), -(ATT_TQ - 1), axis=1)
    rows = jnp.tile(signal, (1, ATT_TQ))[:, :ATT_TQ * (period - 1)]
    table = rows.reshape(nh, ATT_TQ, period - 1)[:, :, :ATT_WIN]
    r = np.arange(ATT_TQ)[:, None]
    s = np.arange(ATT_WIN)[None, :]
    back = s // CHUNK - r // CHUNK
    in_band = (back >= 0) & (back <= N_PAST_CHUNKS)
    tiles = np.arange(3)[:, None, None]
    real = (tiles >= 2) | ((ATT_TQ // CHUNK) * tiles - N_PAST_CHUNKS + s // CHUNK >= 0)
    return jnp.where(jnp.asarray(in_band & real)[:, None], table[None], NEG_BIG)


def _inv_unit_lower(mats, ri, ci):
    eye = jnp.where(ri == ci, 1.0, 0.0)
    diag8 = (ri >> 3) == (ci >> 3)
    n1 = [jnp.where(diag8, -a, 0.0) for a in mats]
    n1b = [n.astype(BF16) for n in n1]
    n2b = [_dot(n, n).astype(BF16) for n in n1b]
    n4b = [_dot(n, n).astype(BF16) for n in n2b]
    ts = [eye + n for n in n1]
    ts = [t + _dot(t.astype(BF16), n) for t, n in zip(ts, n2b)]
    ts = [t + _dot(t.astype(BF16), n) for t, n in zip(ts, n4b)]
    for sh in range(3, 7):
        off = ((ri >> (sh + 1)) == (ci >> (sh + 1))) & ((ri >> sh) != (ci >> sh))
        tb = [t.astype(BF16) for t in ts]
        lt = [_dot(jnp.where(off, a, 0.0).astype(BF16), t).astype(BF16) for a, t in zip(mats, tb)]
        ts = [t - _dot(t_b, x) for t, t_b, x in zip(ts, tb, lt)]
    return ts


def _gdn_kernel(x_ref, gate_ref, ab_ref, abt_ref, cw_ref, alog_ref, dtb_ref, alogt_ref, dtbt_ref,
                og_ref, o_ref, s_ref, xbuf_ref):
    i = pl.program_id(1)

    @pl.when(i == 0)
    def _():
        s_ref[...] = jnp.zeros_like(s_ref)
        xbuf_ref[0:DN_TAIL, :] = jnp.zeros((DN_TAIL, 3 * DN_WIDTH), F32)

    x = x_ref[...].astype(F32)
    xbuf_ref[DN_TAIL:DN_TAIL + DN_TB, :] = x
    y = cw_ref[CONV_K - 1:CONV_K, :] * x
    for j in range(CONV_K - 1):
        start = DN_TAIL - (CONV_K - 1) + j
        y = y + cw_ref[j:j + 1, :] * xbuf_ref[start:start + DN_TB, :]
    xbuf_ref[0:DN_TAIL, :] = x[DN_TB - DN_TAIL:DN_TB, :]
    y = y * _sigmoid(y)

    ab = ab_ref[...]
    g_col = -jnp.exp(alog_ref[...]) * _softplus(ab + dtb_ref[...])
    beta_col = _sigmoid(ab)
    abt = abt_ref[0:8, :]
    g_row = -jnp.exp(alogt_ref[...]) * _softplus(abt + dtbt_ref[...])

    rr = lax.broadcasted_iota(jnp.int32, (DN_TB, DN_TB), 0)
    cc = lax.broadcasted_iota(jnp.int32, (DN_TB, DN_TB), 1)
    same = (rr >> 7) == (cc >> 7)
    ltri = jnp.where(same & (cc <= rr), 1.0, 0.0).astype(BF16)
    utri = jnp.where(same & (rr <= cc), 1.0, 0.0).astype(BF16)
    gc1, gc2, gc3 = _split3(g_col)
    d_col = _dot(ltri, gc1) + (_dot(ltri, gc2) + _dot(ltri, gc3))
    gr1, gr2, gr3 = _split3(g_row)
    d_row = _dot(gr1, utri) + (_dot(gr2, utri) + _dot(gr3, utri))

    ri = lax.broadcasted_iota(jnp.int32, (DN_C, DN_C), 0)
    ci = lax.broadcasted_iota(jnp.int32, (DN_C, DN_C), 1)
    og = og_ref[...]
    n_chunks = DN_TB // DN_C
    items = [(c, h) for c in range(n_chunks) for h in range(DN_HEADS)]

    loc = []
    for c, h in items:
        rows = slice(c * DN_C, (c + 1) * DN_C)
        q = y[rows, h * DN_HEAD_DIM:(h + 1) * DN_HEAD_DIM]
        k = y[rows, DN_WIDTH + h * DN_HEAD_DIM:DN_WIDTH + (h + 1) * DN_HEAD_DIM]
        v = y[rows, 2 * DN_WIDTH + h * DN_HEAD_DIM:2 * DN_WIDTH + (h + 1) * DN_HEAD_DIM]
        q = q * lax.rsqrt(jnp.sum(q * q, axis=-1, keepdims=True) + EPS) * (DN_HEAD_DIM ** -0.5)
        k = k * lax.rsqrt(jnp.sum(k * k, axis=-1, keepdims=True) + EPS)
        dc = d_col[rows, h:h + 1]
        dr = d_row[h:h + 1, rows]
        dl = dr[:, DN_C - 1:DN_C]
        bc = beta_col[rows, DN_HEADS + h:DN_HEADS + h + 1]
        lmask = jnp.exp(jnp.where(ci <= ri, dc - dr, -jnp.inf))
        kb = k * bc
        edc = jnp.exp(dc)
        loc.append(dict(
            lmask=lmask, dl=dl, kbf=k.astype(BF16),
            kbq=jnp.concatenate([kb, q], axis=0).astype(BF16),
            rhs=jnp.concatenate([v * bc, kb * edc], axis=1).astype(BF16),
            qd=(q * edc).astype(BF16), kt=(k * jnp.exp(dl - dc)).astype(BF16)))
    ps = [_dot_nt(m["kbq"], m["kbf"]) for m in loc]
    a_mats = [jnp.where(ci < ri, p[0:DN_C] * m["lmask"], 0.0) for p, m in zip(ps, loc)]
    atts = [(p[DN_C:2 * DN_C] * m["lmask"]).astype(BF16) for p, m in zip(ps, loc)]
    tinvs = _inv_unit_lower(a_mats, ri, ci)
    ws = [_dot(t.astype(BF16), m["rhs"]) for t, m in zip(tinvs, loc)]

    for c in range(n_chunks):
        idx = [c * DN_HEADS + h for h in range(DN_HEADS)]
        s_old = [s_ref[h] for h in range(DN_HEADS)]
        rs = [_dot(jnp.concatenate([ws[n][:, DN_HEAD_DIM:].astype(BF16), loc[n]["qd"]], axis=0),
                   s.astype(BF16)) for n, s in zip(idx, s_old)]
        vnb = [(ws[n][:, 0:DN_HEAD_DIM] - r[0:DN_C]).astype(BF16) for n, r in zip(idx, rs)]
        outs = [r[DN_C:2 * DN_C] + _dot(atts[n], vn) for n, r, vn in zip(idx, rs, vnb)]
        for h, (n, vn) in enumerate(zip(idx, vnb)):
            s_ref[h] = s_old[h] * jnp.exp(loc[n]["dl"]) + _dot_tn(loc[n]["kt"], vn)
        rows = slice(c * DN_C, (c + 1) * DN_C)
        for h, o in enumerate(outs):
            hs = slice(h * DN_HEAD_DIM, (h + 1) * DN_HEAD_DIM)
            ms = jnp.mean(o * o, axis=-1, keepdims=True)
            gt = gate_ref[rows, hs].astype(F32)
            o_ref[rows, hs] = (o * lax.rsqrt(ms + EPS) * og * (gt * _sigmoid(gt))).astype(BF16)


def _deltanet(main, ab, abt, conv_w, alog, dtb, alogt, dtbt, og, batch, seq):
    nb = seq // DN_TB
    const = lambda b, i: (0, 0)
    return pl.pallas_call(
        _gdn_kernel,
        grid=(batch, nb),
        in_specs=[
            pl.BlockSpec((DN_TB, 3 * DN_WIDTH), lambda b, i: (b * nb + i, 1)),
            pl.BlockSpec((DN_TB, DN_WIDTH), lambda b, i: (b * nb + i, 6)),
            pl.BlockSpec((DN_TB, AB_PAD), lambda b, i: (b * nb + i, 0)),
            pl.BlockSpec((AB_PAD, DN_TB), lambda b, i: (0, b * nb + i)),
            pl.BlockSpec((CONV_K, 3 * DN_WIDTH), const),
            pl.BlockSpec((1, AB_PAD), const),
            pl.BlockSpec((1, AB_PAD), const),
            pl.BlockSpec((8, 1), const),
            pl.BlockSpec((8, 1), const),
            pl.BlockSpec((1, DN_HEAD_DIM), const),
        ],
        out_specs=pl.BlockSpec((DN_TB, DN_WIDTH), lambda b, i: (b * nb + i, 0)),
        out_shape=jax.ShapeDtypeStruct((batch * seq, DN_WIDTH), BF16),
        scratch_shapes=[
            pltpu.VMEM((DN_HEADS, DN_HEAD_DIM, DN_HEAD_DIM), F32),
            pltpu.VMEM((DN_TAIL + DN_TB, 3 * DN_WIDTH), F32),
        ],
        compiler_params=pltpu.CompilerParams(
            dimension_semantics=("arbitrary", "arbitrary"), vmem_limit_bytes=VMEM_LIMIT),
        name="gated_deltanet",
    )(main, main, ab, abt, conv_w, alog, dtb, alogt, dtbt, og)


def _out_ffn_kernel(x_ref, oa_ref, od_ref, wo_ref, fg_ref, w1_ref, w2_ref, o_ref):
    mixed = _dot(oa_ref[...], wo_ref[0:ATT_WIDTH, :]) + _dot(od_ref[...], wo_ref[ATT_WIDTH:, :])
    h1 = x_ref[...] + mixed
    ms = jnp.mean(h1 * h1, axis=-1, keepdims=True)
    n = (h1 * lax.rsqrt(ms + EPS) * fg_ref[...]).astype(BF16)
    acc = h1
    for c in range(w1_ref.shape[1] // FFN_TF):
        z = _dot(n, w1_ref[:, c * FFN_TF:(c + 1) * FFN_TF])
        act = jnp.square(jnp.maximum(z, 0.0)).astype(BF16)
        acc = acc + _dot(act, w2_ref[c * FFN_TF:(c + 1) * FFN_TF, :])
    o_ref[...] = acc


def _out_ffn(xf, o_att, o_dn, w_out, fg, w1, w2):
    t, d = xf.shape
    dff = w1.shape[1]
    const = lambda i: (0, 0)
    resident = dict(pipeline_mode=pl.Buffered(1))
    return pl.pallas_call(
        _out_ffn_kernel,
        grid=(t // FFN_TM,),
        in_specs=[
            pl.BlockSpec((FFN_TM, d), lambda i: (i, 0)),
            pl.BlockSpec((FFN_TM, ATT_WIDTH), lambda i: (i, 0)),
            pl.BlockSpec((FFN_TM, DN_WIDTH), lambda i: (i, 0)),
            pl.BlockSpec((d, d), const, **resident),
            pl.BlockSpec((1, d), const),
            pl.BlockSpec((d, dff), const, **resident),
            pl.BlockSpec((dff, d), const, **resident),
        ],
        out_specs=pl.BlockSpec((FFN_TM, d), lambda i: (i, 0)),
        out_shape=jax.ShapeDtypeStruct((t, d), F32),
        compiler_params=pltpu.CompilerParams(
            dimension_semantics=("parallel",), vmem_limit_bytes=VMEM_LIMIT),
        name="out_ffn",
    )(xf, o_att, o_dn, w_out, fg, w1, w2)


def _branches(h, p, batch, seq):
    w_in = p["w_in"]
    w_main = w_in[:, :MAIN_COLS].astype(BF16)
    w_ab = jnp.pad(w_in[:, MAIN_COLS:], ((0, 0), (0, AB_PAD - 2 * DN_HEADS))).astype(BF16)
    head_of = np.arange(ATT_WIDTH) // ATT_HEAD_DIM
    seg = jnp.asarray((head_of[:, None] == head_of[None, :]) / ATT_HEAD_DIM, BF16)
    qg = (jnp.tile(p["att_q_gain"].astype(F32), ATT_HEADS) * (ATT_HEAD_DIM ** -0.5 * LOG2_E))[None]
    kg = jnp.tile(p["att_k_gain"].astype(F32), ATT_HEADS)[None]
    main, ab, abt = _inproj(h, p["mix_norm_gain"].astype(F32)[None], w_main, w_ab, w_ab.T,
                            qg, kg, seg)

    o_att = _attention(main, _attention_bias(p["rel_bias"]),
                       p["att_out_gain"].astype(F32)[None], batch, seq)

    pad_row = lambda v: jnp.pad(v.astype(F32), (0, AB_PAD - DN_HEADS))[None]
    pad_col = lambda v: jnp.pad(v.astype(F32), (0, 8 - DN_HEADS))[:, None]
    o_dn = _deltanet(main, ab, abt, p["dn_conv_w"].astype(F32),
                     pad_row(p["dn_a_log"]), pad_row(p["dn_dt_bias"]),
                     pad_col(p["dn_a_log"]), pad_col(p["dn_dt_bias"]),
                     p["dn_out_gain"].astype(F32)[None], batch, seq)
    return o_att, o_dn


def _layer(h, p, batch, seq):
    o_att, o_dn = _branches(h, p, batch, seq)
    return _out_ffn(h, o_att, o_dn, p["w_out"].astype(BF16), p["ffn_norm_gain"].astype(F32)[None],
                    p["w_ff1"].astype(BF16), p["w_ff2"].astype(BF16))


def kernel(x, mix_norm_gain, w_in, att_q_gain, att_k_gain, rel_bias, att_out_gain, dn_conv_w,
           dn_a_log, dn_dt_bias, dn_out_gain, w_out, ffn_norm_gain, w_ff1, w_ff2):
    batch, seq, d = x.shape
    assert seq % max(ATT_TQ, DN_TB) == 0 and (batch * seq) % max(IN_TM, FFN_TM) == 0
    params = dict(mix_norm_gain=mix_norm_gain, w_in=w_in, att_q_gain=att_q_gain,
                  att_k_gain=att_k_gain, rel_bias=rel_bias, att_out_gain=att_out_gain,
                  dn_conv_w=dn_conv_w, dn_a_log=dn_a_log, dn_dt_bias=dn_dt_bias,
                  dn_out_gain=dn_out_gain, w_out=w_out, ffn_norm_gain=ffn_norm_gain,
                  w_ff1=w_ff1, w_ff2=w_ff2)
    h = x.reshape(batch * seq, d)
    for layer in range(w_in.shape[0]):
        h = _layer(h, {name: v[layer] for name, v in params.items()}, batch, seq)
    return h.reshape(batch, seq, d)
```

```python
import functools

import numpy as np
import jax
import jax.numpy as jnp
from jax import lax
from jax.experimental import pallas as pl
from jax.experimental.pallas import tpu as pltpu

F32 = jnp.float32
BF16 = jnp.bfloat16

CHUNK = 64
N_PAST_CHUNKS = 8
ATT_HEADS = 8
ATT_HEAD_DIM = 64
ATT_WIDTH = ATT_HEADS * ATT_HEAD_DIM
MAX_REL = 256
DN_HEADS = 4
DN_HEAD_DIM = 128
DN_WIDTH = DN_HEADS * DN_HEAD_DIM
CONV_K = 4
EPS = 1e-6

MAIN_COLS = 3 * ATT_WIDTH + 4 * DN_WIDTH
AB_PAD = 128

LANES = 128
VMEM_LIMIT = 56 * 1024 * 1024

IN_TM = 512
IN_TN = 512
ATT_TQ = 4 * CHUNK
ATT_WIN = ATT_TQ + N_PAST_CHUNKS * CHUNK
DN_C = 128
DN_TB = 2 * DN_C
DN_TAIL = 8
DN_ITEMS = (DN_TB // DN_C) * DN_HEADS
DN_CONV_FIRST = 3 * ATT_WIDTH // IN_TN
DN_CONV_CHUNKS = 3 * DN_WIDTH // IN_TN
FFN_TM = 512
FFN_TF = 1024
NEG_BIG = -1e30
LOG2_E = 1.4426950408889634


def _dot(a, b):
    return jnp.dot(a, b, preferred_element_type=F32)


def _dot_nt(a, b):
    return lax.dot_general(a, b, (((1,), (1,)), ((), ())), preferred_element_type=F32)


def _dot_tn(a, b):
    return lax.dot_general(a, b, (((0,), (0,)), ((), ())), preferred_element_type=F32)


def _split3(a):
    p1 = a.astype(BF16)
    r1 = a - p1.astype(F32)
    p2 = r1.astype(BF16)
    p3 = (r1 - p2.astype(F32)).astype(BF16)
    return p1, p2, p3


def _sigmoid(x):
    return 1.0 / (1.0 + jnp.exp2(x * -LOG2_E))


def _softplus(x):
    return jnp.maximum(x, 0.0) + jnp.log(1.0 + jnp.exp(-jnp.abs(x)))


def _inproj_kernel(x_ref, g_ref, wm_ref, wab_ref, qg_ref, kg_ref, seg_ref, cw_ref,
                   main_ref, ab_ref, abt_ref, stage_ref, *, tiles_per_seq):
    i = pl.program_id(0)

    @pl.when(i % tiles_per_seq == 0)
    def _():
        stage_ref[:, 0:DN_TAIL, :] = jnp.zeros((DN_CONV_CHUNKS, DN_TAIL, IN_TN), F32)

    x = x_ref[...]
    ms = jnp.mean(x * x, axis=-1, keepdims=True)
    u = (x * lax.rsqrt(ms + EPS) * g_ref[...]).astype(BF16)
    for c in range(MAIN_COLS // IN_TN):
        cols = slice(c * IN_TN, (c + 1) * IN_TN)
        y = _dot(u, wm_ref[:, cols])
        if c < 2:
            gain = qg_ref if c == 0 else kg_ref
            m = _dot((y * y).astype(BF16), seg_ref[...])
            y = y * lax.rsqrt(m + EPS) * gain[...]
        elif DN_CONV_FIRST <= c < DN_CONV_FIRST + DN_CONV_CHUNKS:
            j = c - DN_CONV_FIRST
            wcols = slice(j * IN_TN, (j + 1) * IN_TN)
            stage_ref[j, DN_TAIL:DN_TAIL + IN_TM, :] = y
            z = cw_ref[CONV_K - 1:CONV_K, wcols] * y
            for t in range(CONV_K - 1):
                start = DN_TAIL - (CONV_K - 1) + t
                z = z + cw_ref[t:t + 1, wcols] * stage_ref[j, start:start + IN_TM, :]
            stage_ref[j, 0:DN_TAIL, :] = y[IN_TM - DN_TAIL:IN_TM, :]
            y = z * _sigmoid(z)
        main_ref[:, cols] = y.astype(BF16)
    ab = _dot(u, wab_ref[...])
    ab_ref[...] = ab
    abt_ref[...] = ab.T


def _inproj(xf, gain, w_main, w_ab, qg, kg, seg, conv_w, seq):
    t, d = xf.shape
    const = lambda i: (0, 0)
    return pl.pallas_call(
        functools.partial(_inproj_kernel, tiles_per_seq=seq // IN_TM),
        grid=(t // IN_TM,),
        in_specs=[
            pl.BlockSpec((IN_TM, d), lambda i: (i, 0)),
            pl.BlockSpec((1, d), const),
            pl.BlockSpec((d, MAIN_COLS), const),
            pl.BlockSpec((d, AB_PAD), const),
            pl.BlockSpec((1, ATT_WIDTH), const),
            pl.BlockSpec((1, ATT_WIDTH), const),
            pl.BlockSpec((ATT_WIDTH, ATT_WIDTH), const),
            pl.BlockSpec((CONV_K, 3 * DN_WIDTH), const),
        ],
        out_specs=[
            pl.BlockSpec((IN_TM, MAIN_COLS), lambda i: (i, 0)),
            pl.BlockSpec((IN_TM, AB_PAD), lambda i: (i, 0)),
            pl.BlockSpec((AB_PAD, IN_TM), lambda i: (0, i)),
        ],
        out_shape=[
            jax.ShapeDtypeStruct((t, MAIN_COLS), BF16),
            jax.ShapeDtypeStruct((t, AB_PAD), F32),
            jax.ShapeDtypeStruct((AB_PAD, t), F32),
        ],
        scratch_shapes=[pltpu.VMEM((DN_CONV_CHUNKS, DN_TAIL + IN_TM, IN_TN), F32)],
        compiler_params=pltpu.CompilerParams(
            dimension_semantics=("arbitrary",), vmem_limit_bytes=VMEM_LIMIT),
        name="inproj",
    )(xf, gain, w_main, w_ab, qg, kg, seg, conv_w)


def _attn_kernel(q_ref, k0_ref, k1_ref, k2_ref, v0_ref, v1_ref, v2_ref, bias_ref, og_ref,
                 o_ref):
    lane = lax.broadcasted_iota(jnp.int32, (1, LANES), 1)
    outs = []
    for p in range(ATT_WIDTH // LANES):
        sl = slice(p * LANES, (p + 1) * LANES)
        qp = q_ref[:, sl]
        kw = jnp.concatenate([k0_ref[:, sl], k1_ref[:, sl], k2_ref[:, sl]], axis=0)
        vw = jnp.concatenate([v0_ref[:, sl], v1_ref[:, sl], v2_ref[:, sl]], axis=0)
        normed = []
        for e in range(LANES // ATT_HEAD_DIM):
            mine = (lane >= e * ATT_HEAD_DIM) & (lane < (e + 1) * ATT_HEAD_DIM)
            q_e = jnp.where(mine, qp, jnp.zeros_like(qp))
            s = _dot_nt(q_e, kw) + bias_ref[0, 2 * p + e]
            mx = jnp.max(s, axis=-1, keepdims=True)
            pe = jnp.exp2(s - mx).astype(BF16)
            v_e = jnp.where(mine, vw, jnp.ones_like(vw))
            o_e = _dot(pe, v_e)
            normed.append(o_e / pltpu.roll(o_e, ATT_HEAD_DIM, axis=1))
        outs.append(jnp.where(lane < ATT_HEAD_DIM, normed[0], normed[1]))
    o = jnp.concatenate(outs, axis=1)
    ms = jnp.mean(o * o, axis=-1, keepdims=True)
    o_ref[...] = (o * lax.rsqrt(ms + EPS) * og_ref[...]).astype(BF16)


def _attention(main, bias, og, batch, seq):
    nb = seq // ATT_TQ
    blk = (ATT_TQ, ATT_WIDTH)
    n_tables = bias.shape[0]

    def kv_spec(col, back):
        return pl.BlockSpec(blk, lambda b, i: (b * nb + jnp.maximum(i - back, 0), col))

    return pl.pallas_call(
        _attn_kernel,
        grid=(batch, nb),
        in_specs=[
            pl.BlockSpec(blk, lambda b, i: (b * nb + i, 0)),
            kv_spec(1, 2), kv_spec(1, 1), kv_spec(1, 0),
            kv_spec(2, 2), kv_spec(2, 1), kv_spec(2, 0),
            pl.BlockSpec((1, ATT_HEADS, ATT_TQ, ATT_WIN),
                         lambda b, i: (jnp.minimum(i, n_tables - 1), 0, 0, 0)),
            pl.BlockSpec((1, ATT_WIDTH), lambda b, i: (0, 0)),
        ],
        out_specs=pl.BlockSpec(blk, lambda b, i: (b * nb + i, 0)),
        out_shape=jax.ShapeDtypeStruct((batch * seq, ATT_WIDTH), BF16),
        compiler_params=pltpu.CompilerParams(
            dimension_semantics=("parallel", "parallel"), vmem_limit_bytes=VMEM_LIMIT),
        name="band_attention",
    )(main, main, main, main, main, main, main, bias, og)


def _attention_bias(rel_bias):
    nh = rel_bias.shape[0]
    lo = N_PAST_CHUNKS * CHUNK - (ATT_WIN - 1) + MAX_REL
    n_dist = ATT_TQ + ATT_WIN - 1
    n_tail = n_dist - (2 * MAX_REL + 1 - lo)
    rb = rel_bias.astype(F32) * LOG2_E
    by_dist = jnp.concatenate([rb[:, lo:], jnp.broadcast_to(rb[:, -1:], (nh, n_tail))], axis=1)
    period = ATT_TQ + ATT_WIN
    signal = jnp.roll(jnp.pad(by_dist[:, ::-1], ((0, 0), (0, period - n_dist))), -(ATT_TQ - 1), axis=1)
    rows = jnp.tile(signal, (1, ATT_TQ))[:, :ATT_TQ * (period - 1)]
    table = rows.reshape(nh, ATT_TQ, period - 1)[:, :, :ATT_WIN]
    r = np.arange(ATT_TQ)[:, None]
    s = np.arange(ATT_WIN)[None, :]
    back = s // CHUNK - r // CHUNK
    in_band = (back >= 0) & (back <= N_PAST_CHUNKS)
    tiles = np.arange(3)[:, None, None]
    real = (tiles >= 2) | ((ATT_TQ // CHUNK) * tiles - N_PAST_CHUNKS + s // CHUNK >= 0)
    return jnp.where(jnp.asarray(in_band & real)[:, None], table[None], NEG_BIG)


def _interleave(*stages):
    live = list(stages)
    while live:
        for g in list(live):
            if next(g, StopIteration) is StopIteration:
                live.remove(g)


def _inv_unit_lower(mats, ri, ci, out):
    eye = jnp.where(ri == ci, 1.0, 0.0)
    diag8 = (ri >> 3) == (ci >> 3)
    n1 = [jnp.where(diag8, -a, 0.0) for a in mats]
    n1b = [n.astype(BF16) for n in n1]
    n2b = [_dot(n, n).astype(BF16) for n in n1b]
    n4b = [_dot(n, n).astype(BF16) for n in n2b]
    ts = [eye + n for n in n1]
    ts = [t + _dot(t.astype(BF16), n) for t, n in zip(ts, n2b)]
    ts = [t + _dot(t.astype(BF16), n) for t, n in zip(ts, n4b)]
    yield
    half = DN_C // 2
    r_h = lax.broadcasted_iota(jnp.int32, (half, DN_C), 0)
    c_o = lax.broadcasted_iota(jnp.int32, (half, DN_C), 1)
    for sh in range(3, 7):
        blk = 1 << sh
        odd = [slice(s0, s0 + blk) for s0 in range(blk, DN_C, 2 * blk)]
        even = [slice(s0, s0 + blk) for s0 in range(0, DN_C, 2 * blk)]
        take = lambda m, parts: jnp.concatenate([m[p] for p in parts], axis=0)
        r_o = r_h + (((r_h >> sh) + 1) << sh)
        pair =((r_o >> (sh + 1)) == (c_o >> (sh + 1))) & ((r_o >> sh) != (c_o >> sh))
        tb = [t.astype(BF16) for t in ts]
        lt = [_dot(jnp.where(pair, take(a, odd), 0.0).astype(BF16), t) for a, t in zip(mats, tb)]
        zero = jnp.zeros((blk, DN_C), F32)
        lt = [jnp.concatenate([p for m in range(half // blk) for p in (zero, x[m * blk:(m + 1) * blk])],
                              axis=0).astype(BF16) for x in lt]
        t_odd = [take(t, odd) for t in ts]
        new = [t - _dot(t.astype(BF16), x) for t, x in zip(t_odd, lt)]
        ts = [jnp.concatenate([p for m in range(half // blk)
                               for p in (t[even[m]], n[m * blk:(m + 1) * blk])], axis=0)
              for t, n in zip(ts, new)]
        if sh == 4:
            yield
    out.extend(ts)


def _gdn_local(x_ref, ab_ref, abt_ref, ltri_ref, utri_ref, alog_ref, dtb_ref, alogt_ref, dtbt_ref, slot,
               a_scr, att_scr, rhs_scr, qd_scr, kt_scr, edl_scr):
    ab = ab_ref[...]
    g_col = -jnp.exp(alog_ref[...]) * _softplus(ab + dtb_ref[...])
    beta_col = _sigmoid(ab)
    g_row = -jnp.exp(alogt_ref[...]) * _softplus(abt_ref[0:8, :] + dtbt_ref[...])
    ltri = ltri_ref[...]
    utri = utri_ref[...]
    gc1, gc2, gc3 = _split3(g_col)
    d_col = _dot(ltri, gc1) + (_dot(ltri, gc2) + _dot(ltri, gc3))
    gr1, gr2, gr3 = _split3(g_row)
    d_row = _dot(gr1, utri) + (_dot(gr2, utri) + _dot(gr3, utri))

    y = x_ref[...].astype(F32)

    ones = jnp.ones((DN_HEAD_DIM, DN_HEAD_DIM), BF16)
    sq = [(t * t).astype(BF16) for t in (y[:, g * DN_HEAD_DIM:(g + 1) * DN_HEAD_DIM]
                                         for g in range(2 * DN_HEADS))]
    ssq = [_dot(t, ones) for t in sq]
    yield
    qk = []
    for g, tot in enumerate(ssq):
        inv = lax.rsqrt(tot + EPS)
        qk.append(y[:, g * DN_HEAD_DIM:(g + 1) * DN_HEAD_DIM]
                  * (inv * DN_HEAD_DIM ** -0.5 if g < DN_HEADS else inv))

    ri = lax.broadcasted_iota(jnp.int32, (DN_C, DN_C), 0)
    ci = lax.broadcasted_iota(jnp.int32, (DN_C, DN_C), 1)
    lmasks, kbqs, kbfs = [], [], []
    for n in range(DN_ITEMS):
        c, h = divmod(n, DN_HEADS)
        rows = slice(c * DN_C, (c + 1) * DN_C)
        q = qk[h][rows]
        k = qk[DN_HEADS + h][rows]
        v = y[rows, 2 * DN_WIDTH + h * DN_HEAD_DIM:2 * DN_WIDTH + (h + 1) * DN_HEAD_DIM]
        dc = d_col[rows, h:h + 1]
        dr = d_row[h:h + 1, rows]
        dl = dr[:, DN_C - 1:DN_C]
        bc = beta_col[rows, DN_HEADS + h:DN_HEADS + h + 1]
        kb = k * bc
        edc = jnp.exp(dc)
        rhs_scr[slot, n] = jnp.concatenate([v * bc, kb * edc], axis=1).astype(BF16)
        qd_scr[slot, n] = (q * edc).astype(BF16)
        kt_scr[slot, n] = (k * jnp.exp(dl - dc)).astype(BF16)
        edl_scr[slot, n] = jnp.broadcast_to(jnp.exp(dl), (8, DN_HEAD_DIM))
        lmasks.append(jnp.exp(jnp.where(ci <= ri, dc - dr, -jnp.inf)))
        kbqs.append(jnp.concatenate([kb, q], axis=0).astype(BF16))
        kbfs.append(k.astype(BF16))
    ps = [_dot_nt(kbq, kbf) for kbq, kbf in zip(kbqs, kbfs)]
    for n, (p, lmask) in enumerate(zip(ps, lmasks)):
        a_scr[slot, n] = jnp.where(ci < ri, p[0:DN_C] * lmask, 0.0)
        att_scr[slot, n] = (p[DN_C:2 * DN_C] * lmask).astype(BF16)


def _gdn_state(slot, gate_ref, og_ref, o_ref, out_row0, s_ref, a_scr, att_scr, rhs_scr, qd_scr, kt_scr,
               edl_scr):
    ri = lax.broadcasted_iota(jnp.int32, (DN_C, DN_C), 0)
    ci = lax.broadcasted_iota(jnp.int32, (DN_C, DN_C), 1)
    og = og_ref[...]
    tinvs = []
    yield from _inv_unit_lower([a_scr[slot, n] for n in range(DN_ITEMS)], ri, ci, tinvs)
    ws = [_dot(t.astype(BF16), rhs_scr[slot, n]) for n, t in enumerate(tinvs)]
    for c in range(DN_TB // DN_C):
        idx = [c * DN_HEADS + h for h in range(DN_HEADS)]
        s_old = [s_ref[h] for h in range(DN_HEADS)]
        rs = [_dot(jnp.concatenate([ws[n][:, DN_HEAD_DIM:].astype(BF16), qd_scr[slot, n]], axis=0),
                   s.astype(BF16)) for n, s in zip(idx, s_old)]
        vnb = [(ws[n][:, 0:DN_HEAD_DIM] - r[0:DN_C]).astype(BF16) for n, r in zip(idx, rs)]
        outs = [r[DN_C:2 * DN_C] + _dot(att_scr[slot, n], vn) for n, r, vn in zip(idx, rs, vnb)]
        for h, (n, vn) in enumerate(zip(idx, vnb)):
            s_ref[h] = s_old[h] * edl_scr[slot, n, 0:1, :] + _dot_tn(kt_scr[slot, n], vn)
        rows = slice(out_row0 + c * DN_C, out_row0 + (c + 1) * DN_C)
        for h, o in enumerate(outs):
            hs = slice(h * DN_HEAD_DIM, (h + 1) * DN_HEAD_DIM)
            ms = jnp.mean(o * o, axis=-1, keepdims=True)
            gt = gate_ref[rows, hs].astype(F32)
            o_ref[rows, hs] = (o * lax.rsqrt(ms + EPS) * og * (gt * _sigmoid(gt))).astype(BF16)


def _gdn_kernel(x0_ref, x1_ref, x2_ref, ab0_ref, ab1_ref, ab2_ref, abt0_ref, abt1_ref, abt2_ref,
                gate_ref, ltri_ref, utri_ref, alog_ref, dtb_ref, alogt_ref, dtbt_ref, og_ref,
                o_ref, s_ref, *local_scr):
    i = pl.program_id(1)
    consts = (ltri_ref, utri_ref, alog_ref, dtb_ref, alogt_ref, dtbt_ref)

    def local(x_ref, ab_ref, abt_ref, slot):
        return _gdn_local(x_ref, ab_ref, abt_ref, *consts, slot, *local_scr)

    def state(slot, out_row0):
        return _gdn_state(slot, gate_ref, og_ref, o_ref, out_row0, s_ref, *local_scr)

    @pl.when(i == 0)
    def _():
        s_ref[...] = jnp.zeros_like(s_ref)
        _interleave(local(x0_ref, ab0_ref, abt0_ref, 0))

    _interleave(local(x1_ref, ab1_ref, abt1_ref, 1), state(0, 0))
    _interleave(local(x2_ref, ab2_ref, abt2_ref, 0), state(1, DN_TB))


def _deltanet(main, ab, abt, alog, dtb, alogt, dtbt, og, batch, seq):
    nb = seq // DN_TB
    steps = nb // 2
    const = lambda b, i: (0, 0)
    first = lambda b, i: b * nb
    odd = lambda b, i: b * nb + 2 * i + 1
    nxt = lambda b, i: b * nb + jnp.minimum(2 * i + 2, nb - 1)

    def x_spec(blk):
        return pl.BlockSpec((DN_TB, 3 * DN_WIDTH), lambda b, i: (blk(b, i), 1))

    def ab_spec(blk):
        return pl.BlockSpec((DN_TB, AB_PAD), lambda b, i: (blk(b, i), 0))

    def abt_spec(blk):
        return pl.BlockSpec((AB_PAD, DN_TB), lambda b, i: (0, blk(b, i)))

    r = np.arange(DN_TB)
    same = (r[:, None] // DN_C) == (r[None, :] // DN_C)
    ltri = jnp.asarray(same & (r[None, :] <= r[:, None]), BF16)
    utri = jnp.asarray(same & (r[:, None] <= r[None, :]), BF16)
    item = lambda *shape: (2, DN_ITEMS) + shape
    return pl.pallas_call(
        _gdn_kernel,
        grid=(batch, steps),
        in_specs=[
            x_spec(first), x_spec(odd), x_spec(nxt),
            ab_spec(first), ab_spec(odd), ab_spec(nxt),
            abt_spec(first), abt_spec(odd), abt_spec(nxt),
            pl.BlockSpec((2 * DN_TB, DN_WIDTH), lambda b, i: (b * steps + i, 6)),
            pl.BlockSpec((DN_TB, DN_TB), const),
            pl.BlockSpec((DN_TB, DN_TB), const),
            pl.BlockSpec((1, AB_PAD), const),
            pl.BlockSpec((1, AB_PAD), const),
            pl.BlockSpec((8, 1), const),
            pl.BlockSpec((8, 1), const),
            pl.BlockSpec((1, DN_HEAD_DIM), const),
        ],
        out_specs=pl.BlockSpec((2 * DN_TB, DN_WIDTH), lambda b, i: (b * steps + i, 0)),
        out_shape=jax.ShapeDtypeStruct((batch * seq, DN_WIDTH), BF16),
        scratch_shapes=[
            pltpu.VMEM((DN_HEADS, DN_HEAD_DIM, DN_HEAD_DIM), F32),
            pltpu.VMEM(item(DN_C, DN_C), F32),
            pltpu.VMEM(item(DN_C, DN_C), BF16),
            pltpu.VMEM(item(DN_C, 2 * DN_HEAD_DIM), BF16),
            pltpu.VMEM(item(DN_C, DN_HEAD_DIM), BF16),
            pltpu.VMEM(item(DN_C, DN_HEAD_DIM), BF16),
            pltpu.VMEM(item(8, DN_HEAD_DIM), F32),
        ],
        compiler_params=pltpu.CompilerParams(
            dimension_semantics=("arbitrary", "arbitrary"), vmem_limit_bytes=VMEM_LIMIT),
        name="gated_deltanet",
    )(main, main, main, ab, ab, ab, abt, abt, abt, main, ltri, utri, alog, dtb, alogt, dtbt, og)


def _out_ffn_kernel(x_ref, oa_ref, od_ref, wo_ref, fg_ref, w1_ref, w2_ref, o_ref):
    mixed = _dot(oa_ref[...], wo_ref[0:ATT_WIDTH, :]) + _dot(od_ref[...], wo_ref[ATT_WIDTH:, :])
    h1 = x_ref[...] + mixed
    ms = jnp.mean(h1 * h1, axis=-1, keepdims=True)
    n = (h1 * lax.rsqrt(ms + EPS) * fg_ref[...]).astype(BF16)
    acc = h1
    for c in range(w1_ref.shape[1] // FFN_TF):
        z = _dot(n, w1_ref[:, c * FFN_TF:(c + 1) * FFN_TF])
        act = jnp.square(jnp.maximum(z, 0.0)).astype(BF16)
        acc = acc + _dot(act, w2_ref[c * FFN_TF:(c + 1) * FFN_TF, :])
    o_ref[...] = acc


def _out_ffn(xf, o_att, o_dn, w_out, fg, w1, w2):
    t, d = xf.shape
    dff = w1.shape[1]
    const = lambda i: (0, 0)
    resident = dict(pipeline_mode=pl.Buffered(1))
    return pl.pallas_call(
        _out_ffn_kernel,
        grid=(t // FFN_TM,),
        in_specs=[
            pl.BlockSpec((FFN_TM, d), lambda i: (i, 0)),
            pl.BlockSpec((FFN_TM, ATT_WIDTH), lambda i: (i, 0)),
            pl.BlockSpec((FFN_TM, DN_WIDTH), lambda i: (i, 0)),
            pl.BlockSpec((d, d), const, **resident),
            pl.BlockSpec((1, d), const),
            pl.BlockSpec((d, dff), const, **resident),
            pl.BlockSpec((dff, d), const, **resident),
        ],
        out_specs=pl.BlockSpec((FFN_TM, d), lambda i: (i, 0)),
        out_shape=jax.ShapeDtypeStruct((t, d), F32),
        compiler_params=pltpu.CompilerParams(
            dimension_semantics=("parallel",), vmem_limit_bytes=VMEM_LIMIT),
        name="out_ffn",
    )(xf, o_att, o_dn, w_out, fg, w1, w2)


def _branches(h, p, batch, seq):
    w_in = p["w_in"]
    w_main = w_in[:, :MAIN_COLS].astype(BF16)
    w_ab = jnp.pad(w_in[:, MAIN_COLS:], ((0, 0), (0, AB_PAD - 2 * DN_HEADS))).astype(BF16)
    head_of = np.arange(ATT_WIDTH) // ATT_HEAD_DIM
    seg = jnp.asarray((head_of[:, None] == head_of[None, :]) / ATT_HEAD_DIM, BF16)
    qg = (jnp.tile(p["att_q_gain"].astype(F32), ATT_HEADS) * (ATT_HEAD_DIM ** -0.5 * LOG2_E))[None]
    kg = jnp.tile(p["att_k_gain"].astype(F32), ATT_HEADS)[None]
    main, ab, abt = _inproj(h, p["mix_norm_gain"].astype(F32)[None], w_main, w_ab, qg, kg, seg,
                            p["dn_conv_w"].astype(F32), seq)

    o_att = _attention(main, _attention_bias(p["rel_bias"]),
                       p["att_out_gain"].astype(F32)[None], batch, seq)

    pad_row = lambda v: jnp.pad(v.astype(F32), (0, AB_PAD - DN_HEADS))[None]
    pad_col = lambda v: jnp.pad(v.astype(F32), (0, 8 - DN_HEADS))[:, None]
    o_dn = _deltanet(main, ab, abt, pad_row(p["dn_a_log"]), pad_row(p["dn_dt_bias"]),
                     pad_col(p["dn_a_log"]), pad_col(p["dn_dt_bias"]),
                     p["dn_out_gain"].astype(F32)[None], batch, seq)
    return o_att, o_dn


def _layer(h, p, batch, seq):
    o_att, o_dn = _branches(h, p, batch, seq)
    return _out_ffn(h, o_att, o_dn, p["w_out"].astype(BF16), p["ffn_norm_gain"].astype(F32)[None],
                    p["w_ff1"].astype(BF16), p["w_ff2"].astype(BF16))


def kernel(x, mix_norm_gain, w_in, att_q_gain, att_k_gain, rel_bias, att_out_gain, dn_conv_w,
           dn_a_log, dn_dt_bias, dn_out_gain, w_out, ffn_norm_gain, w_ff1, w_ff2):
    batch, seq, d = x.shape
    assert seq % max(ATT_TQ, 2 * DN_TB) == 0 and (batch * seq) % max(IN_TM, FFN_TM) == 0
    params = dict(mix_norm_gain=mix_norm_gain, w_in=w_in, att_q_gain=att_q_gain,
                  att_k_gain=att_k_gain, rel_bias=rel_bias, att_out_gain=att_out_gain,
                  dn_conv_w=dn_conv_w, dn_a_log=dn_a_log, dn_dt_bias=dn_dt_bias,
                  dn_out_gain=dn_out_gain, w_out=w_out, ffn_norm_gain=ffn_norm_gain,
                  w_ff1=w_ff1, w_ff2=w_ff2)
    h = x.reshape(batch * seq, d)
    for layer in range(w_in.shape[0]):
        h = _layer(h, {name: v[layer] for name, v in params.items()}, batch, seq)
    return h.reshape(batch, seq, d)
```

```python
import functools

import numpy as np
import jax
import jax.numpy as jnp
from jax import lax
from jax.experimental import pallas as pl
from jax.experimental.pallas import tpu as pltpu

F32 = jnp.float32
BF16 = jnp.bfloat16

CHUNK = 64
N_PAST_CHUNKS = 8
ATT_HEADS = 8
ATT_HEAD_DIM = 64
ATT_WIDTH = ATT_HEADS * ATT_HEAD_DIM
MAX_REL = 256
DN_HEADS = 4
DN_HEAD_DIM = 128
DN_WIDTH = DN_HEADS * DN_HEAD_DIM
CONV_K = 4
EPS = 1e-6

MAIN_COLS = 3 * ATT_WIDTH + 4 * DN_WIDTH
AB_PAD = 128

LANES = 128
VMEM_LIMIT = 56 * 1024 * 1024

IN_TM = 512
IN_TN = 512
ATT_TQ = 4 * CHUNK
ATT_WIN = ATT_TQ + N_PAST_CHUNKS * CHUNK
DN_C = 128
DN_TB = 2 * DN_C
DN_TAIL = 8
DN_ITEMS = (DN_TB // DN_C) * DN_HEADS
DN_CONV_FIRST = 3 * ATT_WIDTH // IN_TN
DN_CONV_CHUNKS = 3 * DN_WIDTH // IN_TN
FFN_TM = 512
FFN_TF = 1024
NEG_BIG = -1e30
LOG2_E = 1.4426950408889634


def _dot(a, b):
    return jnp.dot(a, b, preferred_element_type=F32)


def _dot_nt(a, b):
    return lax.dot_general(a, b, (((1,), (1,)), ((), ())), preferred_element_type=F32)


def _dot_tn(a, b):
    return lax.dot_general(a, b, (((0,), (0,)), ((), ())), preferred_element_type=F32)


def _split3(a):
    p1 = a.astype(BF16)
    r1 = a - p1.astype(F32)
    p2 = r1.astype(BF16)
    p3 = (r1 - p2.astype(F32)).astype(BF16)
    return p1, p2, p3


def _sigmoid(x):
    return 1.0 / (1.0 + jnp.exp2(x * -LOG2_E))


def _softplus(x):
    return jnp.maximum(x, 0.0) + jnp.log(1.0 + jnp.exp(-jnp.abs(x)))


def _inproj_kernel(x_ref, g_ref, wm_ref, wab_ref, qg_ref, kg_ref, seg_ref, cw_ref,
                   main_ref, ab_ref, abt_ref, stage_ref, *, tiles_per_seq):
    i = pl.program_id(0)

    @pl.when(i % tiles_per_seq == 0)
    def _():
        stage_ref[...] = jnp.zeros_like(stage_ref)

    x = x_ref[...]
    ms = jnp.mean(x * x, axis=-1, keepdims=True)
    u = (x * lax.rsqrt(ms + EPS) * g_ref[...]).astype(BF16)
    n_chunks = MAIN_COLS // IN_TN
    project = lambda c: _dot(u, wm_ref[:, c * IN_TN:(c + 1) * IN_TN])
    y_next = project(0)
    for c in range(n_chunks):
        cols = slice(c * IN_TN, (c + 1) * IN_TN)
        y = y_next
        if c + 1 < n_chunks:
            y_next = project(c + 1)
        if c < 2:
            gain = qg_ref if c == 0 else kg_ref
            m = _dot((y * y).astype(BF16), seg_ref[...])
            y = y * lax.rsqrt(m + EPS) * gain[...]
        elif DN_CONV_FIRST <= c < DN_CONV_FIRST + DN_CONV_CHUNKS:
            j = c - DN_CONV_FIRST
            wcols = slice(j * IN_TN, (j + 1) * IN_TN)
            ext = jnp.concatenate([stage_ref[j], y], axis=0)
            z = cw_ref[CONV_K - 1:CONV_K, wcols] * y
            for t in range(CONV_K - 1):
                back = CONV_K - 1 - t
                z = z + cw_ref[t:t + 1, wcols] * pltpu.roll(ext, back, axis=0)[DN_TAIL:]
            stage_ref[j] = y[IN_TM - DN_TAIL:IN_TM, :]
            y = z * _sigmoid(z)
        main_ref[:, cols] = y.astype(BF16)
    ab = _dot(u, wab_ref[...])
    ab_ref[...] = ab
    abt_ref[...] = ab.T


def _inproj(xf, gain, w_main, w_ab, qg, kg, seg, conv_w, seq):
    t, d = xf.shape
    const = lambda i: (0, 0)
    return pl.pallas_call(
        functools.partial(_inproj_kernel, tiles_per_seq=seq // IN_TM),
        grid=(t // IN_TM,),
        in_specs=[
            pl.BlockSpec((IN_TM, d), lambda i: (i, 0)),
            pl.BlockSpec((1, d), const),
            pl.BlockSpec((d, MAIN_COLS), const),
            pl.BlockSpec((d, AB_PAD), const),
            pl.BlockSpec((1, ATT_WIDTH), const),
            pl.BlockSpec((1, ATT_WIDTH), const),
            pl.BlockSpec((ATT_WIDTH, ATT_WIDTH), const),
            pl.BlockSpec((CONV_K, 3 * DN_WIDTH), const),
        ],
        out_specs=[
            pl.BlockSpec((IN_TM, MAIN_COLS), lambda i: (i, 0)),
            pl.BlockSpec((IN_TM, AB_PAD), lambda i: (i, 0)),
            pl.BlockSpec((AB_PAD, IN_TM), lambda i: (0, i)),
        ],
        out_shape=[
            jax.ShapeDtypeStruct((t, MAIN_COLS), BF16),
            jax.ShapeDtypeStruct((t, AB_PAD), F32),
            jax.ShapeDtypeStruct((AB_PAD, t), F32),
        ],
        scratch_shapes=[pltpu.VMEM((DN_CONV_CHUNKS, DN_TAIL, IN_TN), F32)],
        compiler_params=pltpu.CompilerParams(
            dimension_semantics=("arbitrary",), vmem_limit_bytes=VMEM_LIMIT),
        name="inproj",
    )(xf, gain, w_main, w_ab, qg, kg, seg, conv_w)


def _attn_kernel(q_ref, k0_ref, k1_ref, k2_ref, v0_ref, v1_ref, v2_ref, bias_ref, og_ref,
                 o_ref):
    lane = lax.broadcasted_iota(jnp.int32, (1, LANES), 1)
    per_tile = LANES // ATT_HEAD_DIM

    def mine(h):
        e = h % per_tile
        return (lane >= e * ATT_HEAD_DIM) & (lane < (e + 1) * ATT_HEAD_DIM)

    def window(r0, r1, r2, h):
        sl = slice(h // per_tile * LANES, (h // per_tile + 1) * LANES)
        return jnp.concatenate([r0[:, sl], r1[:, sl], r2[:, sl]], axis=0)

    def scores(h):
        qp = q_ref[:, h // per_tile * LANES:(h // per_tile + 1) * LANES]
        q_e = jnp.where(mine(h), qp, jnp.zeros_like(qp))
        return _dot_nt(q_e, window(k0_ref, k1_ref, k2_ref, h)) + bias_ref[0, h]

    def attend(h, s):
        mx = jnp.max(s, axis=-1, keepdims=True)
        pe = jnp.exp2(s - mx).astype(BF16)
        vw = window(v0_ref, v1_ref, v2_ref, h)
        o_e = _dot(pe, jnp.where(mine(h), vw, jnp.ones_like(vw)))
        return o_e / pltpu.roll(o_e, ATT_HEAD_DIM, axis=1)

    normed = []
    s_next = scores(0)
    for h in range(ATT_HEADS):
        s_cur = s_next
        if h + 1 < ATT_HEADS:
            s_next = scores(h + 1)
        normed.append(attend(h, s_cur))
    outs = [jnp.where(lane < ATT_HEAD_DIM, normed[t * per_tile], normed[t * per_tile + 1])
            for t in range(ATT_WIDTH // LANES)]
    o = jnp.concatenate(outs, axis=1)
    ms = jnp.mean(o * o, axis=-1, keepdims=True)
    o_ref[...] = (o * lax.rsqrt(ms + EPS) * og_ref[...]).astype(BF16)


def _attention(main, bias, og, batch, seq):
    nb = seq // ATT_TQ
    blk = (ATT_TQ, ATT_WIDTH)
    n_tables = bias.shape[0]

    def kv_spec(col, back):
        return pl.BlockSpec(blk, lambda b, i: (b * nb + jnp.maximum(i - back, 0), col))

    return pl.pallas_call(
        _attn_kernel,
        grid=(batch, nb),
        in_specs=[
            pl.BlockSpec(blk, lambda b, i: (b * nb + i, 0)),
            kv_spec(1, 2), kv_spec(1, 1), kv_spec(1, 0),
            kv_spec(2, 2), kv_spec(2, 1), kv_spec(2, 0),
            pl.BlockSpec((1, ATT_HEADS, ATT_TQ, ATT_WIN),
                         lambda b, i: (jnp.minimum(i, n_tables - 1), 0, 0, 0)),
            pl.BlockSpec((1, ATT_WIDTH), lambda b, i: (0, 0)),
        ],
        out_specs=pl.BlockSpec(blk, lambda b, i: (b * nb + i, 0)),
        out_shape=jax.ShapeDtypeStruct((batch * seq, ATT_WIDTH), BF16),
        compiler_params=pltpu.CompilerParams(
            dimension_semantics=("parallel", "parallel"), vmem_limit_bytes=VMEM_LIMIT),
        name="band_attention",
    )(main, main, main, main, main, main, main, bias, og)


def _attention_bias(rel_bias):
    nh = rel_bias.shape[0]
    lo = N_PAST_CHUNKS * CHUNK - (ATT_WIN - 1) + MAX_REL
    n_dist = ATT_TQ + ATT_WIN - 1
    n_tail = n_dist - (2 * MAX_REL + 1 - lo)
    rb = rel_bias.astype(F32) * LOG2_E
    by_dist = jnp.concatenate([rb[:, lo:], jnp.broadcast_to(rb[:, -1:], (nh, n_tail))], axis=1)
    period = ATT_TQ + ATT_WIN
    signal = jnp.roll(jnp.pad(by_dist[:, ::-1], ((0, 0), (0, period - n_dist))), -(ATT_TQ - 1), axis=1)
    rows = jnp.tile(signal, (1, ATT_TQ))[:, :ATT_TQ * (period - 1)]
    table = rows.reshape(nh, ATT_TQ, period - 1)[:, :, :ATT_WIN]
    r = np.arange(ATT_TQ)[:, None]
    s = np.arange(ATT_WIN)[None, :]
    back = s // CHUNK - r // CHUNK
    in_band = (back >= 0) & (back <= N_PAST_CHUNKS)
    tiles = np.arange(3)[:, None, None]
    real = (tiles >= 2) | ((ATT_TQ // CHUNK) * tiles - N_PAST_CHUNKS + s // CHUNK >= 0)
    return jnp.where(jnp.asarray(in_band & real)[:, None], table[None], NEG_BIG)


def _interleave(*stages):
    live = list(stages)
    while live:
        for g in list(live):
            if next(g, StopIteration) is StopIteration:
                live.remove(g)


def _inv_unit_lower(mats, ri, ci, out):
    eye = jnp.where(ri == ci, 1.0, 0.0)
    diag8 = (ri >> 3) == (ci >> 3)
    n1 = [jnp.where(diag8, -a, 0.0) for a in mats]
    n1b = [n.astype(BF16) for n in n1]
    n2b = [_dot(n, n).astype(BF16) for n in n1b]
    n4b = [_dot(n, n).astype(BF16) for n in n2b]
    ts = [eye + n for n in n1]
    ts = [t + _dot(t.astype(BF16), n) for t, n in zip(ts, n2b)]
    ts = [t + _dot(t.astype(BF16), n) for t, n in zip(ts, n4b)]
    yield
    half = DN_C // 2
    r_h = lax.broadcasted_iota(jnp.int32, (half, DN_C), 0)
    c_o = lax.broadcasted_iota(jnp.int32, (half, DN_C), 1)
    for sh in range(3, 7):
        blk = 1 << sh
        odd = [slice(s0, s0 + blk) for s0 in range(blk, DN_C, 2 * blk)]
        even = [slice(s0, s0 + blk) for s0 in range(0, DN_C, 2 * blk)]
        take = lambda m, parts: jnp.concatenate([m[p] for p in parts], axis=0)
        r_o = r_h + (((r_h >> sh) + 1) << sh)
        pair =((r_o >> (sh + 1)) == (c_o >> (sh + 1))) & ((r_o >> sh) != (c_o >> sh))
        tb = [t.astype(BF16) for t in ts]
        lt = [_dot(jnp.where(pair, take(a, odd), 0.0).astype(BF16), t) for a, t in zip(mats, tb)]
        zero = jnp.zeros((blk, DN_C), F32)
        lt = [jnp.concatenate([p for m in range(half // blk) for p in (zero, x[m * blk:(m + 1) * blk])],
                              axis=0).astype(BF16) for x in lt]
        t_odd = [take(t, odd) for t in ts]
        new = [t - _dot(t.astype(BF16), x) for t, x in zip(t_odd, lt)]
        ts = [jnp.concatenate([p for m in range(half // blk)
                               for p in (t[even[m]], n[m * blk:(m + 1) * blk])], axis=0)
              for t, n in zip(ts, new)]
        if sh == 4:
            yield
    out.extend(ts)


def _gdn_local(x_ref, ab_ref, abt_ref, ltri_ref, utri_ref, alog_ref, dtb_ref, alogt_ref, dtbt_ref, slot,
               a_scr, att_scr, rhs_scr, qd_scr, kt_scr, edl_scr):
    ab = ab_ref[...]
    g_col = -jnp.exp(alog_ref[...]) * _softplus(ab + dtb_ref[...])
    beta_col = _sigmoid(ab)
    g_row = -jnp.exp(alogt_ref[...]) * _softplus(abt_ref[0:8, :] + dtbt_ref[...])
    ltri = ltri_ref[...]
    utri = utri_ref[...]
    gc1, gc2, gc3 = _split3(g_col)
    d_col = _dot(ltri, gc1) + (_dot(ltri, gc2) + _dot(ltri, gc3))
    gr1, gr2, gr3 = _split3(g_row)
    d_row = _dot(gr1, utri) + (_dot(gr2, utri) + _dot(gr3, utri))

    y = x_ref[...].astype(F32)

    ones = jnp.ones((DN_HEAD_DIM, DN_HEAD_DIM), BF16)
    sq = [(t * t).astype(BF16) for t in (y[:, g * DN_HEAD_DIM:(g + 1) * DN_HEAD_DIM]
                                         for g in range(2 * DN_HEADS))]
    ssq = [_dot(t, ones) for t in sq]
    yield
    qk = []
    for g, tot in enumerate(ssq):
        inv = lax.rsqrt(tot + EPS)
        qk.append(y[:, g * DN_HEAD_DIM:(g + 1) * DN_HEAD_DIM]
                  * (inv * DN_HEAD_DIM ** -0.5 if g < DN_HEADS else inv))

    ri = lax.broadcasted_iota(jnp.int32, (DN_C, DN_C), 0)
    ci = lax.broadcasted_iota(jnp.int32, (DN_C, DN_C), 1)
    lmasks, kbqs, kbfs = [], [], []
    for n in range(DN_ITEMS):
        c, h = divmod(n, DN_HEADS)
        rows = slice(c * DN_C, (c + 1) * DN_C)
        q = qk[h][rows]
        k = qk[DN_HEADS + h][rows]
        v = y[rows, 2 * DN_WIDTH + h * DN_HEAD_DIM:2 * DN_WIDTH + (h + 1) * DN_HEAD_DIM]
        dc = d_col[rows, h:h + 1]
        dr = d_row[h:h + 1, rows]
        dl = dr[:, DN_C - 1:DN_C]
        bc = beta_col[rows, DN_HEADS + h:DN_HEADS + h + 1]
        kb = k * bc
        edc = jnp.exp(dc)
        rhs_scr[slot, n] = jnp.concatenate([v * bc, kb * edc], axis=1).astype(BF16)
        qd_scr[slot, n] = (q * edc).astype(BF16)
        kt_scr[slot, n] = (k * jnp.exp(dl - dc)).astype(BF16)
        edl_scr[slot, n] = jnp.broadcast_to(jnp.exp(dl), (8, DN_HEAD_DIM))
        lmasks.append(jnp.exp(jnp.where(ci <= ri, dc - dr, -jnp.inf)))
        kbqs.append(jnp.concatenate([kb, q], axis=0).astype(BF16))
        kbfs.append(k.astype(BF16))
    ps = [_dot_nt(kbq, kbf) for kbq, kbf in zip(kbqs, kbfs)]
    for n, (p, lmask) in enumerate(zip(ps, lmasks)):
        a_scr[slot, n] = jnp.where(ci < ri, p[0:DN_C] * lmask, 0.0)
        att_scr[slot, n] = (p[DN_C:2 * DN_C] * lmask).astype(BF16)


def _gdn_state(slot, gate_ref, og_ref, o_ref, out_row0, s_ref, a_scr, att_scr, rhs_scr, qd_scr, kt_scr,
               edl_scr):
    ri = lax.broadcasted_iota(jnp.int32, (DN_C, DN_C), 0)
    ci = lax.broadcasted_iota(jnp.int32, (DN_C, DN_C), 1)
    og = og_ref[...]
    tinvs = []
    yield from _inv_unit_lower([a_scr[slot, n] for n in range(DN_ITEMS)], ri, ci, tinvs)
    ws = [_dot(t.astype(BF16), rhs_scr[slot, n]) for n, t in enumerate(tinvs)]
    for c in range(DN_TB // DN_C):
        idx = [c * DN_HEADS + h for h in range(DN_HEADS)]
        s_old = [s_ref[h] for h in range(DN_HEADS)]
        rs = [_dot(jnp.concatenate([ws[n][:, DN_HEAD_DIM:].astype(BF16), qd_scr[slot, n]], axis=0),
                   s.astype(BF16)) for n, s in zip(idx, s_old)]
        vnb = [(ws[n][:, 0:DN_HEAD_DIM] - r[0:DN_C]).astype(BF16) for n, r in zip(idx, rs)]
        outs = [r[DN_C:2 * DN_C] + _dot(att_scr[slot, n], vn) for n, r, vn in zip(idx, rs, vnb)]
        for h, (n, vn) in enumerate(zip(idx, vnb)):
            s_ref[h] = s_old[h] * edl_scr[slot, n, 0:1, :] + _dot_tn(kt_scr[slot, n], vn)
        rows = slice(out_row0 + c * DN_C, out_row0 + (c + 1) * DN_C)
        for h, o in enumerate(outs):
            hs = slice(h * DN_HEAD_DIM, (h + 1) * DN_HEAD_DIM)
            ms = jnp.mean(o * o, axis=-1, keepdims=True)
            gt = gate_ref[rows, hs].astype(F32)
            o_ref[rows, hs] = (o * lax.rsqrt(ms + EPS) * og * (gt * _sigmoid(gt))).astype(BF16)


def _gdn_kernel(x0_ref, x1_ref, x2_ref, ab0_ref, ab1_ref, ab2_ref, abt0_ref, abt1_ref, abt2_ref,
                gate_ref, ltri_ref, utri_ref, alog_ref, dtb_ref, alogt_ref, dtbt_ref, og_ref,
                o_ref, s_ref, *local_scr):
    i = pl.program_id(1)
    consts = (ltri_ref, utri_ref, alog_ref, dtb_ref, alogt_ref, dtbt_ref)

    def local(x_ref, ab_ref, abt_ref, slot):
        return _gdn_local(x_ref, ab_ref, abt_ref, *consts, slot, *local_scr)

    def state(slot, out_row0):
        return _gdn_state(slot, gate_ref, og_ref, o_ref, out_row0, s_ref, *local_scr)

    @pl.when(i == 0)
    def _():
        s_ref[...] = jnp.zeros_like(s_ref)
        _interleave(local(x0_ref, ab0_ref, abt0_ref, 0))

    _interleave(local(x1_ref, ab1_ref, abt1_ref, 1), state(0, 0))
    _interleave(local(x2_ref, ab2_ref, abt2_ref, 0), state(1, DN_TB))


def _deltanet(main, ab, abt, alog, dtb, alogt, dtbt, og, batch, seq):
    nb = seq // DN_TB
    steps = nb // 2
    const = lambda b, i: (0, 0)
    first = lambda b, i: b * nb
    odd = lambda b, i: b * nb + 2 * i + 1
    nxt = lambda b, i: b * nb + jnp.minimum(2 * i + 2, nb - 1)

    def x_spec(blk):
        return pl.BlockSpec((DN_TB, 3 * DN_WIDTH), lambda b, i: (blk(b, i), 1))

    def ab_spec(blk):
        return pl.BlockSpec((DN_TB, AB_PAD), lambda b, i: (blk(b, i), 0))

    def abt_spec(blk):
        return pl.BlockSpec((AB_PAD, DN_TB), lambda b, i: (0, blk(b, i)))

    r = np.arange(DN_TB)
    same = (r[:, None] // DN_C) == (r[None, :] // DN_C)
    ltri = jnp.asarray(same & (r[None, :] <= r[:, None]), BF16)
    utri = jnp.asarray(same & (r[:, None] <= r[None, :]), BF16)
    item = lambda *shape: (2, DN_ITEMS) + shape
    return pl.pallas_call(
        _gdn_kernel,
        grid=(batch, steps),
        in_specs=[
            x_spec(first), x_spec(odd), x_spec(nxt),
            ab_spec(first), ab_spec(odd), ab_spec(nxt),
            abt_spec(first), abt_spec(odd), abt_spec(nxt),
            pl.BlockSpec((2 * DN_TB, DN_WIDTH), lambda b, i: (b * steps + i, 6)),
            pl.BlockSpec((DN_TB, DN_TB), const),
            pl.BlockSpec((DN_TB, DN_TB), const),
            pl.BlockSpec((1, AB_PAD), const),
            pl.BlockSpec((1, AB_PAD), const),
            pl.BlockSpec((8, 1), const),
            pl.BlockSpec((8, 1), const),
            pl.BlockSpec((1, DN_HEAD_DIM), const),
        ],
        out_specs=pl.BlockSpec((2 * DN_TB, DN_WIDTH), lambda b, i: (b * steps + i, 0)),
        out_shape=jax.ShapeDtypeStruct((batch * seq, DN_WIDTH), BF16),
        scratch_shapes=[
            pltpu.VMEM((DN_HEADS, DN_HEAD_DIM, DN_HEAD_DIM), F32),
            pltpu.VMEM(item(DN_C, DN_C), F32),
            pltpu.VMEM(item(DN_C, DN_C), BF16),
            pltpu.VMEM(item(DN_C, 2 * DN_HEAD_DIM), BF16),
            pltpu.VMEM(item(DN_C, DN_HEAD_DIM), BF16),
            pltpu.VMEM(item(DN_C, DN_HEAD_DIM), BF16),
            pltpu.VMEM(item(8, DN_HEAD_DIM), F32),
        ],
        compiler_params=pltpu.CompilerParams(
            dimension_semantics=("arbitrary", "arbitrary"), vmem_limit_bytes=VMEM_LIMIT),
        name="gated_deltanet",
    )(main, main, main, ab, ab, ab, abt, abt, abt, main, ltri, utri, alog, dtb, alogt, dtbt, og)


def _out_ffn_kernel(x_ref, oa_ref, od_ref, wo_ref, fg_ref, w1_ref, w2_ref, o_ref):
    mixed = _dot(oa_ref[...], wo_ref[0:ATT_WIDTH, :]) + _dot(od_ref[...], wo_ref[ATT_WIDTH:, :])
    h1 = x_ref[...] + mixed
    ms = jnp.mean(h1 * h1, axis=-1, keepdims=True)
    n = (h1 * lax.rsqrt(ms + EPS) * fg_ref[...]).astype(BF16)
    n_chunks = w1_ref.shape[1] // FFN_TF
    up = lambda c: _dot(n, w1_ref[:, c * FFN_TF:(c + 1) * FFN_TF])
    acc = h1
    z_next = up(0)
    for c in range(n_chunks):
        z = z_next
        if c + 1 < n_chunks:
            z_next = up(c + 1)
        act = jnp.square(jnp.maximum(z, 0.0)).astype(BF16)
        acc = acc + _dot(act, w2_ref[c * FFN_TF:(c + 1) * FFN_TF, :])
    o_ref[...] = acc


def _out_ffn(xf, o_att, o_dn, w_out, fg, w1, w2):
    t, d = xf.shape
    dff = w1.shape[1]
    const = lambda i: (0, 0)
    resident = dict(pipeline_mode=pl.Buffered(1))
    return pl.pallas_call(
        _out_ffn_kernel,
        grid=(t // FFN_TM,),
        in_specs=[
            pl.BlockSpec((FFN_TM, d), lambda i: (i, 0)),
            pl.BlockSpec((FFN_TM, ATT_WIDTH), lambda i: (i, 0)),
            pl.BlockSpec((FFN_TM, DN_WIDTH), lambda i: (i, 0)),
            pl.BlockSpec((d, d), const, **resident),
            pl.BlockSpec((1, d), const),
            pl.BlockSpec((d, dff), const, **resident),
            pl.BlockSpec((dff, d), const, **resident),
        ],
        out_specs=pl.BlockSpec((FFN_TM, d), lambda i: (i, 0)),
        out_shape=jax.ShapeDtypeStruct((t, d), F32),
        compiler_params=pltpu.CompilerParams(
            dimension_semantics=("parallel",), vmem_limit_bytes=VMEM_LIMIT),
        name="out_ffn",
    )(xf, o_att, o_dn, w_out, fg, w1, w2)


def _branches(h, p, batch, seq):
    w_in = p["w_in"]
    w_main = w_in[:, :MAIN_COLS].astype(BF16)
    w_ab = jnp.pad(w_in[:, MAIN_COLS:], ((0, 0), (0, AB_PAD - 2 * DN_HEADS))).astype(BF16)
    head_of = np.arange(ATT_WIDTH) // ATT_HEAD_DIM
    seg = jnp.asarray((head_of[:, None] == head_of[None, :]) / ATT_HEAD_DIM, BF16)
    qg = (jnp.tile(p["att_q_gain"].astype(F32), ATT_HEADS) * (ATT_HEAD_DIM ** -0.5 * LOG2_E))[None]
    kg = jnp.tile(p["att_k_gain"].astype(F32), ATT_HEADS)[None]
    main, ab, abt = _inproj(h, p["mix_norm_gain"].astype(F32)[None], w_main, w_ab, qg, kg, seg,
                            p["dn_conv_w"].astype(F32), seq)

    o_att = _attention(main, _attention_bias(p["rel_bias"]),
                       p["att_out_gain"].astype(F32)[None], batch, seq)

    pad_row = lambda v: jnp.pad(v.astype(F32), (0, AB_PAD - DN_HEADS))[None]
    pad_col = lambda v: jnp.pad(v.astype(F32), (0, 8 - DN_HEADS))[:, None]
    o_dn = _deltanet(main, ab, abt, pad_row(p["dn_a_log"]), pad_row(p["dn_dt_bias"]),
                     pad_col(p["dn_a_log"]), pad_col(p["dn_dt_bias"]),
                     p["dn_out_gain"].astype(F32)[None], batch, seq)
    return o_att, o_dn


def _layer(h, p, batch, seq):
    o_att, o_dn = _branches(h, p, batch, seq)
    return _out_ffn(h, o_att, o_dn, p["w_out"].astype(BF16), p["ffn_norm_gain"].astype(F32)[None],
                    p["w_ff1"].astype(BF16), p["w_ff2"].astype(BF16))


def kernel(x, mix_norm_gain, w_in, att_q_gain, att_k_gain, rel_bias, att_out_gain, dn_conv_w,
           dn_a_log, dn_dt_bias, dn_out_gain, w_out, ffn_norm_gain, w_ff1, w_ff2):
    batch, seq, d = x.shape
    assert seq % max(ATT_TQ, 2 * DN_TB) == 0 and (batch * seq) % max(IN_TM, FFN_TM) == 0
    params = dict(mix_norm_gain=mix_norm_gain, w_in=w_in, att_q_gain=att_q_gain,
                  att_k_gain=att_k_gain, rel_bias=rel_bias, att_out_gain=att_out_gain,
                  dn_conv_w=dn_conv_w, dn_a_log=dn_a_log, dn_dt_bias=dn_dt_bias,
                  dn_out_gain=dn_out_gain, w_out=w_out, ffn_norm_gain=ffn_norm_gain,
                  w_ff1=w_ff1, w_ff2=w_ff2)
    h = x.reshape(batch * seq, d)
    for layer in range(w_in.shape[0]):
        h = _layer(h, {name: v[layer] for name, v in params.items()}, batch, seq)
    return h.reshape(batch, seq, d)
```

```python
import functools

import numpy as np
import jax
import jax.numpy as jnp
from jax import lax
from jax.experimental import pallas as pl
from jax.experimental.pallas import tpu as pltpu

F32 = jnp.float32
BF16 = jnp.bfloat16

CHUNK = 64
N_PAST_CHUNKS = 8
ATT_HEADS = 8
ATT_HEAD_DIM = 64
ATT_WIDTH = ATT_HEADS * ATT_HEAD_DIM
MAX_REL = 256
DN_HEADS = 4
DN_HEAD_DIM = 128
DN_WIDTH = DN_HEADS * DN_HEAD_DIM
CONV_K = 4
EPS = 1e-6

MAIN_COLS = 3 * ATT_WIDTH + 4 * DN_WIDTH
AB_PAD = 128

LANES = 128
VMEM_LIMIT = 56 * 1024 * 1024

IN_TM = 512
IN_TN = 512
ATT_TQ = 4 * CHUNK
ATT_WIN = ATT_TQ + N_PAST_CHUNKS * CHUNK
DN_C = 128
DN_TB = 2 * DN_C
DN_TAIL = 8
DN_ITEMS = (DN_TB // DN_C) * DN_HEADS
DN_CONV_FIRST = 3 * ATT_WIDTH // IN_TN
DN_CONV_CHUNKS = 3 * DN_WIDTH // IN_TN
DN_STATE_PER_LOCAL = 4
FFN_TM = 512
FFN_TF = 1024
NEG_BIG = -1e30
LOG2_E = 1.4426950408889634


def _dot(a, b):
    return jnp.dot(a, b, preferred_element_type=F32)


def _dot_nt(a, b):
    return lax.dot_general(a, b, (((1,), (1,)), ((), ())), preferred_element_type=F32)


def _dot_tn(a, b):
    return lax.dot_general(a, b, (((0,), (0,)), ((), ())), preferred_element_type=F32)


def _split3(a):
    p1 = a.astype(BF16)
    r1 = a - p1.astype(F32)
    p2 = r1.astype(BF16)
    p3 = (r1 - p2.astype(F32)).astype(BF16)
    return p1, p2, p3


def _sigmoid(x):
    return 1.0 / (1.0 + jnp.exp2(x * -LOG2_E))


def _softplus(x):
    return jnp.maximum(x, 0.0) + jnp.log(1.0 + jnp.exp(-jnp.abs(x)))


def _inproj_kernel(x_ref, g_ref, wm_ref, wab_ref, qg_ref, kg_ref, seg_ref, cw_ref,
                   main_ref, ab_ref, abt_ref, stage_ref, *, tiles_per_seq):
    i = pl.program_id(0)

    @pl.when(i % tiles_per_seq == 0)
    def _():
        stage_ref[...] = jnp.zeros_like(stage_ref)

    x = x_ref[...]
    ms = jnp.mean(x * x, axis=-1, keepdims=True)
    u = (x * lax.rsqrt(ms + EPS) * g_ref[...]).astype(BF16)
    conv = list(range(DN_CONV_FIRST, DN_CONV_FIRST + DN_CONV_CHUNKS))
    plain = [c for c in range(MAIN_COLS // IN_TN) if c not in conv]
    order = [c for pair in zip(plain, conv) for c in pair] + plain[len(conv):]
    project = lambda c: _dot(u, wm_ref[:, c * IN_TN:(c + 1) * IN_TN])
    y_next = project(order[0])
    for n, c in enumerate(order):
        cols = slice(c * IN_TN, (c + 1) * IN_TN)
        y = y_next
        if n + 1 < len(order):
            y_next = project(order[n + 1])
        if c < 2:
            gain = qg_ref if c == 0 else kg_ref
            m = _dot((y * y).astype(BF16), seg_ref[...])
            y = y * lax.rsqrt(m + EPS) * gain[...]
        elif DN_CONV_FIRST <= c < DN_CONV_FIRST + DN_CONV_CHUNKS:
            j = c - DN_CONV_FIRST
            wcols = slice(j * IN_TN, (j + 1) * IN_TN)
            ext = jnp.concatenate([stage_ref[j], y], axis=0)
            z = cw_ref[CONV_K - 1:CONV_K, wcols] * y
            for t in range(CONV_K - 1):
                back = CONV_K - 1 - t
                z = z + cw_ref[t:t + 1, wcols] * pltpu.roll(ext, back, axis=0)[DN_TAIL:]
            stage_ref[j] = y[IN_TM - DN_TAIL:IN_TM, :]
            y = z * _sigmoid(z)
        main_ref[:, cols] = y.astype(BF16)
    ab = _dot(u, wab_ref[...])
    ab_ref[...] = ab
    abt_ref[...] = ab.T


def _inproj(xf, gain, w_main, w_ab, qg, kg, seg, conv_w, seq):
    t, d = xf.shape
    const = lambda i: (0, 0)
    return pl.pallas_call(
        functools.partial(_inproj_kernel, tiles_per_seq=seq // IN_TM),
        grid=(t // IN_TM,),
        in_specs=[
            pl.BlockSpec((IN_TM, d), lambda i: (i, 0)),
            pl.BlockSpec((1, d), const),
            pl.BlockSpec((d, MAIN_COLS), const),
            pl.BlockSpec((d, AB_PAD), const),
            pl.BlockSpec((1, ATT_WIDTH), const),
            pl.BlockSpec((1, ATT_WIDTH), const),
            pl.BlockSpec((ATT_WIDTH, ATT_WIDTH), const),
            pl.BlockSpec((CONV_K, 3 * DN_WIDTH), const),
        ],
        out_specs=[
            pl.BlockSpec((IN_TM, MAIN_COLS), lambda i: (i, 0)),
            pl.BlockSpec((IN_TM, AB_PAD), lambda i: (i, 0)),
            pl.BlockSpec((AB_PAD, IN_TM), lambda i: (0, i)),
        ],
        out_shape=[
            jax.ShapeDtypeStruct((t, MAIN_COLS), BF16),
            jax.ShapeDtypeStruct((t, AB_PAD), F32),
            jax.ShapeDtypeStruct((AB_PAD, t), F32),
        ],
        scratch_shapes=[pltpu.VMEM((DN_CONV_CHUNKS, DN_TAIL, IN_TN), F32)],
        compiler_params=pltpu.CompilerParams(
            dimension_semantics=("arbitrary",), vmem_limit_bytes=VMEM_LIMIT),
        name="inproj",
    )(xf, gain, w_main, w_ab, qg, kg, seg, conv_w)


def _attn_tile(q_ref, row0, k_refs, v_refs, bias_ref, og_ref, o_ref):
    lane = lax.broadcasted_iota(jnp.int32, (1, LANES), 1)
    per_tile = LANES // ATT_HEAD_DIM
    rows = slice(row0, row0 + ATT_TQ)

    def mine(h):
        e = h % per_tile
        return (lane >= e * ATT_HEAD_DIM) & (lane < (e + 1) * ATT_HEAD_DIM)

    def window(refs, h):
        sl = slice(h // per_tile * LANES, (h // per_tile + 1) * LANES)
        return jnp.concatenate([r[:, sl] for r in refs], axis=0)

    def scores(h):
        qp = q_ref[rows, h // per_tile * LANES:(h // per_tile + 1) * LANES]
        q_e = jnp.where(mine(h), qp, jnp.zeros_like(qp))
        return _dot_nt(q_e, window(k_refs, h)) + bias_ref[0, h]

    def attend(h, s):
        mx = jnp.max(s, axis=-1, keepdims=True)
        pe = jnp.exp2(s - mx).astype(BF16)
        vw = window(v_refs, h)
        o_e = _dot(pe, jnp.where(mine(h), vw, jnp.ones_like(vw)))
        return o_e / pltpu.roll(o_e, ATT_HEAD_DIM, axis=1)

    normed = []
    s_next = scores(0)
    for h in range(ATT_HEADS):
        s_cur = s_next
        if h + 1 < ATT_HEADS:
            s_next = scores(h + 1)
        normed.append(attend(h, s_cur))
        yield
    outs = [jnp.where(lane < ATT_HEAD_DIM, normed[t * per_tile], normed[t * per_tile + 1])
            for t in range(ATT_WIDTH // LANES)]
    o = jnp.concatenate(outs, axis=1)
    ms = jnp.mean(o * o, axis=-1, keepdims=True)
    o_ref[rows, :] = (o * lax.rsqrt(ms + EPS) * og_ref[...]).astype(BF16)


def _attn_kernel(q_ref, k0_ref, k1_ref, k2_ref, v0_ref, v1_ref, v2_ref, bias_ref, og_ref, o_ref):
    _run(_attn_tile(q_ref, 0, (k0_ref, k1_ref, k2_ref), (v0_ref, v1_ref, v2_ref), bias_ref, og_ref,
                    o_ref))


def _attention(main, bias, og, batch, seq):
    nb = seq // ATT_TQ
    blk = (ATT_TQ, ATT_WIDTH)
    n_tables = bias.shape[0]

    def kv_spec(col, back):
        return pl.BlockSpec(blk, lambda b, i: (b * nb + jnp.maximum(i - back, 0), col))

    return pl.pallas_call(
        _attn_kernel,
        grid=(batch, nb),
        in_specs=[
            pl.BlockSpec(blk, lambda b, i: (b * nb + i, 0)),
            kv_spec(1, 2), kv_spec(1, 1), kv_spec(1, 0),
            kv_spec(2, 2), kv_spec(2, 1), kv_spec(2, 0),
            pl.BlockSpec((1, ATT_HEADS, ATT_TQ, ATT_WIN),
                         lambda b, i: (jnp.minimum(i, n_tables - 1), 0, 0, 0)),
            pl.BlockSpec((1, ATT_WIDTH), lambda b, i: (0, 0)),
        ],
        out_specs=pl.BlockSpec(blk, lambda b, i: (b * nb + i, 0)),
        out_shape=jax.ShapeDtypeStruct((batch * seq, ATT_WIDTH), BF16),
        compiler_params=pltpu.CompilerParams(
            dimension_semantics=("parallel", "parallel"), vmem_limit_bytes=VMEM_LIMIT),
        name="band_attention",
    )(main, main, main, main, main, main, main, bias, og)


def _attention_bias(rel_bias):
    nh = rel_bias.shape[0]
    lo = N_PAST_CHUNKS * CHUNK - (ATT_WIN - 1) + MAX_REL
    n_dist = ATT_TQ + ATT_WIN - 1
    n_tail = n_dist - (2 * MAX_REL + 1 - lo)
    rb = rel_bias.astype(F32) * LOG2_E
    by_dist = jnp.concatenate([rb[:, lo:], jnp.broadcast_to(rb[:, -1:], (nh, n_tail))], axis=1)
    period = ATT_TQ + ATT_WIN
    signal = jnp.roll(jnp.pad(by_dist[:, ::-1], ((0, 0), (0, period - n_dist))), -(ATT_TQ - 1), axis=1)
    rows = jnp.tile(signal, (1, ATT_TQ))[:, :ATT_TQ * (period - 1)]
    table = rows.reshape(nh, ATT_TQ, period - 1)[:, :, :ATT_WIN]
    r = np.arange(ATT_TQ)[:, None]
    s = np.arange(ATT_WIN)[None, :]
    back = s // CHUNK - r // CHUNK
    in_band = (back >= 0) & (back <= N_PAST_CHUNKS)
    tiles = np.arange(3)[:, None, None]
    real = (tiles >= 2) | ((ATT_TQ // CHUNK) * tiles - N_PAST_CHUNKS + s // CHUNK >= 0)
    return jnp.where(jnp.asarray(in_band & real)[:, None], table[None], NEG_BIG)


def _weave(main, side, every):
    n = 0
    done = object()
    while True:
        if n % every == 0:
            next(side, done)
        if next(main, done) is done:
            break
        n += 1
        yield
    _run(side)


def _run(stage):
    for _ in stage:
        pass


def _inv_unit_lower(mats, ri, ci, out):
    eye = jnp.where(ri == ci, 1.0, 0.0)
    diag8 = (ri >> 3) == (ci >> 3)
    n1 = [jnp.where(diag8, -a, 0.0) for a in mats]
    n1b = [n.astype(BF16) for n in n1]
    n2b = [_dot(n, n).astype(BF16) for n in n1b]
    yield
    n4b = [_dot(n, n).astype(BF16) for n in n2b]
    ts = [eye + n for n in n1]
    yield
    ts = [t + _dot(t.astype(BF16), n) for t, n in zip(ts, n2b)]
    yield
    ts = [t + _dot(t.astype(BF16), n) for t, n in zip(ts, n4b)]
    yield
    half = DN_C // 2
    r_h = lax.broadcasted_iota(jnp.int32, (half, DN_C), 0)
    c_o = lax.broadcasted_iota(jnp.int32, (half, DN_C), 1)
    for sh in range(3, 7):
        blk = 1 << sh
        odd = [slice(s0, s0 + blk) for s0 in range(blk, DN_C, 2 * blk)]
        even = [slice(s0, s0 + blk) for s0 in range(0, DN_C, 2 * blk)]
        take = lambda m, parts: jnp.concatenate([m[p] for p in parts], axis=0)
        r_o = r_h + (((r_h >> sh) + 1) << sh)
        pair =((r_o >> (sh + 1)) == (c_o >> (sh + 1))) & ((r_o >> sh) != (c_o >> sh))
        tb = [t.astype(BF16) for t in ts]
        lt = [_dot(jnp.where(pair, take(a, odd), 0.0).astype(BF16), t) for a, t in zip(mats, tb)]
        yield
        zero = jnp.zeros((blk, DN_C), F32)
        lt = [jnp.concatenate([p for m in range(half // blk) for p in (zero, x[m * blk:(m + 1) * blk])],
                              axis=0).astype(BF16) for x in lt]
        t_odd = [take(t, odd) for t in ts]
        new = [t - _dot(t.astype(BF16), x) for t, x in zip(t_odd, lt)]
        ts = [jnp.concatenate([p for m in range(half // blk)
                               for p in (t[even[m]], n[m * blk:(m + 1) * blk])], axis=0)
              for t, n in zip(ts, new)]
        yield
    out.extend(ts)


def _gdn_local(x_ref, ab_ref, abt_ref, ltri_ref, utri_ref, alog_ref, dtb_ref, alogt_ref, dtbt_ref, slot,
               a_scr, att_scr, rhs_scr, qd_scr, kt_scr, edl_scr):
    ab = ab_ref[...]
    g_col = -jnp.exp(alog_ref[...]) * _softplus(ab + dtb_ref[...])
    beta_col = _sigmoid(ab)
    g_row = -jnp.exp(alogt_ref[...]) * _softplus(abt_ref[0:8, :] + dtbt_ref[...])
    gc1, gc2, gc3 = _split3(g_col)
    gr1, gr2, gr3 = _split3(g_row)

    y = x_ref[...].astype(F32)

    ones = jnp.ones((DN_HEAD_DIM, DN_HEAD_DIM), BF16)
    sq = [(t * t).astype(BF16) for t in (y[:, g * DN_HEAD_DIM:(g + 1) * DN_HEAD_DIM]
                                         for g in range(2 * DN_HEADS))]
    ssq = [_dot(t, ones) for t in sq]
    yield
    ltri = ltri_ref[...]
    utri = utri_ref[...]
    d_col = _dot(ltri, gc1) + (_dot(ltri, gc2) + _dot(ltri, gc3))
    d_row = _dot(gr1, utri) + (_dot(gr2, utri) + _dot(gr3, utri))
    qk = []
    for g, tot in enumerate(ssq):
        inv = lax.rsqrt(tot + EPS)
        qk.append(y[:, g * DN_HEAD_DIM:(g + 1) * DN_HEAD_DIM]
                  * (inv * DN_HEAD_DIM ** -0.5 if g < DN_HEADS else inv))

    ri = lax.broadcasted_iota(jnp.int32, (DN_C, DN_C), 0)
    ci = lax.broadcasted_iota(jnp.int32, (DN_C, DN_C), 1)
    lmasks, kbqs, kbfs = [], [], []
    for n in range(DN_ITEMS):
        c, h = divmod(n, DN_HEADS)
        rows = slice(c * DN_C, (c + 1) * DN_C)
        q = qk[h][rows]
        k = qk[DN_HEADS + h][rows]
        v = y[rows, 2 * DN_WIDTH + h * DN_HEAD_DIM:2 * DN_WIDTH + (h + 1) * DN_HEAD_DIM]
        dc = d_col[rows, h:h + 1]
        dr = d_row[h:h + 1, rows]
        dl = dr[:, DN_C - 1:DN_C]
        bc = beta_col[rows, DN_HEADS + h:DN_HEADS + h + 1]
        kb = k * bc
        edc = jnp.exp(dc)
        rhs_scr[slot, n] = jnp.concatenate([v * bc, kb * edc], axis=1).astype(BF16)
        qd_scr[slot, n] = (q * edc).astype(BF16)
        kt_scr[slot, n] = (k * jnp.exp(dl - dc)).astype(BF16)
        edl_scr[slot, n] = jnp.broadcast_to(jnp.exp(dl), (8, DN_HEAD_DIM))
        lmasks.append(jnp.exp(jnp.where(ci <= ri, dc - dr, -jnp.inf)))
        kbqs.append(jnp.concatenate([kb, q], axis=0).astype(BF16))
        kbfs.append(k.astype(BF16))
    yield
    ps =[_dot_nt(kbq, kbf) for kbq, kbf in zip(kbqs, kbfs)]
    for n, (p, lmask) in enumerate(zip(ps, lmasks)):
        a_scr[slot, n] = jnp.where(ci < ri, p[0:DN_C] * lmask, 0.0)
        att_scr[slot, n] = (p[DN_C:2 * DN_C] * lmask).astype(BF16)


def _gdn_state(slot, gate_ref, og_ref, o_ref, out_row0, s_ref, a_scr, att_scr, rhs_scr, qd_scr, kt_scr,
               edl_scr):
    ri = lax.broadcasted_iota(jnp.int32, (DN_C, DN_C), 0)
    ci = lax.broadcasted_iota(jnp.int32, (DN_C, DN_C), 1)
    og = og_ref[...]
    tinvs = []
    yield from _inv_unit_lower([a_scr[slot, n] for n in range(DN_ITEMS)], ri, ci, tinvs)
    ws = [_dot(t.astype(BF16), rhs_scr[slot, n]) for n, t in enumerate(tinvs)]
    yield
    for c in range(DN_TB // DN_C):
        idx = [c * DN_HEADS + h for h in range(DN_HEADS)]
        s_old = [s_ref[h] for h in range(DN_HEADS)]
        rs = [_dot(jnp.concatenate([ws[n][:, DN_HEAD_DIM:].astype(BF16), qd_scr[slot, n]], axis=0),
                   s.astype(BF16)) for n, s in zip(idx, s_old)]
        yield
        vnb = [(ws[n][:, 0:DN_HEAD_DIM] - r[0:DN_C]).astype(BF16) for n, r in zip(idx, rs)]
        outs = [r[DN_C:2 * DN_C] + _dot(att_scr[slot, n], vn) for n, r, vn in zip(idx, rs, vnb)]
        yield
        for h, (n, vn) in enumerate(zip(idx, vnb)):
            s_ref[h] = s_old[h] * edl_scr[slot, n, 0:1, :] + _dot_tn(kt_scr[slot, n], vn)
        yield
        rows = slice(out_row0 + c * DN_C, out_row0 + (c + 1) * DN_C)
        for h, o in enumerate(outs):
            hs = slice(h * DN_HEAD_DIM, (h + 1) * DN_HEAD_DIM)
            ms = jnp.mean(o * o, axis=-1, keepdims=True)
            gt = gate_ref[rows, hs].astype(F32)
            o_ref[rows, hs] = (o * lax.rsqrt(ms + EPS) * og * (gt * _sigmoid(gt))).astype(BF16)


def _gdn_kernel(x0_ref, x1_ref, x2_ref, ab0_ref, ab1_ref, ab2_ref, abt0_ref, abt1_ref, abt2_ref,
                gate_ref, ltri_ref, utri_ref, alog_ref, dtb_ref, alogt_ref, dtbt_ref, og_ref,
                o_ref, s_ref, *local_scr):
    i = pl.program_id(1)
    consts = (ltri_ref, utri_ref, alog_ref, dtb_ref, alogt_ref, dtbt_ref)

    def local(x_ref, ab_ref, abt_ref, slot):
        return _gdn_local(x_ref, ab_ref, abt_ref, *consts, slot, *local_scr)

    def state(slot, out_row0):
        return _gdn_state(slot, gate_ref, og_ref, o_ref, out_row0, s_ref, *local_scr)

    @pl.when(i == 0)
    def _():
        s_ref[...] = jnp.zeros_like(s_ref)
        _run(local(x0_ref, ab0_ref, abt0_ref, 0))

    _run(_weave(state(0, 0), local(x1_ref, ab1_ref, abt1_ref, 1), DN_STATE_PER_LOCAL))
    _run(_weave(state(1, DN_TB), local(x2_ref, ab2_ref, abt2_ref, 0), DN_STATE_PER_LOCAL))


def _deltanet(main, ab, abt, alog, dtb, alogt, dtbt, og, batch, seq):
    nb = seq // DN_TB
    steps = nb // 2
    const = lambda b, i: (0, 0)
    first = lambda b, i: b * nb
    odd = lambda b, i: b * nb + 2 * i + 1
    nxt = lambda b, i: b * nb + jnp.minimum(2 * i + 2, nb - 1)

    def x_spec(blk):
        return pl.BlockSpec((DN_TB, 3 * DN_WIDTH), lambda b, i: (blk(b, i), 1))

    def ab_spec(blk):
        return pl.BlockSpec((DN_TB, AB_PAD), lambda b, i: (blk(b, i), 0))

    def abt_spec(blk):
        return pl.BlockSpec((AB_PAD, DN_TB), lambda b, i: (0, blk(b, i)))

    r = np.arange(DN_TB)
    same = (r[:, None] // DN_C) == (r[None, :] // DN_C)
    ltri = jnp.asarray(same & (r[None, :] <= r[:, None]), BF16)
    utri = jnp.asarray(same & (r[:, None] <= r[None, :]), BF16)
    item = lambda *shape: (2, DN_ITEMS) + shape
    return pl.pallas_call(
        _gdn_kernel,
        grid=(batch, steps),
        in_specs=[
            x_spec(first), x_spec(odd), x_spec(nxt),
            ab_spec(first), ab_spec(odd), ab_spec(nxt),
            abt_spec(first), abt_spec(odd), abt_spec(nxt),
            pl.BlockSpec((2 * DN_TB, DN_WIDTH), lambda b, i: (b * steps + i, 6)),
            pl.BlockSpec((DN_TB, DN_TB), const),
            pl.BlockSpec((DN_TB, DN_TB), const),
            pl.BlockSpec((1, AB_PAD), const),
            pl.BlockSpec((1, AB_PAD), const),
            pl.BlockSpec((8, 1), const),
            pl.BlockSpec((8, 1), const),
            pl.BlockSpec((1, DN_HEAD_DIM), const),
        ],
        out_specs=pl.BlockSpec((2 * DN_TB, DN_WIDTH), lambda b, i: (b * steps + i, 0)),
        out_shape=jax.ShapeDtypeStruct((batch * seq, DN_WIDTH), BF16),
        scratch_shapes=[
            pltpu.VMEM((DN_HEADS, DN_HEAD_DIM, DN_HEAD_DIM), F32),
            pltpu.VMEM(item(DN_C, DN_C), F32),
            pltpu.VMEM(item(DN_C, DN_C), BF16),
            pltpu.VMEM(item(DN_C, 2 * DN_HEAD_DIM), BF16),
            pltpu.VMEM(item(DN_C, DN_HEAD_DIM), BF16),
            pltpu.VMEM(item(DN_C, DN_HEAD_DIM), BF16),
            pltpu.VMEM(item(8, DN_HEAD_DIM), F32),
        ],
        compiler_params=pltpu.CompilerParams(
            dimension_semantics=("arbitrary", "arbitrary"), vmem_limit_bytes=VMEM_LIMIT),
        name="gated_deltanet",
    )(main, main, main, ab, ab, ab, abt, abt, abt, main, ltri, utri, alog, dtb, alogt, dtbt, og)


def _out_ffn_kernel(x_ref, oa_ref, od_ref, wo_ref, fg_ref, w1_ref, w2_ref, o_ref):
    mixed = _dot(oa_ref[...], wo_ref[0:ATT_WIDTH, :]) + _dot(od_ref[...], wo_ref[ATT_WIDTH:, :])
    h1 = x_ref[...] + mixed
    ms = jnp.mean(h1 * h1, axis=-1, keepdims=True)
    n = (h1 * lax.rsqrt(ms + EPS) * fg_ref[...]).astype(BF16)
    n_chunks = w1_ref.shape[1] // FFN_TF
    up = lambda c: _dot(n, w1_ref[:, c * FFN_TF:(c + 1) * FFN_TF])
    acc = h1
    z_next = up(0)
    for c in range(n_chunks):
        z = z_next
        if c + 1 < n_chunks:
            z_next = up(c + 1)
        act = jnp.square(jnp.maximum(z, 0.0)).astype(BF16)
        acc = acc + _dot(act, w2_ref[c * FFN_TF:(c + 1) * FFN_TF, :])
    o_ref[...] = acc


def _out_ffn(xf, o_att, o_dn, w_out, fg, w1, w2):
    t, d = xf.shape
    dff = w1.shape[1]
    const = lambda i: (0, 0)
    resident = dict(pipeline_mode=pl.Buffered(1))
    return pl.pallas_call(
        _out_ffn_kernel,
        grid=(t // FFN_TM,),
        in_specs=[
            pl.BlockSpec((FFN_TM, d), lambda i: (i, 0)),
            pl.BlockSpec((FFN_TM, ATT_WIDTH), lambda i: (i, 0)),
            pl.BlockSpec((FFN_TM, DN_WIDTH), lambda i: (i, 0)),
            pl.BlockSpec((d, d), const, **resident),
            pl.BlockSpec((1, d), const),
            pl.BlockSpec((d, dff), const, **resident),
            pl.BlockSpec((dff, d), const, **resident),
        ],
        out_specs=pl.BlockSpec((FFN_TM, d), lambda i: (i, 0)),
        out_shape=jax.ShapeDtypeStruct((t, d), F32),
        compiler_params=pltpu.CompilerParams(
            dimension_semantics=("parallel",), vmem_limit_bytes=VMEM_LIMIT),
        name="out_ffn",
    )(xf, o_att, o_dn, w_out, fg, w1, w2)


def _branches(h, p, batch, seq):
    w_in = p["w_in"]
    w_main = w_in[:, :MAIN_COLS].astype(BF16)
    w_ab = jnp.pad(w_in[:, MAIN_COLS:], ((0, 0), (0, AB_PAD - 2 * DN_HEADS))).astype(BF16)
    head_of = np.arange(ATT_WIDTH) // ATT_HEAD_DIM
    seg = jnp.asarray((head_of[:, None] == head_of[None, :]) / ATT_HEAD_DIM, BF16)
    qg = (jnp.tile(p["att_q_gain"].astype(F32), ATT_HEADS) * (ATT_HEAD_DIM ** -0.5 * LOG2_E))[None]
    kg = jnp.tile(p["att_k_gain"].astype(F32), ATT_HEADS)[None]
    main, ab, abt = _inproj(h, p["mix_norm_gain"].astype(F32)[None], w_main, w_ab, qg, kg, seg,
                            p["dn_conv_w"].astype(F32), seq)

    pad_row = lambda v: jnp.pad(v.astype(F32), (0, AB_PAD - DN_HEADS))[None]
    pad_col = lambda v: jnp.pad(v.astype(F32), (0, 8 - DN_HEADS))[:, None]
    o_att = _attention(main, _attention_bias(p["rel_bias"]),
                       p["att_out_gain"].astype(F32)[None], batch, seq)
    o_dn = _deltanet(main, ab, abt, pad_row(p["dn_a_log"]), pad_row(p["dn_dt_bias"]),
                     pad_col(p["dn_a_log"]), pad_col(p["dn_dt_bias"]),
                     p["dn_out_gain"].astype(F32)[None], batch, seq)
    return o_att, o_dn


def _layer(h, p, batch, seq):
    o_att, o_dn = _branches(h, p, batch, seq)
    return _out_ffn(h, o_att, o_dn, p["w_out"].astype(BF16), p["ffn_norm_gain"].astype(F32)[None],
                    p["w_ff1"].astype(BF16), p["w_ff2"].astype(BF16))


def kernel(x, mix_norm_gain, w_in, att_q_gain, att_k_gain, rel_bias, att_out_gain, dn_conv_w,
           dn_a_log, dn_dt_bias, dn_out_gain, w_out, ffn_norm_gain, w_ff1, w_ff2):
    batch, seq, d = x.shape
    assert seq % max(ATT_TQ, 2 * DN_TB) == 0 and (batch * seq) % max(IN_TM, FFN_TM) == 0
    params = dict(mix_norm_gain=mix_norm_gain, w_in=w_in, att_q_gain=att_q_gain,
                  att_k_gain=att_k_gain, rel_bias=rel_bias, att_out_gain=att_out_gain,
                  dn_conv_w=dn_conv_w, dn_a_log=dn_a_log, dn_dt_bias=dn_dt_bias,
                  dn_out_gain=dn_out_gain, w_out=w_out, ffn_norm_gain=ffn_norm_gain,
                  w_ff1=w_ff1, w_ff2=w_ff2)
    h = x.reshape(batch * seq, d)
    for layer in range(w_in.shape[0]):
        h = _layer(h, {name: v[layer] for name, v in params.items()}, batch, seq)
    return h.reshape(batch, seq, d)
```

```python
import functools

import numpy as np
import jax
import jax.numpy as jnp
from jax import lax
from jax.experimental import pallas as pl
from jax.experimental.pallas import tpu as pltpu

F32 = jnp.float32
BF16 = jnp.bfloat16

CHUNK = 64
N_PAST_CHUNKS = 8
ATT_HEADS = 8
ATT_HEAD_DIM = 64
ATT_WIDTH = ATT_HEADS * ATT_HEAD_DIM
MAX_REL = 256
DN_HEADS = 4
DN_HEAD_DIM = 128
DN_WIDTH = DN_HEADS * DN_HEAD_DIM
CONV_K = 4
EPS = 1e-6

MAIN_COLS = 3 * ATT_WIDTH + 4 * DN_WIDTH
AB_PAD = 128

LANES = 128
VMEM_LIMIT = 56 * 1024 * 1024

IN_TM = 512
IN_TN = 512
ATT_TQ = 4 * CHUNK
ATT_WIN = ATT_TQ + N_PAST_CHUNKS * CHUNK
DN_C = 128
DN_TB = 2 * DN_C
DN_TAIL = 8
DN_ITEMS = (DN_TB // DN_C) * DN_HEADS
DN_CONV_FIRST = 3 * ATT_WIDTH // IN_TN
DN_CONV_CHUNKS = 3 * DN_WIDTH // IN_TN
DN_STATE_PER_LOCAL = 4
FFN_TM = 512
FFN_TF = 1024
NEG_BIG = -1e30
LOG2_E = 1.4426950408889634


def _dot(a, b):
    return jnp.dot(a, b, preferred_element_type=F32)


def _dot_nt(a, b):
    return lax.dot_general(a, b, (((1,), (1,)), ((), ())), preferred_element_type=F32)


def _dot_tn(a, b):
    return lax.dot_general(a, b, (((0,), (0,)), ((), ())), preferred_element_type=F32)


def _split3(a):
    p1 = a.astype(BF16)
    r1 = a - p1.astype(F32)
    p2 = r1.astype(BF16)
    p3 = (r1 - p2.astype(F32)).astype(BF16)
    return p1, p2, p3


def _sigmoid(x):
    return 1.0 / (1.0 + jnp.exp2(x * -LOG2_E))


def _softplus(x):
    return jnp.maximum(x, 0.0) + jnp.log(1.0 + jnp.exp(-jnp.abs(x)))


def _inproj_kernel(x_ref, g_ref, wm_ref, wab_ref, qg_ref, kg_ref, seg_ref, cw_ref,
                   main_ref, ab_ref, abt_ref, stage_ref, *, tiles_per_seq):
    i = pl.program_id(0)

    @pl.when(i % tiles_per_seq == 0)
    def _():
        stage_ref[...] = jnp.zeros_like(stage_ref)

    x = x_ref[...]
    ms = jnp.mean(x * x, axis=-1, keepdims=True)
    u = (x * lax.rsqrt(ms + EPS) * g_ref[...]).astype(BF16)
    conv = list(range(DN_CONV_FIRST, DN_CONV_FIRST + DN_CONV_CHUNKS))
    plain = [c for c in range(MAIN_COLS // IN_TN) if c not in conv]
    order = [c for pair in zip(plain, conv) for c in pair] + plain[len(conv):]
    project = lambda c: _dot(u, wm_ref[:, c * IN_TN:(c + 1) * IN_TN])
    y_next = project(order[0])
    for n, c in enumerate(order):
        cols = slice(c * IN_TN, (c + 1) * IN_TN)
        y = y_next
        if n + 1 < len(order):
            y_next = project(order[n + 1])
        if c < 2:
            gain = qg_ref if c == 0 else kg_ref
            m = _dot((y * y).astype(BF16), seg_ref[...])
            y = y * lax.rsqrt(m + EPS) * gain[...]
        elif DN_CONV_FIRST <= c < DN_CONV_FIRST + DN_CONV_CHUNKS:
            j = c - DN_CONV_FIRST
            wcols = slice(j * IN_TN, (j + 1) * IN_TN)
            ext = jnp.concatenate([stage_ref[j], y], axis=0)
            z = cw_ref[CONV_K - 1:CONV_K, wcols] * y
            for t in range(CONV_K - 1):
                back = CONV_K - 1 - t
                z = z + cw_ref[t:t + 1, wcols] * pltpu.roll(ext, back, axis=0)[DN_TAIL:]
            stage_ref[j] = y[IN_TM - DN_TAIL:IN_TM, :]
            y = z * _sigmoid(z)
        main_ref[:, cols] = y.astype(BF16)
    ab = _dot(u, wab_ref[...])
    ab_ref[...] = ab
    abt_ref[...] = ab.T


def _inproj(xf, gain, w_main, w_ab, qg, kg, seg, conv_w, seq):
    t, d = xf.shape
    const = lambda i: (0, 0)
    return pl.pallas_call(
        functools.partial(_inproj_kernel, tiles_per_seq=seq // IN_TM),
        grid=(t // IN_TM,),
        in_specs=[
            pl.BlockSpec((IN_TM, d), lambda i: (i, 0)),
            pl.BlockSpec((1, d), const),
            pl.BlockSpec((d, MAIN_COLS), const),
            pl.BlockSpec((d, AB_PAD), const),
            pl.BlockSpec((1, ATT_WIDTH), const),
            pl.BlockSpec((1, ATT_WIDTH), const),
            pl.BlockSpec((ATT_WIDTH, ATT_WIDTH), const),
            pl.BlockSpec((CONV_K, 3 * DN_WIDTH), const),
        ],
        out_specs=[
            pl.BlockSpec((IN_TM, MAIN_COLS), lambda i: (i, 0)),
            pl.BlockSpec((IN_TM, AB_PAD), lambda i: (i, 0)),
            pl.BlockSpec((AB_PAD, IN_TM), lambda i: (0, i)),
        ],
        out_shape=[
            jax.ShapeDtypeStruct((t, MAIN_COLS), BF16),
            jax.ShapeDtypeStruct((t, AB_PAD), F32),
            jax.ShapeDtypeStruct((AB_PAD, t), F32),
        ],
        scratch_shapes=[pltpu.VMEM((DN_CONV_CHUNKS, DN_TAIL, IN_TN), F32)],
        compiler_params=pltpu.CompilerParams(
            dimension_semantics=("arbitrary",), vmem_limit_bytes=VMEM_LIMIT),
        name="inproj",
    )(xf, gain, w_main, w_ab, qg, kg, seg, conv_w)


def _attn_tile(q_ref, row0, k_refs, v_refs, bias_ref, start_mask, og_ref, o_ref):
    lane = lax.broadcasted_iota(jnp.int32, (1, LANES), 1)
    per_tile = LANES // ATT_HEAD_DIM
    rows = slice(row0, row0 + ATT_TQ)

    def mine(h):
        e = h % per_tile
        return (lane >= e * ATT_HEAD_DIM) & (lane < (e + 1) * ATT_HEAD_DIM)

    def window(refs, h):
        sl = slice(h // per_tile * LANES, (h // per_tile + 1) * LANES)
        return jnp.concatenate([r[:, sl] for r in refs], axis=0)

    def scores(h):
        qp = q_ref[rows, h // per_tile * LANES:(h // per_tile + 1) * LANES]
        q_e = jnp.where(mine(h), qp, jnp.zeros_like(qp))
        bias = bias_ref[h] if start_mask is None else bias_ref[h] + start_mask
        return _dot_nt(q_e, window(k_refs, h)) + bias

    def attend(h, s):
        mx = jnp.max(s, axis=-1, keepdims=True)
        pe = jnp.exp2(s - mx).astype(BF16)
        vw = window(v_refs, h)
        o_e = _dot(pe, jnp.where(mine(h), vw, jnp.ones_like(vw)))
        return o_e / pltpu.roll(o_e, ATT_HEAD_DIM, axis=1)

    normed = []
    s_next = scores(0)
    for h in range(ATT_HEADS):
        s_cur = s_next
        if h + 1 < ATT_HEADS:
            s_next = scores(h + 1)
        normed.append(attend(h, s_cur))
        yield
    outs = [jnp.where(lane < ATT_HEAD_DIM, normed[t * per_tile], normed[t * per_tile + 1])
            for t in range(ATT_WIDTH // LANES)]
    o = jnp.concatenate(outs, axis=1)
    ms = jnp.mean(o * o, axis=-1, keepdims=True)
    o_ref[rows, :] = (o * lax.rsqrt(ms + EPS) * og_ref[...]).astype(BF16)


def _attn_kernel(q_ref, k0_ref, k1_ref, k2_ref, v0_ref, v1_ref, v2_ref, sig_ref, og_ref, o_ref, bias_ref):
    i = pl.program_id(1)

    @pl.when((pl.program_id(0) == 0) & (i == 0))
    def _():
        r = lax.broadcasted_iota(jnp.int32, (ATT_TQ, ATT_WIN), 0)
        s = lax.broadcasted_iota(jnp.int32, (ATT_TQ, ATT_WIN), 1)
        back = (s >> 6) - (r >> 6)
        in_band = (back >= 0) & (back <= N_PAST_CHUNKS)
        for h in range(ATT_HEADS):
            rows = jnp.broadcast_to(sig_ref[h:h + 1, :], (ATT_TQ, sig_ref.shape[1]))
            toep = pltpu.roll(rows, 0, 1, stride=1, stride_axis=0)
            bias_ref[h] = jnp.where(in_band, toep[:, 0:ATT_WIN], NEG_BIG)

    def tile(start_mask):
        _run(_attn_tile(q_ref, 0, (k0_ref, k1_ref, k2_ref), (v0_ref, v1_ref, v2_ref), bias_ref,
                        start_mask, og_ref, o_ref))

    tiles_in_past = N_PAST_CHUNKS * CHUNK // ATT_TQ

    @pl.when(i < tiles_in_past)
    def _():
        jchunk = lax.broadcasted_iota(jnp.int32, (1, ATT_WIN), 1) >> 6
        tile(jnp.where(jchunk + (ATT_TQ // CHUNK) * i - N_PAST_CHUNKS >= 0, 0.0, NEG_BIG))

    @pl.when(i >= tiles_in_past)
    def _():
        tile(None)


def _attention(main, signal, og, batch, seq):
    nb = seq // ATT_TQ
    blk = (ATT_TQ, ATT_WIDTH)

    def kv_spec(col, back):
        return pl.BlockSpec(blk, lambda b, i: (b * nb + jnp.maximum(i - back, 0), col))

    return pl.pallas_call(
        _attn_kernel,
        grid=(batch, nb),
        in_specs=[
            pl.BlockSpec(blk, lambda b, i: (b * nb + i, 0)),
            kv_spec(1, 2), kv_spec(1, 1), kv_spec(1, 0),
            kv_spec(2, 2), kv_spec(2, 1), kv_spec(2, 0),
            pl.BlockSpec(signal.shape, lambda b, i: (0, 0)),
            pl.BlockSpec((1, ATT_WIDTH), lambda b, i: (0, 0)),
        ],
        out_specs=pl.BlockSpec(blk, lambda b, i: (b * nb + i, 0)),
        out_shape=jax.ShapeDtypeStruct((batch * seq, ATT_WIDTH), BF16),
        scratch_shapes=[pltpu.VMEM((ATT_HEADS, ATT_TQ, ATT_WIN), F32)],
        compiler_params=pltpu.CompilerParams(
            dimension_semantics=("arbitrary", "arbitrary"), vmem_limit_bytes=VMEM_LIMIT),
        name="band_attention",
    )(main, main, main, main, main, main, main, signal, og)


def _bias_signal(rel_bias):
    nh = rel_bias.shape[0]
    lo = N_PAST_CHUNKS * CHUNK - (ATT_WIN - 1) + MAX_REL
    n_dist = ATT_TQ + ATT_WIN - 1
    n_tail = n_dist - (2 * MAX_REL + 1 - lo)
    rb = rel_bias.astype(F32) * LOG2_E
    by_dist = jnp.concatenate([rb[:, lo:], jnp.broadcast_to(rb[:, -1:], (nh, n_tail))], axis=1)
    period = ATT_TQ + ATT_WIN
    return jnp.roll(jnp.pad(by_dist[:, ::-1], ((0, 0), (0, period - n_dist))), -(ATT_TQ - 1), axis=1)


def _weave(main, side, every):
    n = 0
    done = object()
    while True:
        if n % every == 0:
            next(side, done)
        if next(main, done) is done:
            break
        n += 1
        yield
    _run(side)


def _run(stage):
    for _ in stage:
        pass


def _inv_unit_lower(mats, ri, ci, out):
    eye = jnp.where(ri == ci, 1.0, 0.0)
    diag8 = (ri >> 3) == (ci >> 3)
    n1 = [jnp.where(diag8, -a, 0.0) for a in mats]
    n1b = [n.astype(BF16) for n in n1]
    n2b = [_dot(n, n).astype(BF16) for n in n1b]
    yield
    n4b = [_dot(n, n).astype(BF16) for n in n2b]
    ts = [eye + n for n in n1]
    yield
    ts = [t + _dot(t.astype(BF16), n) for t, n in zip(ts, n2b)]
    yield
    ts = [t + _dot(t.astype(BF16), n) for t, n in zip(ts, n4b)]
    yield
    half = DN_C // 2
    r_h = lax.broadcasted_iota(jnp.int32, (half, DN_C), 0)
    c_o = lax.broadcasted_iota(jnp.int32, (half, DN_C), 1)
    for sh in range(3, 7):
        blk = 1 << sh
        odd = [slice(s0, s0 + blk) for s0 in range(blk, DN_C, 2 * blk)]
        even = [slice(s0, s0 + blk) for s0 in range(0, DN_C, 2 * blk)]
        take = lambda m, parts: jnp.concatenate([m[p] for p in parts], axis=0)
        r_o = r_h + (((r_h >> sh) + 1) << sh)
        pair =((r_o >> (sh + 1)) == (c_o >> (sh + 1))) & ((r_o >> sh) != (c_o >> sh))
        tb = [t.astype(BF16) for t in ts]
        lt = [_dot(jnp.where(pair, take(a, odd), 0.0).astype(BF16), t) for a, t in zip(mats, tb)]
        yield
        zero = jnp.zeros((blk, DN_C), F32)
        lt = [jnp.concatenate([p for m in range(half // blk) for p in (zero, x[m * blk:(m + 1) * blk])],
                              axis=0).astype(BF16) for x in lt]
        t_odd = [take(t, odd) for t in ts]
        new = [t - _dot(t.astype(BF16), x) for t, x in zip(t_odd, lt)]
        ts = [jnp.concatenate([p for m in range(half // blk)
                               for p in (t[even[m]], n[m * blk:(m + 1) * blk])], axis=0)
              for t, n in zip(ts, new)]
        yield
    out.extend(ts)


def _gdn_local(x_ref, ab_ref, abt_ref, ltri_ref, utri_ref, alog_ref, dtb_ref, alogt_ref, dtbt_ref, slot,
               a_scr, att_scr, rhs_scr, qd_scr, kt_scr, edl_scr):
    ab = ab_ref[...]
    g_col = -jnp.exp(alog_ref[...]) * _softplus(ab + dtb_ref[...])
    beta_col = _sigmoid(ab)
    g_row = -jnp.exp(alogt_ref[...]) * _softplus(abt_ref[0:8, :] + dtbt_ref[...])
    gc1, gc2, gc3 = _split3(g_col)
    gr1, gr2, gr3 = _split3(g_row)

    y = x_ref[...].astype(F32)

    ones = jnp.ones((DN_HEAD_DIM, DN_HEAD_DIM), BF16)
    sq = [(t * t).astype(BF16) for t in (y[:, g * DN_HEAD_DIM:(g + 1) * DN_HEAD_DIM]
                                         for g in range(2 * DN_HEADS))]
    ssq = [_dot(t, ones) for t in sq]
    yield
    ltri = ltri_ref[...]
    utri = utri_ref[...]
    d_col = _dot(ltri, gc1) + (_dot(ltri, gc2) + _dot(ltri, gc3))
    d_row = _dot(gr1, utri) + (_dot(gr2, utri) + _dot(gr3, utri))
    qk = []
    for g, tot in enumerate(ssq):
        inv = lax.rsqrt(tot + EPS)
        qk.append(y[:, g * DN_HEAD_DIM:(g + 1) * DN_HEAD_DIM]
                  * (inv * DN_HEAD_DIM ** -0.5 if g < DN_HEADS else inv))

    ri = lax.broadcasted_iota(jnp.int32, (DN_C, DN_C), 0)
    ci = lax.broadcasted_iota(jnp.int32, (DN_C, DN_C), 1)
    lmasks, kbqs, kbfs = [], [], []
    for n in range(DN_ITEMS):
        c, h = divmod(n, DN_HEADS)
        rows = slice(c * DN_C, (c + 1) * DN_C)
        q = qk[h][rows]
        k = qk[DN_HEADS + h][rows]
        v = y[rows, 2 * DN_WIDTH + h * DN_HEAD_DIM:2 * DN_WIDTH + (h + 1) * DN_HEAD_DIM]
        dc = d_col[rows, h:h + 1]
        dr = d_row[h:h + 1, rows]
        dl = dr[:, DN_C - 1:DN_C]
        bc = beta_col[rows, DN_HEADS + h:DN_HEADS + h + 1]
        kb = k * bc
        edc = jnp.exp(dc)
        rhs_scr[slot, n] = jnp.concatenate([v * bc, kb * edc], axis=1).astype(BF16)
        qd_scr[slot, n] = (q * edc).astype(BF16)
        kt_scr[slot, n] = (k * jnp.exp(dl - dc)).astype(BF16)
        edl_scr[slot, n] = jnp.broadcast_to(jnp.exp(dl), (8, DN_HEAD_DIM))
        lmasks.append(jnp.exp(jnp.where(ci <= ri, dc - dr, -jnp.inf)))
        kbqs.append(jnp.concatenate([kb, q], axis=0).astype(BF16))
        kbfs.append(k.astype(BF16))
    yield
    ps =[_dot_nt(kbq, kbf) for kbq, kbf in zip(kbqs, kbfs)]
    for n, (p, lmask) in enumerate(zip(ps, lmasks)):
        a_scr[slot, n] = jnp.where(ci < ri, p[0:DN_C] * lmask, 0.0)
        att_scr[slot, n] = (p[DN_C:2 * DN_C] * lmask).astype(BF16)


def _gdn_state(slot, gate_ref, og_ref, o_ref, out_row0, s_ref, a_scr, att_scr, rhs_scr, qd_scr, kt_scr,
               edl_scr):
    ri = lax.broadcasted_iota(jnp.int32, (DN_C, DN_C), 0)
    ci = lax.broadcasted_iota(jnp.int32, (DN_C, DN_C), 1)
    og = og_ref[...]
    tinvs = []
    yield from _inv_unit_lower([a_scr[slot, n] for n in range(DN_ITEMS)], ri, ci, tinvs)
    ws = [_dot(t.astype(BF16), rhs_scr[slot, n]) for n, t in enumerate(tinvs)]
    yield
    for c in range(DN_TB // DN_C):
        idx = [c * DN_HEADS + h for h in range(DN_HEADS)]
        s_old = [s_ref[h] for h in range(DN_HEADS)]
        rs = [_dot(jnp.concatenate([ws[n][:, DN_HEAD_DIM:].astype(BF16), qd_scr[slot, n]], axis=0),
                   s.astype(BF16)) for n, s in zip(idx, s_old)]
        yield
        vnb = [(ws[n][:, 0:DN_HEAD_DIM] - r[0:DN_C]).astype(BF16) for n, r in zip(idx, rs)]
        outs = [r[DN_C:2 * DN_C] + _dot(att_scr[slot, n], vn) for n, r, vn in zip(idx, rs, vnb)]
        yield
        for h, (n, vn) in enumerate(zip(idx, vnb)):
            s_ref[h] = s_old[h] * edl_scr[slot, n, 0:1, :] + _dot_tn(kt_scr[slot, n], vn)
        yield
        rows = slice(out_row0 + c * DN_C, out_row0 + (c + 1) * DN_C)
        for h, o in enumerate(outs):
            hs = slice(h * DN_HEAD_DIM, (h + 1) * DN_HEAD_DIM)
            ms = jnp.mean(o * o, axis=-1, keepdims=True)
            gt = gate_ref[rows, hs].astype(F32)
            o_ref[rows, hs] = (o * lax.rsqrt(ms + EPS) * og * (gt * _sigmoid(gt))).astype(BF16)


def _gdn_kernel(x0_ref, x1_ref, x2_ref, ab0_ref, ab1_ref, ab2_ref, abt0_ref, abt1_ref, abt2_ref,
                gate_ref, ltri_ref, utri_ref, alog_ref, dtb_ref, alogt_ref, dtbt_ref, og_ref,
                o_ref, s_ref, *local_scr):
    i = pl.program_id(1)
    consts = (ltri_ref, utri_ref, alog_ref, dtb_ref, alogt_ref, dtbt_ref)

    def local(x_ref, ab_ref, abt_ref, slot):
        return _gdn_local(x_ref, ab_ref, abt_ref, *consts, slot, *local_scr)

    def state(slot, out_row0):
        return _gdn_state(slot, gate_ref, og_ref, o_ref, out_row0, s_ref, *local_scr)

    @pl.when(i == 0)
    def _():
        s_ref[...] = jnp.zeros_like(s_ref)
        _run(local(x0_ref, ab0_ref, abt0_ref, 0))

    _run(_weave(state(0, 0), local(x1_ref, ab1_ref, abt1_ref, 1), DN_STATE_PER_LOCAL))
    _run(_weave(state(1, DN_TB), local(x2_ref, ab2_ref, abt2_ref, 0), DN_STATE_PER_LOCAL))


def _deltanet(main, ab, abt, alog, dtb, alogt, dtbt, og, batch, seq):
    nb = seq // DN_TB
    steps = nb // 2
    const = lambda b, i: (0, 0)
    first = lambda b, i: b * nb
    odd = lambda b, i: b * nb + 2 * i + 1
    nxt = lambda b, i: b * nb + jnp.minimum(2 * i + 2, nb - 1)

    def x_spec(blk):
        return pl.BlockSpec((DN_TB, 3 * DN_WIDTH), lambda b, i: (blk(b, i), 1))

    def ab_spec(blk):
        return pl.BlockSpec((DN_TB, AB_PAD), lambda b, i: (blk(b, i), 0))

    def abt_spec(blk):
        return pl.BlockSpec((AB_PAD, DN_TB), lambda b, i: (0, blk(b, i)))

    r = np.arange(DN_TB)
    same = (r[:, None] // DN_C) == (r[None, :] // DN_C)
    ltri = jnp.asarray(same & (r[None, :] <= r[:, None]), BF16)
    utri = jnp.asarray(same & (r[:, None] <= r[None, :]), BF16)
    item = lambda *shape: (2, DN_ITEMS) + shape
    return pl.pallas_call(
        _gdn_kernel,
        grid=(batch, steps),
        in_specs=[
            x_spec(first), x_spec(odd), x_spec(nxt),
            ab_spec(first), ab_spec(odd), ab_spec(nxt),
            abt_spec(first), abt_spec(odd), abt_spec(nxt),
            pl.BlockSpec((2 * DN_TB, DN_WIDTH), lambda b, i: (b * steps + i, 6)),
            pl.BlockSpec((DN_TB, DN_TB), const),
            pl.BlockSpec((DN_TB, DN_TB), const),
            pl.BlockSpec((1, AB_PAD), const),
            pl.BlockSpec((1, AB_PAD), const),
            pl.BlockSpec((8, 1), const),
            pl.BlockSpec((8, 1), const),
            pl.BlockSpec((1, DN_HEAD_DIM), const),
        ],
        out_specs=pl.BlockSpec((2 * DN_TB, DN_WIDTH), lambda b, i: (b * steps + i, 0)),
        out_shape=jax.ShapeDtypeStruct((batch * seq, DN_WIDTH), BF16),
        scratch_shapes=[
            pltpu.VMEM((DN_HEADS, DN_HEAD_DIM, DN_HEAD_DIM), F32),
            pltpu.VMEM(item(DN_C, DN_C), F32),
            pltpu.VMEM(item(DN_C, DN_C), BF16),
            pltpu.VMEM(item(DN_C, 2 * DN_HEAD_DIM), BF16),
            pltpu.VMEM(item(DN_C, DN_HEAD_DIM), BF16),
            pltpu.VMEM(item(DN_C, DN_HEAD_DIM), BF16),
            pltpu.VMEM(item(8, DN_HEAD_DIM), F32),
        ],
        compiler_params=pltpu.CompilerParams(
            dimension_semantics=("arbitrary", "arbitrary"), vmem_limit_bytes=VMEM_LIMIT),
        name="gated_deltanet",
    )(main, main, main, ab, ab, ab, abt, abt, abt, main, ltri, utri, alog, dtb, alogt, dtbt, og)


def _out_ffn_kernel(x_ref, oa_ref, od_ref, wo_ref, fg_ref, w1_ref, w2_ref, o_ref):
    mixed = _dot(oa_ref[...], wo_ref[0:ATT_WIDTH, :]) + _dot(od_ref[...], wo_ref[ATT_WIDTH:, :])
    h1 = x_ref[...] + mixed
    ms = jnp.mean(h1 * h1, axis=-1, keepdims=True)
    n = (h1 * lax.rsqrt(ms + EPS) * fg_ref[...]).astype(BF16)
    n_chunks = w1_ref.shape[1] // FFN_TF
    up = lambda c: _dot(n, w1_ref[:, c * FFN_TF:(c + 1) * FFN_TF])
    acc = h1
    z_next = up(0)
    for c in range(n_chunks):
        z = z_next
        if c + 1 < n_chunks:
            z_next = up(c + 1)
        act = jnp.square(jnp.maximum(z, 0.0)).astype(BF16)
        acc = acc + _dot(act, w2_ref[c * FFN_TF:(c + 1) * FFN_TF, :])
    o_ref[...] = acc


def _out_ffn(xf, o_att, o_dn, w_out, fg, w1, w2):
    t, d = xf.shape
    dff = w1.shape[1]
    const = lambda i: (0, 0)
    resident = dict(pipeline_mode=pl.Buffered(1))
    return pl.pallas_call(
        _out_ffn_kernel,
        grid=(t // FFN_TM,),
        in_specs=[
            pl.BlockSpec((FFN_TM, d), lambda i: (i, 0)),
            pl.BlockSpec((FFN_TM, ATT_WIDTH), lambda i: (i, 0)),
            pl.BlockSpec((FFN_TM, DN_WIDTH), lambda i: (i, 0)),
            pl.BlockSpec((d, d), const, **resident),
            pl.BlockSpec((1, d), const),
            pl.BlockSpec((d, dff), const, **resident),
            pl.BlockSpec((dff, d), const, **resident),
        ],
        out_specs=pl.BlockSpec((FFN_TM, d), lambda i: (i, 0)),
        out_shape=jax.ShapeDtypeStruct((t, d), F32),
        compiler_params=pltpu.CompilerParams(
            dimension_semantics=("parallel",), vmem_limit_bytes=VMEM_LIMIT),
        name="out_ffn",
    )(xf, o_att, o_dn, w_out, fg, w1, w2)


def _branches(h, p, batch, seq):
    w_in = p["w_in"]
    w_main = w_in.astype(BF16)
    w_ab = jnp.pad(w_in[:, MAIN_COLS:], ((0, 0), (0, AB_PAD - 2 * DN_HEADS))).astype(BF16)
    head_of = np.arange(ATT_WIDTH) // ATT_HEAD_DIM
    seg = jnp.asarray((head_of[:, None] == head_of[None, :]) / ATT_HEAD_DIM, BF16)
    qg = (jnp.tile(p["att_q_gain"].astype(F32), ATT_HEADS) * (ATT_HEAD_DIM ** -0.5 * LOG2_E))[None]
    kg = jnp.tile(p["att_k_gain"].astype(F32), ATT_HEADS)[None]
    main, ab, abt = _inproj(h, p["mix_norm_gain"].astype(F32)[None], w_main, w_ab, qg, kg, seg,
                            p["dn_conv_w"].astype(F32), seq)

    pad_row = lambda v: jnp.pad(v.astype(F32), (0, AB_PAD - DN_HEADS))[None]
    pad_col = lambda v: jnp.pad(v.astype(F32), (0, 8 - DN_HEADS))[:, None]
    o_att = _attention(main, _bias_signal(p["rel_bias"]),
                       p["att_out_gain"].astype(F32)[None], batch, seq)
    o_dn = _deltanet(main, ab, abt, pad_row(p["dn_a_log"]), pad_row(p["dn_dt_bias"]),
                     pad_col(p["dn_a_log"]), pad_col(p["dn_dt_bias"]),
                     p["dn_out_gain"].astype(F32)[None], batch, seq)
    return o_att, o_dn


def _layer(h, p, batch, seq):
    o_att, o_dn = _branches(h, p, batch, seq)
    return _out_ffn(h, o_att, o_dn, p["w_out"].astype(BF16), p["ffn_norm_gain"].astype(F32)[None],
                    p["w_ff1"].astype(BF16), p["w_ff2"].astype(BF16))


def kernel(x, mix_norm_gain, w_in, att_q_gain, att_k_gain, rel_bias, att_out_gain, dn_conv_w,
           dn_a_log, dn_dt_bias, dn_out_gain, w_out, ffn_norm_gain, w_ff1, w_ff2):
    batch, seq, d = x.shape
    assert seq % max(ATT_TQ, 2 * DN_TB) == 0 and (batch * seq) % max(IN_TM, FFN_TM) == 0
    params = dict(mix_norm_gain=mix_norm_gain, w_in=w_in, att_q_gain=att_q_gain,
                  att_k_gain=att_k_gain, rel_bias=rel_bias, att_out_gain=att_out_gain,
                  dn_conv_w=dn_conv_w, dn_a_log=dn_a_log, dn_dt_bias=dn_dt_bias,
                  dn_out_gain=dn_out_gain, w_out=w_out, ffn_norm_gain=ffn_norm_gain,
                  w_ff1=w_ff1, w_ff2=w_ff2)
    h = x.reshape(batch * seq, d)
    for layer in range(w_in.shape[0]):
        h = _layer(h, {name: v[layer] for name, v in params.items()}, batch, seq)
    return h.reshape(batch, seq, d)
```

```python
import functools

import numpy as np
import jax
import jax.numpy as jnp
from jax import lax
from jax.experimental import pallas as pl
from jax.experimental.pallas import tpu as pltpu

F32 = jnp.float32
BF16 = jnp.bfloat16

CHUNK = 64
N_PAST_CHUNKS = 8
ATT_HEADS = 8
ATT_HEAD_DIM = 64
ATT_WIDTH = ATT_HEADS * ATT_HEAD_DIM
MAX_REL = 256
DN_HEADS = 4
DN_HEAD_DIM = 128
DN_WIDTH = DN_HEADS * DN_HEAD_DIM
CONV_K = 4
EPS = 1e-6

MAIN_COLS = 3 * ATT_WIDTH + 4 * DN_WIDTH
AB_PAD = 128

LANES = 128
VMEM_LIMIT = 56 * 1024 * 1024

IN_TM = 512
IN_TN = 512
ATT_TQ = 4 * CHUNK
ATT_WIN = ATT_TQ + N_PAST_CHUNKS * CHUNK
DN_C = 128
DN_TB = 2 * DN_C
DN_TAIL = 8
DN_ITEMS = (DN_TB // DN_C) * DN_HEADS
DN_CONV_FIRST = 3 * ATT_WIDTH // IN_TN
DN_CONV_CHUNKS = 3 * DN_WIDTH // IN_TN
DN_STATE_PER_LOCAL = 4
FFN_TM = 512
FFN_TF = 1024
FFN_TN = 256
NEG_BIG = -1e30
LOG2_E = 1.4426950408889634


def _dot(a, b):
    return jnp.dot(a, b, preferred_element_type=F32)


def _dot_nt(a, b):
    return lax.dot_general(a, b, (((1,), (1,)), ((), ())), preferred_element_type=F32)


def _dot_tn(a, b):
    return lax.dot_general(a, b, (((0,), (0,)), ((), ())), preferred_element_type=F32)


def _split3(a):
    p1 = a.astype(BF16)
    r1 = a - p1.astype(F32)
    p2 = r1.astype(BF16)
    p3 = (r1 - p2.astype(F32)).astype(BF16)
    return p1, p2, p3


def _sigmoid(x):
    return 1.0 / (1.0 + jnp.exp2(x * -LOG2_E))


def _softplus(x):
    return jnp.maximum(x, 0.0) + jnp.log(1.0 + jnp.exp(-jnp.abs(x)))


def _inproj_kernel(x_ref, g_ref, wm_ref, wab_ref, qg_ref, kg_ref, seg_ref, cw_ref,
                   main_ref, ab_ref, abt_ref, stage_ref, *, tiles_per_seq):
    i = pl.program_id(0)

    @pl.when(i % tiles_per_seq == 0)
    def _():
        stage_ref[...] = jnp.zeros_like(stage_ref)

    x = x_ref[...]
    ms = jnp.mean(x * x, axis=-1, keepdims=True)
    u = (x * lax.rsqrt(ms + EPS) * g_ref[...]).astype(BF16)
    conv = list(range(DN_CONV_FIRST, DN_CONV_FIRST + DN_CONV_CHUNKS))
    plain = [c for c in range(MAIN_COLS // IN_TN) if c not in conv]
    order = [c for pair in zip(plain, conv) for c in pair] + plain[len(conv):]
    project = lambda c: _dot(u, wm_ref[:, c * IN_TN:(c + 1) * IN_TN])
    y_next = project(order[0])
    for n, c in enumerate(order):
        cols = slice(c * IN_TN, (c + 1) * IN_TN)
        y = y_next
        if n + 1 < len(order):
            y_next = project(order[n + 1])
        if c < 2:
            gain = qg_ref if c == 0 else kg_ref
            m = _dot((y * y).astype(BF16), seg_ref[...])
            y = y * lax.rsqrt(m + EPS) * gain[...]
        elif DN_CONV_FIRST <= c < DN_CONV_FIRST + DN_CONV_CHUNKS:
            j = c - DN_CONV_FIRST
            wcols = slice(j * IN_TN, (j + 1) * IN_TN)
            ext = jnp.concatenate([stage_ref[j], y], axis=0)
            z = cw_ref[0:1, wcols] * ext
            for t in range(1, CONV_K):
                z = cw_ref[t:t + 1, wcols] * ext + pltpu.roll(z, 1, axis=0)
            z = z[DN_TAIL:]
            stage_ref[j] = y[IN_TM - DN_TAIL:IN_TM, :]
            y = z * _sigmoid(z)
        main_ref[:, cols] = y.astype(BF16)
    ab = _dot(u, wab_ref[...])
    ab_ref[...] = ab
    abt_ref[...] = ab.T


def _inproj(xf, gain, w_main, w_ab, qg, kg, seg, conv_w, seq):
    t, d = xf.shape
    const = lambda i: (0, 0)
    return pl.pallas_call(
        functools.partial(_inproj_kernel, tiles_per_seq=seq // IN_TM),
        grid=(t // IN_TM,),
        in_specs=[
            pl.BlockSpec((IN_TM, d), lambda i: (i, 0)),
            pl.BlockSpec((1, d), const),
            pl.BlockSpec((d, MAIN_COLS), const),
            pl.BlockSpec((d, AB_PAD), const),
            pl.BlockSpec((1, ATT_WIDTH), const),
            pl.BlockSpec((1, ATT_WIDTH), const),
            pl.BlockSpec((ATT_WIDTH, ATT_WIDTH), const),
            pl.BlockSpec((CONV_K, 3 * DN_WIDTH), const),
        ],
        out_specs=[
            pl.BlockSpec((IN_TM, MAIN_COLS), lambda i: (i, 0)),
            pl.BlockSpec((IN_TM, AB_PAD), lambda i: (i, 0)),
            pl.BlockSpec((AB_PAD, IN_TM), lambda i: (0, i)),
        ],
        out_shape=[
            jax.ShapeDtypeStruct((t, MAIN_COLS), BF16),
            jax.ShapeDtypeStruct((t, AB_PAD), F32),
            jax.ShapeDtypeStruct((AB_PAD, t), F32),
        ],
        scratch_shapes=[pltpu.VMEM((DN_CONV_CHUNKS, DN_TAIL, IN_TN), F32)],
        compiler_params=pltpu.CompilerParams(
            dimension_semantics=("arbitrary",), vmem_limit_bytes=VMEM_LIMIT),
        name="inproj",
    )(xf, gain, w_main, w_ab, qg, kg, seg, conv_w)


def _attn_tile(q_ref, row0, k_refs, v_refs, bias_ref, start_mask, og_ref, o_ref):
    lane = lax.broadcasted_iota(jnp.int32, (1, LANES), 1)
    per_tile = LANES // ATT_HEAD_DIM
    rows = slice(row0, row0 + ATT_TQ)

    def mine(h):
        e = h % per_tile
        return (lane >= e * ATT_HEAD_DIM) & (lane < (e + 1) * ATT_HEAD_DIM)

    def window(refs, h):
        sl = slice(h // per_tile * LANES, (h // per_tile + 1) * LANES)
        return jnp.concatenate([r[:, sl] for r in refs], axis=0)

    def scores(h):
        qp = q_ref[rows, h // per_tile * LANES:(h // per_tile + 1) * LANES]
        q_e = jnp.where(mine(h), qp, jnp.zeros_like(qp))
        bias = bias_ref[h] if start_mask is None else bias_ref[h] + start_mask
        return _dot_nt(q_e, window(k_refs, h)) + bias

    def attend(h, s):
        mx = jnp.max(s, axis=-1, keepdims=True)
        pe = jnp.exp2(s - mx).astype(BF16)
        vw = window(v_refs, h)
        o_e = _dot(pe, jnp.where(mine(h), vw, jnp.ones_like(vw)))
        return o_e / pltpu.roll(o_e, ATT_HEAD_DIM, axis=1)

    normed = []
    s_next = scores(0)
    for h in range(ATT_HEADS):
        s_cur = s_next
        if h + 1 < ATT_HEADS:
            s_next = scores(h + 1)
        normed.append(attend(h, s_cur))
        yield
    outs = [jnp.where(lane < ATT_HEAD_DIM, normed[t * per_tile], normed[t * per_tile + 1])
            for t in range(ATT_WIDTH // LANES)]
    o = jnp.concatenate(outs, axis=1)
    ms = jnp.mean(o * o, axis=-1, keepdims=True)
    o_ref[rows, :] = (o * lax.rsqrt(ms + EPS) * og_ref[...]).astype(BF16)


def _attn_kernel(q_ref, k0_ref, k1_ref, k2_ref, v0_ref, v1_ref, v2_ref, sig_ref, og_ref, o_ref, bias_ref):
    i = pl.program_id(1)

    @pl.when((pl.program_id(0) == 0) & (i == 0))
    def _():
        r = lax.broadcasted_iota(jnp.int32, (ATT_TQ, ATT_WIN), 0)
        s = lax.broadcasted_iota(jnp.int32, (ATT_TQ, ATT_WIN), 1)
        back = (s >> 6) - (r >> 6)
        in_band = (back >= 0) & (back <= N_PAST_CHUNKS)
        for h in range(ATT_HEADS):
            rows = jnp.broadcast_to(sig_ref[h:h + 1, :], (ATT_TQ, sig_ref.shape[1]))
            toep = pltpu.roll(rows, 0, 1, stride=1, stride_axis=0)
            bias_ref[h] = jnp.where(in_band, toep[:, 0:ATT_WIN], NEG_BIG)

    def tile(start_mask):
        _run(_attn_tile(q_ref, 0, (k0_ref, k1_ref, k2_ref), (v0_ref, v1_ref, v2_ref), bias_ref,
                        start_mask, og_ref, o_ref))

    tiles_in_past = N_PAST_CHUNKS * CHUNK // ATT_TQ

    @pl.when(i < tiles_in_past)
    def _():
        jchunk = lax.broadcasted_iota(jnp.int32, (1, ATT_WIN), 1) >> 6
        tile(jnp.where(jchunk + (ATT_TQ // CHUNK) * i - N_PAST_CHUNKS >= 0, 0.0, NEG_BIG))

    @pl.when(i >= tiles_in_past)
    def _():
        tile(None)


def _attention(main, signal, og, batch, seq):
    nb = seq // ATT_TQ
    blk = (ATT_TQ, ATT_WIDTH)

    def kv_spec(col, back):
        return pl.BlockSpec(blk, lambda b, i: (b * nb + jnp.maximum(i - back, 0), col))

    return pl.pallas_call(
        _attn_kernel,
        grid=(batch, nb),
        in_specs=[
            pl.BlockSpec(blk, lambda b, i: (b * nb + i, 0)),
            kv_spec(1, 2), kv_spec(1, 1), kv_spec(1, 0),
            kv_spec(2, 2), kv_spec(2, 1), kv_spec(2, 0),
            pl.BlockSpec(signal.shape, lambda b, i: (0, 0)),
            pl.BlockSpec((1, ATT_WIDTH), lambda b, i: (0, 0)),
        ],
        out_specs=pl.BlockSpec(blk, lambda b, i: (b * nb + i, 0)),
        out_shape=jax.ShapeDtypeStruct((batch * seq, ATT_WIDTH), BF16),
        scratch_shapes=[pltpu.VMEM((ATT_HEADS, ATT_TQ, ATT_WIN), F32)],
        compiler_params=pltpu.CompilerParams(
            dimension_semantics=("arbitrary", "arbitrary"), vmem_limit_bytes=VMEM_LIMIT),
        name="band_attention",
    )(main, main, main, main, main, main, main, signal, og)


def _bias_signal(rel_bias):
    nh = rel_bias.shape[0]
    lo = N_PAST_CHUNKS * CHUNK - (ATT_WIN - 1) + MAX_REL
    n_dist = ATT_TQ + ATT_WIN - 1
    n_tail = n_dist - (2 * MAX_REL + 1 - lo)
    rb = rel_bias.astype(F32) * LOG2_E
    by_dist = jnp.concatenate([rb[:, lo:], jnp.broadcast_to(rb[:, -1:], (nh, n_tail))], axis=1)
    period = ATT_TQ + ATT_WIN
    return jnp.roll(jnp.pad(by_dist[:, ::-1], ((0, 0), (0, period - n_dist))), -(ATT_TQ - 1), axis=1)


def _weave(main, side, every):
    n = 0
    done = object()
    while True:
        if n % every == 0:
            next(side, done)
        if next(main, done) is done:
            break
        n += 1
        yield
    _run(side)


def _chain(*stages):
    for g in stages:
        yield from g


def _run(stage):
    for _ in stage:
        pass


def _inv_unit_lower(mats, ri, ci, out):
    eye = jnp.where(ri == ci, 1.0, 0.0)
    diag8 = (ri >> 3) == (ci >> 3)
    n1 = [jnp.where(diag8, -a, 0.0) for a in mats]
    n1b = [n.astype(BF16) for n in n1]
    n2b = [_dot(n, n).astype(BF16) for n in n1b]
    yield
    n4b = [_dot(n, n).astype(BF16) for n in n2b]
    ts = [eye + n for n in n1]
    yield
    ts = [t + _dot(t.astype(BF16), n) for t, n in zip(ts, n2b)]
    yield
    ts = [t + _dot(t.astype(BF16), n) for t, n in zip(ts, n4b)]
    yield
    half = DN_C // 2
    r_h = lax.broadcasted_iota(jnp.int32, (half, DN_C), 0)
    c_o = lax.broadcasted_iota(jnp.int32, (half, DN_C), 1)
    for sh in range(3, 7):
        blk = 1 << sh
        odd = [slice(s0, s0 + blk) for s0 in range(blk, DN_C, 2 * blk)]
        even = [slice(s0, s0 + blk) for s0 in range(0, DN_C, 2 * blk)]
        take = lambda m, parts: jnp.concatenate([m[p] for p in parts], axis=0)
        r_o = r_h + (((r_h >> sh) + 1) << sh)
        pair =((r_o >> (sh + 1)) == (c_o >> (sh + 1))) & ((r_o >> sh) != (c_o >> sh))
        tb = [t.astype(BF16) for t in ts]
        lt = [_dot(jnp.where(pair, take(a, odd), 0.0).astype(BF16), t) for a, t in zip(mats, tb)]
        yield
        zero = jnp.zeros((blk, DN_C), F32)
        lt = [jnp.concatenate([p for m in range(half // blk) for p in (zero, x[m * blk:(m + 1) * blk])],
                              axis=0).astype(BF16) for x in lt]
        t_odd = [take(t, odd) for t in ts]
        new = [t - _dot(t.astype(BF16), x) for t, x in zip(t_odd, lt)]
        ts = [jnp.concatenate([p for m in range(half // blk)
                               for p in (t[even[m]], n[m * blk:(m + 1) * blk])], axis=0)
              for t, n in zip(ts, new)]
        yield
    out.extend(ts)


def _gdn_local(x_ref, ab_ref, abt_ref, ltri_ref, utri_ref, alog_ref, dtb_ref, alogt_ref, dtbt_ref, slot,
               a_scr, att_scr, rhs_scr, qd_scr, kt_scr, edl_scr):
    ab = ab_ref[...]
    g_col = -jnp.exp(alog_ref[...]) * _softplus(ab + dtb_ref[...])
    beta_col = _sigmoid(ab)
    g_row = -jnp.exp(alogt_ref[...]) * _softplus(abt_ref[0:8, :] + dtbt_ref[...])
    gc1, gc2, gc3 = _split3(g_col)
    gr1, gr2, gr3 = _split3(g_row)

    y = x_ref[...].astype(F32)

    ones = jnp.ones((DN_HEAD_DIM, DN_HEAD_DIM), BF16)
    sq = [(t * t).astype(BF16) for t in (y[:, g * DN_HEAD_DIM:(g + 1) * DN_HEAD_DIM]
                                         for g in range(2 * DN_HEADS))]
    ssq = [_dot(t, ones) for t in sq]
    yield
    ltri = ltri_ref[...]
    utri = utri_ref[...]
    d_col = _dot(ltri, gc1) + (_dot(ltri, gc2) + _dot(ltri, gc3))
    d_row = _dot(gr1, utri) + (_dot(gr2, utri) + _dot(gr3, utri))
    qk = []
    for g, tot in enumerate(ssq):
        inv = lax.rsqrt(tot + EPS)
        qk.append(y[:, g * DN_HEAD_DIM:(g + 1) * DN_HEAD_DIM]
                  * (inv * DN_HEAD_DIM ** -0.5 if g < DN_HEADS else inv))

    ri = lax.broadcasted_iota(jnp.int32, (DN_C, DN_C), 0)
    ci = lax.broadcasted_iota(jnp.int32, (DN_C, DN_C), 1)
    lmasks, kbqs, kbfs = [], [], []
    for n in range(DN_ITEMS):
        c, h = divmod(n, DN_HEADS)
        rows = slice(c * DN_C, (c + 1) * DN_C)
        q = qk[h][rows]
        k = qk[DN_HEADS + h][rows]
        v = y[rows, 2 * DN_WIDTH + h * DN_HEAD_DIM:2 * DN_WIDTH + (h + 1) * DN_HEAD_DIM]
        dc = d_col[rows, h:h + 1]
        dr = d_row[h:h + 1, rows]
        dl = dr[:, DN_C - 1:DN_C]
        bc = beta_col[rows, DN_HEADS + h:DN_HEADS + h + 1]
        kb = k * bc
        edc = jnp.exp(dc)
        rhs_scr[slot, n] = jnp.concatenate([v * bc, kb * edc], axis=1).astype(BF16)
        qd_scr[slot, n] = (q * edc).astype(BF16)
        kt_scr[slot, n] = (k * jnp.exp(dl - dc)).astype(BF16)
        edl_scr[slot, n] = jnp.broadcast_to(jnp.exp(dl), (8, DN_HEAD_DIM))
        lmasks.append(jnp.exp(jnp.where(ci <= ri, dc - dr, -jnp.inf)))
        kbqs.append(jnp.concatenate([kb, q], axis=0).astype(BF16))
        kbfs.append(k.astype(BF16))
    yield
    ps =[_dot_nt(kbq, kbf) for kbq, kbf in zip(kbqs, kbfs)]
    for n, (p, lmask) in enumerate(zip(ps, lmasks)):
        a_scr[slot, n] = jnp.where(ci < ri, p[0:DN_C] * lmask, 0.0)
        att_scr[slot, n] = (p[DN_C:2 * DN_C] * lmask).astype(BF16)


def _gdn_state(slot, gate_ref, og_ref, o_ref, out_row0, s_ref, a_scr, att_scr, rhs_scr, qd_scr, kt_scr,
               edl_scr):
    ri = lax.broadcasted_iota(jnp.int32, (DN_C, DN_C), 0)
    ci = lax.broadcasted_iota(jnp.int32, (DN_C, DN_C), 1)
    og = og_ref[...]
    tinvs = []
    yield from _inv_unit_lower([a_scr[slot, n] for n in range(DN_ITEMS)], ri, ci, tinvs)
    ws = [_dot(t.astype(BF16), rhs_scr[slot, n]) for n, t in enumerate(tinvs)]
    yield
    for c in range(DN_TB // DN_C):
        idx = [c * DN_HEADS + h for h in range(DN_HEADS)]
        s_old = [s_ref[h] for h in range(DN_HEADS)]
        rs = [_dot(jnp.concatenate([ws[n][:, DN_HEAD_DIM:].astype(BF16), qd_scr[slot, n]], axis=0),
                   s.astype(BF16)) for n, s in zip(idx, s_old)]
        yield
        vnb = [(ws[n][:, 0:DN_HEAD_DIM] - r[0:DN_C]).astype(BF16) for n, r in zip(idx, rs)]
        outs = [r[DN_C:2 * DN_C] + _dot(att_scr[slot, n], vn) for n, r, vn in zip(idx, rs, vnb)]
        yield
        for h, (n, vn) in enumerate(zip(idx, vnb)):
            s_ref[h] = s_old[h] * edl_scr[slot, n, 0:1, :] + _dot_tn(kt_scr[slot, n], vn)
        yield
        rows = slice(out_row0 + c * DN_C, out_row0 + (c + 1) * DN_C)
        for h, o in enumerate(outs):
            hs = slice(h * DN_HEAD_DIM, (h + 1) * DN_HEAD_DIM)
            ms = jnp.mean(o * o, axis=-1, keepdims=True)
            gt = gate_ref[rows, hs].astype(F32)
            o_ref[rows, hs] = (o * lax.rsqrt(ms + EPS) * og * (gt * _sigmoid(gt))).astype(BF16)


def _ffn_stages(x_ref, oa_ref, od_ref, wo_ref, fg_ref, w1_ref, w2_ref, o_ref):
    d = x_ref.shape[1]
    oa = oa_ref[...]
    od = od_ref[...]
    mixed = []
    for j in range(d // FFN_TN):
        cols = slice(j * FFN_TN, (j + 1) * FFN_TN)
        mixed.append(_dot(oa, wo_ref[0:ATT_WIDTH, cols]) + _dot(od, wo_ref[ATT_WIDTH:, cols]))
        yield
    h1 = x_ref[...] + jnp.concatenate(mixed, axis=1)
    ms = jnp.mean(h1 * h1, axis=-1, keepdims=True)
    n = (h1 * lax.rsqrt(ms + EPS) * fg_ref[...]).astype(BF16)

    def up(c, out):
        for j in range(FFN_TF // FFN_TN):
            col0 = c * FFN_TF + j * FFN_TN
            out.append(_dot(n, w1_ref[:, col0:col0 + FFN_TN]))
            yield

    n_chunks = w1_ref.shape[1] // FFN_TF
    acc = h1
    z_next = []
    yield from up(0, z_next)
    for c in range(n_chunks):
        z = z_next
        if c + 1 < n_chunks:
            z_next = []
            yield from up(c + 1, z_next)
        act = jnp.concatenate([jnp.square(jnp.maximum(piece, 0.0)).astype(BF16) for piece in z], axis=1)
        down = []
        for j in range(d // FFN_TN):
            down.append(_dot(act, w2_ref[c * FFN_TF:(c + 1) * FFN_TF, j * FFN_TN:(j + 1) * FFN_TN]))
            yield
        acc = acc + jnp.concatenate(down, axis=1)
    o_ref[...] = acc


def _gdn_ffn_kernel(x0_ref, x1_ref, x2_ref, ab0_ref, ab1_ref, ab2_ref, abt0_ref, abt1_ref, abt2_ref,
                    gate_ref, ltri_ref, utri_ref, alog_ref, dtb_ref, alogt_ref, dtbt_ref, dn_og_ref,
                    xres_ref, oa_ref, wo_ref, fg_ref, w1_ref, w2_ref, o_ref, s_ref, od_scr, *local_scr,
                    steps_per_seq):
    s = pl.program_id(0)
    cur = s % 2
    consts = (ltri_ref, utri_ref, alog_ref, dtb_ref, alogt_ref, dtbt_ref)

    def local(x_ref, ab_ref, abt_ref, slot):
        return _gdn_local(x_ref, ab_ref, abt_ref, *consts, slot, *local_scr)

    def state(slot, out_row0):
        return _gdn_state(slot, gate_ref, dn_og_ref, od_scr.at[cur], out_row0, s_ref, *local_scr)

    @pl.when(s == 0)
    def _():
        od_scr[1] = jnp.zeros(od_scr.shape[1:], od_scr.dtype)

    @pl.when(s % steps_per_seq == 0)
    def _():
        s_ref[...] = jnp.zeros_like(s_ref)
        _run(local(x0_ref, ab0_ref, abt0_ref, 0))

    deltanet = _chain(
        _weave(state(0, 0), local(x1_ref, ab1_ref, abt1_ref, 1), DN_STATE_PER_LOCAL),
        _weave(state(1, DN_TB), local(x2_ref, ab2_ref, abt2_ref, 0), DN_STATE_PER_LOCAL))
    ffn = _ffn_stages(xres_ref, oa_ref, od_scr.at[1 - cur], wo_ref, fg_ref, w1_ref, w2_ref, o_ref)
    _run(_weave(deltanet, ffn, 1))


def _deltanet_ffn(xf, main, ab, abt, o_att, alog, dtb, alogt, dtbt, dn_og, w_out, fg, w1, w2, seq):
    t, d = xf.shape
    dff = w1.shape[1]
    assert FFN_TM == 2 * DN_TB
    nb = seq // DN_TB
    spq = nb // 2
    steps = t // FFN_TM
    tile = lambda s: jnp.minimum(s, steps - 1)
    prev = lambda s: jnp.maximum(s - 1, 0)
    seq_blk0 = lambda s: tile(s) // spq * nb
    first = seq_blk0
    odd = lambda s: 2 * tile(s) + 1
    nxt = lambda s: jnp.minimum(2 * tile(s) + 2, seq_blk0(s) + nb - 1)
    const = lambda s: (0, 0)

    def x_spec(blk):
        return pl.BlockSpec((DN_TB, 3 * DN_WIDTH), lambda s: (blk(s), 1))

    def ab_spec(blk):
        return pl.BlockSpec((DN_TB, AB_PAD), lambda s: (blk(s), 0))

    def abt_spec(blk):
        return pl.BlockSpec((AB_PAD, DN_TB), lambda s: (0, blk(s)))

    r = np.arange(DN_TB)
    same = (r[:, None] // DN_C) == (r[None, :] // DN_C)
    ltri = jnp.asarray(same & (r[None, :] <= r[:, None]), BF16)
    utri = jnp.asarray(same & (r[:, None] <= r[None, :]), BF16)
    item = lambda *shape: (2, DN_ITEMS) + shape
    resident = dict(pipeline_mode=pl.Buffered(1))
    return pl.pallas_call(
        functools.partial(_gdn_ffn_kernel, steps_per_seq=spq),
        grid=(steps + 1,),
        in_specs=[
            x_spec(first), x_spec(odd), x_spec(nxt),
            ab_spec(first), ab_spec(odd), ab_spec(nxt),
            abt_spec(first), abt_spec(odd), abt_spec(nxt),
            pl.BlockSpec((2 * DN_TB, DN_WIDTH), lambda s: (tile(s), 6)),
            pl.BlockSpec((DN_TB, DN_TB), const),
            pl.BlockSpec((DN_TB, DN_TB), const),
            pl.BlockSpec((1, AB_PAD), const),
            pl.BlockSpec((1, AB_PAD), const),
            pl.BlockSpec((8, 1), const),
            pl.BlockSpec((8, 1), const),
            pl.BlockSpec((1, DN_HEAD_DIM), const),
            pl.BlockSpec((FFN_TM, d), lambda s: (prev(s), 0)),
            pl.BlockSpec((FFN_TM, ATT_WIDTH), lambda s: (prev(s), 0)),
            pl.BlockSpec((d, d), const, **resident),
            pl.BlockSpec((1, d), const),
            pl.BlockSpec((d, dff), const, **resident),
            pl.BlockSpec((dff, d), const, **resident),
        ],
        out_specs=pl.BlockSpec((FFN_TM, d), lambda s: (prev(s), 0)),
        out_shape=jax.ShapeDtypeStruct((t, d), F32),
        scratch_shapes=[
            pltpu.VMEM((DN_HEADS, DN_HEAD_DIM, DN_HEAD_DIM), F32),
            pltpu.VMEM((2, FFN_TM, DN_WIDTH), BF16),
            pltpu.VMEM(item(DN_C, DN_C), F32),
            pltpu.VMEM(item(DN_C, DN_C), BF16),
            pltpu.VMEM(item(DN_C, 2 * DN_HEAD_DIM), BF16),
            pltpu.VMEM(item(DN_C, DN_HEAD_DIM), BF16),
            pltpu.VMEM(item(DN_C, DN_HEAD_DIM), BF16),
            pltpu.VMEM(item(8, DN_HEAD_DIM), F32),
        ],
        compiler_params=pltpu.CompilerParams(
            dimension_semantics=("arbitrary",), vmem_limit_bytes=VMEM_LIMIT),
        name="deltanet_ffn",
    )(main, main, main, ab, ab, ab, abt, abt, abt, main, ltri, utri, alog, dtb, alogt, dtbt, dn_og,
      xf, o_att, w_out, fg, w1, w2)


def _layer(h, p, batch, seq):
    w_in = p["w_in"]
    w_main = w_in.astype(BF16)
    w_ab = jnp.pad(w_in[:, MAIN_COLS:], ((0, 0), (0, AB_PAD - 2 * DN_HEADS))).astype(BF16)
    head_of = np.arange(ATT_WIDTH) // ATT_HEAD_DIM
    seg = jnp.asarray((head_of[:, None] == head_of[None, :]) / ATT_HEAD_DIM, BF16)
    qg = (jnp.tile(p["att_q_gain"].astype(F32), ATT_HEADS) * (ATT_HEAD_DIM ** -0.5 * LOG2_E))[None]
    kg = jnp.tile(p["att_k_gain"].astype(F32), ATT_HEADS)[None]
    main, ab, abt = _inproj(h, p["mix_norm_gain"].astype(F32)[None], w_main, w_ab, qg, kg, seg,
                            p["dn_conv_w"].astype(F32), seq)

    o_att = _attention(main, _bias_signal(p["rel_bias"]),
                       p["att_out_gain"].astype(F32)[None], batch, seq)

    pad_row = lambda v: jnp.pad(v.astype(F32), (0, AB_PAD - DN_HEADS))[None]
    pad_col = lambda v: jnp.pad(v.astype(F32), (0, 8 - DN_HEADS))[:, None]
    return _deltanet_ffn(h, main, ab, abt, o_att, pad_row(p["dn_a_log"]), pad_row(p["dn_dt_bias"]),
                         pad_col(p["dn_a_log"]), pad_col(p["dn_dt_bias"]),
                         p["dn_out_gain"].astype(F32)[None], p["w_out"].astype(BF16),
                         p["ffn_norm_gain"].astype(F32)[None], p["w_ff1"].astype(BF16),
                         p["w_ff2"].astype(BF16), seq)


def kernel(x, mix_norm_gain, w_in, att_q_gain, att_k_gain, rel_bias, att_out_gain, dn_conv_w,
           dn_a_log, dn_dt_bias, dn_out_gain, w_out, ffn_norm_gain, w_ff1, w_ff2):
    batch, seq, d = x.shape
    assert seq % max(ATT_TQ, 2 * DN_TB) == 0 and (batch * seq) % max(IN_TM, FFN_TM) == 0
    params = dict(mix_norm_gain=mix_norm_gain, w_in=w_in, att_q_gain=att_q_gain,
                  att_k_gain=att_k_gain, rel_bias=rel_bias, att_out_gain=att_out_gain,
                  dn_conv_w=dn_conv_w, dn_a_log=dn_a_log, dn_dt_bias=dn_dt_bias,
                  dn_out_gain=dn_out_gain, w_out=w_out, ffn_norm_gain=ffn_norm_gain,
                  w_ff1=w_ff1, w_ff2=w_ff2)
    h = x.reshape(batch * seq, d)
    for layer in range(w_in.shape[0]):
        h = _layer(h, {name: v[layer] for name, v in params.items()}, batch, seq)
    return h.reshape(batch, seq, d)
```

```python
import functools

import numpy as np
import jax
import jax.numpy as jnp
from jax import lax
from jax.experimental import pallas as pl
from jax.experimental.pallas import tpu as pltpu

F32 = jnp.float32
BF16 = jnp.bfloat16

CHUNK = 64
N_PAST_CHUNKS = 8
ATT_HEADS = 8
ATT_HEAD_DIM = 64
ATT_WIDTH = ATT_HEADS * ATT_HEAD_DIM
MAX_REL = 256
DN_HEADS = 4
DN_HEAD_DIM = 128
DN_WIDTH = DN_HEADS * DN_HEAD_DIM
CONV_K = 4
EPS = 1e-6

MAIN_COLS = 3 * ATT_WIDTH + 4 * DN_WIDTH
AB_PAD = 128

LANES = 128
VMEM_LIMIT = 56 * 1024 * 1024

IN_TM = 512
IN_TN = 512
ATT_TQ = 4 * CHUNK
ATT_WIN = ATT_TQ + N_PAST_CHUNKS * CHUNK
DN_C = 128
DN_TB = 2 * DN_C
DN_TAIL = 8
DN_ITEMS = (DN_TB // DN_C) * DN_HEADS
DN_CONV_FIRST = 3 * ATT_WIDTH // IN_TN
DN_CONV_CHUNKS = 3 * DN_WIDTH // IN_TN
DN_STATE_PER_LOCAL = 4
FFN_TM = 512
FFN_TF = 1024
FFN_TN = 256
NEG_BIG = -1e30
LOG2_E = 1.4426950408889634


def _dot(a, b):
    return jnp.dot(a, b, preferred_element_type=F32)


def _dot_nt(a, b):
    return lax.dot_general(a, b, (((1,), (1,)), ((), ())), preferred_element_type=F32)


def _dot_tn(a, b):
    return lax.dot_general(a, b, (((0,), (0,)), ((), ())), preferred_element_type=F32)


def _split3(a):
    p1 = a.astype(BF16)
    r1 = a - p1.astype(F32)
    p2 = r1.astype(BF16)
    p3 = (r1 - p2.astype(F32)).astype(BF16)
    return p1, p2, p3


def _sigmoid(x):
    return 1.0 / (1.0 + jnp.exp2(x * -LOG2_E))


def _softplus(x):
    return jnp.maximum(x, 0.0) + jnp.log(1.0 + jnp.exp(-jnp.abs(x)))


def _inproj_kernel(x_ref, g_ref, wm_ref, wab_ref, qg_ref, kg_ref, seg_ref, cw_ref, *rest, tiles_per_seq):
    n_cast = (len(rest) - 4) // 2
    cast_in, (main_ref, ab_ref, abt_ref) = rest[:n_cast], rest[n_cast:n_cast + 3]
    cast_out, stage_ref = rest[n_cast + 3:2 * n_cast + 3], rest[2 * n_cast + 3]
    i = pl.program_id(0)
    for src, dst in zip(cast_in, cast_out):
        dst[...] = src[...].astype(BF16)

    @pl.when(i % tiles_per_seq == 0)
    def _():
        stage_ref[...] = jnp.zeros_like(stage_ref)

    x = x_ref[...]
    ms = jnp.mean(x * x, axis=-1, keepdims=True)
    u = (x * lax.rsqrt(ms + EPS) * g_ref[...]).astype(BF16)
    conv = list(range(DN_CONV_FIRST, DN_CONV_FIRST + DN_CONV_CHUNKS))
    plain = [c for c in range(MAIN_COLS // IN_TN) if c not in conv]
    order = [c for pair in zip(plain, conv) for c in pair] + plain[len(conv):]
    project = lambda c: _dot(u, wm_ref[:, c * IN_TN:(c + 1) * IN_TN])
    y_next = project(order[0])
    for n, c in enumerate(order):
        cols = slice(c * IN_TN, (c + 1) * IN_TN)
        y = y_next
        if n + 1 < len(order):
            y_next = project(order[n + 1])
        if c < 2:
            gain = qg_ref if c == 0 else kg_ref
            m = _dot((y * y).astype(BF16), seg_ref[...])
            y = y * lax.rsqrt(m + EPS) * gain[...]
        elif DN_CONV_FIRST <= c < DN_CONV_FIRST + DN_CONV_CHUNKS:
            j = c - DN_CONV_FIRST
            wcols = slice(j * IN_TN, (j + 1) * IN_TN)
            ext = jnp.concatenate([stage_ref[j], y], axis=0)
            z = cw_ref[0:1, wcols] * ext
            for t in range(1, CONV_K):
                z = cw_ref[t:t + 1, wcols] * ext + pltpu.roll(z, 1, axis=0)
            z = z[DN_TAIL:]
            stage_ref[j] = y[IN_TM - DN_TAIL:IN_TM, :]
            y = z * _sigmoid(z)
        main_ref[:, cols] = y.astype(BF16)
    ab = _dot(u, wab_ref[...])
    ab_ref[...] = ab
    abt_ref[...] = ab.T


def _inproj(xf, gain, w_main, w_ab, qg, kg, seg, conv_w, seq, to_cast):
    t, d = xf.shape
    steps = t // IN_TM
    const = lambda i: (0, 0)
    slab = lambda w: pl.BlockSpec((w.shape[0] // steps, w.shape[1]), lambda i: (i, 0))
    outs = pl.pallas_call(
        functools.partial(_inproj_kernel, tiles_per_seq=seq // IN_TM),
        grid=(steps,),
        in_specs=[
            pl.BlockSpec((IN_TM, d), lambda i: (i, 0)),
            pl.BlockSpec((1, d), const),
            pl.BlockSpec((d, MAIN_COLS), const),
            pl.BlockSpec((d, AB_PAD), const),
            pl.BlockSpec((1, ATT_WIDTH), const),
            pl.BlockSpec((1, ATT_WIDTH), const),
            pl.BlockSpec((ATT_WIDTH, ATT_WIDTH), const),
            pl.BlockSpec((CONV_K, 3 * DN_WIDTH), const),
        ] + [slab(w) for w in to_cast],
        out_specs=[
            pl.BlockSpec((IN_TM, MAIN_COLS), lambda i: (i, 0)),
            pl.BlockSpec((IN_TM, AB_PAD), lambda i: (i, 0)),
            pl.BlockSpec((AB_PAD, IN_TM), lambda i: (0, i)),
        ] + [slab(w) for w in to_cast],
        out_shape=[
            jax.ShapeDtypeStruct((t, MAIN_COLS), BF16),
            jax.ShapeDtypeStruct((t, AB_PAD), F32),
            jax.ShapeDtypeStruct((AB_PAD, t), F32),
        ] + [jax.ShapeDtypeStruct(w.shape, BF16) for w in to_cast],
        scratch_shapes=[pltpu.VMEM((DN_CONV_CHUNKS, DN_TAIL, IN_TN), F32)],
        compiler_params=pltpu.CompilerParams(
            dimension_semantics=("arbitrary",), vmem_limit_bytes=VMEM_LIMIT),
        name="inproj",
    )(xf, gain, w_main, w_ab, qg, kg, seg, conv_w, *to_cast)
    return outs[:3], outs[3:]


def _attn_tile(q_ref, row0, k_refs, v_refs, bias_ref, start_mask, og_ref, o_ref):
    lane = lax.broadcasted_iota(jnp.int32, (1, LANES), 1)
    per_tile = LANES // ATT_HEAD_DIM
    rows = slice(row0, row0 + ATT_TQ)

    def mine(h):
        e = h % per_tile
        return (lane >= e * ATT_HEAD_DIM) & (lane < (e + 1) * ATT_HEAD_DIM)

    def window(refs, h):
        sl = slice(h // per_tile * LANES, (h // per_tile + 1) * LANES)
        return jnp.concatenate([r[:, sl] for r in refs], axis=0)

    def scores(h):
        qp = q_ref[rows, h // per_tile * LANES:(h // per_tile + 1) * LANES]
        q_e = jnp.where(mine(h), qp, jnp.zeros_like(qp))
        bias = bias_ref[h] if start_mask is None else bias_ref[h] + start_mask
        return _dot_nt(q_e, window(k_refs, h)) + bias

    def attend(h, s):
        mx = jnp.max(s, axis=-1, keepdims=True)
        pe = jnp.exp2(s - mx).astype(BF16)
        vw = window(v_refs, h)
        o_e = _dot(pe, jnp.where(mine(h), vw, jnp.ones_like(vw)))
        return o_e / pltpu.roll(o_e, ATT_HEAD_DIM, axis=1)

    normed = []
    s_next = scores(0)
    for h in range(ATT_HEADS):
        s_cur = s_next
        if h + 1 < ATT_HEADS:
            s_next = scores(h + 1)
        normed.append(attend(h, s_cur))
        yield
    outs = [jnp.where(lane < ATT_HEAD_DIM, normed[t * per_tile], normed[t * per_tile + 1])
            for t in range(ATT_WIDTH // LANES)]
    o = jnp.concatenate(outs, axis=1)
    ms = jnp.mean(o * o, axis=-1, keepdims=True)
    o_ref[rows, :] = (o * lax.rsqrt(ms + EPS) * og_ref[...]).astype(BF16)


def _attn_kernel(q_ref, k0_ref, k1_ref, k2_ref, v0_ref, v1_ref, v2_ref, sig_ref, og_ref, o_ref, bias_ref):
    i = pl.program_id(1)

    @pl.when((pl.program_id(0) == 0) & (i == 0))
    def _():
        r = lax.broadcasted_iota(jnp.int32, (ATT_TQ, ATT_WIN), 0)
        s = lax.broadcasted_iota(jnp.int32, (ATT_TQ, ATT_WIN), 1)
        back = (s >> 6) - (r >> 6)
        in_band = (back >= 0) & (back <= N_PAST_CHUNKS)
        for h in range(ATT_HEADS):
            rows = jnp.broadcast_to(sig_ref[h:h + 1, :], (ATT_TQ, sig_ref.shape[1]))
            toep = pltpu.roll(rows, 0, 1, stride=1, stride_axis=0)
            bias_ref[h] = jnp.where(in_band, toep[:, 0:ATT_WIN], NEG_BIG)

    def tile(start_mask):
        _run(_attn_tile(q_ref, 0, (k0_ref, k1_ref, k2_ref), (v0_ref, v1_ref, v2_ref), bias_ref,
                        start_mask, og_ref, o_ref))

    tiles_in_past = N_PAST_CHUNKS * CHUNK // ATT_TQ

    @pl.when(i < tiles_in_past)
    def _():
        jchunk = lax.broadcasted_iota(jnp.int32, (1, ATT_WIN), 1) >> 6
        tile(jnp.where(jchunk + (ATT_TQ // CHUNK) * i - N_PAST_CHUNKS >= 0, 0.0, NEG_BIG))

    @pl.when(i >= tiles_in_past)
    def _():
        tile(None)


def _attention(main, signal, og, batch, seq):
    nb = seq // ATT_TQ
    blk = (ATT_TQ, ATT_WIDTH)

    def kv_spec(col, back):
        return pl.BlockSpec(blk, lambda b, i: (b * nb + jnp.maximum(i - back, 0), col))

    return pl.pallas_call(
        _attn_kernel,
        grid=(batch, nb),
        in_specs=[
            pl.BlockSpec(blk, lambda b, i: (b * nb + i, 0)),
            kv_spec(1, 2), kv_spec(1, 1), kv_spec(1, 0),
            kv_spec(2, 2), kv_spec(2, 1), kv_spec(2, 0),
            pl.BlockSpec(signal.shape, lambda b, i: (0, 0)),
            pl.BlockSpec((1, ATT_WIDTH), lambda b, i: (0, 0)),
        ],
        out_specs=pl.BlockSpec(blk, lambda b, i: (b * nb + i, 0)),
        out_shape=jax.ShapeDtypeStruct((batch * seq, ATT_WIDTH), BF16),
        scratch_shapes=[pltpu.VMEM((ATT_HEADS, ATT_TQ, ATT_WIN), F32)],
        compiler_params=pltpu.CompilerParams(
            dimension_semantics=("arbitrary", "arbitrary"), vmem_limit_bytes=VMEM_LIMIT),
        name="band_attention",
    )(main, main, main, main, main, main, main, signal, og)


def _bias_signal(rel_bias):
    nh = rel_bias.shape[0]
    lo = N_PAST_CHUNKS * CHUNK - (ATT_WIN - 1) + MAX_REL
    n_dist = ATT_TQ + ATT_WIN - 1
    n_tail = n_dist - (2 * MAX_REL + 1 - lo)
    rb = rel_bias.astype(F32) * LOG2_E
    by_dist = jnp.concatenate([rb[:, lo:], jnp.broadcast_to(rb[:, -1:], (nh, n_tail))], axis=1)
    period = ATT_TQ + ATT_WIN
    return jnp.roll(jnp.pad(by_dist[:, ::-1], ((0, 0), (0, period - n_dist))), -(ATT_TQ - 1), axis=1)


def _weave(main, side, every):
    n = 0
    done = object()
    while True:
        if n % every == 0:
            next(side, done)
        if next(main, done) is done:
            break
        n += 1
        yield
    _run(side)


def _chain(*stages):
    for g in stages:
        yield from g


def _run(stage):
    for _ in stage:
        pass


def _inv_unit_lower(mats, ri, ci, out):
    eye = jnp.where(ri == ci, 1.0, 0.0)
    diag8 = (ri >> 3) == (ci >> 3)
    n1 = [jnp.where(diag8, -a, 0.0) for a in mats]
    n1b = [n.astype(BF16) for n in n1]
    n2b = [_dot(n, n).astype(BF16) for n in n1b]
    yield
    n4b = [_dot(n, n).astype(BF16) for n in n2b]
    ts = [eye + n for n in n1]
    yield
    ts = [t + _dot(t.astype(BF16), n) for t, n in zip(ts, n2b)]
    yield
    ts = [t + _dot(t.astype(BF16), n) for t, n in zip(ts, n4b)]
    yield
    half = DN_C // 2
    r_h = lax.broadcasted_iota(jnp.int32, (half, DN_C), 0)
    c_o = lax.broadcasted_iota(jnp.int32, (half, DN_C), 1)
    for sh in range(3, 7):
        blk = 1 << sh
        odd = [slice(s0, s0 + blk) for s0 in range(blk, DN_C, 2 * blk)]
        even = [slice(s0, s0 + blk) for s0 in range(0, DN_C, 2 * blk)]
        take = lambda m, parts: jnp.concatenate([m[p] for p in parts], axis=0)
        r_o = r_h + (((r_h >> sh) + 1) << sh)
        pair =((r_o >> (sh + 1)) == (c_o >> (sh + 1))) & ((r_o >> sh) != (c_o >> sh))
        tb = [t.astype(BF16) for t in ts]
        lt = [_dot(jnp.where(pair, take(a, odd), 0.0).astype(BF16), t) for a, t in zip(mats, tb)]
        yield
        zero = jnp.zeros((blk, DN_C), F32)
        lt = [jnp.concatenate([p for m in range(half // blk) for p in (zero, x[m * blk:(m + 1) * blk])],
                              axis=0).astype(BF16) for x in lt]
        t_odd = [take(t, odd) for t in ts]
        new = [t - _dot(t.astype(BF16), x) for t, x in zip(t_odd, lt)]
        ts = [jnp.concatenate([p for m in range(half // blk)
                               for p in (t[even[m]], n[m * blk:(m + 1) * blk])], axis=0)
              for t, n in zip(ts, new)]
        yield
    out.extend(ts)


def _gdn_local(x_ref, ab_ref, abt_ref, ltri_ref, utri_ref, alog_ref, dtb_ref, alogt_ref, dtbt_ref, slot,
               a_scr, att_scr, rhs_scr, qd_scr, kt_scr, edl_scr):
    ab = ab_ref[...]
    g_col = -jnp.exp(alog_ref[...]) * _softplus(ab + dtb_ref[...])
    beta_col = _sigmoid(ab)
    g_row = -jnp.exp(alogt_ref[...]) * _softplus(abt_ref[0:8, :] + dtbt_ref[...])
    gc1, gc2, gc3 = _split3(g_col)
    gr1, gr2, gr3 = _split3(g_row)

    y = x_ref[...].astype(F32)

    ones = jnp.ones((DN_HEAD_DIM, DN_HEAD_DIM), BF16)
    sq = [(t * t).astype(BF16) for t in (y[:, g * DN_HEAD_DIM:(g + 1) * DN_HEAD_DIM]
                                         for g in range(2 * DN_HEADS))]
    ssq = [_dot(t, ones) for t in sq]
    yield
    ltri = ltri_ref[...]
    utri = utri_ref[...]
    d_col = _dot(ltri, gc1) + (_dot(ltri, gc2) + _dot(ltri, gc3))
    d_row = _dot(gr1, utri) + (_dot(gr2, utri) + _dot(gr3, utri))
    qk = []
    for g, tot in enumerate(ssq):
        inv = lax.rsqrt(tot + EPS)
        qk.append(y[:, g * DN_HEAD_DIM:(g + 1) * DN_HEAD_DIM]
                  * (inv * DN_HEAD_DIM ** -0.5 if g < DN_HEADS else inv))

    ri = lax.broadcasted_iota(jnp.int32, (DN_C, DN_C), 0)
    ci = lax.broadcasted_iota(jnp.int32, (DN_C, DN_C), 1)
    lmasks, kbqs, kbfs = [], [], []
    for n in range(DN_ITEMS):
        c, h = divmod(n, DN_HEADS)
        rows = slice(c * DN_C, (c + 1) * DN_C)
        q = qk[h][rows]
        k = qk[DN_HEADS + h][rows]
        v = y[rows, 2 * DN_WIDTH + h * DN_HEAD_DIM:2 * DN_WIDTH + (h + 1) * DN_HEAD_DIM]
        dc = d_col[rows, h:h + 1]
        dr = d_row[h:h + 1, rows]
        dl = dr[:, DN_C - 1:DN_C]
        bc = beta_col[rows, DN_HEADS + h:DN_HEADS + h + 1]
        kb = k * bc
        edc = jnp.exp(dc)
        rhs_scr[slot, n] = jnp.concatenate([v * bc, kb * edc], axis=1).astype(BF16)
        qd_scr[slot, n] = (q * edc).astype(BF16)
        kt_scr[slot, n] = (k * jnp.exp(dl - dc)).astype(BF16)
        edl_scr[slot, n] = jnp.broadcast_to(jnp.exp(dl), (8, DN_HEAD_DIM))
        lmasks.append(jnp.exp(jnp.where(ci <= ri, dc - dr, -jnp.inf)))
        kbqs.append(jnp.concatenate([kb, q], axis=0).astype(BF16))
        kbfs.append(k.astype(BF16))
    yield
    ps =[_dot_nt(kbq, kbf) for kbq, kbf in zip(kbqs, kbfs)]
    for n, (p, lmask) in enumerate(zip(ps, lmasks)):
        a_scr[slot, n] = jnp.where(ci < ri, p[0:DN_C] * lmask, 0.0)
        att_scr[slot, n] = (p[DN_C:2 * DN_C] * lmask).astype(BF16)


def _gdn_state(slot, gate_ref, og_ref, o_ref, out_row0, s_ref, a_scr, att_scr, rhs_scr, qd_scr, kt_scr,
               edl_scr):
    ri = lax.broadcasted_iota(jnp.int32, (DN_C, DN_C), 0)
    ci = lax.broadcasted_iota(jnp.int32, (DN_C, DN_C), 1)
    og = og_ref[...]
    tinvs = []
    yield from _inv_unit_lower([a_scr[slot, n] for n in range(DN_ITEMS)], ri, ci, tinvs)
    ws = [_dot(t.astype(BF16), rhs_scr[slot, n]) for n, t in enumerate(tinvs)]
    yield
    for c in range(DN_TB // DN_C):
        idx = [c * DN_HEADS + h for h in range(DN_HEADS)]
        s_old = [s_ref[h] for h in range(DN_HEADS)]
        rs = [_dot(jnp.concatenate([ws[n][:, DN_HEAD_DIM:].astype(BF16), qd_scr[slot, n]], axis=0),
                   s.astype(BF16)) for n, s in zip(idx, s_old)]
        yield
        vnb = [(ws[n][:, 0:DN_HEAD_DIM] - r[0:DN_C]).astype(BF16) for n, r in zip(idx, rs)]
        outs = [r[DN_C:2 * DN_C] + _dot(att_scr[slot, n], vn) for n, r, vn in zip(idx, rs, vnb)]
        yield
        for h, (n, vn) in enumerate(zip(idx, vnb)):
            s_ref[h] = s_old[h] * edl_scr[slot, n, 0:1, :] + _dot_tn(kt_scr[slot, n], vn)
        yield
        rows = slice(out_row0 + c * DN_C, out_row0 + (c + 1) * DN_C)
        for h, o in enumerate(outs):
            hs = slice(h * DN_HEAD_DIM, (h + 1) * DN_HEAD_DIM)
            ms = jnp.mean(o * o, axis=-1, keepdims=True)
            gt = gate_ref[rows, hs].astype(F32)
            o_ref[rows, hs] = (o * lax.rsqrt(ms + EPS) * og * (gt * _sigmoid(gt))).astype(BF16)


def _ffn_stages(x_ref, oa_ref, od_ref, wo_ref, fg_ref, w1_ref, w2_ref, o_ref):
    d = x_ref.shape[1]
    oa = oa_ref[...]
    od = od_ref[...]
    mixed = []
    for j in range(d // FFN_TN):
        cols = slice(j * FFN_TN, (j + 1) * FFN_TN)
        mixed.append(_dot(oa, wo_ref[0:ATT_WIDTH, cols]) + _dot(od, wo_ref[ATT_WIDTH:, cols]))
        yield
    h1 = x_ref[...] + jnp.concatenate(mixed, axis=1)
    ms = jnp.mean(h1 * h1, axis=-1, keepdims=True)
    n = (h1 * lax.rsqrt(ms + EPS) * fg_ref[...]).astype(BF16)

    def up(c, out):
        for j in range(FFN_TF // FFN_TN):
            col0 = c * FFN_TF + j * FFN_TN
            out.append(_dot(n, w1_ref[:, col0:col0 + FFN_TN]))
            yield

    n_chunks = w1_ref.shape[1] // FFN_TF
    acc = h1
    z_next = []
    yield from up(0, z_next)
    for c in range(n_chunks):
        z = z_next
        if c + 1 < n_chunks:
            z_next = []
            yield from up(c + 1, z_next)
        act = jnp.concatenate([jnp.square(jnp.maximum(piece, 0.0)).astype(BF16) for piece in z], axis=1)
        down = []
        for j in range(d // FFN_TN):
            down.append(_dot(act, w2_ref[c * FFN_TF:(c + 1) * FFN_TF, j * FFN_TN:(j + 1) * FFN_TN]))
            yield
        acc = acc + jnp.concatenate(down, axis=1)
    o_ref[...] = acc


def _gdn_ffn_kernel(x0_ref, x1_ref, x2_ref, ab0_ref, ab1_ref, ab2_ref, abt0_ref, abt1_ref, abt2_ref,
                    gate_ref, ltri_ref, utri_ref, alog_ref, dtb_ref, alogt_ref, dtbt_ref, dn_og_ref,
                    xres_ref, oa_ref, wo_ref, fg_ref, w1_ref, w2_ref, o_ref, s_ref, od_scr, *local_scr,
                    steps_per_seq):
    s = pl.program_id(0)
    cur = s % 2
    consts = (ltri_ref, utri_ref, alog_ref, dtb_ref, alogt_ref, dtbt_ref)

    def local(x_ref, ab_ref, abt_ref, slot):
        return _gdn_local(x_ref, ab_ref, abt_ref, *consts, slot, *local_scr)

    def state(slot, out_row0):
        return _gdn_state(slot, gate_ref, dn_og_ref, od_scr.at[cur], out_row0, s_ref, *local_scr)

    @pl.when(s == 0)
    def _():
        od_scr[1] = jnp.zeros(od_scr.shape[1:], od_scr.dtype)

    @pl.when(s % steps_per_seq == 0)
    def _():
        s_ref[...] = jnp.zeros_like(s_ref)
        _run(local(x0_ref, ab0_ref, abt0_ref, 0))

    deltanet = _chain(
        _weave(state(0, 0), local(x1_ref, ab1_ref, abt1_ref, 1), DN_STATE_PER_LOCAL),
        _weave(state(1, DN_TB), local(x2_ref, ab2_ref, abt2_ref, 0), DN_STATE_PER_LOCAL))
    ffn = _ffn_stages(xres_ref, oa_ref, od_scr.at[1 - cur], wo_ref, fg_ref, w1_ref, w2_ref, o_ref)
    _run(_weave(deltanet, ffn, 1))


def _deltanet_ffn(xf, main, ab, abt, o_att, alog, dtb, alogt, dtbt, dn_og, w_out, fg, w1, w2, seq):
    t, d = xf.shape
    dff = w1.shape[1]
    assert FFN_TM == 2 * DN_TB
    nb = seq // DN_TB
    spq = nb // 2
    steps = t // FFN_TM
    tile = lambda s: jnp.minimum(s, steps - 1)
    prev = lambda s: jnp.maximum(s - 1, 0)
    seq_blk0 = lambda s: tile(s) // spq * nb
    first = seq_blk0
    odd = lambda s: 2 * tile(s) + 1
    nxt = lambda s: jnp.minimum(2 * tile(s) + 2, seq_blk0(s) + nb - 1)
    const = lambda s: (0, 0)

    def x_spec(blk):
        return pl.BlockSpec((DN_TB, 3 * DN_WIDTH), lambda s: (blk(s), 1))

    def ab_spec(blk):
        return pl.BlockSpec((DN_TB, AB_PAD), lambda s: (blk(s), 0))

    def abt_spec(blk):
        return pl.BlockSpec((AB_PAD, DN_TB), lambda s: (0, blk(s)))

    r = np.arange(DN_TB)
    same = (r[:, None] // DN_C) == (r[None, :] // DN_C)
    ltri = jnp.asarray(same & (r[None, :] <= r[:, None]), BF16)
    utri = jnp.asarray(same & (r[:, None] <= r[None, :]), BF16)
    item = lambda *shape: (2, DN_ITEMS) + shape
    resident = dict(pipeline_mode=pl.Buffered(1))
    return pl.pallas_call(
        functools.partial(_gdn_ffn_kernel, steps_per_seq=spq),
        grid=(steps + 1,),
        in_specs=[
            x_spec(first), x_spec(odd), x_spec(nxt),
            ab_spec(first), ab_spec(odd), ab_spec(nxt),
            abt_spec(first), abt_spec(odd), abt_spec(nxt),
            pl.BlockSpec((2 * DN_TB, DN_WIDTH), lambda s: (tile(s), 6)),
            pl.BlockSpec((DN_TB, DN_TB), const),
            pl.BlockSpec((DN_TB, DN_TB), const),
            pl.BlockSpec((1, AB_PAD), const),
            pl.BlockSpec((1, AB_PAD), const),
            pl.BlockSpec((8, 1), const),
            pl.BlockSpec((8, 1), const),
            pl.BlockSpec((1, DN_HEAD_DIM), const),
            pl.BlockSpec((FFN_TM, d), lambda s: (prev(s), 0)),
            pl.BlockSpec((FFN_TM, ATT_WIDTH), lambda s: (prev(s), 0)),
            pl.BlockSpec((d, d), const, **resident),
            pl.BlockSpec((1, d), const),
            pl.BlockSpec((d, dff), const, **resident),
            pl.BlockSpec((dff, d), const, **resident),
        ],
        out_specs=pl.BlockSpec((FFN_TM, d), lambda s: (prev(s), 0)),
        out_shape=jax.ShapeDtypeStruct((t, d), F32),
        scratch_shapes=[
            pltpu.VMEM((DN_HEADS, DN_HEAD_DIM, DN_HEAD_DIM), F32),
            pltpu.VMEM((2, FFN_TM, DN_WIDTH), BF16),
            pltpu.VMEM(item(DN_C, DN_C), F32),
            pltpu.VMEM(item(DN_C, DN_C), BF16),
            pltpu.VMEM(item(DN_C, 2 * DN_HEAD_DIM), BF16),
            pltpu.VMEM(item(DN_C, DN_HEAD_DIM), BF16),
            pltpu.VMEM(item(DN_C, DN_HEAD_DIM), BF16),
            pltpu.VMEM(item(8, DN_HEAD_DIM), F32),
        ],
        compiler_params=pltpu.CompilerParams(
            dimension_semantics=("arbitrary",), vmem_limit_bytes=VMEM_LIMIT),
        name="deltanet_ffn",
    )(main, main, main, ab, ab, ab, abt, abt, abt, main, ltri, utri, alog, dtb, alogt, dtbt, dn_og,
      xf, o_att, w_out, fg, w1, w2)


def _layer(h, p, batch, seq):
    w_in = p["w_in"]
    w_main = w_in.astype(BF16)
    w_ab = jnp.pad(w_in[:, MAIN_COLS:], ((0, 0), (0, AB_PAD - 2 * DN_HEADS))).astype(BF16)
    head_of = np.arange(ATT_WIDTH) // ATT_HEAD_DIM
    seg = jnp.asarray((head_of[:, None] == head_of[None, :]) / ATT_HEAD_DIM, BF16)
    qg = (jnp.tile(p["att_q_gain"].astype(F32), ATT_HEADS) * (ATT_HEAD_DIM ** -0.5 * LOG2_E))[None]
    kg = jnp.tile(p["att_k_gain"].astype(F32), ATT_HEADS)[None]
    (main, ab, abt), (w_out, w_ff1, w_ff2) = _inproj(
        h, p["mix_norm_gain"].astype(F32)[None], w_main, w_ab, qg, kg, seg, p["dn_conv_w"].astype(F32), seq,
        [p["w_out"].astype(F32), p["w_ff1"].astype(F32), p["w_ff2"].astype(F32)])

    o_att = _attention(main, _bias_signal(p["rel_bias"]),
                       p["att_out_gain"].astype(F32)[None], batch, seq)

    pad_row = lambda v: jnp.pad(v.astype(F32), (0, AB_PAD - DN_HEADS))[None]
    pad_col = lambda v: jnp.pad(v.astype(F32), (0, 8 - DN_HEADS))[:, None]
    return _deltanet_ffn(h, main, ab, abt, o_att, pad_row(p["dn_a_log"]), pad_row(p["dn_dt_bias"]),
                         pad_col(p["dn_a_log"]), pad_col(p["dn_dt_bias"]),
                         p["dn_out_gain"].astype(F32)[None], w_out, p["ffn_norm_gain"].astype(F32)[None],
                         w_ff1, w_ff2, seq)


def kernel(x, mix_norm_gain, w_in, att_q_gain, att_k_gain, rel_bias, att_out_gain, dn_conv_w,
           dn_a_log, dn_dt_bias, dn_out_gain, w_out, ffn_norm_gain, w_ff1, w_ff2):
    batch, seq, d = x.shape
    assert seq % max(ATT_TQ, 2 * DN_TB) == 0 and (batch * seq) % max(IN_TM, FFN_TM) == 0
    params = dict(mix_norm_gain=mix_norm_gain, w_in=w_in, att_q_gain=att_q_gain,
                  att_k_gain=att_k_gain, rel_bias=rel_bias, att_out_gain=att_out_gain,
                  dn_conv_w=dn_conv_w, dn_a_log=dn_a_log, dn_dt_bias=dn_dt_bias,
                  dn_out_gain=dn_out_gain, w_out=w_out, ffn_norm_gain=ffn_norm_gain,
                  w_ff1=w_ff1, w_ff2=w_ff2)
    h = x.reshape(batch * seq, d)
    for layer in range(w_in.shape[0]):
        h = _layer(h, {name: v[layer] for name, v in params.items()}, batch, seq)
    return h.reshape(batch, seq, d)
```

```python
import functools

import numpy as np
import jax
import jax.numpy as jnp
from jax import lax
from jax.experimental import pallas as pl
from jax.experimental.pallas import tpu as pltpu

F32 = jnp.float32
BF16 = jnp.bfloat16

CHUNK = 64
N_PAST_CHUNKS = 8
ATT_HEADS = 8
ATT_HEAD_DIM = 64
ATT_WIDTH = ATT_HEADS * ATT_HEAD_DIM
MAX_REL = 256
DN_HEADS = 4
DN_HEAD_DIM = 128
DN_WIDTH = DN_HEADS * DN_HEAD_DIM
CONV_K = 4
EPS = 1e-6

MAIN_COLS = 3 * ATT_WIDTH + 4 * DN_WIDTH
AB_PAD = 128

LANES = 128
VMEM_LIMIT = 56 * 1024 * 1024

IN_TM = 512
IN_TN = 512
ATT_TQ = 4 * CHUNK
ATT_WIN = ATT_TQ + N_PAST_CHUNKS * CHUNK
DN_C = 128
DN_TB = 2 * DN_C
DN_TAIL = 8
DN_ITEMS = (DN_TB // DN_C) * DN_HEADS
DN_CONV_FIRST = 3 * ATT_WIDTH // IN_TN
DN_CONV_CHUNKS = 3 * DN_WIDTH // IN_TN
DN_STATE_PER_LOCAL = 4
FFN_TM = 512
FFN_TF = 1024
FFN_TN = 256
NEG_BIG = -1e30
LOG2_E = 1.4426950408889634


def _dot(a, b):
    return jnp.dot(a, b, preferred_element_type=F32)


def _dot_nt(a, b):
    return lax.dot_general(a, b, (((1,), (1,)), ((), ())), preferred_element_type=F32)


def _dot_tn(a, b):
    return lax.dot_general(a, b, (((0,), (0,)), ((), ())), preferred_element_type=F32)


def _split3(a):
    p1 = a.astype(BF16)
    r1 = a - p1.astype(F32)
    p2 = r1.astype(BF16)
    p3 = (r1 - p2.astype(F32)).astype(BF16)
    return p1, p2, p3


def _sigmoid(x):
    return 1.0 / (1.0 + jnp.exp2(x * -LOG2_E))


def _softplus(x):
    return jnp.maximum(x, 0.0) + jnp.log(1.0 + jnp.exp(-jnp.abs(x)))


def _inproj_kernel(x_ref, g_ref, wf_ref, wab_ref, qg_ref, kg_ref, seg_ref, cw_ref, *rest, tiles_per_seq):
    n_cast = (len(rest) - 5) // 2
    cast_in, (main_ref, ab_ref, abt_ref) = rest[:n_cast], rest[n_cast:n_cast + 3]
    cast_out, (stage_ref, wm_ref) = rest[n_cast + 3:2 * n_cast + 3], rest[2 * n_cast + 3:]
    i = pl.program_id(0)

    @pl.when(i == 0)
    def _():
        wm_ref[...] = wf_ref[...].astype(BF16)

    for src, dst in zip(cast_in, cast_out):
        dst[...] = src[...].astype(BF16)

    @pl.when(i % tiles_per_seq == 0)
    def _():
        stage_ref[...] = jnp.zeros_like(stage_ref)

    x = x_ref[...]
    ms = jnp.mean(x * x, axis=-1, keepdims=True)
    u = (x * lax.rsqrt(ms + EPS) * g_ref[...]).astype(BF16)
    conv = list(range(DN_CONV_FIRST, DN_CONV_FIRST + DN_CONV_CHUNKS))
    plain = [c for c in range(MAIN_COLS // IN_TN) if c not in conv]
    order = [c for pair in zip(plain, conv) for c in pair] + plain[len(conv):]
    project = lambda c: _dot(u, wm_ref[:, c * IN_TN:(c + 1) * IN_TN])
    y_next = project(order[0])
    for n, c in enumerate(order):
        cols = slice(c * IN_TN, (c + 1) * IN_TN)
        y = y_next
        if n + 1 < len(order):
            y_next = project(order[n + 1])
        if c < 2:
            gain = qg_ref if c == 0 else kg_ref
            m = _dot((y * y).astype(BF16), seg_ref[...])
            y = y * lax.rsqrt(m + EPS) * gain[...]
        elif DN_CONV_FIRST <= c < DN_CONV_FIRST + DN_CONV_CHUNKS:
            j = c - DN_CONV_FIRST
            wcols = slice(j * IN_TN, (j + 1) * IN_TN)
            ext = jnp.concatenate([stage_ref[j], y], axis=0)
            z = cw_ref[0:1, wcols] * ext
            for t in range(1, CONV_K):
                z = cw_ref[t:t + 1, wcols] * ext + pltpu.roll(z, 1, axis=0)
            z = z[DN_TAIL:]
            stage_ref[j] = y[IN_TM - DN_TAIL:IN_TM, :]
            y = z * _sigmoid(z)
        main_ref[:, cols] = y.astype(BF16)
    ab = _dot(u, wab_ref[...])
    ab_ref[...] = ab
    abt_ref[...] = ab.T


def _inproj(xf, gain, w_main, w_ab, qg, kg, seg, conv_w, seq, to_cast):
    t, d = xf.shape
    steps = t // IN_TM
    const = lambda i: (0, 0)
    slab = lambda w: pl.BlockSpec((w.shape[0] // steps, w.shape[1]), lambda i: (i, 0))
    outs = pl.pallas_call(
        functools.partial(_inproj_kernel, tiles_per_seq=seq // IN_TM),
        grid=(steps,),
        in_specs=[
            pl.BlockSpec((IN_TM, d), lambda i: (i, 0)),
            pl.BlockSpec((1, d), const),
            pl.BlockSpec((d, MAIN_COLS), const, pipeline_mode=pl.Buffered(1)),
            pl.BlockSpec((d, AB_PAD), const),
            pl.BlockSpec((1, ATT_WIDTH), const),
            pl.BlockSpec((1, ATT_WIDTH), const),
            pl.BlockSpec((ATT_WIDTH, ATT_WIDTH), const),
            pl.BlockSpec((CONV_K, 3 * DN_WIDTH), const),
        ] + [slab(w) for w in to_cast],
        out_specs=[
            pl.BlockSpec((IN_TM, MAIN_COLS), lambda i: (i, 0)),
            pl.BlockSpec((IN_TM, AB_PAD), lambda i: (i, 0)),
            pl.BlockSpec((AB_PAD, IN_TM), lambda i: (0, i)),
        ] + [slab(w) for w in to_cast],
        out_shape=[
            jax.ShapeDtypeStruct((t, MAIN_COLS), BF16),
            jax.ShapeDtypeStruct((t, AB_PAD), F32),
            jax.ShapeDtypeStruct((AB_PAD, t), F32),
        ] + [jax.ShapeDtypeStruct(w.shape, BF16) for w in to_cast],
        scratch_shapes=[
            pltpu.VMEM((DN_CONV_CHUNKS, DN_TAIL, IN_TN), F32),
            pltpu.VMEM((d, MAIN_COLS), BF16),
        ],
        compiler_params=pltpu.CompilerParams(
            dimension_semantics=("arbitrary",), vmem_limit_bytes=VMEM_LIMIT),
        name="inproj",
    )(xf, gain, w_main, w_ab, qg, kg, seg, conv_w, *to_cast)
    return outs[:3], outs[3:]


def _attn_tile(q_ref, row0, k_refs, v_refs, bias_ref, start_mask, og_ref, o_ref):
    lane = lax.broadcasted_iota(jnp.int32, (1, LANES), 1)
    per_tile = LANES // ATT_HEAD_DIM
    rows = slice(row0, row0 + ATT_TQ)

    def mine(h):
        e = h % per_tile
        return (lane >= e * ATT_HEAD_DIM) & (lane < (e + 1) * ATT_HEAD_DIM)

    def window(refs, h):
        sl = slice(h // per_tile * LANES, (h // per_tile + 1) * LANES)
        return jnp.concatenate([r[:, sl] for r in refs], axis=0)

    def scores(h):
        qp = q_ref[rows, h // per_tile * LANES:(h // per_tile + 1) * LANES]
        q_e = jnp.where(mine(h), qp, jnp.zeros_like(qp))
        bias = bias_ref[h] if start_mask is None else bias_ref[h] + start_mask
        return _dot_nt(q_e, window(k_refs, h)) + bias

    def attend(h, s):
        mx = jnp.max(s, axis=-1, keepdims=True)
        pe = jnp.exp2(s - mx).astype(BF16)
        vw = window(v_refs, h)
        o_e = _dot(pe, jnp.where(mine(h), vw, jnp.ones_like(vw)))
        return o_e / pltpu.roll(o_e, ATT_HEAD_DIM, axis=1)

    normed = []
    s_next = scores(0)
    for h in range(ATT_HEADS):
        s_cur = s_next
        if h + 1 < ATT_HEADS:
            s_next = scores(h + 1)
        normed.append(attend(h, s_cur))
        yield
    outs = [jnp.where(lane < ATT_HEAD_DIM, normed[t * per_tile], normed[t * per_tile + 1])
            for t in range(ATT_WIDTH // LANES)]
    o = jnp.concatenate(outs, axis=1)
    ms = jnp.mean(o * o, axis=-1, keepdims=True)
    o_ref[rows, :] = (o * lax.rsqrt(ms + EPS) * og_ref[...]).astype(BF16)


def _attn_kernel(q_ref, k0_ref, k1_ref, k2_ref, v0_ref, v1_ref, v2_ref, sig_ref, og_ref, o_ref, bias_ref):
    i = pl.program_id(1)

    @pl.when((pl.program_id(0) == 0) & (i == 0))
    def _():
        r = lax.broadcasted_iota(jnp.int32, (ATT_TQ, ATT_WIN), 0)
        s = lax.broadcasted_iota(jnp.int32, (ATT_TQ, ATT_WIN), 1)
        back = (s >> 6) - (r >> 6)
        in_band = (back >= 0) & (back <= N_PAST_CHUNKS)
        for h in range(ATT_HEADS):
            rows = jnp.broadcast_to(sig_ref[h:h + 1, :], (ATT_TQ, sig_ref.shape[1]))
            toep = pltpu.roll(rows, 0, 1, stride=1, stride_axis=0)
            bias_ref[h] = jnp.where(in_band, toep[:, 0:ATT_WIN], NEG_BIG)

    def tile(start_mask):
        _run(_attn_tile(q_ref, 0, (k0_ref, k1_ref, k2_ref), (v0_ref, v1_ref, v2_ref), bias_ref,
                        start_mask, og_ref, o_ref))

    tiles_in_past = N_PAST_CHUNKS * CHUNK // ATT_TQ

    @pl.when(i < tiles_in_past)
    def _():
        jchunk = lax.broadcasted_iota(jnp.int32, (1, ATT_WIN), 1) >> 6
        tile(jnp.where(jchunk + (ATT_TQ // CHUNK) * i - N_PAST_CHUNKS >= 0, 0.0, NEG_BIG))

    @pl.when(i >= tiles_in_past)
    def _():
        tile(None)


def _attention(main, signal, og, batch, seq):
    nb = seq // ATT_TQ
    blk = (ATT_TQ, ATT_WIDTH)

    def kv_spec(col, back):
        return pl.BlockSpec(blk, lambda b, i: (b * nb + jnp.maximum(i - back, 0), col))

    return pl.pallas_call(
        _attn_kernel,
        grid=(batch, nb),
        in_specs=[
            pl.BlockSpec(blk, lambda b, i: (b * nb + i, 0)),
            kv_spec(1, 2), kv_spec(1, 1), kv_spec(1, 0),
            kv_spec(2, 2), kv_spec(2, 1), kv_spec(2, 0),
            pl.BlockSpec(signal.shape, lambda b, i: (0, 0)),
            pl.BlockSpec((1, ATT_WIDTH), lambda b, i: (0, 0)),
        ],
        out_specs=pl.BlockSpec(blk, lambda b, i: (b * nb + i, 0)),
        out_shape=jax.ShapeDtypeStruct((batch * seq, ATT_WIDTH), BF16),
        scratch_shapes=[pltpu.VMEM((ATT_HEADS, ATT_TQ, ATT_WIN), F32)],
        compiler_params=pltpu.CompilerParams(
            dimension_semantics=("arbitrary", "arbitrary"), vmem_limit_bytes=VMEM_LIMIT),
        name="band_attention",
    )(main, main, main, main, main, main, main, signal, og)


def _bias_signal(rel_bias):
    nh = rel_bias.shape[0]
    lo = N_PAST_CHUNKS * CHUNK - (ATT_WIN - 1) + MAX_REL
    n_dist = ATT_TQ + ATT_WIN - 1
    n_tail = n_dist - (2 * MAX_REL + 1 - lo)
    rb = rel_bias.astype(F32) * LOG2_E
    by_dist = jnp.concatenate([rb[:, lo:], jnp.broadcast_to(rb[:, -1:], (nh, n_tail))], axis=1)
    period = ATT_TQ + ATT_WIN
    return jnp.roll(jnp.pad(by_dist[:, ::-1], ((0, 0), (0, period - n_dist))), -(ATT_TQ - 1), axis=1)


def _weave(main, side, every):
    n = 0
    done = object()
    while True:
        if n % every == 0:
            next(side, done)
        if next(main, done) is done:
            break
        n += 1
        yield
    _run(side)


def _chain(*stages):
    for g in stages:
        yield from g


def _run(stage):
    for _ in stage:
        pass


def _inv_unit_lower(mats, ri, ci, out):
    eye = jnp.where(ri == ci, 1.0, 0.0)
    diag8 = (ri >> 3) == (ci >> 3)
    n1 = [jnp.where(diag8, -a, 0.0) for a in mats]
    n1b = [n.astype(BF16) for n in n1]
    n2b = [_dot(n, n).astype(BF16) for n in n1b]
    yield
    n4b = [_dot(n, n).astype(BF16) for n in n2b]
    ts = [eye + n for n in n1]
    yield
    ts = [t + _dot(t.astype(BF16), n) for t, n in zip(ts, n2b)]
    yield
    ts = [t + _dot(t.astype(BF16), n) for t, n in zip(ts, n4b)]
    yield
    half = DN_C // 2
    r_h = lax.broadcasted_iota(jnp.int32, (half, DN_C), 0)
    c_o = lax.broadcasted_iota(jnp.int32, (half, DN_C), 1)
    for sh in range(3, 7):
        blk = 1 << sh
        odd = [slice(s0, s0 + blk) for s0 in range(blk, DN_C, 2 * blk)]
        even = [slice(s0, s0 + blk) for s0 in range(0, DN_C, 2 * blk)]
        take = lambda m, parts: jnp.concatenate([m[p] for p in parts], axis=0)
        r_o = r_h + (((r_h >> sh) + 1) << sh)
        pair =((r_o >> (sh + 1)) == (c_o >> (sh + 1))) & ((r_o >> sh) != (c_o >> sh))
        tb = [t.astype(BF16) for t in ts]
        lt = [_dot(jnp.where(pair, take(a, odd), 0.0).astype(BF16), t) for a, t in zip(mats, tb)]
        yield
        zero = jnp.zeros((blk, DN_C), F32)
        lt = [jnp.concatenate([p for m in range(half // blk) for p in (zero, x[m * blk:(m + 1) * blk])],
                              axis=0).astype(BF16) for x in lt]
        t_odd = [take(t, odd) for t in ts]
        new = [t - _dot(t.astype(BF16), x) for t, x in zip(t_odd, lt)]
        ts = [jnp.concatenate([p for m in range(half // blk)
                               for p in (t[even[m]], n[m * blk:(m + 1) * blk])], axis=0)
              for t, n in zip(ts, new)]
        yield
    out.extend(ts)


def _gdn_local(x_ref, ab_ref, abt_ref, ltri_ref, utri_ref, alog_ref, dtb_ref, alogt_ref, dtbt_ref, slot,
               a_scr, att_scr, rhs_scr, qd_scr, kt_scr, edl_scr):
    ab = ab_ref[...]
    g_col = -jnp.exp(alog_ref[...]) * _softplus(ab + dtb_ref[...])
    beta_col = _sigmoid(ab)
    g_row = -jnp.exp(alogt_ref[...]) * _softplus(abt_ref[0:8, :] + dtbt_ref[...])
    gc1, gc2, gc3 = _split3(g_col)
    gr1, gr2, gr3 = _split3(g_row)

    y = x_ref[...].astype(F32)

    ones = jnp.ones((DN_HEAD_DIM, DN_HEAD_DIM), BF16)
    sq = [(t * t).astype(BF16) for t in (y[:, g * DN_HEAD_DIM:(g + 1) * DN_HEAD_DIM]
                                         for g in range(2 * DN_HEADS))]
    ssq = [_dot(t, ones) for t in sq]
    yield
    ltri = ltri_ref[...]
    utri = utri_ref[...]
    d_col = _dot(ltri, gc1) + (_dot(ltri, gc2) + _dot(ltri, gc3))
    d_row = _dot(gr1, utri) + (_dot(gr2, utri) + _dot(gr3, utri))
    qk = []
    for g, tot in enumerate(ssq):
        inv = lax.rsqrt(tot + EPS)
        qk.append(y[:, g * DN_HEAD_DIM:(g + 1) * DN_HEAD_DIM]
                  * (inv * DN_HEAD_DIM ** -0.5 if g < DN_HEADS else inv))

    ri = lax.broadcasted_iota(jnp.int32, (DN_C, DN_C), 0)
    ci = lax.broadcasted_iota(jnp.int32, (DN_C, DN_C), 1)
    lmasks, kbqs, kbfs = [], [], []
    for n in range(DN_ITEMS):
        c, h = divmod(n, DN_HEADS)
        rows = slice(c * DN_C, (c + 1) * DN_C)
        q = qk[h][rows]
        k = qk[DN_HEADS + h][rows]
        v = y[rows, 2 * DN_WIDTH + h * DN_HEAD_DIM:2 * DN_WIDTH + (h + 1) * DN_HEAD_DIM]
        dc = d_col[rows, h:h + 1]
        dr = d_row[h:h + 1, rows]
        dl = dr[:, DN_C - 1:DN_C]
        bc = beta_col[rows, DN_HEADS + h:DN_HEADS + h + 1]
        kb = k * bc
        edc = jnp.exp(dc)
        rhs_scr[slot, n] = jnp.concatenate([v * bc, kb * edc], axis=1).astype(BF16)
        qd_scr[slot, n] = (q * edc).astype(BF16)
        kt_scr[slot, n] = (k * jnp.exp(dl - dc)).astype(BF16)
        edl_scr[slot, n] = jnp.broadcast_to(jnp.exp(dl), (8, DN_HEAD_DIM))
        lmasks.append(jnp.exp(jnp.where(ci <= ri, dc - dr, -jnp.inf)))
        kbqs.append(jnp.concatenate([kb, q], axis=0).astype(BF16))
        kbfs.append(k.astype(BF16))
    yield
    ps =[_dot_nt(kbq, kbf) for kbq, kbf in zip(kbqs, kbfs)]
    for n, (p, lmask) in enumerate(zip(ps, lmasks)):
        a_scr[slot, n] = jnp.where(ci < ri, p[0:DN_C] * lmask, 0.0)
        att_scr[slot, n] = (p[DN_C:2 * DN_C] * lmask).astype(BF16)


def _gdn_state(slot, gate_ref, og_ref, o_ref, out_row0, s_ref, a_scr, att_scr, rhs_scr, qd_scr, kt_scr,
               edl_scr):
    ri = lax.broadcasted_iota(jnp.int32, (DN_C, DN_C), 0)
    ci = lax.broadcasted_iota(jnp.int32, (DN_C, DN_C), 1)
    og = og_ref[...]
    tinvs = []
    yield from _inv_unit_lower([a_scr[slot, n] for n in range(DN_ITEMS)], ri, ci, tinvs)
    ws = [_dot(t.astype(BF16), rhs_scr[slot, n]) for n, t in enumerate(tinvs)]
    yield
    for c in range(DN_TB // DN_C):
        idx = [c * DN_HEADS + h for h in range(DN_HEADS)]
        s_old = [s_ref[h] for h in range(DN_HEADS)]
        rs = [_dot(jnp.concatenate([ws[n][:, DN_HEAD_DIM:].astype(BF16), qd_scr[slot, n]], axis=0),
                   s.astype(BF16)) for n, s in zip(idx, s_old)]
        yield
        vnb = [(ws[n][:, 0:DN_HEAD_DIM] - r[0:DN_C]).astype(BF16) for n, r in zip(idx, rs)]
        outs = [r[DN_C:2 * DN_C] + _dot(att_scr[slot, n], vn) for n, r, vn in zip(idx, rs, vnb)]
        yield
        for h, (n, vn) in enumerate(zip(idx, vnb)):
            s_ref[h] = s_old[h] * edl_scr[slot, n, 0:1, :] + _dot_tn(kt_scr[slot, n], vn)
        yield
        rows = slice(out_row0 + c * DN_C, out_row0 + (c + 1) * DN_C)
        for h, o in enumerate(outs):
            hs = slice(h * DN_HEAD_DIM, (h + 1) * DN_HEAD_DIM)
            ms = jnp.mean(o * o, axis=-1, keepdims=True)
            gt = gate_ref[rows, hs].astype(F32)
            o_ref[rows, hs] = (o * lax.rsqrt(ms + EPS) * og * (gt * _sigmoid(gt))).astype(BF16)


def _ffn_stages(x_ref, oa_ref, od_ref, wo_ref, fg_ref, w1_ref, w2_ref, o_ref):
    d = x_ref.shape[1]
    oa = oa_ref[...]
    od = od_ref[...]
    mixed = []
    for j in range(d // FFN_TN):
        cols = slice(j * FFN_TN, (j + 1) * FFN_TN)
        mixed.append(_dot(oa, wo_ref[0:ATT_WIDTH, cols]) + _dot(od, wo_ref[ATT_WIDTH:, cols]))
        yield
    h1 = x_ref[...] + jnp.concatenate(mixed, axis=1)
    ms = jnp.mean(h1 * h1, axis=-1, keepdims=True)
    n = (h1 * lax.rsqrt(ms + EPS) * fg_ref[...]).astype(BF16)

    def up(c, out):
        for j in range(FFN_TF // FFN_TN):
            col0 = c * FFN_TF + j * FFN_TN
            out.append(_dot(n, w1_ref[:, col0:col0 + FFN_TN]))
            yield

    n_chunks = w1_ref.shape[1] // FFN_TF
    acc = h1
    z_next = []
    yield from up(0, z_next)
    for c in range(n_chunks):
        z = z_next
        if c + 1 < n_chunks:
            z_next = []
            yield from up(c + 1, z_next)
        act = jnp.concatenate([jnp.square(jnp.maximum(piece, 0.0)).astype(BF16) for piece in z], axis=1)
        down = []
        for j in range(d // FFN_TN):
            down.append(_dot(act, w2_ref[c * FFN_TF:(c + 1) * FFN_TF, j * FFN_TN:(j + 1) * FFN_TN]))
            yield
        acc = acc + jnp.concatenate(down, axis=1)
    o_ref[...] = acc


def _gdn_ffn_kernel(x0_ref, x1_ref, x2_ref, ab0_ref, ab1_ref, ab2_ref, abt0_ref, abt1_ref, abt2_ref,
                    gate_ref, ltri_ref, utri_ref, alog_ref, dtb_ref, alogt_ref, dtbt_ref, dn_og_ref,
                    xres_ref, oa_ref, wo_ref, fg_ref, w1_ref, w2_ref, o_ref, s_ref, od_scr, *local_scr,
                    steps_per_seq):
    s = pl.program_id(0)
    cur = s % 2
    consts = (ltri_ref, utri_ref, alog_ref, dtb_ref, alogt_ref, dtbt_ref)

    def local(x_ref, ab_ref, abt_ref, slot):
        return _gdn_local(x_ref, ab_ref, abt_ref, *consts, slot, *local_scr)

    def state(slot, out_row0):
        return _gdn_state(slot, gate_ref, dn_og_ref, od_scr.at[cur], out_row0, s_ref, *local_scr)

    @pl.when(s == 0)
    def _():
        od_scr[1] = jnp.zeros(od_scr.shape[1:], od_scr.dtype)

    @pl.when(s % steps_per_seq == 0)
    def _():
        s_ref[...] = jnp.zeros_like(s_ref)
        _run(local(x0_ref, ab0_ref, abt0_ref, 0))

    deltanet = _chain(
        _weave(state(0, 0), local(x1_ref, ab1_ref, abt1_ref, 1), DN_STATE_PER_LOCAL),
        _weave(state(1, DN_TB), local(x2_ref, ab2_ref, abt2_ref, 0), DN_STATE_PER_LOCAL))
    ffn = _ffn_stages(xres_ref, oa_ref, od_scr.at[1 - cur], wo_ref, fg_ref, w1_ref, w2_ref, o_ref)
    _run(_weave(deltanet, ffn, 1))


def _deltanet_ffn(xf, main, ab, abt, o_att, alog, dtb, alogt, dtbt, dn_og, w_out, fg, w1, w2, seq):
    t, d = xf.shape
    dff = w1.shape[1]
    assert FFN_TM == 2 * DN_TB
    nb = seq // DN_TB
    spq = nb // 2
    steps = t // FFN_TM
    tile = lambda s: jnp.minimum(s, steps - 1)
    prev = lambda s: jnp.maximum(s - 1, 0)
    seq_blk0 = lambda s: tile(s) // spq * nb
    first = seq_blk0
    odd = lambda s: 2 * tile(s) + 1
    nxt = lambda s: jnp.minimum(2 * tile(s) + 2, seq_blk0(s) + nb - 1)
    const = lambda s: (0, 0)

    def x_spec(blk):
        return pl.BlockSpec((DN_TB, 3 * DN_WIDTH), lambda s: (blk(s), 1))

    def ab_spec(blk):
        return pl.BlockSpec((DN_TB, AB_PAD), lambda s: (blk(s), 0))

    def abt_spec(blk):
        return pl.BlockSpec((AB_PAD, DN_TB), lambda s: (0, blk(s)))

    r = np.arange(DN_TB)
    same = (r[:, None] // DN_C) == (r[None, :] // DN_C)
    ltri = jnp.asarray(same & (r[None, :] <= r[:, None]), BF16)
    utri = jnp.asarray(same & (r[:, None] <= r[None, :]), BF16)
    item = lambda *shape: (2, DN_ITEMS) + shape
    resident = dict(pipeline_mode=pl.Buffered(1))
    return pl.pallas_call(
        functools.partial(_gdn_ffn_kernel, steps_per_seq=spq),
        grid=(steps + 1,),
        in_specs=[
            x_spec(first), x_spec(odd), x_spec(nxt),
            ab_spec(first), ab_spec(odd), ab_spec(nxt),
            abt_spec(first), abt_spec(odd), abt_spec(nxt),
            pl.BlockSpec((2 * DN_TB, DN_WIDTH), lambda s: (tile(s), 6)),
            pl.BlockSpec((DN_TB, DN_TB), const),
            pl.BlockSpec((DN_TB, DN_TB), const),
            pl.BlockSpec((1, AB_PAD), const),
            pl.BlockSpec((1, AB_PAD), const),
            pl.BlockSpec((8, 1), const),
            pl.BlockSpec((8, 1), const),
            pl.BlockSpec((1, DN_HEAD_DIM), const),
            pl.BlockSpec((FFN_TM, d), lambda s: (prev(s), 0)),
            pl.BlockSpec((FFN_TM, ATT_WIDTH), lambda s: (prev(s), 0)),
            pl.BlockSpec((d, d), const, **resident),
            pl.BlockSpec((1, d), const),
            pl.BlockSpec((d, dff), const, **resident),
            pl.BlockSpec((dff, d), const, **resident),
        ],
        out_specs=pl.BlockSpec((FFN_TM, d), lambda s: (prev(s), 0)),
        out_shape=jax.ShapeDtypeStruct((t, d), F32),
        scratch_shapes=[
            pltpu.VMEM((DN_HEADS, DN_HEAD_DIM, DN_HEAD_DIM), F32),
            pltpu.VMEM((2, FFN_TM, DN_WIDTH), BF16),
            pltpu.VMEM(item(DN_C, DN_C), F32),
            pltpu.VMEM(item(DN_C, DN_C), BF16),
            pltpu.VMEM(item(DN_C, 2 * DN_HEAD_DIM), BF16),
            pltpu.VMEM(item(DN_C, DN_HEAD_DIM), BF16),
            pltpu.VMEM(item(DN_C, DN_HEAD_DIM), BF16),
            pltpu.VMEM(item(8, DN_HEAD_DIM), F32),
        ],
        compiler_params=pltpu.CompilerParams(
            dimension_semantics=("arbitrary",), vmem_limit_bytes=VMEM_LIMIT),
        name="deltanet_ffn",
    )(main, main, main, ab, ab, ab, abt, abt, abt, main, ltri, utri, alog, dtb, alogt, dtbt, dn_og,
      xf, o_att, w_out, fg, w1, w2)


def _layer(h, p, batch, seq):
    w_in = p["w_in"]
    w_main = w_in.astype(F32)
    w_ab = jnp.pad(w_in[:, MAIN_COLS:], ((0, 0), (0, AB_PAD - 2 * DN_HEADS))).astype(BF16)
    head_of = np.arange(ATT_WIDTH) // ATT_HEAD_DIM
    seg = jnp.asarray((head_of[:, None] == head_of[None, :]) / ATT_HEAD_DIM, BF16)
    qg = (jnp.tile(p["att_q_gain"].astype(F32), ATT_HEADS) * (ATT_HEAD_DIM ** -0.5 * LOG2_E))[None]
    kg = jnp.tile(p["att_k_gain"].astype(F32), ATT_HEADS)[None]
    (main, ab, abt), (w_out, w_ff1, w_ff2) = _inproj(
        h, p["mix_norm_gain"].astype(F32)[None], w_main, w_ab, qg, kg, seg, p["dn_conv_w"].astype(F32), seq,
        [p["w_out"].astype(F32), p["w_ff1"].astype(F32), p["w_ff2"].astype(F32)])

    o_att = _attention(main, _bias_signal(p["rel_bias"]),
                       p["att_out_gain"].astype(F32)[None], batch, seq)

    pad_row = lambda v: jnp.pad(v.astype(F32), (0, AB_PAD - DN_HEADS))[None]
    pad_col = lambda v: jnp.pad(v.astype(F32), (0, 8 - DN_HEADS))[:, None]
    return _deltanet_ffn(h, main, ab, abt, o_att, pad_row(p["dn_a_log"]), pad_row(p["dn_dt_bias"]),
                         pad_col(p["dn_a_log"]), pad_col(p["dn_dt_bias"]),
                         p["dn_out_gain"].astype(F32)[None], w_out, p["ffn_norm_gain"].astype(F32)[None],
                         w_ff1, w_ff2, seq)


def kernel(x, mix_norm_gain, w_in, att_q_gain, att_k_gain, rel_bias, att_out_gain, dn_conv_w,
           dn_a_log, dn_dt_bias, dn_out_gain, w_out, ffn_norm_gain, w_ff1, w_ff2):
    batch, seq, d = x.shape
    assert seq % max(ATT_TQ, 2 * DN_TB) == 0 and (batch * seq) % max(IN_TM, FFN_TM) == 0
    params = dict(mix_norm_gain=mix_norm_gain, w_in=w_in, att_q_gain=att_q_gain,
                  att_k_gain=att_k_gain, rel_bias=rel_bias, att_out_gain=att_out_gain,
                  dn_conv_w=dn_conv_w, dn_a_log=dn_a_log, dn_dt_bias=dn_dt_bias,
                  dn_out_gain=dn_out_gain, w_out=w_out, ffn_norm_gain=ffn_norm_gain,
                  w_ff1=w_ff1, w_ff2=w_ff2)
    h = x.reshape(batch * seq, d)
    for layer in range(w_in.shape[0]):
        h = _layer(h, {name: v[layer] for name, v in params.items()}, batch, seq)
    return h.reshape(batch, seq, d)
```

```python
import functools

import numpy as np
import jax
import jax.numpy as jnp
from jax import lax
from jax.experimental import pallas as pl
from jax.experimental.pallas import tpu as pltpu

F32 = jnp.float32
BF16 = jnp.bfloat16

CHUNK = 64
N_PAST_CHUNKS = 8
ATT_HEADS = 8
ATT_HEAD_DIM = 64
ATT_WIDTH = ATT_HEADS * ATT_HEAD_DIM
MAX_REL = 256
DN_HEADS = 4
DN_HEAD_DIM = 128
DN_WIDTH = DN_HEADS * DN_HEAD_DIM
CONV_K = 4
EPS = 1e-6

MAIN_COLS = 3 * ATT_WIDTH + 4 * DN_WIDTH
AB_PAD = 128

LANES = 128
VMEM_LIMIT = 56 * 1024 * 1024

IN_TM = 512
IN_TN = 512
ATT_TQ = 4 * CHUNK
ATT_WIN = ATT_TQ + N_PAST_CHUNKS * CHUNK
DN_C = 128
DN_TB = 2 * DN_C
DN_TAIL = 8
DN_ITEMS = (DN_TB // DN_C) * DN_HEADS
DN_CONV_FIRST = 3 * ATT_WIDTH // IN_TN
DN_CONV_CHUNKS = 3 * DN_WIDTH // IN_TN
DN_STATE_PER_LOCAL = 4
FFN_TM = 512
FFN_TF = 1024
FFN_TN = 256
NEG_BIG = -1e30
LOG2_E = 1.4426950408889634


def _dot(a, b):
    return jnp.dot(a, b, preferred_element_type=F32)


def _dot_nt(a, b):
    return lax.dot_general(a, b, (((1,), (1,)), ((), ())), preferred_element_type=F32)


def _dot_tn(a, b):
    return lax.dot_general(a, b, (((0,), (0,)), ((), ())), preferred_element_type=F32)


def _split3(a):
    p1 = a.astype(BF16)
    r1 = a - p1.astype(F32)
    p2 = r1.astype(BF16)
    p3 = (r1 - p2.astype(F32)).astype(BF16)
    return p1, p2, p3


def _sigmoid(x):
    return 1.0 / (1.0 + jnp.exp2(x * -LOG2_E))


def _softplus(x):
    return jnp.maximum(x, 0.0) + jnp.log(1.0 + jnp.exp(-jnp.abs(x)))


def _inproj_kernel(x_ref, g_ref, wf_ref, wab_ref, qg_ref, kg_ref, seg_ref, cw_ref, *rest, tiles_per_seq):
    n_cast = (len(rest) - 5) // 2
    cast_in, (main_ref, ab_ref, abt_ref) = rest[:n_cast], rest[n_cast:n_cast + 3]
    cast_out, (stage_ref, wm_ref) = rest[n_cast + 3:2 * n_cast + 3], rest[2 * n_cast + 3:]
    i = pl.program_id(0)

    @pl.when(i == 0)
    def _():
        wm_ref[...] = wf_ref[...].astype(BF16)

    for src, dst in zip(cast_in, cast_out):
        dst[...] = src[...].astype(BF16)

    @pl.when(i % tiles_per_seq == 0)
    def _():
        stage_ref[...] = jnp.zeros_like(stage_ref)

    x = x_ref[...]
    ms = jnp.mean(x * x, axis=-1, keepdims=True)
    u = (x * lax.rsqrt(ms + EPS) * g_ref[...]).astype(BF16)
    conv = list(range(DN_CONV_FIRST, DN_CONV_FIRST + DN_CONV_CHUNKS))
    plain = [c for c in range(MAIN_COLS // IN_TN) if c not in conv]
    order = [c for pair in zip(plain, conv) for c in pair] + plain[len(conv):]
    project = lambda c: _dot(u, wm_ref[:, c * IN_TN:(c + 1) * IN_TN])
    y_next = project(order[0])
    for n, c in enumerate(order):
        cols = slice(c * IN_TN, (c + 1) * IN_TN)
        y = y_next
        if n + 1 < len(order):
            y_next = project(order[n + 1])
        if c < 2:
            gain = qg_ref if c == 0 else kg_ref
            m = _dot((y * y).astype(BF16), seg_ref[...])
            y = y * lax.rsqrt(m + EPS) * gain[...]
        elif DN_CONV_FIRST <= c < DN_CONV_FIRST + DN_CONV_CHUNKS:
            j = c - DN_CONV_FIRST
            wcols = slice(j * IN_TN, (j + 1) * IN_TN)
            ext = jnp.concatenate([stage_ref[j], y], axis=0)
            z = cw_ref[0:1, wcols] * ext
            for t in range(1, CONV_K):
                z = cw_ref[t:t + 1, wcols] * ext + pltpu.roll(z, 1, axis=0)
            z = z[DN_TAIL:]
            stage_ref[j] = y[IN_TM - DN_TAIL:IN_TM, :]
            y = z * _sigmoid(z)
        main_ref[:, cols] = y.astype(BF16)
    ab = _dot(u, wab_ref[...].astype(BF16))
    ab_ref[...] = ab
    abt_ref[...] = ab.T


def _inproj(xf, gain, w_main, w_ab, qg, kg, seg, conv_w, seq, to_cast):
    t, d = xf.shape
    steps = t // IN_TM
    const = lambda i: (0, 0)
    slab = lambda w: pl.BlockSpec((w.shape[0] // steps, w.shape[1]), lambda i: (i, 0))
    outs = pl.pallas_call(
        functools.partial(_inproj_kernel, tiles_per_seq=seq // IN_TM),
        grid=(steps,),
        in_specs=[
            pl.BlockSpec((IN_TM, d), lambda i: (i, 0)),
            pl.BlockSpec((1, d), const),
            pl.BlockSpec((None, d, MAIN_COLS), lambda i: (0, 0, 0), pipeline_mode=pl.Buffered(1)),
            pl.BlockSpec((d, AB_PAD), const),
            pl.BlockSpec((1, ATT_WIDTH), const),
            pl.BlockSpec((1, ATT_WIDTH), const),
            pl.BlockSpec((ATT_WIDTH, ATT_WIDTH), const),
            pl.BlockSpec((CONV_K, 3 * DN_WIDTH), const),
        ] + [slab(w) for w in to_cast],
        out_specs=[
            pl.BlockSpec((IN_TM, MAIN_COLS), lambda i: (i, 0)),
            pl.BlockSpec((IN_TM, AB_PAD), lambda i: (i, 0)),
            pl.BlockSpec((AB_PAD, IN_TM), lambda i: (0, i)),
        ] + [slab(w) for w in to_cast],
        out_shape=[
            jax.ShapeDtypeStruct((t, MAIN_COLS), BF16),
            jax.ShapeDtypeStruct((t, AB_PAD), F32),
            jax.ShapeDtypeStruct((AB_PAD, t), F32),
        ] + [jax.ShapeDtypeStruct(w.shape, BF16) for w in to_cast],
        scratch_shapes=[
            pltpu.VMEM((DN_CONV_CHUNKS, DN_TAIL, IN_TN), F32),
            pltpu.VMEM((d, MAIN_COLS), BF16),
        ],
        compiler_params=pltpu.CompilerParams(
            dimension_semantics=("arbitrary",), vmem_limit_bytes=VMEM_LIMIT),
        name="inproj",
    )(xf, gain, w_main, w_ab, qg, kg, seg, conv_w, *to_cast)
    return outs[:3], outs[3:]


def _attn_tile(q_ref, row0, k_refs, v_refs, bias_ref, start_mask, og_ref, o_ref):
    lane = lax.broadcasted_iota(jnp.int32, (1, LANES), 1)
    per_tile = LANES // ATT_HEAD_DIM
    rows = slice(row0, row0 + ATT_TQ)

    def mine(h):
        e = h % per_tile
        return (lane >= e * ATT_HEAD_DIM) & (lane < (e + 1) * ATT_HEAD_DIM)

    def window(refs, h):
        sl = slice(h // per_tile * LANES, (h // per_tile + 1) * LANES)
        return jnp.concatenate([r[:, sl] for r in refs], axis=0)

    def scores(h):
        qp = q_ref[rows, h // per_tile * LANES:(h // per_tile + 1) * LANES]
        q_e = jnp.where(mine(h), qp, jnp.zeros_like(qp))
        bias = bias_ref[h] if start_mask is None else bias_ref[h] + start_mask
        return _dot_nt(q_e, window(k_refs, h)) + bias

    def attend(h, s):
        mx = jnp.max(s, axis=-1, keepdims=True)
        pe = jnp.exp2(s - mx).astype(BF16)
        vw = window(v_refs, h)
        o_e = _dot(pe, jnp.where(mine(h), vw, jnp.ones_like(vw)))
        return o_e / pltpu.roll(o_e, ATT_HEAD_DIM, axis=1)

    normed = []
    s_next = scores(0)
    for h in range(ATT_HEADS):
        s_cur = s_next
        if h + 1 < ATT_HEADS:
            s_next = scores(h + 1)
        normed.append(attend(h, s_cur))
        yield
    outs = [jnp.where(lane < ATT_HEAD_DIM, normed[t * per_tile], normed[t * per_tile + 1])
            for t in range(ATT_WIDTH // LANES)]
    o = jnp.concatenate(outs, axis=1)
    ms = jnp.mean(o * o, axis=-1, keepdims=True)
    o_ref[rows, :] = (o * lax.rsqrt(ms + EPS) * og_ref[...]).astype(BF16)


def _attn_kernel(q_ref, k0_ref, k1_ref, k2_ref, v0_ref, v1_ref, v2_ref, sig_ref, og_ref, o_ref, bias_ref):
    i = pl.program_id(1)

    @pl.when((pl.program_id(0) == 0) & (i == 0))
    def _():
        r = lax.broadcasted_iota(jnp.int32, (ATT_TQ, ATT_WIN), 0)
        s = lax.broadcasted_iota(jnp.int32, (ATT_TQ, ATT_WIN), 1)
        back = (s >> 6) - (r >> 6)
        in_band = (back >= 0) & (back <= N_PAST_CHUNKS)
        for h in range(ATT_HEADS):
            rows = jnp.broadcast_to(sig_ref[h:h + 1, :], (ATT_TQ, sig_ref.shape[1]))
            toep = pltpu.roll(rows, 0, 1, stride=1, stride_axis=0)
            bias_ref[h] = jnp.where(in_band, toep[:, 0:ATT_WIN], NEG_BIG)

    def tile(start_mask):
        _run(_attn_tile(q_ref, 0, (k0_ref, k1_ref, k2_ref), (v0_ref, v1_ref, v2_ref), bias_ref,
                        start_mask, og_ref, o_ref))

    tiles_in_past = N_PAST_CHUNKS * CHUNK // ATT_TQ

    @pl.when(i < tiles_in_past)
    def _():
        jchunk = lax.broadcasted_iota(jnp.int32, (1, ATT_WIN), 1) >> 6
        tile(jnp.where(jchunk + (ATT_TQ // CHUNK) * i - N_PAST_CHUNKS >= 0, 0.0, NEG_BIG))

    @pl.when(i >= tiles_in_past)
    def _():
        tile(None)


def _attention(main, signal, og, batch, seq):
    nb = seq // ATT_TQ
    blk = (ATT_TQ, ATT_WIDTH)

    def kv_spec(col, back):
        return pl.BlockSpec(blk, lambda b, i: (b * nb + jnp.maximum(i - back, 0), col))

    return pl.pallas_call(
        _attn_kernel,
        grid=(batch, nb),
        in_specs=[
            pl.BlockSpec(blk, lambda b, i: (b * nb + i, 0)),
            kv_spec(1, 2), kv_spec(1, 1), kv_spec(1, 0),
            kv_spec(2, 2), kv_spec(2, 1), kv_spec(2, 0),
            pl.BlockSpec(signal.shape, lambda b, i: (0, 0)),
            pl.BlockSpec((1, ATT_WIDTH), lambda b, i: (0, 0)),
        ],
        out_specs=pl.BlockSpec(blk, lambda b, i: (b * nb + i, 0)),
        out_shape=jax.ShapeDtypeStruct((batch * seq, ATT_WIDTH), BF16),
        scratch_shapes=[pltpu.VMEM((ATT_HEADS, ATT_TQ, ATT_WIN), F32)],
        compiler_params=pltpu.CompilerParams(
            dimension_semantics=("arbitrary", "arbitrary"), vmem_limit_bytes=VMEM_LIMIT),
        name="band_attention",
    )(main, main, main, main, main, main, main, signal, og)


def _bias_signal(rel_bias):
    nh = rel_bias.shape[0]
    lo = N_PAST_CHUNKS * CHUNK - (ATT_WIN - 1) + MAX_REL
    n_dist = ATT_TQ + ATT_WIN - 1
    n_tail = n_dist - (2 * MAX_REL + 1 - lo)
    rb = rel_bias.astype(F32) * LOG2_E
    by_dist = jnp.concatenate([rb[:, lo:], jnp.broadcast_to(rb[:, -1:], (nh, n_tail))], axis=1)
    period = ATT_TQ + ATT_WIN
    return jnp.roll(jnp.pad(by_dist[:, ::-1], ((0, 0), (0, period - n_dist))), -(ATT_TQ - 1), axis=1)


def _weave(main, side, every):
    n = 0
    done = object()
    while True:
        if n % every == 0:
            next(side, done)
        if next(main, done) is done:
            break
        n += 1
        yield
    _run(side)


def _chain(*stages):
    for g in stages:
        yield from g


def _run(stage):
    for _ in stage:
        pass


def _inv_unit_lower(mats, ri, ci, out):
    eye = jnp.where(ri == ci, 1.0, 0.0)
    diag8 = (ri >> 3) == (ci >> 3)
    n1 = [jnp.where(diag8, -a, 0.0) for a in mats]
    n1b = [n.astype(BF16) for n in n1]
    n2b = [_dot(n, n).astype(BF16) for n in n1b]
    yield
    n4b = [_dot(n, n).astype(BF16) for n in n2b]
    ts = [eye + n for n in n1]
    yield
    ts = [t + _dot(t.astype(BF16), n) for t, n in zip(ts, n2b)]
    yield
    ts = [t + _dot(t.astype(BF16), n) for t, n in zip(ts, n4b)]
    yield
    half = DN_C // 2
    r_h = lax.broadcasted_iota(jnp.int32, (half, DN_C), 0)
    c_o = lax.broadcasted_iota(jnp.int32, (half, DN_C), 1)
    for sh in range(3, 7):
        blk = 1 << sh
        odd = [slice(s0, s0 + blk) for s0 in range(blk, DN_C, 2 * blk)]
        even = [slice(s0, s0 + blk) for s0 in range(0, DN_C, 2 * blk)]
        take = lambda m, parts: jnp.concatenate([m[p] for p in parts], axis=0)
        r_o = r_h + (((r_h >> sh) + 1) << sh)
        pair =((r_o >> (sh + 1)) == (c_o >> (sh + 1))) & ((r_o >> sh) != (c_o >> sh))
        tb = [t.astype(BF16) for t in ts]
        lt = [_dot(jnp.where(pair, take(a, odd), 0.0).astype(BF16), t) for a, t in zip(mats, tb)]
        yield
        zero = jnp.zeros((blk, DN_C), F32)
        lt = [jnp.concatenate([p for m in range(half // blk) for p in (zero, x[m * blk:(m + 1) * blk])],
                              axis=0).astype(BF16) for x in lt]
        t_odd = [take(t, odd) for t in ts]
        new = [t - _dot(t.astype(BF16), x) for t, x in zip(t_odd, lt)]
        ts = [jnp.concatenate([p for m in range(half // blk)
                               for p in (t[even[m]], n[m * blk:(m + 1) * blk])], axis=0)
              for t, n in zip(ts, new)]
        yield
    out.extend(ts)


def _gdn_local(x_ref, ab_ref, abt_ref, ltri_ref, utri_ref, alog_ref, dtb_ref, alogt_ref, dtbt_ref, slot,
               a_scr, att_scr, rhs_scr, qd_scr, kt_scr, edl_scr):
    ab = ab_ref[...]
    g_col = -jnp.exp(alog_ref[...]) * _softplus(ab + dtb_ref[...])
    beta_col = _sigmoid(ab)
    g_row = -jnp.exp(alogt_ref[...]) * _softplus(abt_ref[0:8, :] + dtbt_ref[...])
    gc1, gc2, gc3 = _split3(g_col)
    gr1, gr2, gr3 = _split3(g_row)

    y = x_ref[...].astype(F32)

    ones = jnp.ones((DN_HEAD_DIM, DN_HEAD_DIM), BF16)
    sq = [(t * t).astype(BF16) for t in (y[:, g * DN_HEAD_DIM:(g + 1) * DN_HEAD_DIM]
                                         for g in range(2 * DN_HEADS))]
    ssq = [_dot(t, ones) for t in sq]
    yield
    ltri = ltri_ref[...]
    utri = utri_ref[...]
    d_col = _dot(ltri, gc1) + (_dot(ltri, gc2) + _dot(ltri, gc3))
    d_row = _dot(gr1, utri) + (_dot(gr2, utri) + _dot(gr3, utri))
    qk = []
    for g, tot in enumerate(ssq):
        inv = lax.rsqrt(tot + EPS)
        qk.append(y[:, g * DN_HEAD_DIM:(g + 1) * DN_HEAD_DIM]
                  * (inv * DN_HEAD_DIM ** -0.5 if g < DN_HEADS else inv))

    ri = lax.broadcasted_iota(jnp.int32, (DN_C, DN_C), 0)
    ci = lax.broadcasted_iota(jnp.int32, (DN_C, DN_C), 1)
    lmasks, kbqs, kbfs = [], [], []
    for n in range(DN_ITEMS):
        c, h = divmod(n, DN_HEADS)
        rows = slice(c * DN_C, (c + 1) * DN_C)
        q = qk[h][rows]
        k = qk[DN_HEADS + h][rows]
        v = y[rows, 2 * DN_WIDTH + h * DN_HEAD_DIM:2 * DN_WIDTH + (h + 1) * DN_HEAD_DIM]
        dc = d_col[rows, h:h + 1]
        dr = d_row[h:h + 1, rows]
        dl = dr[:, DN_C - 1:DN_C]
        bc = beta_col[rows, DN_HEADS + h:DN_HEADS + h + 1]
        kb = k * bc
        edc = jnp.exp(dc)
        rhs_scr[slot, n] = jnp.concatenate([v * bc, kb * edc], axis=1).astype(BF16)
        qd_scr[slot, n] = (q * edc).astype(BF16)
        kt_scr[slot, n] = (k * jnp.exp(dl - dc)).astype(BF16)
        edl_scr[slot, n] = jnp.broadcast_to(jnp.exp(dl), (8, DN_HEAD_DIM))
        lmasks.append(jnp.exp(jnp.where(ci <= ri, dc - dr, -jnp.inf)))
        kbqs.append(jnp.concatenate([kb, q], axis=0).astype(BF16))
        kbfs.append(k.astype(BF16))
    yield
    ps =[_dot_nt(kbq, kbf) for kbq, kbf in zip(kbqs, kbfs)]
    for n, (p, lmask) in enumerate(zip(ps, lmasks)):
        a_scr[slot, n] = jnp.where(ci < ri, p[0:DN_C] * lmask, 0.0)
        att_scr[slot, n] = (p[DN_C:2 * DN_C] * lmask).astype(BF16)


def _gdn_state(slot, gate_ref, og_ref, o_ref, out_row0, s_ref, a_scr, att_scr, rhs_scr, qd_scr, kt_scr,
               edl_scr):
    ri = lax.broadcasted_iota(jnp.int32, (DN_C, DN_C), 0)
    ci = lax.broadcasted_iota(jnp.int32, (DN_C, DN_C), 1)
    og = og_ref[...]
    tinvs = []
    yield from _inv_unit_lower([a_scr[slot, n] for n in range(DN_ITEMS)], ri, ci, tinvs)
    ws = [_dot(t.astype(BF16), rhs_scr[slot, n]) for n, t in enumerate(tinvs)]
    yield
    for c in range(DN_TB // DN_C):
        idx = [c * DN_HEADS + h for h in range(DN_HEADS)]
        s_old = [s_ref[h] for h in range(DN_HEADS)]
        rs = [_dot(jnp.concatenate([ws[n][:, DN_HEAD_DIM:].astype(BF16), qd_scr[slot, n]], axis=0),
                   s.astype(BF16)) for n, s in zip(idx, s_old)]
        yield
        vnb = [(ws[n][:, 0:DN_HEAD_DIM] - r[0:DN_C]).astype(BF16) for n, r in zip(idx, rs)]
        outs = [r[DN_C:2 * DN_C] + _dot(att_scr[slot, n], vn) for n, r, vn in zip(idx, rs, vnb)]
        yield
        for h, (n, vn) in enumerate(zip(idx, vnb)):
            s_ref[h] = s_old[h] * edl_scr[slot, n, 0:1, :] + _dot_tn(kt_scr[slot, n], vn)
        yield
        rows = slice(out_row0 + c * DN_C, out_row0 + (c + 1) * DN_C)
        for h, o in enumerate(outs):
            hs = slice(h * DN_HEAD_DIM, (h + 1) * DN_HEAD_DIM)
            ms = jnp.mean(o * o, axis=-1, keepdims=True)
            gt = gate_ref[rows, hs].astype(F32)
            o_ref[rows, hs] = (o * lax.rsqrt(ms + EPS) * og * (gt * _sigmoid(gt))).astype(BF16)


def _ffn_stages(x_ref, oa_ref, od_ref, wo_ref, fg_ref, w1_ref, w2_ref, o_ref):
    d = x_ref.shape[1]
    oa = oa_ref[...]
    od = od_ref[...]
    mixed = []
    for j in range(d // FFN_TN):
        cols = slice(j * FFN_TN, (j + 1) * FFN_TN)
        mixed.append(_dot(oa, wo_ref[0:ATT_WIDTH, cols]) + _dot(od, wo_ref[ATT_WIDTH:, cols]))
        yield
    h1 = x_ref[...] + jnp.concatenate(mixed, axis=1)
    ms = jnp.mean(h1 * h1, axis=-1, keepdims=True)
    n = (h1 * lax.rsqrt(ms + EPS) * fg_ref[...]).astype(BF16)

    def up(c, out):
        for j in range(FFN_TF // FFN_TN):
            col0 = c * FFN_TF + j * FFN_TN
            out.append(_dot(n, w1_ref[:, col0:col0 + FFN_TN]))
            yield

    n_chunks = w1_ref.shape[1] // FFN_TF
    acc = h1
    z_next = []
    yield from up(0, z_next)
    for c in range(n_chunks):
        z = z_next
        if c + 1 < n_chunks:
            z_next = []
            yield from up(c + 1, z_next)
        act = jnp.concatenate([jnp.square(jnp.maximum(piece, 0.0)).astype(BF16) for piece in z], axis=1)
        down = []
        for j in range(d // FFN_TN):
            down.append(_dot(act, w2_ref[c * FFN_TF:(c + 1) * FFN_TF, j * FFN_TN:(j + 1) * FFN_TN]))
            yield
        acc = acc + jnp.concatenate(down, axis=1)
    o_ref[...] = acc


def _gdn_ffn_kernel(x0_ref, x1_ref, x2_ref, ab0_ref, ab1_ref, ab2_ref, abt0_ref, abt1_ref, abt2_ref,
                    gate_ref, ltri_ref, utri_ref, alog_ref, dtb_ref, alogt_ref, dtbt_ref, dn_og_ref,
                    xres_ref, oa_ref, wo_ref, fg_ref, w1_ref, w2_ref, o_ref, s_ref, od_scr, *local_scr,
                    steps_per_seq):
    s = pl.program_id(0)
    cur = s % 2
    consts = (ltri_ref, utri_ref, alog_ref, dtb_ref, alogt_ref, dtbt_ref)

    def local(x_ref, ab_ref, abt_ref, slot):
        return _gdn_local(x_ref, ab_ref, abt_ref, *consts, slot, *local_scr)

    def state(slot, out_row0):
        return _gdn_state(slot, gate_ref, dn_og_ref, od_scr.at[cur], out_row0, s_ref, *local_scr)

    @pl.when(s == 0)
    def _():
        od_scr[1] = jnp.zeros(od_scr.shape[1:], od_scr.dtype)

    @pl.when(s % steps_per_seq == 0)
    def _():
        s_ref[...] = jnp.zeros_like(s_ref)
        _run(local(x0_ref, ab0_ref, abt0_ref, 0))

    deltanet = _chain(
        _weave(state(0, 0), local(x1_ref, ab1_ref, abt1_ref, 1), DN_STATE_PER_LOCAL),
        _weave(state(1, DN_TB), local(x2_ref, ab2_ref, abt2_ref, 0), DN_STATE_PER_LOCAL))
    ffn = _ffn_stages(xres_ref, oa_ref, od_scr.at[1 - cur], wo_ref, fg_ref, w1_ref, w2_ref, o_ref)
    _run(_weave(deltanet, ffn, 1))


def _deltanet_ffn(xf, main, ab, abt, o_att, alog, dtb, alogt, dtbt, dn_og, w_out, fg, w1, w2, seq):
    t, d = xf.shape
    dff = w1.shape[1]
    assert FFN_TM == 2 * DN_TB
    nb = seq // DN_TB
    spq = nb // 2
    steps = t // FFN_TM
    tile = lambda s: jnp.minimum(s, steps - 1)
    prev = lambda s: jnp.maximum(s - 1, 0)
    seq_blk0 = lambda s: tile(s) // spq * nb
    first = seq_blk0
    odd = lambda s: 2 * tile(s) + 1
    nxt = lambda s: jnp.minimum(2 * tile(s) + 2, seq_blk0(s) + nb - 1)
    const = lambda s: (0, 0)

    def x_spec(blk):
        return pl.BlockSpec((DN_TB, 3 * DN_WIDTH), lambda s: (blk(s), 1))

    def ab_spec(blk):
        return pl.BlockSpec((DN_TB, AB_PAD), lambda s: (blk(s), 0))

    def abt_spec(blk):
        return pl.BlockSpec((AB_PAD, DN_TB), lambda s: (0, blk(s)))

    r = np.arange(DN_TB)
    same = (r[:, None] // DN_C) == (r[None, :] // DN_C)
    ltri = jnp.asarray(same & (r[None, :] <= r[:, None]), BF16)
    utri = jnp.asarray(same & (r[:, None] <= r[None, :]), BF16)
    item = lambda *shape: (2, DN_ITEMS) + shape
    resident = dict(pipeline_mode=pl.Buffered(1))
    return pl.pallas_call(
        functools.partial(_gdn_ffn_kernel, steps_per_seq=spq),
        grid=(steps + 1,),
        in_specs=[
            x_spec(first), x_spec(odd), x_spec(nxt),
            ab_spec(first), ab_spec(odd), ab_spec(nxt),
            abt_spec(first), abt_spec(odd), abt_spec(nxt),
            pl.BlockSpec((2 * DN_TB, DN_WIDTH), lambda s: (tile(s), 6)),
            pl.BlockSpec((DN_TB, DN_TB), const),
            pl.BlockSpec((DN_TB, DN_TB), const),
            pl.BlockSpec((1, AB_PAD), const),
            pl.BlockSpec((1, AB_PAD), const),
            pl.BlockSpec((8, 1), const),
            pl.BlockSpec((8, 1), const),
            pl.BlockSpec((1, DN_HEAD_DIM), const),
            pl.BlockSpec((FFN_TM, d), lambda s: (prev(s), 0)),
            pl.BlockSpec((FFN_TM, ATT_WIDTH), lambda s: (prev(s), 0)),
            pl.BlockSpec((d, d), const, **resident),
            pl.BlockSpec((1, d), const),
            pl.BlockSpec((d, dff), const, **resident),
            pl.BlockSpec((dff, d), const, **resident),
        ],
        out_specs=pl.BlockSpec((FFN_TM, d), lambda s: (prev(s), 0)),
        out_shape=jax.ShapeDtypeStruct((t, d), F32),
        scratch_shapes=[
            pltpu.VMEM((DN_HEADS, DN_HEAD_DIM, DN_HEAD_DIM), F32),
            pltpu.VMEM((2, FFN_TM, DN_WIDTH), BF16),
            pltpu.VMEM(item(DN_C, DN_C), F32),
            pltpu.VMEM(item(DN_C, DN_C), BF16),
            pltpu.VMEM(item(DN_C, 2 * DN_HEAD_DIM), BF16),
            pltpu.VMEM(item(DN_C, DN_HEAD_DIM), BF16),
            pltpu.VMEM(item(DN_C, DN_HEAD_DIM), BF16),
            pltpu.VMEM(item(8, DN_HEAD_DIM), F32),
        ],
        compiler_params=pltpu.CompilerParams(
            dimension_semantics=("arbitrary",), vmem_limit_bytes=VMEM_LIMIT),
        name="deltanet_ffn",
    )(main, main, main, ab, ab, ab, abt, abt, abt, main, ltri, utri, alog, dtb, alogt, dtbt, dn_og,
      xf, o_att, w_out, fg, w1, w2)


def _layer(h, p, batch, seq):
    w_in = p["w_in"]
    w_main = w_in.astype(F32)[None]
    w_ab = jnp.pad(w_in[:, MAIN_COLS:].astype(F32), ((0, 0), (0, AB_PAD - 2 * DN_HEADS)))
    head_of = np.arange(ATT_WIDTH) // ATT_HEAD_DIM
    seg = jnp.asarray((head_of[:, None] == head_of[None, :]) / ATT_HEAD_DIM, BF16)
    qg = (jnp.tile(p["att_q_gain"].astype(F32), ATT_HEADS) * (ATT_HEAD_DIM ** -0.5 * LOG2_E))[None]
    kg = jnp.tile(p["att_k_gain"].astype(F32), ATT_HEADS)[None]
    (main, ab, abt), (w_out, w_ff1, w_ff2) = _inproj(
        h, p["mix_norm_gain"].astype(F32)[None], w_main, w_ab, qg, kg, seg, p["dn_conv_w"].astype(F32), seq,
        [p["w_out"].astype(F32), p["w_ff1"].astype(F32), p["w_ff2"].astype(F32)])

    o_att = _attention(main, _bias_signal(p["rel_bias"]),
                       p["att_out_gain"].astype(F32)[None], batch, seq)

    pad_row = lambda v: jnp.pad(v.astype(F32), (0, AB_PAD - DN_HEADS))[None]
    pad_col = lambda v: jnp.pad(v.astype(F32), (0, 8 - DN_HEADS))[:, None]
    return _deltanet_ffn(h, main, ab, abt, o_att, pad_row(p["dn_a_log"]), pad_row(p["dn_dt_bias"]),
                         pad_col(p["dn_a_log"]), pad_col(p["dn_dt_bias"]),
                         p["dn_out_gain"].astype(F32)[None], w_out, p["ffn_norm_gain"].astype(F32)[None],
                         w_ff1, w_ff2, seq)


def kernel(x, mix_norm_gain, w_in, att_q_gain, att_k_gain, rel_bias, att_out_gain, dn_conv_w,
           dn_a_log, dn_dt_bias, dn_out_gain, w_out, ffn_norm_gain, w_ff1, w_ff2):
    batch, seq, d = x.shape
    assert seq % max(ATT_TQ, 2 * DN_TB) == 0 and (batch * seq) % max(IN_TM, FFN_TM) == 0
    params = dict(mix_norm_gain=mix_norm_gain, w_in=w_in, att_q_gain=att_q_gain,
                  att_k_gain=att_k_gain, rel_bias=rel_bias, att_out_gain=att_out_gain,
                  dn_conv_w=dn_conv_w, dn_a_log=dn_a_log, dn_dt_bias=dn_dt_bias,
                  dn_out_gain=dn_out_gain, w_out=w_out, ffn_norm_gain=ffn_norm_gain,
                  w_ff1=w_ff1, w_ff2=w_ff2)
    h = x.reshape(batch * seq, d)
    for layer in range(w_in.shape[0]):
        h = _layer(h, {name: v[layer] for name, v in params.items()}, batch, seq)
    return h.reshape(batch, seq, d)
```

```python
import functools

import numpy as np
import jax
import jax.numpy as jnp
from jax import lax
from jax.experimental import pallas as pl
from jax.experimental.pallas import tpu as pltpu

F32 = jnp.float32
BF16 = jnp.bfloat16

CHUNK = 64
N_PAST_CHUNKS = 8
ATT_HEADS = 8
ATT_HEAD_DIM = 64
ATT_WIDTH = ATT_HEADS * ATT_HEAD_DIM
MAX_REL = 256
DN_HEADS = 4
DN_HEAD_DIM = 128
DN_WIDTH = DN_HEADS * DN_HEAD_DIM
CONV_K = 4
EPS = 1e-6

MAIN_COLS = 3 * ATT_WIDTH + 4 * DN_WIDTH
AB_PAD = 128

LANES = 128
VMEM_LIMIT = 56 * 1024 * 1024

IN_TM = 512
IN_TN = 512
ATT_TQ = 4 * CHUNK
ATT_WIN = ATT_TQ + N_PAST_CHUNKS * CHUNK
DN_C = 128
DN_TB = 2 * DN_C
DN_TAIL = 8
DN_ITEMS = (DN_TB // DN_C) * DN_HEADS
DN_CONV_FIRST = 3 * ATT_WIDTH // IN_TN
DN_CONV_CHUNKS = 3 * DN_WIDTH // IN_TN
DN_STATE_PER_LOCAL = 4
FFN_TM = 512
FFN_TF = 1024
FFN_TN = 256
NEG_BIG = -1e30
LOG2_E = 1.4426950408889634


def _dot(a, b):
    return jnp.dot(a, b, preferred_element_type=F32)


def _dot_nt(a, b):
    return lax.dot_general(a, b, (((1,), (1,)), ((), ())), preferred_element_type=F32)


def _dot_tn(a, b):
    return lax.dot_general(a, b, (((0,), (0,)), ((), ())), preferred_element_type=F32)


def _split3(a):
    p1 = a.astype(BF16)
    r1 = a - p1.astype(F32)
    p2 = r1.astype(BF16)
    p3 = (r1 - p2.astype(F32)).astype(BF16)
    return p1, p2, p3


def _sigmoid(x):
    return 1.0 / (1.0 + jnp.exp2(x * -LOG2_E))


def _softplus(x):
    return jnp.maximum(x, 0.0) + jnp.log(1.0 + jnp.exp(-jnp.abs(x)))


def _inproj_kernel(x_ref, g_ref, wf_ref, wab_ref, qg_ref, kg_ref, seg_ref, cw_ref, *rest, tiles_per_seq):
    n_cast = (len(rest) - 5) // 2
    cast_in, (main_ref, ab_ref, abt_ref) = rest[:n_cast], rest[n_cast:n_cast + 3]
    cast_out, (stage_ref, wm_ref) = rest[n_cast + 3:2 * n_cast + 3], rest[2 * n_cast + 3:]
    i = pl.program_id(0)

    @pl.when(i == 0)
    def _():
        for c in range(MAIN_COLS // IN_TN):
            cols = slice(c * IN_TN, (c + 1) * IN_TN)
            wm_ref[:, cols] = wf_ref[cols, :].T.astype(BF16)

    for src, dst in zip(cast_in, cast_out):
        dst[...] = src[...].astype(BF16)

    @pl.when(i % tiles_per_seq == 0)
    def _():
        stage_ref[...] = jnp.zeros_like(stage_ref)

    x = x_ref[...]
    ms = jnp.mean(x * x, axis=-1, keepdims=True)
    u = (x * lax.rsqrt(ms + EPS) * g_ref[...]).astype(BF16)
    conv = list(range(DN_CONV_FIRST, DN_CONV_FIRST + DN_CONV_CHUNKS))
    plain = [c for c in range(MAIN_COLS // IN_TN) if c not in conv]
    order = [c for pair in zip(plain, conv) for c in pair] + plain[len(conv):]
    project = lambda c: _dot(u, wm_ref[:, c * IN_TN:(c + 1) * IN_TN])
    y_next = project(order[0])
    for n, c in enumerate(order):
        cols = slice(c * IN_TN, (c + 1) * IN_TN)
        y = y_next
        if n + 1 < len(order):
            y_next = project(order[n + 1])
        if c < 2:
            gain = qg_ref if c == 0 else kg_ref
            m = _dot((y * y).astype(BF16), seg_ref[...])
            y = y * lax.rsqrt(m + EPS) * gain[...]
        elif DN_CONV_FIRST <= c < DN_CONV_FIRST + DN_CONV_CHUNKS:
            j = c - DN_CONV_FIRST
            wcols = slice(j * IN_TN, (j + 1) * IN_TN)
            ext = jnp.concatenate([stage_ref[j], y], axis=0)
            z = cw_ref[0:1, wcols] * ext
            for t in range(1, CONV_K):
                z = cw_ref[t:t + 1, wcols] * ext + pltpu.roll(z, 1, axis=0)
            z = z[DN_TAIL:]
            stage_ref[j] = y[IN_TM - DN_TAIL:IN_TM, :]
            y = z * _sigmoid(z)
        main_ref[:, cols] = y.astype(BF16)
    ab = _dot(u, wab_ref[...].astype(BF16))
    ab_ref[...] = ab
    abt_ref[...] = ab.T


def _inproj(xf, gain, w_main, w_ab, qg, kg, seg, conv_w, seq, to_cast):
    t, d = xf.shape
    steps = t // IN_TM
    const = lambda i: (0, 0)
    slab = lambda w: pl.BlockSpec((w.shape[0] // steps, w.shape[1]), lambda i: (i, 0))
    outs = pl.pallas_call(
        functools.partial(_inproj_kernel, tiles_per_seq=seq // IN_TM),
        grid=(steps,),
        in_specs=[
            pl.BlockSpec((IN_TM, d), lambda i: (i, 0)),
            pl.BlockSpec((1, d), const),
            pl.BlockSpec((None, MAIN_COLS, d), lambda i: (0, 0, 0), pipeline_mode=pl.Buffered(1)),
            pl.BlockSpec((d, AB_PAD), const),
            pl.BlockSpec((1, ATT_WIDTH), const),
            pl.BlockSpec((1, ATT_WIDTH), const),
            pl.BlockSpec((ATT_WIDTH, ATT_WIDTH), const),
            pl.BlockSpec((CONV_K, 3 * DN_WIDTH), const),
        ] + [slab(w) for w in to_cast],
        out_specs=[
            pl.BlockSpec((IN_TM, MAIN_COLS), lambda i: (i, 0)),
            pl.BlockSpec((IN_TM, AB_PAD), lambda i: (i, 0)),
            pl.BlockSpec((AB_PAD, IN_TM), lambda i: (0, i)),
        ] + [slab(w) for w in to_cast],
        out_shape=[
            jax.ShapeDtypeStruct((t, MAIN_COLS), BF16),
            jax.ShapeDtypeStruct((t, AB_PAD), F32),
            jax.ShapeDtypeStruct((AB_PAD, t), F32),
        ] + [jax.ShapeDtypeStruct(w.shape, BF16) for w in to_cast],
        scratch_shapes=[
            pltpu.VMEM((DN_CONV_CHUNKS, DN_TAIL, IN_TN), F32),
            pltpu.VMEM((d, MAIN_COLS), BF16),
        ],
        compiler_params=pltpu.CompilerParams(
            dimension_semantics=("arbitrary",), vmem_limit_bytes=VMEM_LIMIT),
        name="inproj",
    )(xf, gain, w_main, w_ab, qg, kg, seg, conv_w, *to_cast)
    return outs[:3], outs[3:]


def _attn_tile(q_ref, row0, k_refs, v_refs, bias_ref, start_mask, og_ref, o_ref):
    lane = lax.broadcasted_iota(jnp.int32, (1, LANES), 1)
    per_tile = LANES // ATT_HEAD_DIM
    rows = slice(row0, row0 + ATT_TQ)

    def mine(h):
        e = h % per_tile
        return (lane >= e * ATT_HEAD_DIM) & (lane < (e + 1) * ATT_HEAD_DIM)

    def window(refs, h):
        sl = slice(h // per_tile * LANES, (h // per_tile + 1) * LANES)
        return jnp.concatenate([r[:, sl] for r in refs], axis=0)

    def scores(h):
        qp = q_ref[rows, h // per_tile * LANES:(h // per_tile + 1) * LANES]
        q_e = jnp.where(mine(h), qp, jnp.zeros_like(qp))
        bias = bias_ref[h] if start_mask is None else bias_ref[h] + start_mask
        return _dot_nt(q_e, window(k_refs, h)) + bias

    def attend(h, s):
        mx = jnp.max(s, axis=-1, keepdims=True)
        pe = jnp.exp2(s - mx).astype(BF16)
        vw = window(v_refs, h)
        o_e = _dot(pe, jnp.where(mine(h), vw, jnp.ones_like(vw)))
        return o_e / pltpu.roll(o_e, ATT_HEAD_DIM, axis=1)

    normed = []
    s_next = scores(0)
    for h in range(ATT_HEADS):
        s_cur = s_next
        if h + 1 < ATT_HEADS:
            s_next = scores(h + 1)
        normed.append(attend(h, s_cur))
        yield
    outs = [jnp.where(lane < ATT_HEAD_DIM, normed[t * per_tile], normed[t * per_tile + 1])
            for t in range(ATT_WIDTH // LANES)]
    o = jnp.concatenate(outs, axis=1)
    ms = jnp.mean(o * o, axis=-1, keepdims=True)
    o_ref[rows, :] = (o * lax.rsqrt(ms + EPS) * og_ref[...]).astype(BF16)


def _attn_kernel(q_ref, k0_ref, k1_ref, k2_ref, v0_ref, v1_ref, v2_ref, sig_ref, og_ref, o_ref, bias_ref):
    i = pl.program_id(1)

    @pl.when((pl.program_id(0) == 0) & (i == 0))
    def _():
        r = lax.broadcasted_iota(jnp.int32, (ATT_TQ, ATT_WIN), 0)
        s = lax.broadcasted_iota(jnp.int32, (ATT_TQ, ATT_WIN), 1)
        back = (s >> 6) - (r >> 6)
        in_band = (back >= 0) & (back <= N_PAST_CHUNKS)
        for h in range(ATT_HEADS):
            rows = jnp.broadcast_to(sig_ref[h:h + 1, :], (ATT_TQ, sig_ref.shape[1]))
            toep = pltpu.roll(rows, 0, 1, stride=1, stride_axis=0)
            bias_ref[h] = jnp.where(in_band, toep[:, 0:ATT_WIN], NEG_BIG)

    def tile(start_mask):
        _run(_attn_tile(q_ref, 0, (k0_ref, k1_ref, k2_ref), (v0_ref, v1_ref, v2_ref), bias_ref,
                        start_mask, og_ref, o_ref))

    tiles_in_past = N_PAST_CHUNKS * CHUNK // ATT_TQ

    @pl.when(i < tiles_in_past)
    def _():
        jchunk = lax.broadcasted_iota(jnp.int32, (1, ATT_WIN), 1) >> 6
        tile(jnp.where(jchunk + (ATT_TQ // CHUNK) * i - N_PAST_CHUNKS >= 0, 0.0, NEG_BIG))

    @pl.when(i >= tiles_in_past)
    def _():
        tile(None)


def _attention(main, signal, og, batch, seq):
    nb = seq // ATT_TQ
    blk = (ATT_TQ, ATT_WIDTH)

    def kv_spec(col, back):
        return pl.BlockSpec(blk, lambda b, i: (b * nb + jnp.maximum(i - back, 0), col))

    return pl.pallas_call(
        _attn_kernel,
        grid=(batch, nb),
        in_specs=[
            pl.BlockSpec(blk, lambda b, i: (b * nb + i, 0)),
            kv_spec(1, 2), kv_spec(1, 1), kv_spec(1, 0),
            kv_spec(2, 2), kv_spec(2, 1), kv_spec(2, 0),
            pl.BlockSpec(signal.shape, lambda b, i: (0, 0)),
            pl.BlockSpec((1, ATT_WIDTH), lambda b, i: (0, 0)),
        ],
        out_specs=pl.BlockSpec(blk, lambda b, i: (b * nb + i, 0)),
        out_shape=jax.ShapeDtypeStruct((batch * seq, ATT_WIDTH), BF16),
        scratch_shapes=[pltpu.VMEM((ATT_HEADS, ATT_TQ, ATT_WIN), F32)],
        compiler_params=pltpu.CompilerParams(
            dimension_semantics=("arbitrary", "arbitrary"), vmem_limit_bytes=VMEM_LIMIT),
        name="band_attention",
    )(main, main, main, main, main, main, main, signal, og)


def _bias_signal(rel_bias):
    nh = rel_bias.shape[0]
    lo = N_PAST_CHUNKS * CHUNK - (ATT_WIN - 1) + MAX_REL
    n_dist = ATT_TQ + ATT_WIN - 1
    n_tail = n_dist - (2 * MAX_REL + 1 - lo)
    rb = rel_bias.astype(F32) * LOG2_E
    by_dist = jnp.concatenate([rb[:, lo:], jnp.broadcast_to(rb[:, -1:], (nh, n_tail))], axis=1)
    period = ATT_TQ + ATT_WIN
    return jnp.roll(jnp.pad(by_dist[:, ::-1], ((0, 0), (0, period - n_dist))), -(ATT_TQ - 1), axis=1)


def _weave(main, side, every):
    n = 0
    done = object()
    while True:
        if n % every == 0:
            next(side, done)
        if next(main, done) is done:
            break
        n += 1
        yield
    _run(side)


def _chain(*stages):
    for g in stages:
        yield from g


def _run(stage):
    for _ in stage:
        pass


def _inv_unit_lower(mats, ri, ci, out):
    eye = jnp.where(ri == ci, 1.0, 0.0)
    diag8 = (ri >> 3) == (ci >> 3)
    n1 = [jnp.where(diag8, -a, 0.0) for a in mats]
    n1b = [n.astype(BF16) for n in n1]
    n2b = [_dot(n, n).astype(BF16) for n in n1b]
    yield
    n4b = [_dot(n, n).astype(BF16) for n in n2b]
    ts = [eye + n for n in n1]
    yield
    ts = [t + _dot(t.astype(BF16), n) for t, n in zip(ts, n2b)]
    yield
    ts = [t + _dot(t.astype(BF16), n) for t, n in zip(ts, n4b)]
    yield
    half = DN_C // 2
    r_h = lax.broadcasted_iota(jnp.int32, (half, DN_C), 0)
    c_o = lax.broadcasted_iota(jnp.int32, (half, DN_C), 1)
    for sh in range(3, 7):
        blk = 1 << sh
        odd = [slice(s0, s0 + blk) for s0 in range(blk, DN_C, 2 * blk)]
        even = [slice(s0, s0 + blk) for s0 in range(0, DN_C, 2 * blk)]
        take = lambda m, parts: jnp.concatenate([m[p] for p in parts], axis=0)
        r_o = r_h + (((r_h >> sh) + 1) << sh)
        pair =((r_o >> (sh + 1)) == (c_o >> (sh + 1))) & ((r_o >> sh) != (c_o >> sh))
        tb = [t.astype(BF16) for t in ts]
        lt = [_dot(jnp.where(pair, take(a, odd), 0.0).astype(BF16), t) for a, t in zip(mats, tb)]
        yield
        zero = jnp.zeros((blk, DN_C), F32)
        lt = [jnp.concatenate([p for m in range(half // blk) for p in (zero, x[m * blk:(m + 1) * blk])],
                              axis=0).astype(BF16) for x in lt]
        t_odd = [take(t, odd) for t in ts]
        new = [t - _dot(t.astype(BF16), x) for t, x in zip(t_odd, lt)]
        ts = [jnp.concatenate([p for m in range(half // blk)
                               for p in (t[even[m]], n[m * blk:(m + 1) * blk])], axis=0)
              for t, n in zip(ts, new)]
        yield
    out.extend(ts)


def _gdn_local(x_ref, ab_ref, abt_ref, ltri_ref, utri_ref, alog_ref, dtb_ref, alogt_ref, dtbt_ref, slot,
               a_scr, att_scr, rhs_scr, qd_scr, kt_scr, edl_scr):
    ab = ab_ref[...]
    g_col = -jnp.exp(alog_ref[...]) * _softplus(ab + dtb_ref[...])
    beta_col = _sigmoid(ab)
    g_row = -jnp.exp(alogt_ref[...]) * _softplus(abt_ref[0:8, :] + dtbt_ref[...])
    gc1, gc2, gc3 = _split3(g_col)
    gr1, gr2, gr3 = _split3(g_row)

    y = x_ref[...].astype(F32)

    ones = jnp.ones((DN_HEAD_DIM, DN_HEAD_DIM), BF16)
    sq = [(t * t).astype(BF16) for t in (y[:, g * DN_HEAD_DIM:(g + 1) * DN_HEAD_DIM]
                                         for g in range(2 * DN_HEADS))]
    ssq = [_dot(t, ones) for t in sq]
    yield
    ltri = ltri_ref[...]
    utri = utri_ref[...]
    d_col = _dot(ltri, gc1) + (_dot(ltri, gc2) + _dot(ltri, gc3))
    d_row = _dot(gr1, utri) + (_dot(gr2, utri) + _dot(gr3, utri))
    qk = []
    for g, tot in enumerate(ssq):
        inv = lax.rsqrt(tot + EPS)
        qk.append(y[:, g * DN_HEAD_DIM:(g + 1) * DN_HEAD_DIM]
                  * (inv * DN_HEAD_DIM ** -0.5 if g < DN_HEADS else inv))

    ri = lax.broadcasted_iota(jnp.int32, (DN_C, DN_C), 0)
    ci = lax.broadcasted_iota(jnp.int32, (DN_C, DN_C), 1)
    lmasks, kbqs, kbfs = [], [], []
    for n in range(DN_ITEMS):
        c, h = divmod(n, DN_HEADS)
        rows = slice(c * DN_C, (c + 1) * DN_C)
        q = qk[h][rows]
        k = qk[DN_HEADS + h][rows]
        v = y[rows, 2 * DN_WIDTH + h * DN_HEAD_DIM:2 * DN_WIDTH + (h + 1) * DN_HEAD_DIM]
        dc = d_col[rows, h:h + 1]
        dr = d_row[h:h + 1, rows]
        dl = dr[:, DN_C - 1:DN_C]
        bc = beta_col[rows, DN_HEADS + h:DN_HEADS + h + 1]
        kb = k * bc
        edc = jnp.exp(dc)
        rhs_scr[slot, n] = jnp.concatenate([v * bc, kb * edc], axis=1).astype(BF16)
        qd_scr[slot, n] = (q * edc).astype(BF16)
        kt_scr[slot, n] = (k * jnp.exp(dl - dc)).astype(BF16)
        edl_scr[slot, n] = jnp.broadcast_to(jnp.exp(dl), (8, DN_HEAD_DIM))
        lmasks.append(jnp.exp(jnp.where(ci <= ri, dc - dr, -jnp.inf)))
        kbqs.append(jnp.concatenate([kb, q], axis=0).astype(BF16))
        kbfs.append(k.astype(BF16))
    yield
    ps =[_dot_nt(kbq, kbf) for kbq, kbf in zip(kbqs, kbfs)]
    for n, (p, lmask) in enumerate(zip(ps, lmasks)):
        a_scr[slot, n] = jnp.where(ci < ri, p[0:DN_C] * lmask, 0.0)
        att_scr[slot, n] = (p[DN_C:2 * DN_C] * lmask).astype(BF16)


def _gdn_state(slot, gate_ref, og_ref, o_ref, out_row0, s_ref, a_scr, att_scr, rhs_scr, qd_scr, kt_scr,
               edl_scr):
    ri = lax.broadcasted_iota(jnp.int32, (DN_C, DN_C), 0)
    ci = lax.broadcasted_iota(jnp.int32, (DN_C, DN_C), 1)
    og = og_ref[...]
    tinvs = []
    yield from _inv_unit_lower([a_scr[slot, n] for n in range(DN_ITEMS)], ri, ci, tinvs)
    ws = [_dot(t.astype(BF16), rhs_scr[slot, n]) for n, t in enumerate(tinvs)]
    yield
    for c in range(DN_TB // DN_C):
        idx = [c * DN_HEADS + h for h in range(DN_HEADS)]
        s_old = [s_ref[h] for h in range(DN_HEADS)]
        rs = [_dot(jnp.concatenate([ws[n][:, DN_HEAD_DIM:].astype(BF16), qd_scr[slot, n]], axis=0),
                   s.astype(BF16)) for n, s in zip(idx, s_old)]
        yield
        vnb = [(ws[n][:, 0:DN_HEAD_DIM] - r[0:DN_C]).astype(BF16) for n, r in zip(idx, rs)]
        outs = [r[DN_C:2 * DN_C] + _dot(att_scr[slot, n], vn) for n, r, vn in zip(idx, rs, vnb)]
        yield
        for h, (n, vn) in enumerate(zip(idx, vnb)):
            s_ref[h] = s_old[h] * edl_scr[slot, n, 0:1, :] + _dot_tn(kt_scr[slot, n], vn)
        yield
        rows = slice(out_row0 + c * DN_C, out_row0 + (c + 1) * DN_C)
        for h, o in enumerate(outs):
            hs = slice(h * DN_HEAD_DIM, (h + 1) * DN_HEAD_DIM)
            ms = jnp.mean(o * o, axis=-1, keepdims=True)
            gt = gate_ref[rows, hs].astype(F32)
            o_ref[rows, hs] = (o * lax.rsqrt(ms + EPS) * og * (gt * _sigmoid(gt))).astype(BF16)


def _ffn_stages(x_ref, oa_ref, od_ref, wo_ref, fg_ref, w1_ref, w2_ref, o_ref):
    d = x_ref.shape[1]
    oa = oa_ref[...]
    od = od_ref[...]
    mixed = []
    for j in range(d // FFN_TN):
        cols = slice(j * FFN_TN, (j + 1) * FFN_TN)
        mixed.append(_dot(oa, wo_ref[0:ATT_WIDTH, cols]) + _dot(od, wo_ref[ATT_WIDTH:, cols]))
        yield
    h1 = x_ref[...] + jnp.concatenate(mixed, axis=1)
    ms = jnp.mean(h1 * h1, axis=-1, keepdims=True)
    n = (h1 * lax.rsqrt(ms + EPS) * fg_ref[...]).astype(BF16)

    def up(c, out):
        for j in range(FFN_TF // FFN_TN):
            col0 = c * FFN_TF + j * FFN_TN
            out.append(_dot(n, w1_ref[:, col0:col0 + FFN_TN]))
            yield

    n_chunks = w1_ref.shape[1] // FFN_TF
    acc = h1
    z_next = []
    yield from up(0, z_next)
    for c in range(n_chunks):
        z = z_next
        if c + 1 < n_chunks:
            z_next = []
            yield from up(c + 1, z_next)
        act = jnp.concatenate([jnp.square(jnp.maximum(piece, 0.0)).astype(BF16) for piece in z], axis=1)
        down = []
        for j in range(d // FFN_TN):
            down.append(_dot(act, w2_ref[c * FFN_TF:(c + 1) * FFN_TF, j * FFN_TN:(j + 1) * FFN_TN]))
            yield
        acc = acc + jnp.concatenate(down, axis=1)
    o_ref[...] = acc


def _gdn_ffn_kernel(x0_ref, x1_ref, x2_ref, ab0_ref, ab1_ref, ab2_ref, abt0_ref, abt1_ref, abt2_ref,
                    gate_ref, ltri_ref, utri_ref, alog_ref, dtb_ref, alogt_ref, dtbt_ref, dn_og_ref,
                    xres_ref, oa_ref, wo_ref, fg_ref, w1_ref, w2_ref, o_ref, s_ref, od_scr, *local_scr,
                    steps_per_seq):
    s = pl.program_id(0)
    cur = s % 2
    consts = (ltri_ref, utri_ref, alog_ref, dtb_ref, alogt_ref, dtbt_ref)

    def local(x_ref, ab_ref, abt_ref, slot):
        return _gdn_local(x_ref, ab_ref, abt_ref, *consts, slot, *local_scr)

    def state(slot, out_row0):
        return _gdn_state(slot, gate_ref, dn_og_ref, od_scr.at[cur], out_row0, s_ref, *local_scr)

    @pl.when(s == 0)
    def _():
        od_scr[1] = jnp.zeros(od_scr.shape[1:], od_scr.dtype)

    @pl.when(s % steps_per_seq == 0)
    def _():
        s_ref[...] = jnp.zeros_like(s_ref)
        _run(local(x0_ref, ab0_ref, abt0_ref, 0))

    deltanet = _chain(
        _weave(state(0, 0), local(x1_ref, ab1_ref, abt1_ref, 1), DN_STATE_PER_LOCAL),
        _weave(state(1, DN_TB), local(x2_ref, ab2_ref, abt2_ref, 0), DN_STATE_PER_LOCAL))
    ffn = _ffn_stages(xres_ref, oa_ref, od_scr.at[1 - cur], wo_ref, fg_ref, w1_ref, w2_ref, o_ref)
    _run(_weave(deltanet, ffn, 1))


def _deltanet_ffn(xf, main, ab, abt, o_att, alog, dtb, alogt, dtbt, dn_og, w_out, fg, w1, w2, seq):
    t, d = xf.shape
    dff = w1.shape[1]
    assert FFN_TM == 2 * DN_TB
    nb = seq // DN_TB
    spq = nb // 2
    steps = t // FFN_TM
    tile = lambda s: jnp.minimum(s, steps - 1)
    prev = lambda s: jnp.maximum(s - 1, 0)
    seq_blk0 = lambda s: tile(s) // spq * nb
    first = seq_blk0
    odd = lambda s: 2 * tile(s) + 1
    nxt = lambda s: jnp.minimum(2 * tile(s) + 2, seq_blk0(s) + nb - 1)
    const = lambda s: (0, 0)

    def x_spec(blk):
        return pl.BlockSpec((DN_TB, 3 * DN_WIDTH), lambda s: (blk(s), 1))

    def ab_spec(blk):
        return pl.BlockSpec((DN_TB, AB_PAD), lambda s: (blk(s), 0))

    def abt_spec(blk):
        return pl.BlockSpec((AB_PAD, DN_TB), lambda s: (0, blk(s)))

    r = np.arange(DN_TB)
    same = (r[:, None] // DN_C) == (r[None, :] // DN_C)
    ltri = jnp.asarray(same & (r[None, :] <= r[:, None]), BF16)
    utri = jnp.asarray(same & (r[:, None] <= r[None, :]), BF16)
    item = lambda *shape: (2, DN_ITEMS) + shape
    resident = dict(pipeline_mode=pl.Buffered(1))
    return pl.pallas_call(
        functools.partial(_gdn_ffn_kernel, steps_per_seq=spq),
        grid=(steps + 1,),
        in_specs=[
            x_spec(first), x_spec(odd), x_spec(nxt),
            ab_spec(first), ab_spec(odd), ab_spec(nxt),
            abt_spec(first), abt_spec(odd), abt_spec(nxt),
            pl.BlockSpec((2 * DN_TB, DN_WIDTH), lambda s: (tile(s), 6)),
            pl.BlockSpec((DN_TB, DN_TB), const),
            pl.BlockSpec((DN_TB, DN_TB), const),
            pl.BlockSpec((1, AB_PAD), const),
            pl.BlockSpec((1, AB_PAD), const),
            pl.BlockSpec((8, 1), const),
            pl.BlockSpec((8, 1), const),
            pl.BlockSpec((1, DN_HEAD_DIM), const),
            pl.BlockSpec((FFN_TM, d), lambda s: (prev(s), 0)),
            pl.BlockSpec((FFN_TM, ATT_WIDTH), lambda s: (prev(s), 0)),
            pl.BlockSpec((d, d), const, **resident),
            pl.BlockSpec((1, d), const),
            pl.BlockSpec((d, dff), const, **resident),
            pl.BlockSpec((dff, d), const, **resident),
        ],
        out_specs=pl.BlockSpec((FFN_TM, d), lambda s: (prev(s), 0)),
        out_shape=jax.ShapeDtypeStruct((t, d), F32),
        scratch_shapes=[
            pltpu.VMEM((DN_HEADS, DN_HEAD_DIM, DN_HEAD_DIM), F32),
            pltpu.VMEM((2, FFN_TM, DN_WIDTH), BF16),
            pltpu.VMEM(item(DN_C, DN_C), F32),
            pltpu.VMEM(item(DN_C, DN_C), BF16),
            pltpu.VMEM(item(DN_C, 2 * DN_HEAD_DIM), BF16),
            pltpu.VMEM(item(DN_C, DN_HEAD_DIM), BF16),
            pltpu.VMEM(item(DN_C, DN_HEAD_DIM), BF16),
            pltpu.VMEM(item(8, DN_HEAD_DIM), F32),
        ],
        compiler_params=pltpu.CompilerParams(
            dimension_semantics=("arbitrary",), vmem_limit_bytes=VMEM_LIMIT),
        name="deltanet_ffn",
    )(main, main, main, ab, ab, ab, abt, abt, abt, main, ltri, utri, alog, dtb, alogt, dtbt, dn_og,
      xf, o_att, w_out, fg, w1, w2)


def _layer(h, p, batch, seq):
    w_in = p["w_in"]
    w_main = w_in.astype(F32).T[None]
    w_ab = jnp.pad(w_in[:, MAIN_COLS:].astype(F32), ((0, 0), (0, AB_PAD - 2 * DN_HEADS)))
    head_of = np.arange(ATT_WIDTH) // ATT_HEAD_DIM
    seg = jnp.asarray((head_of[:, None] == head_of[None, :]) / ATT_HEAD_DIM, BF16)
    qg = (jnp.tile(p["att_q_gain"].astype(F32), ATT_HEADS) * (ATT_HEAD_DIM ** -0.5 * LOG2_E))[None]
    kg = jnp.tile(p["att_k_gain"].astype(F32), ATT_HEADS)[None]
    (main, ab, abt), (w_out, w_ff1, w_ff2) = _inproj(
        h, p["mix_norm_gain"].astype(F32)[None], w_main, w_ab, qg, kg, seg, p["dn_conv_w"].astype(F32), seq,
        [p["w_out"].astype(F32), p["w_ff1"].astype(F32), p["w_ff2"].astype(F32)])

    o_att = _attention(main, _bias_signal(p["rel_bias"]),
                       p["att_out_gain"].astype(F32)[None], batch, seq)

    pad_row = lambda v: jnp.pad(v.astype(F32), (0, AB_PAD - DN_HEADS))[None]
    pad_col = lambda v: jnp.pad(v.astype(F32), (0, 8 - DN_HEADS))[:, None]
    return _deltanet_ffn(h, main, ab, abt, o_att, pad_row(p["dn_a_log"]), pad_row(p["dn_dt_bias"]),
                         pad_col(p["dn_a_log"]), pad_col(p["dn_dt_bias"]),
                         p["dn_out_gain"].astype(F32)[None], w_out, p["ffn_norm_gain"].astype(F32)[None],
                         w_ff1, w_ff2, seq)


def kernel(x, mix_norm_gain, w_in, att_q_gain, att_k_gain, rel_bias, att_out_gain, dn_conv_w,
           dn_a_log, dn_dt_bias, dn_out_gain, w_out, ffn_norm_gain, w_ff1, w_ff2):
    batch, seq, d = x.shape
    assert seq % max(ATT_TQ, 2 * DN_TB) == 0 and (batch * seq) % max(IN_TM, FFN_TM) == 0
    params = dict(mix_norm_gain=mix_norm_gain, w_in=w_in, att_q_gain=att_q_gain,
                  att_k_gain=att_k_gain, rel_bias=rel_bias, att_out_gain=att_out_gain,
                  dn_conv_w=dn_conv_w, dn_a_log=dn_a_log, dn_dt_bias=dn_dt_bias,
                  dn_out_gain=dn_out_gain, w_out=w_out, ffn_norm_gain=ffn_norm_gain,
                  w_ff1=w_ff1, w_ff2=w_ff2)
    h = x.reshape(batch * seq, d)
    for layer in range(w_in.shape[0]):
        h = _layer(h, {name: v[layer] for name, v in params.items()}, batch, seq)
    return h.reshape(batch, seq, d)
```

```python
import functools

import numpy as np
import jax
import jax.numpy as jnp
from jax import lax
from jax.experimental import pallas as pl
from jax.experimental.pallas import tpu as pltpu

F32 = jnp.float32
BF16 = jnp.bfloat16

CHUNK = 64
N_PAST_CHUNKS = 8
ATT_HEADS = 8
ATT_HEAD_DIM = 64
ATT_WIDTH = ATT_HEADS * ATT_HEAD_DIM
MAX_REL = 256
DN_HEADS = 4
DN_HEAD_DIM = 128
DN_WIDTH = DN_HEADS * DN_HEAD_DIM
CONV_K = 4
EPS = 1e-6

MAIN_COLS = 3 * ATT_WIDTH + 4 * DN_WIDTH
AB_PAD = 128

LANES = 128
VMEM_LIMIT = 56 * 1024 * 1024

IN_TM = 512
IN_TN = 512
ATT_TQ = 4 * CHUNK
ATT_WIN = ATT_TQ + N_PAST_CHUNKS * CHUNK
ATT_TILES = 2
DN_C = 128
DN_TB = 2 * DN_C
DN_TAIL = 8
DN_ITEMS = (DN_TB // DN_C) * DN_HEADS
DN_CONV_FIRST = 3 * ATT_WIDTH // IN_TN
DN_CONV_CHUNKS = 3 * DN_WIDTH // IN_TN
DN_STATE_PER_LOCAL = 4
FFN_TM = 512
FFN_TF = 1024
FFN_TN = 256
NEG_BIG = -1e30
LOG2_E = 1.4426950408889634


def _dot(a, b):
    return jnp.dot(a, b, preferred_element_type=F32)


def _dot_nt(a, b):
    return lax.dot_general(a, b, (((1,), (1,)), ((), ())), preferred_element_type=F32)


def _dot_tn(a, b):
    return lax.dot_general(a, b, (((0,), (0,)), ((), ())), preferred_element_type=F32)


def _split3(a):
    p1 = a.astype(BF16)
    r1 = a - p1.astype(F32)
    p2 = r1.astype(BF16)
    p3 = (r1 - p2.astype(F32)).astype(BF16)
    return p1, p2, p3


def _sigmoid(x):
    return 1.0 / (1.0 + jnp.exp2(x * -LOG2_E))


def _softplus(x):
    return jnp.maximum(x, 0.0) + jnp.log(1.0 + jnp.exp(-jnp.abs(x)))


def _inproj_kernel(x_ref, g_ref, wm_ref, wab_ref, qg_ref, kg_ref, seg_ref, cw_ref, *rest, tiles_per_seq):
    n_cast = (len(rest) - 4) // 2
    cast_in, (main_ref, ab_ref, abt_ref) = rest[:n_cast], rest[n_cast:n_cast + 3]
    cast_out, stage_ref = rest[n_cast + 3:2 * n_cast + 3], rest[2 * n_cast + 3]
    i = pl.program_id(0)
    for src, dst in zip(cast_in, cast_out):
        dst[...] = src[...].astype(BF16)

    @pl.when(i % tiles_per_seq == 0)
    def _():
        stage_ref[...] = jnp.zeros_like(stage_ref)

    x = x_ref[...]
    ms = jnp.mean(x * x, axis=-1, keepdims=True)
    u = (x * lax.rsqrt(ms + EPS) * g_ref[...]).astype(BF16)
    conv = list(range(DN_CONV_FIRST, DN_CONV_FIRST + DN_CONV_CHUNKS))
    plain = [c for c in range(MAIN_COLS // IN_TN) if c not in conv]
    order = [c for pair in zip(plain, conv) for c in pair] + plain[len(conv):]
    project = lambda c: _dot(u, wm_ref[:, c * IN_TN:(c + 1) * IN_TN])
    y_next = project(order[0])
    for n, c in enumerate(order):
        cols = slice(c * IN_TN, (c + 1) * IN_TN)
        y = y_next
        if n + 1 < len(order):
            y_next = project(order[n + 1])
        if c < 2:
            gain = qg_ref if c == 0 else kg_ref
            m = _dot((y * y).astype(BF16), seg_ref[...])
            y = y * lax.rsqrt(m + EPS) * gain[...]
        elif DN_CONV_FIRST <= c < DN_CONV_FIRST + DN_CONV_CHUNKS:
            j = c - DN_CONV_FIRST
            wcols = slice(j * IN_TN, (j + 1) * IN_TN)
            ext = jnp.concatenate([stage_ref[j], y], axis=0)
            z = cw_ref[0:1, wcols] * ext
            for t in range(1, CONV_K):
                z = cw_ref[t:t + 1, wcols] * ext + pltpu.roll(z, 1, axis=0)
            z = z[DN_TAIL:]
            stage_ref[j] = y[IN_TM - DN_TAIL:IN_TM, :]
            y = z * _sigmoid(z)
        main_ref[:, cols] = y.astype(BF16)
    ab = _dot(u, wab_ref[...])
    ab_ref[...] = ab
    abt_ref[...] = ab.T


def _inproj(xf, gain, w_main, w_ab, qg, kg, seg, conv_w, seq, to_cast):
    t, d = xf.shape
    steps = t // IN_TM
    const = lambda i: (0, 0)
    slab = lambda w: pl.BlockSpec((w.shape[0] // steps, w.shape[1]), lambda i: (i, 0))
    outs = pl.pallas_call(
        functools.partial(_inproj_kernel, tiles_per_seq=seq // IN_TM),
        grid=(steps,),
        in_specs=[
            pl.BlockSpec((IN_TM, d), lambda i: (i, 0)),
            pl.BlockSpec((1, d), const),
            pl.BlockSpec((d, MAIN_COLS), const),
            pl.BlockSpec((d, AB_PAD), const),
            pl.BlockSpec((1, ATT_WIDTH), const),
            pl.BlockSpec((1, ATT_WIDTH), const),
            pl.BlockSpec((ATT_WIDTH, ATT_WIDTH), const),
            pl.BlockSpec((CONV_K, 3 * DN_WIDTH), const),
        ] + [slab(w) for w in to_cast],
        out_specs=[
            pl.BlockSpec((IN_TM, MAIN_COLS), lambda i: (i, 0)),
            pl.BlockSpec((IN_TM, AB_PAD), lambda i: (i, 0)),
            pl.BlockSpec((AB_PAD, IN_TM), lambda i: (0, i)),
        ] + [slab(w) for w in to_cast],
        out_shape=[
            jax.ShapeDtypeStruct((t, MAIN_COLS), BF16),
            jax.ShapeDtypeStruct((t, AB_PAD), F32),
            jax.ShapeDtypeStruct((AB_PAD, t), F32),
        ] + [jax.ShapeDtypeStruct(w.shape, BF16) for w in to_cast],
        scratch_shapes=[pltpu.VMEM((DN_CONV_CHUNKS, DN_TAIL, IN_TN), F32)],
        compiler_params=pltpu.CompilerParams(
            dimension_semantics=("arbitrary",), vmem_limit_bytes=VMEM_LIMIT),
        name="inproj",
    )(xf, gain, w_main, w_ab, qg, kg, seg, conv_w, *to_cast)
    return outs[:3], outs[3:]


def _attn_tile(q_ref, row0, k_refs, v_refs, bias_ref, start_mask, og_ref, o_ref):
    lane = lax.broadcasted_iota(jnp.int32, (1, LANES), 1)
    per_tile = LANES // ATT_HEAD_DIM
    rows = slice(row0, row0 + ATT_TQ)

    def mine(h):
        e = h % per_tile
        return (lane >= e * ATT_HEAD_DIM) & (lane < (e + 1) * ATT_HEAD_DIM)

    def window(refs, h):
        sl = slice(h // per_tile * LANES, (h // per_tile + 1) * LANES)
        return jnp.concatenate([r[:, sl] for r in refs], axis=0)

    def scores(h):
        qp = q_ref[rows, h // per_tile * LANES:(h // per_tile + 1) * LANES]
        q_e = jnp.where(mine(h), qp, jnp.zeros_like(qp))
        bias = bias_ref[h] if start_mask is None else bias_ref[h] + start_mask
        return _dot_nt(q_e, window(k_refs, h)) + bias

    def attend(h, s):
        mx = jnp.max(s, axis=-1, keepdims=True)
        pe = jnp.exp2(s - mx).astype(BF16)
        vw = window(v_refs, h)
        o_e = _dot(pe, jnp.where(mine(h), vw, jnp.ones_like(vw)))
        return o_e / pltpu.roll(o_e, ATT_HEAD_DIM, axis=1)

    normed = []
    s_next = scores(0)
    for h in range(ATT_HEADS):
        s_cur = s_next
        if h + 1 < ATT_HEADS:
            s_next = scores(h + 1)
        normed.append(attend(h, s_cur))
        yield
    outs = [jnp.where(lane < ATT_HEAD_DIM, normed[t * per_tile], normed[t * per_tile + 1])
            for t in range(ATT_WIDTH // LANES)]
    o = jnp.concatenate(outs, axis=1)
    ms = jnp.mean(o * o, axis=-1, keepdims=True)
    o_ref[rows, :] = (o * lax.rsqrt(ms + EPS) * og_ref[...]).astype(BF16)


def _attn_kernel(q_ref, k0_ref, k1_ref, k2_ref, k3_ref, v0_ref, v1_ref, v2_ref, v3_ref, sig_ref, og_ref, o_ref,
                 bias_ref):
    i = pl.program_id(1)

    @pl.when((pl.program_id(0) == 0) & (i == 0))
    def _():
        r = lax.broadcasted_iota(jnp.int32, (ATT_TQ, ATT_WIN), 0)
        s = lax.broadcasted_iota(jnp.int32, (ATT_TQ, ATT_WIN), 1)
        back = (s >> 6) - (r >> 6)
        in_band = (back >= 0) & (back <= N_PAST_CHUNKS)
        for h in range(ATT_HEADS):
            rows = jnp.broadcast_to(sig_ref[h:h + 1, :], (ATT_TQ, sig_ref.shape[1]))
            toep = pltpu.roll(rows, 0, 1, stride=1, stride_axis=0)
            bias_ref[h] = jnp.where(in_band, toep[:, 0:ATT_WIN], NEG_BIG)

    def tiles(masks):
        _run(_weave(
            _attn_tile(q_ref, 0, (k0_ref, k1_ref, k2_ref), (v0_ref, v1_ref, v2_ref), bias_ref, masks[0],
                       og_ref, o_ref),
            _attn_tile(q_ref, ATT_TQ, (k1_ref, k2_ref, k3_ref), (v1_ref, v2_ref, v3_ref), bias_ref, masks[1],
                       og_ref, o_ref), 1))

    steps_in_past = N_PAST_CHUNKS * CHUNK // (ATT_TILES * ATT_TQ)

    @pl.when(i < steps_in_past)
    def _():
        jchunk = lax.broadcasted_iota(jnp.int32, (1, ATT_WIN), 1) >> 6
        tiles([jnp.where(jchunk + (ATT_TQ // CHUNK) * (ATT_TILES * i + n) - N_PAST_CHUNKS >= 0, 0.0, NEG_BIG)
               for n in range(ATT_TILES)])

    @pl.when(i >= steps_in_past)
    def _():
        tiles([None] * ATT_TILES)


def _attention(main, signal, og, batch, seq):
    nb = seq // ATT_TQ
    steps = nb // ATT_TILES
    blk = (ATT_TQ, ATT_WIDTH)

    def kv_spec(col, j):
        return pl.BlockSpec(blk, lambda b, i: (b * nb + jnp.maximum(ATT_TILES * i - 2 + j, 0), col))

    step_rows = pl.BlockSpec((ATT_TILES * ATT_TQ, ATT_WIDTH), lambda b, i: (b * steps + i, 0))
    return pl.pallas_call(
        _attn_kernel,
        grid=(batch, steps),
        in_specs=[
            step_rows,
            kv_spec(1, 0), kv_spec(1, 1), kv_spec(1, 2), kv_spec(1, 3),
            kv_spec(2, 0), kv_spec(2, 1), kv_spec(2, 2), kv_spec(2, 3),
            pl.BlockSpec(signal.shape, lambda b, i: (0, 0)),
            pl.BlockSpec((1, ATT_WIDTH), lambda b, i: (0, 0)),
        ],
        out_specs=step_rows,
        out_shape=jax.ShapeDtypeStruct((batch * seq, ATT_WIDTH), BF16),
        scratch_shapes=[pltpu.VMEM((ATT_HEADS, ATT_TQ, ATT_WIN), F32)],
        compiler_params=pltpu.CompilerParams(
            dimension_semantics=("arbitrary", "arbitrary"), vmem_limit_bytes=VMEM_LIMIT),
        name="band_attention",
    )(main, main, main, main, main, main, main, main, main, signal, og)


def _bias_signal(rel_bias):
    nh = rel_bias.shape[0]
    lo = N_PAST_CHUNKS * CHUNK - (ATT_WIN - 1) + MAX_REL
    n_dist = ATT_TQ + ATT_WIN - 1
    n_tail = n_dist - (2 * MAX_REL + 1 - lo)
    rb = rel_bias.astype(F32) * LOG2_E
    by_dist = jnp.concatenate([rb[:, lo:], jnp.broadcast_to(rb[:, -1:], (nh, n_tail))], axis=1)
    period = ATT_TQ + ATT_WIN
    return jnp.roll(jnp.pad(by_dist[:, ::-1], ((0, 0), (0, period - n_dist))), -(ATT_TQ - 1), axis=1)


def _weave(main, side, every):
    n = 0
    done = object()
    while True:
        if n % every == 0:
            next(side, done)
        if next(main, done) is done:
            break
        n += 1
        yield
    _run(side)


def _chain(*stages):
    for g in stages:
        yield from g


def _run(stage):
    for _ in stage:
        pass


def _inv_unit_lower(mats, ri, ci, out):
    eye = jnp.where(ri == ci, 1.0, 0.0)
    diag8 = (ri >> 3) == (ci >> 3)
    n1 = [jnp.where(diag8, -a, 0.0) for a in mats]
    n1b = [n.astype(BF16) for n in n1]
    n2b = [_dot(n, n).astype(BF16) for n in n1b]
    yield
    n4b = [_dot(n, n).astype(BF16) for n in n2b]
    ts = [eye + n for n in n1]
    yield
    ts = [t + _dot(t.astype(BF16), n) for t, n in zip(ts, n2b)]
    yield
    ts = [t + _dot(t.astype(BF16), n) for t, n in zip(ts, n4b)]
    yield
    half = DN_C // 2
    r_h = lax.broadcasted_iota(jnp.int32, (half, DN_C), 0)
    c_o = lax.broadcasted_iota(jnp.int32, (half, DN_C), 1)
    for sh in range(3, 7):
        blk = 1 << sh
        odd = [slice(s0, s0 + blk) for s0 in range(blk, DN_C, 2 * blk)]
        even = [slice(s0, s0 + blk) for s0 in range(0, DN_C, 2 * blk)]
        take = lambda m, parts: jnp.concatenate([m[p] for p in parts], axis=0)
        r_o = r_h + (((r_h >> sh) + 1) << sh)
        pair =((r_o >> (sh + 1)) == (c_o >> (sh + 1))) & ((r_o >> sh) != (c_o >> sh))
        tb = [t.astype(BF16) for t in ts]
        lt = [_dot(jnp.where(pair, take(a, odd), 0.0).astype(BF16), t) for a, t in zip(mats, tb)]
        yield
        zero = jnp.zeros((blk, DN_C), F32)
        lt = [jnp.concatenate([p for m in range(half // blk) for p in (zero, x[m * blk:(m + 1) * blk])],
                              axis=0).astype(BF16) for x in lt]
        t_odd = [take(t, odd) for t in ts]
        new = [t - _dot(t.astype(BF16), x) for t, x in zip(t_odd, lt)]
        ts = [jnp.concatenate([p for m in range(half // blk)
                               for p in (t[even[m]], n[m * blk:(m + 1) * blk])], axis=0)
              for t, n in zip(ts, new)]
        yield
    out.extend(ts)


def _gdn_local(x_ref, ab_ref, abt_ref, ltri_ref, utri_ref, alog_ref, dtb_ref, alogt_ref, dtbt_ref, slot,
               a_scr, att_scr, rhs_scr, qd_scr, kt_scr, edl_scr):
    ab = ab_ref[...]
    g_col = -jnp.exp(alog_ref[...]) * _softplus(ab + dtb_ref[...])
    beta_col = _sigmoid(ab)
    g_row = -jnp.exp(alogt_ref[...]) * _softplus(abt_ref[0:8, :] + dtbt_ref[...])
    gc1, gc2, gc3 = _split3(g_col)
    gr1, gr2, gr3 = _split3(g_row)

    y = x_ref[...].astype(F32)

    ones = jnp.ones((DN_HEAD_DIM, DN_HEAD_DIM), BF16)
    sq = [(t * t).astype(BF16) for t in (y[:, g * DN_HEAD_DIM:(g + 1) * DN_HEAD_DIM]
                                         for g in range(2 * DN_HEADS))]
    ssq = [_dot(t, ones) for t in sq]
    yield
    ltri = ltri_ref[...]
    utri = utri_ref[...]
    d_col = _dot(ltri, gc1) + (_dot(ltri, gc2) + _dot(ltri, gc3))
    d_row = _dot(gr1, utri) + (_dot(gr2, utri) + _dot(gr3, utri))
    qk = []
    for g, tot in enumerate(ssq):
        inv = lax.rsqrt(tot + EPS)
        qk.append(y[:, g * DN_HEAD_DIM:(g + 1) * DN_HEAD_DIM]
                  * (inv * DN_HEAD_DIM ** -0.5 if g < DN_HEADS else inv))

    ri = lax.broadcasted_iota(jnp.int32, (DN_C, DN_C), 0)
    ci = lax.broadcasted_iota(jnp.int32, (DN_C, DN_C), 1)
    lmasks, kbqs, kbfs = [], [], []
    for n in range(DN_ITEMS):
        c, h = divmod(n, DN_HEADS)
        rows = slice(c * DN_C, (c + 1) * DN_C)
        q = qk[h][rows]
        k = qk[DN_HEADS + h][rows]
        v = y[rows, 2 * DN_WIDTH + h * DN_HEAD_DIM:2 * DN_WIDTH + (h + 1) * DN_HEAD_DIM]
        dc = d_col[rows, h:h + 1]
        dr = d_row[h:h + 1, rows]
        dl = dr[:, DN_C - 1:DN_C]
        bc = beta_col[rows, DN_HEADS + h:DN_HEADS + h + 1]
        kb = k * bc
        edc = jnp.exp(dc)
        rhs_scr[slot, n] = jnp.concatenate([v * bc, kb * edc], axis=1).astype(BF16)
        qd_scr[slot, n] = (q * edc).astype(BF16)
        kt_scr[slot, n] = (k * jnp.exp(dl - dc)).astype(BF16)
        edl_scr[slot, n] = jnp.broadcast_to(jnp.exp(dl), (8, DN_HEAD_DIM))
        lmasks.append(jnp.exp(jnp.where(ci <= ri, dc - dr, -jnp.inf)))
        kbqs.append(jnp.concatenate([kb, q], axis=0).astype(BF16))
        kbfs.append(k.astype(BF16))
    yield
    ps =[_dot_nt(kbq, kbf) for kbq, kbf in zip(kbqs, kbfs)]
    for n, (p, lmask) in enumerate(zip(ps, lmasks)):
        a_scr[slot, n] = jnp.where(ci < ri, p[0:DN_C] * lmask, 0.0)
        att_scr[slot, n] = (p[DN_C:2 * DN_C] * lmask).astype(BF16)


def _gdn_state(slot, gate_ref, og_ref, o_ref, out_row0, s_ref, a_scr, att_scr, rhs_scr, qd_scr, kt_scr,
               edl_scr):
    ri = lax.broadcasted_iota(jnp.int32, (DN_C, DN_C), 0)
    ci = lax.broadcasted_iota(jnp.int32, (DN_C, DN_C), 1)
    og = og_ref[...]
    tinvs = []
    yield from _inv_unit_lower([a_scr[slot, n] for n in range(DN_ITEMS)], ri, ci, tinvs)
    ws = [_dot(t.astype(BF16), rhs_scr[slot, n]) for n, t in enumerate(tinvs)]
    yield
    for c in range(DN_TB // DN_C):
        idx = [c * DN_HEADS + h for h in range(DN_HEADS)]
        s_old = [s_ref[h] for h in range(DN_HEADS)]
        rs = [_dot(jnp.concatenate([ws[n][:, DN_HEAD_DIM:].astype(BF16), qd_scr[slot, n]], axis=0),
                   s.astype(BF16)) for n, s in zip(idx, s_old)]
        yield
        vnb = [(ws[n][:, 0:DN_HEAD_DIM] - r[0:DN_C]).astype(BF16) for n, r in zip(idx, rs)]
        outs = [r[DN_C:2 * DN_C] + _dot(att_scr[slot, n], vn) for n, r, vn in zip(idx, rs, vnb)]
        yield
        for h, (n, vn) in enumerate(zip(idx, vnb)):
            s_ref[h] = s_old[h] * edl_scr[slot, n, 0:1, :] + _dot_tn(kt_scr[slot, n], vn)
        yield
        rows = slice(out_row0 + c * DN_C, out_row0 + (c + 1) * DN_C)
        for h, o in enumerate(outs):
            hs = slice(h * DN_HEAD_DIM, (h + 1) * DN_HEAD_DIM)
            ms = jnp.mean(o * o, axis=-1, keepdims=True)
            gt = gate_ref[rows, hs].astype(F32)
            o_ref[rows, hs] = (o * lax.rsqrt(ms + EPS) * og * (gt * _sigmoid(gt))).astype(BF16)


def _ffn_stages(x_ref, oa_ref, od_ref, wo_ref, fg_ref, w1_ref, w2_ref, o_ref):
    d = x_ref.shape[1]
    oa = oa_ref[...]
    od = od_ref[...]
    mixed = []
    for j in range(d // FFN_TN):
        cols = slice(j * FFN_TN, (j + 1) * FFN_TN)
        mixed.append(_dot(oa, wo_ref[0:ATT_WIDTH, cols]) + _dot(od, wo_ref[ATT_WIDTH:, cols]))
        yield
    h1 = x_ref[...] + jnp.concatenate(mixed, axis=1)
    ms = jnp.mean(h1 * h1, axis=-1, keepdims=True)
    n = (h1 * lax.rsqrt(ms + EPS) * fg_ref[...]).astype(BF16)

    def up(c, out):
        for j in range(FFN_TF // FFN_TN):
            col0 = c * FFN_TF + j * FFN_TN
            out.append(_dot(n, w1_ref[:, col0:col0 + FFN_TN]))
            yield

    n_chunks = w1_ref.shape[1] // FFN_TF
    acc = h1
    z_next = []
    yield from up(0, z_next)
    for c in range(n_chunks):
        z = z_next
        if c + 1 < n_chunks:
            z_next = []
            yield from up(c + 1, z_next)
        act = jnp.concatenate([jnp.square(jnp.maximum(piece, 0.0)).astype(BF16) for piece in z], axis=1)
        down = []
        for j in range(d // FFN_TN):
            down.append(_dot(act, w2_ref[c * FFN_TF:(c + 1) * FFN_TF, j * FFN_TN:(j + 1) * FFN_TN]))
            yield
        acc = acc + jnp.concatenate(down, axis=1)
    o_ref[...] = acc


def _gdn_ffn_kernel(x0_ref, x1_ref, x2_ref, ab0_ref, ab1_ref, ab2_ref, abt0_ref, abt1_ref, abt2_ref,
                    gate_ref, ltri_ref, utri_ref, alog_ref, dtb_ref, alogt_ref, dtbt_ref, dn_og_ref,
                    xres_ref, oa_ref, wo_ref, fg_ref, w1_ref, w2_ref, o_ref, s_ref, od_scr, *local_scr,
                    steps_per_seq):
    s = pl.program_id(0)
    cur = s % 2
    consts = (ltri_ref, utri_ref, alog_ref, dtb_ref, alogt_ref, dtbt_ref)

    def local(x_ref, ab_ref, abt_ref, slot):
        return _gdn_local(x_ref, ab_ref, abt_ref, *consts, slot, *local_scr)

    def state(slot, out_row0):
        return _gdn_state(slot, gate_ref, dn_og_ref, od_scr.at[cur], out_row0, s_ref, *local_scr)

    @pl.when(s == 0)
    def _():
        od_scr[1] = jnp.zeros(od_scr.shape[1:], od_scr.dtype)

    @pl.when(s % steps_per_seq == 0)
    def _():
        s_ref[...] = jnp.zeros_like(s_ref)
        _run(local(x0_ref, ab0_ref, abt0_ref, 0))

    deltanet = _chain(
        _weave(state(0, 0), local(x1_ref, ab1_ref, abt1_ref, 1), DN_STATE_PER_LOCAL),
        _weave(state(1, DN_TB), local(x2_ref, ab2_ref, abt2_ref, 0), DN_STATE_PER_LOCAL))
    ffn = _ffn_stages(xres_ref, oa_ref, od_scr.at[1 - cur], wo_ref, fg_ref, w1_ref, w2_ref, o_ref)
    _run(_weave(deltanet, ffn, 1))


def _deltanet_ffn(xf, main, ab, abt, o_att, alog, dtb, alogt, dtbt, dn_og, w_out, fg, w1, w2, seq):
    t, d = xf.shape
    dff = w1.shape[1]
    assert FFN_TM == 2 * DN_TB
    nb = seq // DN_TB
    spq = nb // 2
    steps = t // FFN_TM
    tile = lambda s: jnp.minimum(s, steps - 1)
    prev = lambda s: jnp.maximum(s - 1, 0)
    seq_blk0 = lambda s: tile(s) // spq * nb
    first = seq_blk0
    odd = lambda s: 2 * tile(s) + 1
    nxt = lambda s: jnp.minimum(2 * tile(s) + 2, seq_blk0(s) + nb - 1)
    const = lambda s: (0, 0)

    def x_spec(blk):
        return pl.BlockSpec((DN_TB, 3 * DN_WIDTH), lambda s: (blk(s), 1))

    def ab_spec(blk):
        return pl.BlockSpec((DN_TB, AB_PAD), lambda s: (blk(s), 0))

    def abt_spec(blk):
        return pl.BlockSpec((AB_PAD, DN_TB), lambda s: (0, blk(s)))

    r = np.arange(DN_TB)
    same = (r[:, None] // DN_C) == (r[None, :] // DN_C)
    ltri = jnp.asarray(same & (r[None, :] <= r[:, None]), BF16)
    utri = jnp.asarray(same & (r[:, None] <= r[None, :]), BF16)
    item = lambda *shape: (2, DN_ITEMS) + shape
    resident = dict(pipeline_mode=pl.Buffered(1))
    return pl.pallas_call(
        functools.partial(_gdn_ffn_kernel, steps_per_seq=spq),
        grid=(steps + 1,),
        in_specs=[
            x_spec(first), x_spec(odd), x_spec(nxt),
            ab_spec(first), ab_spec(odd), ab_spec(nxt),
            abt_spec(first), abt_spec(odd), abt_spec(nxt),
            pl.BlockSpec((2 * DN_TB, DN_WIDTH), lambda s: (tile(s), 6)),
            pl.BlockSpec((DN_TB, DN_TB), const),
            pl.BlockSpec((DN_TB, DN_TB), const),
            pl.BlockSpec((1, AB_PAD), const),
            pl.BlockSpec((1, AB_PAD), const),
            pl.BlockSpec((8, 1), const),
            pl.BlockSpec((8, 1), const),
            pl.BlockSpec((1, DN_HEAD_DIM), const),
            pl.BlockSpec((FFN_TM, d), lambda s: (prev(s), 0)),
            pl.BlockSpec((FFN_TM, ATT_WIDTH), lambda s: (prev(s), 0)),
            pl.BlockSpec((d, d), const, **resident),
            pl.BlockSpec((1, d), const),
            pl.BlockSpec((d, dff), const, **resident),
            pl.BlockSpec((dff, d), const, **resident),
        ],
        out_specs=pl.BlockSpec((FFN_TM, d), lambda s: (prev(s), 0)),
        out_shape=jax.ShapeDtypeStruct((t, d), F32),
        scratch_shapes=[
            pltpu.VMEM((DN_HEADS, DN_HEAD_DIM, DN_HEAD_DIM), F32),
            pltpu.VMEM((2, FFN_TM, DN_WIDTH), BF16),
            pltpu.VMEM(item(DN_C, DN_C), F32),
            pltpu.VMEM(item(DN_C, DN_C), BF16),
            pltpu.VMEM(item(DN_C, 2 * DN_HEAD_DIM), BF16),
            pltpu.VMEM(item(DN_C, DN_HEAD_DIM), BF16),
            pltpu.VMEM(item(DN_C, DN_HEAD_DIM), BF16),
            pltpu.VMEM(item(8, DN_HEAD_DIM), F32),
        ],
        compiler_params=pltpu.CompilerParams(
            dimension_semantics=("arbitrary",), vmem_limit_bytes=VMEM_LIMIT),
        name="deltanet_ffn",
    )(main, main, main, ab, ab, ab, abt, abt, abt, main, ltri, utri, alog, dtb, alogt, dtbt, dn_og,
      xf, o_att, w_out, fg, w1, w2)


def _layer(h, p, batch, seq):
    w_in = p["w_in"]
    w_main = w_in.astype(BF16)
    w_ab = jnp.pad(w_in[:, MAIN_COLS:], ((0, 0), (0, AB_PAD - 2 * DN_HEADS))).astype(BF16)
    head_of = np.arange(ATT_WIDTH) // ATT_HEAD_DIM
    seg = jnp.asarray((head_of[:, None] == head_of[None, :]) / ATT_HEAD_DIM, BF16)
    qg = (jnp.tile(p["att_q_gain"].astype(F32), ATT_HEADS) * (ATT_HEAD_DIM ** -0.5 * LOG2_E))[None]
    kg = jnp.tile(p["att_k_gain"].astype(F32), ATT_HEADS)[None]
    (main, ab, abt), (w_out, w_ff1, w_ff2) = _inproj(
        h, p["mix_norm_gain"].astype(F32)[None], w_main, w_ab, qg, kg, seg, p["dn_conv_w"].astype(F32), seq,
        [p["w_out"].astype(F32), p["w_ff1"].astype(F32), p["w_ff2"].astype(F32)])

    o_att = _attention(main, _bias_signal(p["rel_bias"]),
                       p["att_out_gain"].astype(F32)[None], batch, seq)

    pad_row = lambda v: jnp.pad(v.astype(F32), (0, AB_PAD - DN_HEADS))[None]
    pad_col = lambda v: jnp.pad(v.astype(F32), (0, 8 - DN_HEADS))[:, None]
    return _deltanet_ffn(h, main, ab, abt, o_att, pad_row(p["dn_a_log"]), pad_row(p["dn_dt_bias"]),
                         pad_col(p["dn_a_log"]), pad_col(p["dn_dt_bias"]),
                         p["dn_out_gain"].astype(F32)[None], w_out, p["ffn_norm_gain"].astype(F32)[None],
                         w_ff1, w_ff2, seq)


def kernel(x, mix_norm_gain, w_in, att_q_gain, att_k_gain, rel_bias, att_out_gain, dn_conv_w,
           dn_a_log, dn_dt_bias, dn_out_gain, w_out, ffn_norm_gain, w_ff1, w_ff2):
    batch, seq, d = x.shape
    assert seq % max(ATT_TQ, 2 * DN_TB) == 0 and (batch * seq) % max(IN_TM, FFN_TM) == 0
    params = dict(mix_norm_gain=mix_norm_gain, w_in=w_in, att_q_gain=att_q_gain,
                  att_k_gain=att_k_gain, rel_bias=rel_bias, att_out_gain=att_out_gain,
                  dn_conv_w=dn_conv_w, dn_a_log=dn_a_log, dn_dt_bias=dn_dt_bias,
                  dn_out_gain=dn_out_gain, w_out=w_out, ffn_norm_gain=ffn_norm_gain,
                  w_ff1=w_ff1, w_ff2=w_ff2)
    h = x.reshape(batch * seq, d)
    for layer in range(w_in.shape[0]):
        h = _layer(h, {name: v[layer] for name, v in params.items()}, batch, seq)
    return h.reshape(batch, seq, d)
```

```python
import functools

import numpy as np
import jax
import jax.numpy as jnp
from jax import lax
from jax.experimental import pallas as pl
from jax.experimental.pallas import tpu as pltpu

F32 = jnp.float32
BF16 = jnp.bfloat16

CHUNK = 64
N_PAST_CHUNKS = 8
ATT_HEADS = 8
ATT_HEAD_DIM = 64
ATT_WIDTH = ATT_HEADS * ATT_HEAD_DIM
MAX_REL = 256
DN_HEADS = 4
DN_HEAD_DIM = 128
DN_WIDTH = DN_HEADS * DN_HEAD_DIM
CONV_K = 4
EPS = 1e-6

MAIN_COLS = 3 * ATT_WIDTH + 4 * DN_WIDTH
AB_PAD = 128

LANES = 128
VMEM_LIMIT = 56 * 1024 * 1024

IN_TM = 512
IN_TN = 512
ATT_TQ = 4 * CHUNK
ATT_WIN = ATT_TQ + N_PAST_CHUNKS * CHUNK
ATT_TILES = 2
DN_C = 128
DN_TB = 2 * DN_C
DN_TAIL = 8
DN_ITEMS = (DN_TB // DN_C) * DN_HEADS
DN_CONV_FIRST = 3 * ATT_WIDTH // IN_TN
DN_CONV_CHUNKS = 3 * DN_WIDTH // IN_TN
DN_STATE_PER_LOCAL = 4
FFN_TM = 512
FFN_TF = 1024
FFN_TN = 256
NEG_BIG = -1e30
LOG2_E = 1.4426950408889634


def _dot(a, b):
    return jnp.dot(a, b, preferred_element_type=F32)


def _dot_nt(a, b):
    return lax.dot_general(a, b, (((1,), (1,)), ((), ())), preferred_element_type=F32)


def _dot_tn(a, b):
    return lax.dot_general(a, b, (((0,), (0,)), ((), ())), preferred_element_type=F32)


def _split3(a):
    p1 = a.astype(BF16)
    r1 = a - p1.astype(F32)
    p2 = r1.astype(BF16)
    p3 = (r1 - p2.astype(F32)).astype(BF16)
    return p1, p2, p3


def _sigmoid(x):
    return 1.0 / (1.0 + jnp.exp2(x * -LOG2_E))


def _softplus(x):
    return jnp.maximum(x, 0.0) + jnp.log(1.0 + jnp.exp(-jnp.abs(x)))


def _inproj_kernel(x_ref, g_ref, wm_ref, wab_ref, qg_ref, kg_ref, seg_ref, cw_ref, *rest, tiles_per_seq):
    n_cast = (len(rest) - 4) // 2
    cast_in, (main_ref, ab_ref, abt_ref) = rest[:n_cast], rest[n_cast:n_cast + 3]
    cast_out, stage_ref = rest[n_cast + 3:2 * n_cast + 3], rest[2 * n_cast + 3]
    i = pl.program_id(0)
    for src, dst in zip(cast_in, cast_out):
        dst[...] = src[...].astype(BF16)

    @pl.when(i % tiles_per_seq == 0)
    def _():
        stage_ref[...] = jnp.zeros_like(stage_ref)

    x = x_ref[...]
    ms = jnp.mean(x * x, axis=-1, keepdims=True)
    u = (x * lax.rsqrt(ms + EPS) * g_ref[...]).astype(BF16)
    conv = list(range(DN_CONV_FIRST, DN_CONV_FIRST + DN_CONV_CHUNKS))
    plain = [c for c in range(MAIN_COLS // IN_TN) if c not in conv]
    order = [c for pair in zip(plain, conv) for c in pair] + plain[len(conv):]
    project = lambda c: _dot(u, wm_ref[:, c * IN_TN:(c + 1) * IN_TN])
    y_next = project(order[0])
    for n, c in enumerate(order):
        cols = slice(c * IN_TN, (c + 1) * IN_TN)
        y = y_next
        if n + 1 < len(order):
            y_next = project(order[n + 1])
        if c < 2:
            gain = qg_ref if c == 0 else kg_ref
            m = _dot((y * y).astype(BF16), seg_ref[...])
            y = y * lax.rsqrt(m + EPS) * gain[...]
        elif DN_CONV_FIRST <= c < DN_CONV_FIRST + DN_CONV_CHUNKS:
            j = c - DN_CONV_FIRST
            wcols = slice(j * IN_TN, (j + 1) * IN_TN)
            ext = jnp.concatenate([stage_ref[j], y], axis=0)
            z = cw_ref[0:1, wcols] * ext
            for t in range(1, CONV_K):
                z = cw_ref[t:t + 1, wcols] * ext + pltpu.roll(z, 1, axis=0)
            z = z[DN_TAIL:]
            stage_ref[j] = y[IN_TM - DN_TAIL:IN_TM, :]
            y = z * _sigmoid(z)
        main_ref[:, cols] = y.astype(BF16)
    ab = _dot(u, wab_ref[...])
    ab_ref[...] = ab
    abt_ref[...] = ab.T


def _inproj(xf, gain, w_main, w_ab, qg, kg, seg, conv_w, seq, to_cast):
    t, d = xf.shape
    steps = t // IN_TM
    const = lambda i: (0, 0)
    slab = lambda w: pl.BlockSpec((w.shape[0] // steps, w.shape[1]), lambda i: (i, 0))
    outs = pl.pallas_call(
        functools.partial(_inproj_kernel, tiles_per_seq=seq // IN_TM),
        grid=(steps,),
        in_specs=[
            pl.BlockSpec((IN_TM, d), lambda i: (i, 0)),
            pl.BlockSpec((1, d), const),
            pl.BlockSpec((d, MAIN_COLS), const),
            pl.BlockSpec((d, AB_PAD), const),
            pl.BlockSpec((1, ATT_WIDTH), const),
            pl.BlockSpec((1, ATT_WIDTH), const),
            pl.BlockSpec((ATT_WIDTH, ATT_WIDTH), const),
            pl.BlockSpec((CONV_K, 3 * DN_WIDTH), const),
        ] + [slab(w) for w in to_cast],
        out_specs=[
            pl.BlockSpec((IN_TM, MAIN_COLS), lambda i: (i, 0)),
            pl.BlockSpec((IN_TM, AB_PAD), lambda i: (i, 0)),
            pl.BlockSpec((AB_PAD, IN_TM), lambda i: (0, i)),
        ] + [slab(w) for w in to_cast],
        out_shape=[
            jax.ShapeDtypeStruct((t, MAIN_COLS), BF16),
            jax.ShapeDtypeStruct((t, AB_PAD), F32),
            jax.ShapeDtypeStruct((AB_PAD, t), F32),
        ] + [jax.ShapeDtypeStruct(w.shape, BF16) for w in to_cast],
        scratch_shapes=[pltpu.VMEM((DN_CONV_CHUNKS, DN_TAIL, IN_TN), F32)],
        compiler_params=pltpu.CompilerParams(
            dimension_semantics=("arbitrary",), vmem_limit_bytes=VMEM_LIMIT),
        name="inproj",
    )(xf, gain, w_main, w_ab, qg, kg, seg, conv_w, *to_cast)
    return outs[:3], outs[3:]


def _attn_tile(q_ref, row0, k_refs, v_refs, bias_ref, start_mask, og_ref, o_ref):
    lane = lax.broadcasted_iota(jnp.int32, (1, LANES), 1)
    per_tile = LANES // ATT_HEAD_DIM
    rows = slice(row0, row0 + ATT_TQ)

    def mine(h):
        e = h % per_tile
        return (lane >= e * ATT_HEAD_DIM) & (lane < (e + 1) * ATT_HEAD_DIM)

    def window(refs, h):
        sl = slice(h // per_tile * LANES, (h // per_tile + 1) * LANES)
        return jnp.concatenate([r[:, sl] for r in refs], axis=0)

    def scores(h):
        qp = q_ref[rows, h // per_tile * LANES:(h // per_tile + 1) * LANES]
        q_e = jnp.where(mine(h), qp, jnp.zeros_like(qp))
        bias = bias_ref[h] if start_mask is None else bias_ref[h] + start_mask
        return _dot_nt(q_e, window(k_refs, h)) + bias

    def attend(h, s):
        mx = jnp.max(s, axis=-1, keepdims=True)
        pe = jnp.exp2(s - mx).astype(BF16)
        vw = window(v_refs, h)
        o_e = _dot(pe, jnp.where(mine(h), vw, jnp.ones_like(vw)))
        return o_e / pltpu.roll(o_e, ATT_HEAD_DIM, axis=1)

    normed = []
    s_next = scores(0)
    for h in range(ATT_HEADS):
        s_cur = s_next
        if h + 1 < ATT_HEADS:
            s_next = scores(h + 1)
        normed.append(attend(h, s_cur))
        yield
    outs = [jnp.where(lane < ATT_HEAD_DIM, normed[t * per_tile], normed[t * per_tile + 1])
            for t in range(ATT_WIDTH // LANES)]
    o = jnp.concatenate(outs, axis=1)
    ms = jnp.mean(o * o, axis=-1, keepdims=True)
    o_ref[rows, :] = (o * lax.rsqrt(ms + EPS) * og_ref[...]).astype(BF16)


def _attn_kernel(q_ref, k0_ref, k1_ref, k2_ref, k3_ref, v0_ref, v1_ref, v2_ref, v3_ref, sig_ref, og_ref, o_ref,
                 bias_ref):
    i = pl.program_id(1)

    @pl.when((pl.program_id(0) == 0) & (i == 0))
    def _():
        r = lax.broadcasted_iota(jnp.int32, (ATT_TQ, ATT_WIN), 0)
        s = lax.broadcasted_iota(jnp.int32, (ATT_TQ, ATT_WIN), 1)
        back = (s >> 6) - (r >> 6)
        in_band = (back >= 0) & (back <= N_PAST_CHUNKS)
        for h in range(ATT_HEADS):
            rows = jnp.broadcast_to(sig_ref[h:h + 1, :], (ATT_TQ, sig_ref.shape[1]))
            toep = pltpu.roll(rows, 0, 1, stride=1, stride_axis=0)
            bias_ref[h] = jnp.where(in_band, toep[:, 0:ATT_WIN], NEG_BIG)

    def tiles(masks):
        _run(_weave(
            _attn_tile(q_ref, 0, (k0_ref, k1_ref, k2_ref), (v0_ref, v1_ref, v2_ref), bias_ref, masks[0],
                       og_ref, o_ref),
            _attn_tile(q_ref, ATT_TQ, (k1_ref, k2_ref, k3_ref), (v1_ref, v2_ref, v3_ref), bias_ref, masks[1],
                       og_ref, o_ref), 1))

    steps_in_past = N_PAST_CHUNKS * CHUNK // (ATT_TILES * ATT_TQ)

    @pl.when(i < steps_in_past)
    def _():
        jchunk = lax.broadcasted_iota(jnp.int32, (1, ATT_WIN), 1) >> 6
        tiles([jnp.where(jchunk + (ATT_TQ // CHUNK) * (ATT_TILES * i + n) - N_PAST_CHUNKS >= 0, 0.0, NEG_BIG)
               for n in range(ATT_TILES)])

    @pl.when(i >= steps_in_past)
    def _():
        tiles([None] * ATT_TILES)


def _attention(main, signal, og, batch, seq):
    nb = seq // ATT_TQ
    steps = nb // ATT_TILES
    blk = (ATT_TQ, ATT_WIDTH)

    def kv_spec(col, j):
        return pl.BlockSpec(blk, lambda b, i: (b * nb + jnp.maximum(ATT_TILES * i - 2 + j, 0), col))

    step_rows = pl.BlockSpec((ATT_TILES * ATT_TQ, ATT_WIDTH), lambda b, i: (b * steps + i, 0))
    return pl.pallas_call(
        _attn_kernel,
        grid=(batch, steps),
        in_specs=[
            step_rows,
            kv_spec(1, 0), kv_spec(1, 1), kv_spec(1, 2), kv_spec(1, 3),
            kv_spec(2, 0), kv_spec(2, 1), kv_spec(2, 2), kv_spec(2, 3),
            pl.BlockSpec(signal.shape, lambda b, i: (0, 0)),
            pl.BlockSpec((1, ATT_WIDTH), lambda b, i: (0, 0)),
        ],
        out_specs=step_rows,
        out_shape=jax.ShapeDtypeStruct((batch * seq, ATT_WIDTH), BF16),
        scratch_shapes=[pltpu.VMEM((ATT_HEADS, ATT_TQ, ATT_WIN), F32)],
        compiler_params=pltpu.CompilerParams(
            dimension_semantics=("arbitrary", "arbitrary"), vmem_limit_bytes=VMEM_LIMIT),
        name="band_attention",
    )(main, main, main, main, main, main, main, main, main, signal, og)


def _bias_signal(rel_bias):
    nh = rel_bias.shape[0]
    lo = N_PAST_CHUNKS * CHUNK - (ATT_WIN - 1) + MAX_REL
    n_dist = ATT_TQ + ATT_WIN - 1
    n_tail = n_dist - (2 * MAX_REL + 1 - lo)
    rb = rel_bias.astype(F32) * LOG2_E
    by_dist = jnp.concatenate([rb[:, lo:], jnp.broadcast_to(rb[:, -1:], (nh, n_tail))], axis=1)
    period = ATT_TQ + ATT_WIN
    return jnp.roll(jnp.pad(by_dist[:, ::-1], ((0, 0), (0, period - n_dist))), -(ATT_TQ - 1), axis=1)


def _weave(main, side, every):
    n = 0
    done = object()
    while True:
        if n % every == 0:
            next(side, done)
        if next(main, done) is done:
            break
        n += 1
        yield
    _run(side)


def _chain(*stages):
    for g in stages:
        yield from g


def _run(stage):
    for _ in stage:
        pass


def _inv_unit_lower(mats, ri, ci, out):
    eye = jnp.where(ri == ci, 1.0, 0.0)
    diag8 = (ri >> 3) == (ci >> 3)
    n1 = [jnp.where(diag8, -a, 0.0) for a in mats]
    n1b = [n.astype(BF16) for n in n1]
    n2b = [_dot(n, n).astype(BF16) for n in n1b]
    yield
    n4b = [_dot(n, n).astype(BF16) for n in n2b]
    ts = [eye + n for n in n1]
    yield
    ts = [t + _dot(t.astype(BF16), n) for t, n in zip(ts, n2b)]
    yield
    ts = [t + _dot(t.astype(BF16), n) for t, n in zip(ts, n4b)]
    yield
    half = DN_C // 2
    r_h = lax.broadcasted_iota(jnp.int32, (half, DN_C), 0)
    c_o = lax.broadcasted_iota(jnp.int32, (half, DN_C), 1)
    for sh in range(3, 7):
        blk = 1 << sh
        odd = [slice(s0, s0 + blk) for s0 in range(blk, DN_C, 2 * blk)]
        even = [slice(s0, s0 + blk) for s0 in range(0, DN_C, 2 * blk)]
        take = lambda m, parts: jnp.concatenate([m[p] for p in parts], axis=0)
        r_o = r_h + (((r_h >> sh) + 1) << sh)
        pair =((r_o >> (sh + 1)) == (c_o >> (sh + 1))) & ((r_o >> sh) != (c_o >> sh))
        tb = [t.astype(BF16) for t in ts]
        lt = [_dot(jnp.where(pair, take(a, odd), 0.0).astype(BF16), t) for a, t in zip(mats, tb)]
        yield
        zero = jnp.zeros((blk, DN_C), F32)
        lt = [jnp.concatenate([p for m in range(half // blk) for p in (zero, x[m * blk:(m + 1) * blk])],
                              axis=0).astype(BF16) for x in lt]
        t_odd = [take(t, odd) for t in ts]
        new = [t - _dot(t.astype(BF16), x) for t, x in zip(t_odd, lt)]
        ts = [jnp.concatenate([p for m in range(half // blk)
                               for p in (t[even[m]], n[m * blk:(m + 1) * blk])], axis=0)
              for t, n in zip(ts, new)]
        yield
    out.extend(ts)


def _gdn_local(x_ref, ab_ref, abt_ref, ltri_ref, utri_ref, alog_ref, dtb_ref, alogt_ref, dtbt_ref, slot,
               a_scr, att_scr, rhs_scr, qd_scr, kt_scr, edl_scr):
    ab = ab_ref[...]
    g_col = -jnp.exp(alog_ref[...]) * _softplus(ab + dtb_ref[...])
    beta_col = _sigmoid(ab)
    g_row = -jnp.exp(alogt_ref[...]) * _softplus(abt_ref[0:8, :] + dtbt_ref[...])
    gc1, gc2, gc3 = _split3(g_col)
    gr1, gr2, gr3 = _split3(g_row)

    y = x_ref[...].astype(F32)

    ones = jnp.ones((DN_HEAD_DIM, DN_HEAD_DIM), BF16)
    sq = [(t * t).astype(BF16) for t in (y[:, g * DN_HEAD_DIM:(g + 1) * DN_HEAD_DIM]
                                         for g in range(2 * DN_HEADS))]
    ssq = [_dot(t, ones) for t in sq]
    yield
    ltri = ltri_ref[...]
    utri = utri_ref[...]
    d_col = _dot(ltri, gc1) + (_dot(ltri, gc2) + _dot(ltri, gc3))
    d_row = _dot(gr1, utri) + (_dot(gr2, utri) + _dot(gr3, utri))
    qk = []
    for g, tot in enumerate(ssq):
        inv = lax.rsqrt(tot + EPS)
        qk.append(y[:, g * DN_HEAD_DIM:(g + 1) * DN_HEAD_DIM]
                  * (inv * DN_HEAD_DIM ** -0.5 if g < DN_HEADS else inv))

    ri = lax.broadcasted_iota(jnp.int32, (DN_C, DN_C), 0)
    ci = lax.broadcasted_iota(jnp.int32, (DN_C, DN_C), 1)
    lmasks, kbqs, kbfs = [], [], []
    for n in range(DN_ITEMS):
        c, h = divmod(n, DN_HEADS)
        rows = slice(c * DN_C, (c + 1) * DN_C)
        q = qk[h][rows]
        k = qk[DN_HEADS + h][rows]
        v = y[rows, 2 * DN_WIDTH + h * DN_HEAD_DIM:2 * DN_WIDTH + (h + 1) * DN_HEAD_DIM]
        dc = d_col[rows, h:h + 1]
        dr = d_row[h:h + 1, rows]
        dl = dr[:, DN_C - 1:DN_C]
        bc = beta_col[rows, DN_HEADS + h:DN_HEADS + h + 1]
        kb = k * bc
        edc = jnp.exp(dc)
        rhs_scr[slot, n] = jnp.concatenate([v * bc, kb * edc], axis=1).astype(BF16)
        qd_scr[slot, n] = (q * edc).astype(BF16)
        kt_scr[slot, n] = (k * jnp.exp(dl - dc)).astype(BF16)
        edl_scr[slot, n] = jnp.broadcast_to(jnp.exp(dl), (8, DN_HEAD_DIM))
        lmasks.append(jnp.exp(jnp.where(ci <= ri, dc - dr, -jnp.inf)))
        kbqs.append(jnp.concatenate([kb, q], axis=0).astype(BF16))
        kbfs.append(k.astype(BF16))
    yield
    ps =[_dot_nt(kbq, kbf) for kbq, kbf in zip(kbqs, kbfs)]
    for n, (p, lmask) in enumerate(zip(ps, lmasks)):
        a_scr[slot, n] = jnp.where(ci < ri, p[0:DN_C] * lmask, 0.0)
        att_scr[slot, n] = (p[DN_C:2 * DN_C] * lmask).astype(BF16)


def _gdn_state(slot, gate_ref, og_ref, o_ref, out_row0, s_ref, a_scr, att_scr, rhs_scr, qd_scr, kt_scr,
               edl_scr):
    ri = lax.broadcasted_iota(jnp.int32, (DN_C, DN_C), 0)
    ci = lax.broadcasted_iota(jnp.int32, (DN_C, DN_C), 1)
    og = og_ref[...]
    tinvs = []
    yield from _inv_unit_lower([a_scr[slot, n] for n in range(DN_ITEMS)], ri, ci, tinvs)
    ws = [_dot(t.astype(BF16), rhs_scr[slot, n]) for n, t in enumerate(tinvs)]
    yield
    for c in range(DN_TB // DN_C):
        idx = [c * DN_HEADS + h for h in range(DN_HEADS)]
        s_old = [s_ref[h] for h in range(DN_HEADS)]
        rs = [_dot(jnp.concatenate([ws[n][:, DN_HEAD_DIM:].astype(BF16), qd_scr[slot, n]], axis=0),
                   s.astype(BF16)) for n, s in zip(idx, s_old)]
        yield
        vnb = [(ws[n][:, 0:DN_HEAD_DIM] - r[0:DN_C]).astype(BF16) for n, r in zip(idx, rs)]
        outs = [r[DN_C:2 * DN_C] + _dot(att_scr[slot, n], vn) for n, r, vn in zip(idx, rs, vnb)]
        yield
        for h, (n, vn) in enumerate(zip(idx, vnb)):
            s_ref[h] = s_old[h] * edl_scr[slot, n, 0:1, :] + _dot_tn(kt_scr[slot, n], vn)
        yield
        rows = slice(out_row0 + c * DN_C, out_row0 + (c + 1) * DN_C)
        for h, o in enumerate(outs):
            hs = slice(h * DN_HEAD_DIM, (h + 1) * DN_HEAD_DIM)
            ms = jnp.mean(o * o, axis=-1, keepdims=True)
            gt = gate_ref[rows, hs].astype(F32)
            o_ref[rows, hs] = (o * lax.rsqrt(ms + EPS) * og * (gt * _sigmoid(gt))).astype(BF16)


def _ffn_stages(x_ref, oa_ref, od_ref, wo_ref, fg_ref, w1_ref, w2_ref, o_ref):
    d = x_ref.shape[1]
    oa = oa_ref[...]
    od = od_ref[...]
    mixed = []
    for j in range(d // FFN_TN):
        cols = slice(j * FFN_TN, (j + 1) * FFN_TN)
        mixed.append(_dot(oa, wo_ref[0:ATT_WIDTH, cols]) + _dot(od, wo_ref[ATT_WIDTH:, cols]))
        yield
    h1 = x_ref[...] + jnp.concatenate(mixed, axis=1)
    ms = jnp.mean(h1 * h1, axis=-1, keepdims=True)
    n = (h1 * lax.rsqrt(ms + EPS) * fg_ref[...]).astype(BF16)

    def up(c, out):
        for j in range(FFN_TF // FFN_TN):
            col0 = c * FFN_TF + j * FFN_TN
            out.append(_dot(n, w1_ref[:, col0:col0 + FFN_TN]))
            yield

    n_chunks = w1_ref.shape[1] // FFN_TF
    acc = h1
    z_next = []
    yield from up(0, z_next)
    for c in range(n_chunks):
        z = z_next
        if c + 1 < n_chunks:
            z_next = []
            yield from up(c + 1, z_next)
        act = jnp.concatenate([jnp.square(jnp.maximum(piece, 0.0)).astype(BF16) for piece in z], axis=1)
        down = []
        for j in range(d // FFN_TN):
            down.append(_dot(act, w2_ref[c * FFN_TF:(c + 1) * FFN_TF, j * FFN_TN:(j + 1) * FFN_TN]))
            yield
        acc = acc + jnp.concatenate(down, axis=1)
    o_ref[...] = acc


def _gdn_ffn_kernel(x0_ref, x1_ref, x2_ref, ab0_ref, ab1_ref, ab2_ref, abt0_ref, abt1_ref, abt2_ref,
                    gate_ref, ltri_ref, utri_ref, alog_ref, dtb_ref, alogt_ref, dtbt_ref, dn_og_ref,
                    xres_ref, oa_ref, wo_ref, fg_ref, w1_ref, w2_ref, o_ref, s_ref, od_scr, *local_scr,
                    steps_per_seq, steps):
    s = pl.program_id(0)
    cur = s % 2
    consts = (ltri_ref, utri_ref, alog_ref, dtb_ref, alogt_ref, dtbt_ref)

    def local(x_ref, ab_ref, abt_ref, slot):
        return _gdn_local(x_ref, ab_ref, abt_ref, *consts, slot, *local_scr)

    def state(slot, out_row0):
        return _gdn_state(slot, gate_ref, dn_og_ref, od_scr.at[cur], out_row0, s_ref, *local_scr)

    @pl.when((s % steps_per_seq == 0) & (s < steps))
    def _():
        s_ref[...] = jnp.zeros_like(s_ref)
        _run(local(x0_ref, ab0_ref, abt0_ref, 0))

    def deltanet():
        return _chain(
            _weave(state(0, 0), local(x1_ref, ab1_ref, abt1_ref, 1), DN_STATE_PER_LOCAL),
            _weave(state(1, DN_TB), local(x2_ref, ab2_ref, abt2_ref, 0), DN_STATE_PER_LOCAL))

    def ffn():
        return _ffn_stages(xres_ref, oa_ref, od_scr.at[1 - cur], wo_ref, fg_ref, w1_ref, w2_ref, o_ref)

    @pl.when(s == 0)
    def _():
        _run(deltanet())

    @pl.when((s > 0) & (s < steps))
    def _():
        _run(_weave(deltanet(), ffn(), 1))

    @pl.when(s == steps)
    def _():
        _run(ffn())


def _deltanet_ffn(xf, main, ab, abt, o_att, alog, dtb, alogt, dtbt, dn_og, w_out, fg, w1, w2, seq):
    t, d = xf.shape
    dff = w1.shape[1]
    assert FFN_TM == 2 * DN_TB
    nb = seq // DN_TB
    spq = nb // 2
    steps = t // FFN_TM
    tile = lambda s: jnp.minimum(s, steps - 1)
    prev = lambda s: jnp.maximum(s - 1, 0)
    seq_blk0 = lambda s: tile(s) // spq * nb
    first = seq_blk0
    odd = lambda s: 2 * tile(s) + 1
    nxt = lambda s: jnp.minimum(2 * tile(s) + 2, seq_blk0(s) + nb - 1)
    const = lambda s: (0, 0)

    def x_spec(blk):
        return pl.BlockSpec((DN_TB, 3 * DN_WIDTH), lambda s: (blk(s), 1))

    def ab_spec(blk):
        return pl.BlockSpec((DN_TB, AB_PAD), lambda s: (blk(s), 0))

    def abt_spec(blk):
        return pl.BlockSpec((AB_PAD, DN_TB), lambda s: (0, blk(s)))

    r = np.arange(DN_TB)
    same = (r[:, None] // DN_C) == (r[None, :] // DN_C)
    ltri = jnp.asarray(same & (r[None, :] <= r[:, None]), BF16)
    utri = jnp.asarray(same & (r[:, None] <= r[None, :]), BF16)
    item = lambda *shape: (2, DN_ITEMS) + shape
    resident = dict(pipeline_mode=pl.Buffered(1))
    return pl.pallas_call(
        functools.partial(_gdn_ffn_kernel, steps_per_seq=spq, steps=steps),
        grid=(steps + 1,),
        in_specs=[
            x_spec(first), x_spec(odd), x_spec(nxt),
            ab_spec(first), ab_spec(odd), ab_spec(nxt),
            abt_spec(first), abt_spec(odd), abt_spec(nxt),
            pl.BlockSpec((2 * DN_TB, DN_WIDTH), lambda s: (tile(s), 6)),
            pl.BlockSpec((DN_TB, DN_TB), const),
            pl.BlockSpec((DN_TB, DN_TB), const),
            pl.BlockSpec((1, AB_PAD), const),
            pl.BlockSpec((1, AB_PAD), const),
            pl.BlockSpec((8, 1), const),
            pl.BlockSpec((8, 1), const),
            pl.BlockSpec((1, DN_HEAD_DIM), const),
            pl.BlockSpec((FFN_TM, d), lambda s: (prev(s), 0)),
            pl.BlockSpec((FFN_TM, ATT_WIDTH), lambda s: (prev(s), 0)),
            pl.BlockSpec((d, d), const, **resident),
            pl.BlockSpec((1, d), const),
            pl.BlockSpec((d, dff), const, **resident),
            pl.BlockSpec((dff, d), const, **resident),
        ],
        out_specs=pl.BlockSpec((FFN_TM, d), lambda s: (prev(s), 0)),
        out_shape=jax.ShapeDtypeStruct((t, d), F32),
        scratch_shapes=[
            pltpu.VMEM((DN_HEADS, DN_HEAD_DIM, DN_HEAD_DIM), F32),
            pltpu.VMEM((2, FFN_TM, DN_WIDTH), BF16),
            pltpu.VMEM(item(DN_C, DN_C), F32),
            pltpu.VMEM(item(DN_C, DN_C), BF16),
            pltpu.VMEM(item(DN_C, 2 * DN_HEAD_DIM), BF16),
            pltpu.VMEM(item(DN_C, DN_HEAD_DIM), BF16),
            pltpu.VMEM(item(DN_C, DN_HEAD_DIM), BF16),
            pltpu.VMEM(item(8, DN_HEAD_DIM), F32),
        ],
        compiler_params=pltpu.CompilerParams(
            dimension_semantics=("arbitrary",), vmem_limit_bytes=VMEM_LIMIT),
        name="deltanet_ffn",
    )(main, main, main, ab, ab, ab, abt, abt, abt, main, ltri, utri, alog, dtb, alogt, dtbt, dn_og,
      xf, o_att, w_out, fg, w1, w2)


def _layer(h, p, batch, seq):
    w_in = p["w_in"]
    w_main = w_in.astype(BF16)
    w_ab = jnp.pad(w_in[:, MAIN_COLS:], ((0, 0), (0, AB_PAD - 2 * DN_HEADS))).astype(BF16)
    head_of = np.arange(ATT_WIDTH) // ATT_HEAD_DIM
    seg = jnp.asarray((head_of[:, None] == head_of[None, :]) / ATT_HEAD_DIM, BF16)
    qg = (jnp.tile(p["att_q_gain"].astype(F32), ATT_HEADS) * (ATT_HEAD_DIM ** -0.5 * LOG2_E))[None]
    kg = jnp.tile(p["att_k_gain"].astype(F32), ATT_HEADS)[None]
    (main, ab, abt), (w_out, w_ff1, w_ff2) = _inproj(
        h, p["mix_norm_gain"].astype(F32)[None], w_main, w_ab, qg, kg, seg, p["dn_conv_w"].astype(F32), seq,
        [p["w_out"].astype(F32), p["w_ff1"].astype(F32), p["w_ff2"].astype(F32)])

    o_att = _attention(main, _bias_signal(p["rel_bias"]),
                       p["att_out_gain"].astype(F32)[None], batch, seq)

    pad_row = lambda v: jnp.pad(v.astype(F32), (0, AB_PAD - DN_HEADS))[None]
    pad_col = lambda v: jnp.pad(v.astype(F32), (0, 8 - DN_HEADS))[:, None]
    return _deltanet_ffn(h, main, ab, abt, o_att, pad_row(p["dn_a_log"]), pad_row(p["dn_dt_bias"]),
                         pad_col(p["dn_a_log"]), pad_col(p["dn_dt_bias"]),
                         p["dn_out_gain"].astype(F32)[None], w_out, p["ffn_norm_gain"].astype(F32)[None],
                         w_ff1, w_ff2, seq)


def kernel(x, mix_norm_gain, w_in, att_q_gain, att_k_gain, rel_bias, att_out_gain, dn_conv_w,
           dn_a_log, dn_dt_bias, dn_out_gain, w_out, ffn_norm_gain, w_ff1, w_ff2):
    batch, seq, d = x.shape
    assert seq % max(ATT_TQ, 2 * DN_TB) == 0 and (batch * seq) % max(IN_TM, FFN_TM) == 0
    params = dict(mix_norm_gain=mix_norm_gain, w_in=w_in, att_q_gain=att_q_gain,
                  att_k_gain=att_k_gain, rel_bias=rel_bias, att_out_gain=att_out_gain,
                  dn_conv_w=dn_conv_w, dn_a_log=dn_a_log, dn_dt_bias=dn_dt_bias,
                  dn_out_gain=dn_out_gain, w_out=w_out, ffn_norm_gain=ffn_norm_gain,
                  w_ff1=w_ff1, w_ff2=w_ff2)
    h = x.reshape(batch * seq, d)
    for layer in range(w_in.shape[0]):
        h = _layer(h, {name: v[layer] for name, v in params.items()}, batch, seq)
    return h.reshape(batch, seq, d)
```

```python
import functools

import numpy as np
import jax
import jax.numpy as jnp
from jax import lax
from jax.experimental import pallas as pl
from jax.experimental.pallas import tpu as pltpu

F32 = jnp.float32
BF16 = jnp.bfloat16

CHUNK = 64
N_PAST_CHUNKS = 8
ATT_HEADS = 8
ATT_HEAD_DIM = 64
ATT_WIDTH = ATT_HEADS * ATT_HEAD_DIM
MAX_REL = 256
DN_HEADS = 4
DN_HEAD_DIM = 128
DN_WIDTH = DN_HEADS * DN_HEAD_DIM
CONV_K = 4
EPS = 1e-6

MAIN_COLS = 3 * ATT_WIDTH + 4 * DN_WIDTH
AB_PAD = 128

LANES = 128
VMEM_LIMIT = 56 * 1024 * 1024

IN_TM = 512
IN_TN = 256
ATT_TQ = 4 * CHUNK
ATT_WIN = ATT_TQ + N_PAST_CHUNKS * CHUNK
ATT_TILES = 2
DN_C = 128
DN_TB = 2 * DN_C
DN_TAIL = 8
DN_ITEMS = (DN_TB // DN_C) * DN_HEADS
DN_CONV_FIRST = 3 * ATT_WIDTH // IN_TN
DN_CONV_CHUNKS = 3 * DN_WIDTH // IN_TN
DN_STATE_PER_LOCAL = 4
FFN_TM = 512
FFN_TF = 1024
FFN_TN = 256
NEG_BIG = -1e30
LOG2_E = 1.4426950408889634


def _dot(a, b):
    return jnp.dot(a, b, preferred_element_type=F32)


def _dot_nt(a, b):
    return lax.dot_general(a, b, (((1,), (1,)), ((), ())), preferred_element_type=F32)


def _dot_tn(a, b):
    return lax.dot_general(a, b, (((0,), (0,)), ((), ())), preferred_element_type=F32)


def _split3(a):
    p1 = a.astype(BF16)
    r1 = a - p1.astype(F32)
    p2 = r1.astype(BF16)
    p3 = (r1 - p2.astype(F32)).astype(BF16)
    return p1, p2, p3


def _sigmoid(x):
    return 1.0 / (1.0 + jnp.exp2(x * -LOG2_E))


def _softplus(x):
    return jnp.maximum(x, 0.0) + jnp.log(1.0 + jnp.exp(-jnp.abs(x)))


def _inproj_kernel(x_ref, g_ref, wm_ref, wab_ref, qg_ref, kg_ref, seg_ref, cw_ref, *rest, tiles_per_seq):
    n_cast = (len(rest) - 5) // 2
    cast_in, (main_ref, kt_ref, ab_ref, abt_ref) = rest[:n_cast], rest[n_cast:n_cast + 4]
    cast_out, stage_ref = rest[n_cast + 4:2 * n_cast + 4], rest[2 * n_cast + 4]
    i = pl.program_id(0)
    for src, dst in zip(cast_in, cast_out):
        dst[...] = src[...].astype(BF16)

    @pl.when(i % tiles_per_seq == 0)
    def _():
        stage_ref[...] = jnp.zeros_like(stage_ref)

    x = x_ref[...]
    ms = jnp.mean(x * x, axis=-1, keepdims=True)
    u = (x * lax.rsqrt(ms + EPS) * g_ref[...]).astype(BF16)
    conv = list(range(DN_CONV_FIRST, DN_CONV_FIRST + DN_CONV_CHUNKS))
    plain = [c for c in range(MAIN_COLS // IN_TN) if c not in conv]
    order = [c for pair in zip(plain, conv) for c in pair] + plain[len(conv):]
    project = lambda c: _dot(u, wm_ref[:, c * IN_TN:(c + 1) * IN_TN])
    y_next = project(order[0])
    for n, c in enumerate(order):
        cols = slice(c * IN_TN, (c + 1) * IN_TN)
        y = y_next
        if n + 1 < len(order):
            y_next = project(order[n + 1])
        if c < 2 * ATT_WIDTH // IN_TN:
            gain = qg_ref if c < ATT_WIDTH // IN_TN else kg_ref
            g0 = c * IN_TN % ATT_WIDTH
            m = _dot((y * y).astype(BF16), seg_ref[0:IN_TN, 0:IN_TN])
            y = y * lax.rsqrt(m + EPS) * gain[:, g0:g0 + IN_TN]
            if c >= ATT_WIDTH // IN_TN:
                kt_ref[g0:g0 + IN_TN, :] = y.T.astype(BF16)
        elif DN_CONV_FIRST <= c < DN_CONV_FIRST + DN_CONV_CHUNKS:
            j = c - DN_CONV_FIRST
            wcols = slice(j * IN_TN, (j + 1) * IN_TN)
            ext = jnp.concatenate([stage_ref[j], y], axis=0)
            z = cw_ref[0:1, wcols] * ext
            for t in range(1, CONV_K):
                z = cw_ref[t:t + 1, wcols] * ext + pltpu.roll(z, 1, axis=0)
            z = z[DN_TAIL:]
            stage_ref[j] = y[IN_TM - DN_TAIL:IN_TM, :]
            y = z * _sigmoid(z)
        main_ref[:, cols] = y.astype(BF16)
    ab = _dot(u, wab_ref[...])
    ab_ref[...] = ab
    abt_ref[...] = ab.T


def _inproj(xf, gain, w_main, w_ab, qg, kg, seg, conv_w, seq, to_cast):
    t, d = xf.shape
    steps = t // IN_TM
    const = lambda i: (0, 0)
    slab = lambda w: pl.BlockSpec((w.shape[0] // steps, w.shape[1]), lambda i: (i, 0))
    outs = pl.pallas_call(
        functools.partial(_inproj_kernel, tiles_per_seq=seq // IN_TM),
        grid=(steps,),
        in_specs=[
            pl.BlockSpec((IN_TM, d), lambda i: (i, 0)),
            pl.BlockSpec((1, d), const),
            pl.BlockSpec((d, MAIN_COLS), const),
            pl.BlockSpec((d, AB_PAD), const),
            pl.BlockSpec((1, ATT_WIDTH), const),
            pl.BlockSpec((1, ATT_WIDTH), const),
            pl.BlockSpec((ATT_WIDTH, ATT_WIDTH), const),
            pl.BlockSpec((CONV_K, 3 * DN_WIDTH), const),
        ] + [slab(w) for w in to_cast],
        out_specs=[
            pl.BlockSpec((IN_TM, MAIN_COLS), lambda i: (i, 0)),
            pl.BlockSpec((ATT_WIDTH, IN_TM), lambda i: (0, i)),
            pl.BlockSpec((IN_TM, AB_PAD), lambda i: (i, 0)),
            pl.BlockSpec((AB_PAD, IN_TM), lambda i: (0, i)),
        ] + [slab(w) for w in to_cast],
        out_shape=[
            jax.ShapeDtypeStruct((t, MAIN_COLS), BF16),
            jax.ShapeDtypeStruct((ATT_WIDTH, t), BF16),
            jax.ShapeDtypeStruct((t, AB_PAD), F32),
            jax.ShapeDtypeStruct((AB_PAD, t), F32),
        ] + [jax.ShapeDtypeStruct(w.shape, BF16) for w in to_cast],
        scratch_shapes=[pltpu.VMEM((DN_CONV_CHUNKS, DN_TAIL, IN_TN), F32)],
        compiler_params=pltpu.CompilerParams(
            dimension_semantics=("arbitrary",), vmem_limit_bytes=VMEM_LIMIT),
        name="inproj",
    )(xf, gain, w_main, w_ab, qg, kg, seg, conv_w, *to_cast)
    return outs[:4], outs[4:]


def _attn_tile(q_ref, row0, k_refs, v_refs, bias_ref, start_mask, og_ref, o_ref):
    lane = lax.broadcasted_iota(jnp.int32, (1, LANES), 1)
    per_tile = LANES // ATT_HEAD_DIM
    rows = slice(row0, row0 + ATT_TQ)

    def mine(h):
        e = h % per_tile
        return (lane >= e * ATT_HEAD_DIM) & (lane < (e + 1) * ATT_HEAD_DIM)

    def tile_lanes(h):
        return slice(h // per_tile * LANES, (h // per_tile + 1) * LANES)

    def scores(h):
        qp = q_ref[rows, tile_lanes(h)]
        q_e = jnp.where(mine(h), qp, jnp.zeros_like(qp))
        bias = bias_ref[h] if start_mask is None else bias_ref[h] + start_mask
        kw = jnp.concatenate([r[tile_lanes(h), :] for r in k_refs], axis=1)
        return _dot(q_e, kw) + bias

    def attend(h, s):
        mx = jnp.max(s, axis=-1, keepdims=True)
        pe = jnp.exp2(s - mx).astype(BF16)
        vw = jnp.concatenate([r[:, tile_lanes(h)] for r in v_refs], axis=0)
        o_e = _dot(pe, jnp.where(mine(h), vw, jnp.ones_like(vw)))
        return o_e / pltpu.roll(o_e, ATT_HEAD_DIM, axis=1)

    normed = []
    s_next = scores(0)
    for h in range(ATT_HEADS):
        s_cur = s_next
        if h + 1 < ATT_HEADS:
            s_next = scores(h + 1)
        normed.append(attend(h, s_cur))
        yield
    outs = [jnp.where(lane < ATT_HEAD_DIM, normed[t * per_tile], normed[t * per_tile + 1])
            for t in range(ATT_WIDTH // LANES)]
    o = jnp.concatenate(outs, axis=1)
    ms = jnp.mean(o * o, axis=-1, keepdims=True)
    o_ref[rows, :] = (o * lax.rsqrt(ms + EPS) * og_ref[...]).astype(BF16)


def _attn_kernel(q_ref, k0_ref, k1_ref, k2_ref, k3_ref, v0_ref, v1_ref, v2_ref, v3_ref, sig_ref, og_ref, o_ref,
                 bias_ref):
    i = pl.program_id(1)

    @pl.when((pl.program_id(0) == 0) & (i == 0))
    def _():
        r = lax.broadcasted_iota(jnp.int32, (ATT_TQ, ATT_WIN), 0)
        s = lax.broadcasted_iota(jnp.int32, (ATT_TQ, ATT_WIN), 1)
        back = (s >> 6) - (r >> 6)
        in_band = (back >= 0) & (back <= N_PAST_CHUNKS)
        for h in range(ATT_HEADS):
            rows = jnp.broadcast_to(sig_ref[h:h + 1, :], (ATT_TQ, sig_ref.shape[1]))
            toep = pltpu.roll(rows, 0, 1, stride=1, stride_axis=0)
            bias_ref[h] = jnp.where(in_band, toep[:, 0:ATT_WIN], NEG_BIG)

    def tiles(masks):
        _run(_weave(
            _attn_tile(q_ref, 0, (k0_ref, k1_ref, k2_ref), (v0_ref, v1_ref, v2_ref), bias_ref, masks[0],
                       og_ref, o_ref),
            _attn_tile(q_ref, ATT_TQ, (k1_ref, k2_ref, k3_ref), (v1_ref, v2_ref, v3_ref), bias_ref, masks[1],
                       og_ref, o_ref), 1))

    steps_in_past = N_PAST_CHUNKS * CHUNK // (ATT_TILES * ATT_TQ)

    @pl.when(i < steps_in_past)
    def _():
        jchunk = lax.broadcasted_iota(jnp.int32, (1, ATT_WIN), 1) >> 6
        tiles([jnp.where(jchunk + (ATT_TQ // CHUNK) * (ATT_TILES * i + n) - N_PAST_CHUNKS >= 0, 0.0, NEG_BIG)
               for n in range(ATT_TILES)])

    @pl.when(i >= steps_in_past)
    def _():
        tiles([None] * ATT_TILES)


def _attention(main, kt, signal, og, batch, seq):
    nb = seq // ATT_TQ
    steps = nb // ATT_TILES
    blk = (ATT_TQ, ATT_WIDTH)

    window_blk = lambda b, i, j: b * nb + jnp.maximum(ATT_TILES * i - 2 + j, 0)

    def k_spec(j):
        return pl.BlockSpec((ATT_WIDTH, ATT_TQ), lambda b, i: (0, window_blk(b, i, j)))

    def v_spec(j):
        return pl.BlockSpec(blk, lambda b, i: (window_blk(b, i, j), 2))

    step_rows = pl.BlockSpec((ATT_TILES * ATT_TQ, ATT_WIDTH), lambda b, i: (b * steps + i, 0))
    return pl.pallas_call(
        _attn_kernel,
        grid=(batch, steps),
        in_specs=[
            step_rows,
            k_spec(0), k_spec(1), k_spec(2), k_spec(3),
            v_spec(0), v_spec(1), v_spec(2), v_spec(3),
            pl.BlockSpec(signal.shape, lambda b, i: (0, 0)),
            pl.BlockSpec((1, ATT_WIDTH), lambda b, i: (0, 0)),
        ],
        out_specs=step_rows,
        out_shape=jax.ShapeDtypeStruct((batch * seq, ATT_WIDTH), BF16),
        scratch_shapes=[pltpu.VMEM((ATT_HEADS, ATT_TQ, ATT_WIN), F32)],
        compiler_params=pltpu.CompilerParams(
            dimension_semantics=("arbitrary", "arbitrary"), vmem_limit_bytes=VMEM_LIMIT),
        name="band_attention",
    )(main, kt, kt, kt, kt, main, main, main, main, signal, og)


def _bias_signal(rel_bias):
    nh = rel_bias.shape[0]
    lo = N_PAST_CHUNKS * CHUNK - (ATT_WIN - 1) + MAX_REL
    n_dist = ATT_TQ + ATT_WIN - 1
    n_tail = n_dist - (2 * MAX_REL + 1 - lo)
    rb = rel_bias.astype(F32) * LOG2_E
    by_dist = jnp.concatenate([rb[:, lo:], jnp.broadcast_to(rb[:, -1:], (nh, n_tail))], axis=1)
    period = ATT_TQ + ATT_WIN
    return jnp.roll(jnp.pad(by_dist[:, ::-1], ((0, 0), (0, period - n_dist))), -(ATT_TQ - 1), axis=1)


def _weave(main, side, every):
    n = 0
    done = object()
    while True:
        if n % every == 0:
            next(side, done)
        if next(main, done) is done:
            break
        n += 1
        yield
    _run(side)


def _chain(*stages):
    for g in stages:
        yield from g


def _run(stage):
    for _ in stage:
        pass


def _inv_unit_lower(mats, ri, ci, out):
    eye = jnp.where(ri == ci, 1.0, 0.0)
    diag8 = (ri >> 3) == (ci >> 3)
    n1 = [jnp.where(diag8, -a, 0.0) for a in mats]
    n1b = [n.astype(BF16) for n in n1]
    n2b = [_dot(n, n).astype(BF16) for n in n1b]
    yield
    n4b = [_dot(n, n).astype(BF16) for n in n2b]
    ts = [eye + n for n in n1]
    yield
    ts = [t + _dot(t.astype(BF16), n) for t, n in zip(ts, n2b)]
    yield
    ts = [t + _dot(t.astype(BF16), n) for t, n in zip(ts, n4b)]
    yield
    half = DN_C // 2
    r_h = lax.broadcasted_iota(jnp.int32, (half, DN_C), 0)
    c_o = lax.broadcasted_iota(jnp.int32, (half, DN_C), 1)
    for sh in range(3, 7):
        blk = 1 << sh
        odd = [slice(s0, s0 + blk) for s0 in range(blk, DN_C, 2 * blk)]
        even = [slice(s0, s0 + blk) for s0 in range(0, DN_C, 2 * blk)]
        take = lambda m, parts: jnp.concatenate([m[p] for p in parts], axis=0)
        r_o = r_h + (((r_h >> sh) + 1) << sh)
        pair =((r_o >> (sh + 1)) == (c_o >> (sh + 1))) & ((r_o >> sh) != (c_o >> sh))
        tb = [t.astype(BF16) for t in ts]
        lt = [_dot(jnp.where(pair, take(a, odd), 0.0).astype(BF16), t) for a, t in zip(mats, tb)]
        yield
        zero = jnp.zeros((blk, DN_C), F32)
        lt = [jnp.concatenate([p for m in range(half // blk) for p in (zero, x[m * blk:(m + 1) * blk])],
                              axis=0).astype(BF16) for x in lt]
        t_odd = [take(t, odd) for t in ts]
        new = [t - _dot(t.astype(BF16), x) for t, x in zip(t_odd, lt)]
        ts = [jnp.concatenate([p for m in range(half // blk)
                               for p in (t[even[m]], n[m * blk:(m + 1) * blk])], axis=0)
              for t, n in zip(ts, new)]
        yield
    out.extend(ts)


def _gdn_local(x_ref, ab_ref, abt_ref, ltri_ref, utri_ref, alog_ref, dtb_ref, alogt_ref, dtbt_ref, slot,
               a_scr, att_scr, rhs_scr, qd_scr, kt_scr, edl_scr):
    ab = ab_ref[...]
    g_col = -jnp.exp(alog_ref[...]) * _softplus(ab + dtb_ref[...])
    beta_col = _sigmoid(ab)
    g_row = -jnp.exp(alogt_ref[...]) * _softplus(abt_ref[0:8, :] + dtbt_ref[...])
    gc1, gc2, gc3 = _split3(g_col)
    gr1, gr2, gr3 = _split3(g_row)

    y = x_ref[...].astype(F32)

    ones = jnp.ones((DN_HEAD_DIM, DN_HEAD_DIM), BF16)
    sq = [(t * t).astype(BF16) for t in (y[:, g * DN_HEAD_DIM:(g + 1) * DN_HEAD_DIM]
                                         for g in range(2 * DN_HEADS))]
    ssq = [_dot(t, ones) for t in sq]
    yield
    ltri = ltri_ref[...]
    utri = utri_ref[...]
    d_col = _dot(ltri, gc1) + (_dot(ltri, gc2) + _dot(ltri, gc3))
    d_row = _dot(gr1, utri) + (_dot(gr2, utri) + _dot(gr3, utri))
    qk = []
    for g, tot in enumerate(ssq):
        inv = lax.rsqrt(tot + EPS)
        qk.append(y[:, g * DN_HEAD_DIM:(g + 1) * DN_HEAD_DIM]
                  * (inv * DN_HEAD_DIM ** -0.5 if g < DN_HEADS else inv))

    ri = lax.broadcasted_iota(jnp.int32, (DN_C, DN_C), 0)
    ci = lax.broadcasted_iota(jnp.int32, (DN_C, DN_C), 1)
    lmasks, kbqs, kbfs = [], [], []
    for n in range(DN_ITEMS):
        c, h = divmod(n, DN_HEADS)
        rows = slice(c * DN_C, (c + 1) * DN_C)
        q = qk[h][rows]
        k = qk[DN_HEADS + h][rows]
        v = y[rows, 2 * DN_WIDTH + h * DN_HEAD_DIM:2 * DN_WIDTH + (h + 1) * DN_HEAD_DIM]
        dc = d_col[rows, h:h + 1]
        dr = d_row[h:h + 1, rows]
        dl = dr[:, DN_C - 1:DN_C]
        bc = beta_col[rows, DN_HEADS + h:DN_HEADS + h + 1]
        kb = k * bc
        edc = jnp.exp(dc)
        rhs_scr[slot, n] = jnp.concatenate([v * bc, kb * edc], axis=1).astype(BF16)
        qd_scr[slot, n] = (q * edc).astype(BF16)
        kt_scr[slot, n] = (k * jnp.exp(dl - dc)).astype(BF16)
        edl_scr[slot, n] = jnp.broadcast_to(jnp.exp(dl), (8, DN_HEAD_DIM))
        lmasks.append(jnp.exp(jnp.where(ci <= ri, dc - dr, -jnp.inf)))
        kbqs.append(jnp.concatenate([kb, q], axis=0).astype(BF16))
        kbfs.append(k.astype(BF16))
    yield
    ps =[_dot_nt(kbq, kbf) for kbq, kbf in zip(kbqs, kbfs)]
    for n, (p, lmask) in enumerate(zip(ps, lmasks)):
        a_scr[slot, n] = jnp.where(ci < ri, p[0:DN_C] * lmask, 0.0)
        att_scr[slot, n] = (p[DN_C:2 * DN_C] * lmask).astype(BF16)


def _gdn_state(slot, gate_ref, og_ref, o_ref, out_row0, s_ref, a_scr, att_scr, rhs_scr, qd_scr, kt_scr,
               edl_scr):
    ri = lax.broadcasted_iota(jnp.int32, (DN_C, DN_C), 0)
    ci = lax.broadcasted_iota(jnp.int32, (DN_C, DN_C), 1)
    og = og_ref[...]
    tinvs = []
    yield from _inv_unit_lower([a_scr[slot, n] for n in range(DN_ITEMS)], ri, ci, tinvs)
    ws = [_dot(t.astype(BF16), rhs_scr[slot, n]) for n, t in enumerate(tinvs)]
    yield
    for c in range(DN_TB // DN_C):
        idx = [c * DN_HEADS + h for h in range(DN_HEADS)]
        s_old = [s_ref[h] for h in range(DN_HEADS)]
        rs = [_dot(jnp.concatenate([ws[n][:, DN_HEAD_DIM:].astype(BF16), qd_scr[slot, n]], axis=0),
                   s.astype(BF16)) for n, s in zip(idx, s_old)]
        yield
        vnb = [(ws[n][:, 0:DN_HEAD_DIM] - r[0:DN_C]).astype(BF16) for n, r in zip(idx, rs)]
        outs = [r[DN_C:2 * DN_C] + _dot(att_scr[slot, n], vn) for n, r, vn in zip(idx, rs, vnb)]
        yield
        for h, (n, vn) in enumerate(zip(idx, vnb)):
            s_ref[h] = s_old[h] * edl_scr[slot, n, 0:1, :] + _dot_tn(kt_scr[slot, n], vn)
        yield
        rows = slice(out_row0 + c * DN_C, out_row0 + (c + 1) * DN_C)
        for h, o in enumerate(outs):
            hs = slice(h * DN_HEAD_DIM, (h + 1) * DN_HEAD_DIM)
            ms = jnp.mean(o * o, axis=-1, keepdims=True)
            gt = gate_ref[rows, hs].astype(F32)
            o_ref[rows, hs] = (o * lax.rsqrt(ms + EPS) * og * (gt * _sigmoid(gt))).astype(BF16)


def _ffn_stages(x_ref, oa_ref, od_ref, wo_ref, fg_ref, w1_ref, w2_ref, o_ref):
    d = x_ref.shape[1]
    oa = oa_ref[...]
    od = od_ref[...]
    mixed = []
    for j in range(d // FFN_TN):
        cols = slice(j * FFN_TN, (j + 1) * FFN_TN)
        mixed.append(_dot(oa, wo_ref[0:ATT_WIDTH, cols]) + _dot(od, wo_ref[ATT_WIDTH:, cols]))
        yield
    h1 = x_ref[...] + jnp.concatenate(mixed, axis=1)
    ms = jnp.mean(h1 * h1, axis=-1, keepdims=True)
    n = (h1 * lax.rsqrt(ms + EPS) * fg_ref[...]).astype(BF16)

    def up(c, out):
        for j in range(FFN_TF // FFN_TN):
            col0 = c * FFN_TF + j * FFN_TN
            out.append(_dot(n, w1_ref[:, col0:col0 + FFN_TN]))
            yield

    n_chunks = w1_ref.shape[1] // FFN_TF
    acc = h1
    z_next = []
    yield from up(0, z_next)
    for c in range(n_chunks):
        z = z_next
        if c + 1 < n_chunks:
            z_next = []
            yield from up(c + 1, z_next)
        act = jnp.concatenate([jnp.square(jnp.maximum(piece, 0.0)).astype(BF16) for piece in z], axis=1)
        down = []
        for j in range(d // FFN_TN):
            down.append(_dot(act, w2_ref[c * FFN_TF:(c + 1) * FFN_TF, j * FFN_TN:(j + 1) * FFN_TN]))
            yield
        acc = acc + jnp.concatenate(down, axis=1)
    o_ref[...] = acc


def _gdn_ffn_kernel(x0_ref, x1_ref, x2_ref, ab0_ref, ab1_ref, ab2_ref, abt0_ref, abt1_ref, abt2_ref,
                    gate_ref, ltri_ref, utri_ref, alog_ref, dtb_ref, alogt_ref, dtbt_ref, dn_og_ref,
                    xres_ref, oa_ref, wo_ref, fg_ref, w1_ref, w2_ref, o_ref, s_ref, od_scr, *local_scr,
                    steps_per_seq):
    s = pl.program_id(0)
    cur = s % 2
    consts = (ltri_ref, utri_ref, alog_ref, dtb_ref, alogt_ref, dtbt_ref)

    def local(x_ref, ab_ref, abt_ref, slot):
        return _gdn_local(x_ref, ab_ref, abt_ref, *consts, slot, *local_scr)

    def state(slot, out_row0):
        return _gdn_state(slot, gate_ref, dn_og_ref, od_scr.at[cur], out_row0, s_ref, *local_scr)

    @pl.when(s == 0)
    def _():
        od_scr[1] = jnp.zeros(od_scr.shape[1:], od_scr.dtype)

    @pl.when(s % steps_per_seq == 0)
    def _():
        s_ref[...] = jnp.zeros_like(s_ref)
        _run(local(x0_ref, ab0_ref, abt0_ref, 0))

    deltanet = _chain(
        _weave(state(0, 0), local(x1_ref, ab1_ref, abt1_ref, 1), DN_STATE_PER_LOCAL),
        _weave(state(1, DN_TB), local(x2_ref, ab2_ref, abt2_ref, 0), DN_STATE_PER_LOCAL))
    ffn = _ffn_stages(xres_ref, oa_ref, od_scr.at[1 - cur], wo_ref, fg_ref, w1_ref, w2_ref, o_ref)
    _run(_weave(deltanet, ffn, 1))


def _deltanet_ffn(xf, main, ab, abt, o_att, alog, dtb, alogt, dtbt, dn_og, w_out, fg, w1, w2, seq):
    t, d = xf.shape
    dff = w1.shape[1]
    assert FFN_TM == 2 * DN_TB
    nb = seq // DN_TB
    spq = nb // 2
    steps = t // FFN_TM
    tile = lambda s: jnp.minimum(s, steps - 1)
    prev = lambda s: jnp.maximum(s - 1, 0)
    seq_blk0 = lambda s: tile(s) // spq * nb
    first = seq_blk0
    odd = lambda s: 2 * tile(s) + 1
    nxt = lambda s: jnp.minimum(2 * tile(s) + 2, seq_blk0(s) + nb - 1)
    const = lambda s: (0, 0)

    def x_spec(blk):
        return pl.BlockSpec((DN_TB, 3 * DN_WIDTH), lambda s: (blk(s), 1))

    def ab_spec(blk):
        return pl.BlockSpec((DN_TB, AB_PAD), lambda s: (blk(s), 0))

    def abt_spec(blk):
        return pl.BlockSpec((AB_PAD, DN_TB), lambda s: (0, blk(s)))

    r = np.arange(DN_TB)
    same = (r[:, None] // DN_C) == (r[None, :] // DN_C)
    ltri = jnp.asarray(same & (r[None, :] <= r[:, None]), BF16)
    utri = jnp.asarray(same & (r[:, None] <= r[None, :]), BF16)
    item = lambda *shape: (2, DN_ITEMS) + shape
    resident = dict(pipeline_mode=pl.Buffered(1))
    return pl.pallas_call(
        functools.partial(_gdn_ffn_kernel, steps_per_seq=spq),
        grid=(steps + 1,),
        in_specs=[
            x_spec(first), x_spec(odd), x_spec(nxt),
            ab_spec(first), ab_spec(odd), ab_spec(nxt),
            abt_spec(first), abt_spec(odd), abt_spec(nxt),
            pl.BlockSpec((2 * DN_TB, DN_WIDTH), lambda s: (tile(s), 6)),
            pl.BlockSpec((DN_TB, DN_TB), const),
            pl.BlockSpec((DN_TB, DN_TB), const),
            pl.BlockSpec((1, AB_PAD), const),
            pl.BlockSpec((1, AB_PAD), const),
            pl.BlockSpec((8, 1), const),
            pl.BlockSpec((8, 1), const),
            pl.BlockSpec((1, DN_HEAD_DIM), const),
            pl.BlockSpec((FFN_TM, d), lambda s: (prev(s), 0)),
            pl.BlockSpec((FFN_TM, ATT_WIDTH), lambda s: (prev(s), 0)),
            pl.BlockSpec((d, d), const, **resident),
            pl.BlockSpec((1, d), const),
            pl.BlockSpec((d, dff), const, **resident),
            pl.BlockSpec((dff, d), const, **resident),
        ],
        out_specs=pl.BlockSpec((FFN_TM, d), lambda s: (prev(s), 0)),
        out_shape=jax.ShapeDtypeStruct((t, d), F32),
        scratch_shapes=[
            pltpu.VMEM((DN_HEADS, DN_HEAD_DIM, DN_HEAD_DIM), F32),
            pltpu.VMEM((2, FFN_TM, DN_WIDTH), BF16),
            pltpu.VMEM(item(DN_C, DN_C), F32),
            pltpu.VMEM(item(DN_C, DN_C), BF16),
            pltpu.VMEM(item(DN_C, 2 * DN_HEAD_DIM), BF16),
            pltpu.VMEM(item(DN_C, DN_HEAD_DIM), BF16),
            pltpu.VMEM(item(DN_C, DN_HEAD_DIM), BF16),
            pltpu.VMEM(item(8, DN_HEAD_DIM), F32),
        ],
        compiler_params=pltpu.CompilerParams(
            dimension_semantics=("arbitrary",), vmem_limit_bytes=VMEM_LIMIT),
        name="deltanet_ffn",
    )(main, main, main, ab, ab, ab, abt, abt, abt, main, ltri, utri, alog, dtb, alogt, dtbt, dn_og,
      xf, o_att, w_out, fg, w1, w2)


def _layer(h, p, batch, seq):
    w_in = p["w_in"]
    w_main = w_in.astype(BF16)
    w_ab = jnp.pad(w_in[:, MAIN_COLS:], ((0, 0), (0, AB_PAD - 2 * DN_HEADS))).astype(BF16)
    head_of = np.arange(ATT_WIDTH) // ATT_HEAD_DIM
    seg = jnp.asarray((head_of[:, None] == head_of[None, :]) / ATT_HEAD_DIM, BF16)
    qg = (jnp.tile(p["att_q_gain"].astype(F32), ATT_HEADS) * (ATT_HEAD_DIM ** -0.5 * LOG2_E))[None]
    kg = jnp.tile(p["att_k_gain"].astype(F32), ATT_HEADS)[None]
    (main, kt, ab, abt), (w_out, w_ff1, w_ff2) = _inproj(
        h, p["mix_norm_gain"].astype(F32)[None], w_main, w_ab, qg, kg, seg, p["dn_conv_w"].astype(F32), seq,
        [p["w_out"].astype(F32), p["w_ff1"].astype(F32), p["w_ff2"].astype(F32)])

    o_att = _attention(main, kt, _bias_signal(p["rel_bias"]),
                       p["att_out_gain"].astype(F32)[None], batch, seq)

    pad_row = lambda v: jnp.pad(v.astype(F32), (0, AB_PAD - DN_HEADS))[None]
    pad_col = lambda v: jnp.pad(v.astype(F32), (0, 8 - DN_HEADS))[:, None]
    return _deltanet_ffn(h, main, ab, abt, o_att, pad_row(p["dn_a_log"]), pad_row(p["dn_dt_bias"]),
                         pad_col(p["dn_a_log"]), pad_col(p["dn_dt_bias"]),
                         p["dn_out_gain"].astype(F32)[None], w_out, p["ffn_norm_gain"].astype(F32)[None],
                         w_ff1, w_ff2, seq)


def kernel(x, mix_norm_gain, w_in, att_q_gain, att_k_gain, rel_bias, att_out_gain, dn_conv_w,
           dn_a_log, dn_dt_bias, dn_out_gain, w_out, ffn_norm_gain, w_ff1, w_ff2):
    batch, seq, d = x.shape
    assert seq % max(ATT_TQ, 2 * DN_TB) == 0 and (batch * seq) % max(IN_TM, FFN_TM) == 0
    params = dict(mix_norm_gain=mix_norm_gain, w_in=w_in, att_q_gain=att_q_gain,
                  att_k_gain=att_k_gain, rel_bias=rel_bias, att_out_gain=att_out_gain,
                  dn_conv_w=dn_conv_w, dn_a_log=dn_a_log, dn_dt_bias=dn_dt_bias,
                  dn_out_gain=dn_out_gain, w_out=w_out, ffn_norm_gain=ffn_norm_gain,
                  w_ff1=w_ff1, w_ff2=w_ff2)
    h = x.reshape(batch * seq, d)
    for layer in range(w_in.shape[0]):
        h = _layer(h, {name: v[layer] for name, v in params.items()}, batch, seq)
    return h.reshape(batch, seq, d)
```

```python
import functools

import numpy as np
import jax
import jax.numpy as jnp
from jax import lax
from jax.experimental import pallas as pl
from jax.experimental.pallas import tpu as pltpu

F32 = jnp.float32
BF16 = jnp.bfloat16

CHUNK = 64
N_PAST_CHUNKS = 8
ATT_HEADS = 8
ATT_HEAD_DIM = 64
ATT_WIDTH = ATT_HEADS * ATT_HEAD_DIM
MAX_REL = 256
DN_HEADS = 4
DN_HEAD_DIM = 128
DN_WIDTH = DN_HEADS * DN_HEAD_DIM
CONV_K = 4
EPS = 1e-6

MAIN_COLS = 3 * ATT_WIDTH + 4 * DN_WIDTH
AB_PAD = 128

LANES = 128
VMEM_LIMIT = 56 * 1024 * 1024

IN_TM = 512
IN_TN = 256
ATT_TQ = 4 * CHUNK
ATT_WIN = ATT_TQ + N_PAST_CHUNKS * CHUNK
ATT_TILES = 2
DN_C = 128
DN_TB = 2 * DN_C
DN_TAIL = 8
DN_ITEMS = (DN_TB // DN_C) * DN_HEADS
DN_CONV_FIRST = 3 * ATT_WIDTH // IN_TN
DN_CONV_CHUNKS = 3 * DN_WIDTH // IN_TN
DN_STATE_PER_LOCAL = 4
FFN_TM = 512
FFN_TF = 1024
FFN_TN = 256
NEG_BIG = -1e30
LOG2_E = 1.4426950408889634


def _dot(a, b):
    return jnp.dot(a, b, preferred_element_type=F32)


def _dot_nt(a, b):
    return lax.dot_general(a, b, (((1,), (1,)), ((), ())), preferred_element_type=F32)


def _dot_tn(a, b):
    return lax.dot_general(a, b, (((0,), (0,)), ((), ())), preferred_element_type=F32)


def _split3(a):
    p1 = a.astype(BF16)
    r1 = a - p1.astype(F32)
    p2 = r1.astype(BF16)
    p3 = (r1 - p2.astype(F32)).astype(BF16)
    return p1, p2, p3


def _sigmoid(x):
    return 1.0 / (1.0 + jnp.exp2(x * -LOG2_E))


def _softplus(x):
    return jnp.maximum(x, 0.0) + jnp.log(1.0 + jnp.exp(-jnp.abs(x)))


def _inproj_kernel(x_ref, g_ref, wm_ref, wab_ref, qg_ref, kg_ref, seg_ref, cw_ref, *rest, tiles_per_seq):
    n_cast = (len(rest) - 5) // 2
    cast_in, (main_ref, kt_ref, ab_ref, abt_ref) = rest[:n_cast], rest[n_cast:n_cast + 4]
    cast_out, stage_ref = rest[n_cast + 4:2 * n_cast + 4], rest[2 * n_cast + 4]
    i = pl.program_id(0)
    for src, dst in zip(cast_in, cast_out):
        dst[...] = src[...].astype(BF16)

    @pl.when(i % tiles_per_seq == 0)
    def _():
        stage_ref[...] = jnp.zeros_like(stage_ref)

    x = x_ref[...]
    ms = jnp.mean(x * x, axis=-1, keepdims=True)
    u = (x * lax.rsqrt(ms + EPS) * g_ref[...]).astype(BF16)
    conv = list(range(DN_CONV_FIRST, DN_CONV_FIRST + DN_CONV_CHUNKS))
    plain = [c for c in range(MAIN_COLS // IN_TN) if c not in conv]
    order = [c for pair in zip(plain, conv) for c in pair] + plain[len(conv):]
    project = lambda c: _dot(u, wm_ref[:, c * IN_TN:(c + 1) * IN_TN])
    ab = _dot(u, wab_ref[...])
    y_next = project(order[0])
    for n, c in enumerate(order):
        cols = slice(c * IN_TN, (c + 1) * IN_TN)
        y = y_next
        if n + 1 < len(order):
            y_next = project(order[n + 1])
        if n == 0:
            ab_ref[...] = ab
            abt_ref[...] = ab.T
        if c < 2 * ATT_WIDTH // IN_TN:
            gain = qg_ref if c < ATT_WIDTH // IN_TN else kg_ref
            g0 = c * IN_TN % ATT_WIDTH
            m = _dot((y * y).astype(BF16), seg_ref[0:IN_TN, 0:IN_TN])
            y = y * lax.rsqrt(m + EPS) * gain[:, g0:g0 + IN_TN]
            if c >= ATT_WIDTH // IN_TN:
                kt_ref[g0:g0 + IN_TN, :] = y.T.astype(BF16)
        elif DN_CONV_FIRST <= c < DN_CONV_FIRST + DN_CONV_CHUNKS:
            assert CONV_K == 4
            j = c - DN_CONV_FIRST
            wcols = slice(j * IN_TN, (j + 1) * IN_TN)
            ext = jnp.concatenate([stage_ref[j], y], axis=0)
            tap = lambda t: 0.5 * cw_ref[t:t + 1, wcols]
            prev = pltpu.roll(ext, 1, axis=0)
            older = tap(1) * ext + tap(0) * prev
            h = tap(3) * ext + tap(2) * prev + pltpu.roll(older, 2, axis=0)
            h = h[DN_TAIL:]
            stage_ref[j] = y[IN_TM - DN_TAIL:IN_TM, :]
            y = h + h * jnp.tanh(h)
        main_ref[:, cols] = y.astype(BF16)


def _inproj(xf, gain, w_main, w_ab, qg, kg, seg, conv_w, seq, to_cast):
    t, d = xf.shape
    steps = t // IN_TM
    const = lambda i: (0, 0)
    slab = lambda w: pl.BlockSpec((w.shape[0] // steps, w.shape[1]), lambda i: (i, 0))
    outs = pl.pallas_call(
        functools.partial(_inproj_kernel, tiles_per_seq=seq // IN_TM),
        grid=(steps,),
        in_specs=[
            pl.BlockSpec((IN_TM, d), lambda i: (i, 0)),
            pl.BlockSpec((1, d), const),
            pl.BlockSpec((d, MAIN_COLS), const),
            pl.BlockSpec((d, AB_PAD), const),
            pl.BlockSpec((1, ATT_WIDTH), const),
            pl.BlockSpec((1, ATT_WIDTH), const),
            pl.BlockSpec((ATT_WIDTH, ATT_WIDTH), const),
            pl.BlockSpec((CONV_K, 3 * DN_WIDTH), const),
        ] + [slab(w) for w in to_cast],
        out_specs=[
            pl.BlockSpec((IN_TM, MAIN_COLS), lambda i: (i, 0)),
            pl.BlockSpec((ATT_WIDTH, IN_TM), lambda i: (0, i)),
            pl.BlockSpec((IN_TM, AB_PAD), lambda i: (i, 0)),
            pl.BlockSpec((AB_PAD, IN_TM), lambda i: (0, i)),
        ] + [slab(w) for w in to_cast],
        out_shape=[
            jax.ShapeDtypeStruct((t, MAIN_COLS), BF16),
            jax.ShapeDtypeStruct((ATT_WIDTH, t), BF16),
            jax.ShapeDtypeStruct((t, AB_PAD), F32),
            jax.ShapeDtypeStruct((AB_PAD, t), F32),
        ] + [jax.ShapeDtypeStruct(w.shape, BF16) for w in to_cast],
        scratch_shapes=[pltpu.VMEM((DN_CONV_CHUNKS, DN_TAIL, IN_TN), F32)],
        compiler_params=pltpu.CompilerParams(
            dimension_semantics=("arbitrary",), vmem_limit_bytes=VMEM_LIMIT),
        name="inproj",
    )(xf, gain, w_main, w_ab, qg, kg, seg, conv_w, *to_cast)
    return outs[:4], outs[4:]


def _attn_tile(q_ref, row0, k_refs, v_refs, bias_ref, start_mask, og_ref, o_ref):
    lane = lax.broadcasted_iota(jnp.int32, (1, LANES), 1)
    per_tile = LANES // ATT_HEAD_DIM
    rows = slice(row0, row0 + ATT_TQ)

    def mine(h):
        e = h % per_tile
        return (lane >= e * ATT_HEAD_DIM) & (lane < (e + 1) * ATT_HEAD_DIM)

    def tile_lanes(h):
        return slice(h // per_tile * LANES, (h // per_tile + 1) * LANES)

    def scores(h):
        qp = q_ref[rows, tile_lanes(h)]
        q_e = jnp.where(mine(h), qp, jnp.zeros_like(qp))
        bias = bias_ref[h] if start_mask is None else bias_ref[h] + start_mask
        kw = jnp.concatenate([r[tile_lanes(h), :] for r in k_refs], axis=1)
        return _dot(q_e, kw) + bias

    def attend(h, s):
        mx = jnp.max(s, axis=-1, keepdims=True)
        pe = jnp.exp2(s - mx).astype(BF16)
        vw = jnp.concatenate([r[:, tile_lanes(h)] for r in v_refs], axis=0)
        o_e = _dot(pe, jnp.where(mine(h), vw, jnp.ones_like(vw)))
        return o_e / pltpu.roll(o_e, ATT_HEAD_DIM, axis=1)

    normed = []
    s_next = scores(0)
    for h in range(ATT_HEADS):
        s_cur = s_next
        if h + 1 < ATT_HEADS:
            s_next = scores(h + 1)
        normed.append(attend(h, s_cur))
        yield
    outs = [jnp.where(lane < ATT_HEAD_DIM, normed[t * per_tile], normed[t * per_tile + 1])
            for t in range(ATT_WIDTH // LANES)]
    o = jnp.concatenate(outs, axis=1)
    ms = jnp.mean(o * o, axis=-1, keepdims=True)
    o_ref[rows, :] = (o * lax.rsqrt(ms + EPS) * og_ref[...]).astype(BF16)


def _attn_kernel(q_ref, k0_ref, k1_ref, k2_ref, k3_ref, v0_ref, v1_ref, v2_ref, v3_ref, sig_ref, og_ref, o_ref,
                 bias_ref):
    i = pl.program_id(1)

    @pl.when((pl.program_id(0) == 0) & (i == 0))
    def _():
        r = lax.broadcasted_iota(jnp.int32, (ATT_TQ, ATT_WIN), 0)
        s = lax.broadcasted_iota(jnp.int32, (ATT_TQ, ATT_WIN), 1)
        back = (s >> 6) - (r >> 6)
        in_band = (back >= 0) & (back <= N_PAST_CHUNKS)
        for h in range(ATT_HEADS):
            rows = jnp.broadcast_to(sig_ref[h:h + 1, :], (ATT_TQ, sig_ref.shape[1]))
            toep = pltpu.roll(rows, 0, 1, stride=1, stride_axis=0)
            bias_ref[h] = jnp.where(in_band, toep[:, 0:ATT_WIN], NEG_BIG)

    def tiles(masks):
        _run(_weave(
            _attn_tile(q_ref, 0, (k0_ref, k1_ref, k2_ref), (v0_ref, v1_ref, v2_ref), bias_ref, masks[0],
                       og_ref, o_ref),
            _attn_tile(q_ref, ATT_TQ, (k1_ref, k2_ref, k3_ref), (v1_ref, v2_ref, v3_ref), bias_ref, masks[1],
                       og_ref, o_ref), 1))

    steps_in_past = N_PAST_CHUNKS * CHUNK // (ATT_TILES * ATT_TQ)

    @pl.when(i < steps_in_past)
    def _():
        jchunk = lax.broadcasted_iota(jnp.int32, (1, ATT_WIN), 1) >> 6
        tiles([jnp.where(jchunk + (ATT_TQ // CHUNK) * (ATT_TILES * i + n) - N_PAST_CHUNKS >= 0, 0.0, NEG_BIG)
               for n in range(ATT_TILES)])

    @pl.when(i >= steps_in_past)
    def _():
        tiles([None] * ATT_TILES)


def _attention(main, kt, signal, og, batch, seq):
    nb = seq // ATT_TQ
    steps = nb // ATT_TILES
    blk = (ATT_TQ, ATT_WIDTH)

    window_blk = lambda b, i, j: b * nb + jnp.maximum(ATT_TILES * i - 2 + j, 0)

    def k_spec(j):
        return pl.BlockSpec((ATT_WIDTH, ATT_TQ), lambda b, i: (0, window_blk(b, i, j)))

    def v_spec(j):
        return pl.BlockSpec(blk, lambda b, i: (window_blk(b, i, j), 2))

    step_rows = pl.BlockSpec((ATT_TILES * ATT_TQ, ATT_WIDTH), lambda b, i: (b * steps + i, 0))
    return pl.pallas_call(
        _attn_kernel,
        grid=(batch, steps),
        in_specs=[
            step_rows,
            k_spec(0), k_spec(1), k_spec(2), k_spec(3),
            v_spec(0), v_spec(1), v_spec(2), v_spec(3),
            pl.BlockSpec(signal.shape, lambda b, i: (0, 0)),
            pl.BlockSpec((1, ATT_WIDTH), lambda b, i: (0, 0)),
        ],
        out_specs=step_rows,
        out_shape=jax.ShapeDtypeStruct((batch * seq, ATT_WIDTH), BF16),
        scratch_shapes=[pltpu.VMEM((ATT_HEADS, ATT_TQ, ATT_WIN), F32)],
        compiler_params=pltpu.CompilerParams(
            dimension_semantics=("arbitrary", "arbitrary"), vmem_limit_bytes=VMEM_LIMIT),
        name="band_attention",
    )(main, kt, kt, kt, kt, main, main, main, main, signal, og)


def _bias_signal(rel_bias):
    nh = rel_bias.shape[0]
    lo = N_PAST_CHUNKS * CHUNK - (ATT_WIN - 1) + MAX_REL
    n_dist = ATT_TQ + ATT_WIN - 1
    n_tail = n_dist - (2 * MAX_REL + 1 - lo)
    rb = rel_bias.astype(F32) * LOG2_E
    by_dist = jnp.concatenate([rb[:, lo:], jnp.broadcast_to(rb[:, -1:], (nh, n_tail))], axis=1)
    period = ATT_TQ + ATT_WIN
    return jnp.roll(jnp.pad(by_dist[:, ::-1], ((0, 0), (0, period - n_dist))), -(ATT_TQ - 1), axis=1)


def _weave(main, side, every):
    n = 0
    done = object()
    while True:
        if n % every == 0:
            next(side, done)
        if next(main, done) is done:
            break
        n += 1
        yield
    _run(side)


def _chain(*stages):
    for g in stages:
        yield from g


def _run(stage):
    for _ in stage:
        pass


def _inv_unit_lower(mats, ri, ci, out):
    eye = jnp.where(ri == ci, 1.0, 0.0)
    diag8 = (ri >> 3) == (ci >> 3)
    n1 = [jnp.where(diag8, -a, 0.0) for a in mats]
    n1b = [n.astype(BF16) for n in n1]
    n2b = [_dot(n, n).astype(BF16) for n in n1b]
    yield
    n4b = [_dot(n, n).astype(BF16) for n in n2b]
    ts = [eye + n for n in n1]
    yield
    ts = [t + _dot(t.astype(BF16), n) for t, n in zip(ts, n2b)]
    yield
    ts = [t + _dot(t.astype(BF16), n) for t, n in zip(ts, n4b)]
    yield
    half = DN_C // 2
    r_h = lax.broadcasted_iota(jnp.int32, (half, DN_C), 0)
    c_o = lax.broadcasted_iota(jnp.int32, (half, DN_C), 1)
    for sh in range(3, 7):
        blk = 1 << sh
        odd = [slice(s0, s0 + blk) for s0 in range(blk, DN_C, 2 * blk)]
        even = [slice(s0, s0 + blk) for s0 in range(0, DN_C, 2 * blk)]
        take = lambda m, parts: jnp.concatenate([m[p] for p in parts], axis=0)
        r_o = r_h + (((r_h >> sh) + 1) << sh)
        pair =((r_o >> (sh + 1)) == (c_o >> (sh + 1))) & ((r_o >> sh) != (c_o >> sh))
        tb = [t.astype(BF16) for t in ts]
        lt = [_dot(jnp.where(pair, take(a, odd), 0.0).astype(BF16), t) for a, t in zip(mats, tb)]
        yield
        zero = jnp.zeros((blk, DN_C), F32)
        lt = [jnp.concatenate([p for m in range(half // blk) for p in (zero, x[m * blk:(m + 1) * blk])],
                              axis=0).astype(BF16) for x in lt]
        t_odd = [take(t, odd) for t in ts]
        new = [t - _dot(t.astype(BF16), x) for t, x in zip(t_odd, lt)]
        ts = [jnp.concatenate([p for m in range(half // blk)
                               for p in (t[even[m]], n[m * blk:(m + 1) * blk])], axis=0)
              for t, n in zip(ts, new)]
        yield
    out.extend(ts)


def _gdn_local(x_ref, ab_ref, abt_ref, ltri_ref, utri_ref, alog_ref, dtb_ref, alogt_ref, dtbt_ref, slot,
               a_scr, att_scr, rhs_scr, qd_scr, kt_scr, edl_scr):
    ab = ab_ref[...]
    g_col = -jnp.exp(alog_ref[...]) * _softplus(ab + dtb_ref[...])
    beta_col = _sigmoid(ab)
    g_row = -jnp.exp(alogt_ref[...]) * _softplus(abt_ref[0:8, :] + dtbt_ref[...])
    gc1, gc2, gc3 = _split3(g_col)
    gr1, gr2, gr3 = _split3(g_row)

    y = x_ref[...].astype(F32)

    ones = jnp.ones((DN_HEAD_DIM, DN_HEAD_DIM), BF16)
    sq = [(t * t).astype(BF16) for t in (y[:, g * DN_HEAD_DIM:(g + 1) * DN_HEAD_DIM]
                                         for g in range(2 * DN_HEADS))]
    ssq = [_dot(t, ones) for t in sq]
    yield
    ltri = ltri_ref[...]
    utri = utri_ref[...]
    d_col = _dot(ltri, gc1) + (_dot(ltri, gc2) + _dot(ltri, gc3))
    d_row = _dot(gr1, utri) + (_dot(gr2, utri) + _dot(gr3, utri))
    qk = []
    for g, tot in enumerate(ssq):
        inv = lax.rsqrt(tot + EPS)
        qk.append(y[:, g * DN_HEAD_DIM:(g + 1) * DN_HEAD_DIM]
                  * (inv * DN_HEAD_DIM ** -0.5 if g < DN_HEADS else inv))

    ri = lax.broadcasted_iota(jnp.int32, (DN_C, DN_C), 0)
    ci = lax.broadcasted_iota(jnp.int32, (DN_C, DN_C), 1)
    lmasks, kbqs, kbfs = [], [], []
    for n in range(DN_ITEMS):
        c, h = divmod(n, DN_HEADS)
        rows = slice(c * DN_C, (c + 1) * DN_C)
        q = qk[h][rows]
        k = qk[DN_HEADS + h][rows]
        v = y[rows, 2 * DN_WIDTH + h * DN_HEAD_DIM:2 * DN_WIDTH + (h + 1) * DN_HEAD_DIM]
        dc = d_col[rows, h:h + 1]
        dr = d_row[h:h + 1, rows]
        dl = dr[:, DN_C - 1:DN_C]
        bc = beta_col[rows, DN_HEADS + h:DN_HEADS + h + 1]
        kb = k * bc
        edc = jnp.exp(dc)
        rhs_scr[slot, n] = jnp.concatenate([v * bc, kb * edc], axis=1).astype(BF16)
        qd_scr[slot, n] = (q * edc).astype(BF16)
        kt_scr[slot, n] = (k * jnp.exp(dl - dc)).astype(BF16)
        edl_scr[slot, n] = jnp.broadcast_to(jnp.exp(dl), (8, DN_HEAD_DIM))
        lmasks.append(jnp.exp(jnp.where(ci <= ri, dc - dr, -jnp.inf)))
        kbqs.append(jnp.concatenate([kb, q], axis=0).astype(BF16))
        kbfs.append(k.astype(BF16))
    yield
    ps =[_dot_nt(kbq, kbf) for kbq, kbf in zip(kbqs, kbfs)]
    for n, (p, lmask) in enumerate(zip(ps, lmasks)):
        a_scr[slot, n] = jnp.where(ci < ri, p[0:DN_C] * lmask, 0.0)
        att_scr[slot, n] = (p[DN_C:2 * DN_C] * lmask).astype(BF16)


def _gdn_state(slot, gate_ref, og_ref, o_ref, out_row0, s_ref, a_scr, att_scr, rhs_scr, qd_scr, kt_scr,
               edl_scr):
    ri = lax.broadcasted_iota(jnp.int32, (DN_C, DN_C), 0)
    ci = lax.broadcasted_iota(jnp.int32, (DN_C, DN_C), 1)
    og = og_ref[...]
    tinvs = []
    yield from _inv_unit_lower([a_scr[slot, n] for n in range(DN_ITEMS)], ri, ci, tinvs)
    ws = [_dot(t.astype(BF16), rhs_scr[slot, n]) for n, t in enumerate(tinvs)]
    yield
    for c in range(DN_TB // DN_C):
        idx = [c * DN_HEADS + h for h in range(DN_HEADS)]
        s_old = [s_ref[h] for h in range(DN_HEADS)]
        rs = [_dot(jnp.concatenate([ws[n][:, DN_HEAD_DIM:].astype(BF16), qd_scr[slot, n]], axis=0),
                   s.astype(BF16)) for n, s in zip(idx, s_old)]
        yield
        vnb = [(ws[n][:, 0:DN_HEAD_DIM] - r[0:DN_C]).astype(BF16) for n, r in zip(idx, rs)]
        outs = [r[DN_C:2 * DN_C] + _dot(att_scr[slot, n], vn) for n, r, vn in zip(idx, rs, vnb)]
        yield
        for h, (n, vn) in enumerate(zip(idx, vnb)):
            s_ref[h] = s_old[h] * edl_scr[slot, n, 0:1, :] + _dot_tn(kt_scr[slot, n], vn)
        yield
        rows = slice(out_row0 + c * DN_C, out_row0 + (c + 1) * DN_C)
        for h, o in enumerate(outs):
            hs = slice(h * DN_HEAD_DIM, (h + 1) * DN_HEAD_DIM)
            ms = jnp.mean(o * o, axis=-1, keepdims=True)
            gt = gate_ref[rows, hs].astype(F32)
            o_ref[rows, hs] = (o * lax.rsqrt(ms + EPS) * og * (gt * _sigmoid(gt))).astype(BF16)


def _ffn_stages(x_ref, oa_ref, od_ref, wo_ref, fg_ref, w1_ref, w2_ref, o_ref):
    d = x_ref.shape[1]
    oa = oa_ref[...]
    od = od_ref[...]
    mixed = []
    for j in range(d // FFN_TN):
        cols = slice(j * FFN_TN, (j + 1) * FFN_TN)
        mixed.append(_dot(oa, wo_ref[0:ATT_WIDTH, cols]) + _dot(od, wo_ref[ATT_WIDTH:, cols]))
        yield
    h1 = x_ref[...] + jnp.concatenate(mixed, axis=1)
    ms = jnp.mean(h1 * h1, axis=-1, keepdims=True)
    n = (h1 * lax.rsqrt(ms + EPS) * fg_ref[...]).astype(BF16)

    def up(c, out):
        for j in range(FFN_TF // FFN_TN):
            col0 = c * FFN_TF + j * FFN_TN
            out.append(_dot(n, w1_ref[:, col0:col0 + FFN_TN]))
            yield

    n_chunks = w1_ref.shape[1] // FFN_TF
    acc = h1
    z_next = []
    yield from up(0, z_next)
    for c in range(n_chunks):
        z = z_next
        if c + 1 < n_chunks:
            z_next = []
            yield from up(c + 1, z_next)
        act = jnp.concatenate([jnp.square(jnp.maximum(piece, 0.0)).astype(BF16) for piece in z], axis=1)
        down = []
        for j in range(d // FFN_TN):
            down.append(_dot(act, w2_ref[c * FFN_TF:(c + 1) * FFN_TF, j * FFN_TN:(j + 1) * FFN_TN]))
            yield
        acc = acc + jnp.concatenate(down, axis=1)
    o_ref[...] = acc


def _gdn_ffn_kernel(x0_ref, x1_ref, x2_ref, ab0_ref, ab1_ref, ab2_ref, abt0_ref, abt1_ref, abt2_ref,
                    gate_ref, ltri_ref, utri_ref, alog_ref, dtb_ref, alogt_ref, dtbt_ref, dn_og_ref,
                    xres_ref, oa_ref, wo_ref, fg_ref, w1_ref, w2_ref, o_ref, s_ref, od_scr, *local_scr,
                    steps_per_seq):
    s = pl.program_id(0)
    cur = s % 2
    consts = (ltri_ref, utri_ref, alog_ref, dtb_ref, alogt_ref, dtbt_ref)

    def local(x_ref, ab_ref, abt_ref, slot):
        return _gdn_local(x_ref, ab_ref, abt_ref, *consts, slot, *local_scr)

    def state(slot, out_row0):
        return _gdn_state(slot, gate_ref, dn_og_ref, od_scr.at[cur], out_row0, s_ref, *local_scr)

    @pl.when(s == 0)
    def _():
        od_scr[1] = jnp.zeros(od_scr.shape[1:], od_scr.dtype)

    @pl.when(s % steps_per_seq == 0)
    def _():
        s_ref[...] = jnp.zeros_like(s_ref)
        _run(local(x0_ref, ab0_ref, abt0_ref, 0))

    deltanet = _chain(
        _weave(state(0, 0), local(x1_ref, ab1_ref, abt1_ref, 1), DN_STATE_PER_LOCAL),
        _weave(state(1, DN_TB), local(x2_ref, ab2_ref, abt2_ref, 0), DN_STATE_PER_LOCAL))
    ffn = _ffn_stages(xres_ref, oa_ref, od_scr.at[1 - cur], wo_ref, fg_ref, w1_ref, w2_ref, o_ref)
    _run(_weave(deltanet, ffn, 1))


def _deltanet_ffn(xf, main, ab, abt, o_att, alog, dtb, alogt, dtbt, dn_og, w_out, fg, w1, w2, seq):
    t, d = xf.shape
    dff = w1.shape[1]
    assert FFN_TM == 2 * DN_TB
    nb = seq // DN_TB
    spq = nb // 2
    steps = t // FFN_TM
    tile = lambda s: jnp.minimum(s, steps - 1)
    prev = lambda s: jnp.maximum(s - 1, 0)
    seq_blk0 = lambda s: tile(s) // spq * nb
    first = seq_blk0
    odd = lambda s: 2 * tile(s) + 1
    nxt = lambda s: jnp.minimum(2 * tile(s) + 2, seq_blk0(s) + nb - 1)
    const = lambda s: (0, 0)

    def x_spec(blk):
        return pl.BlockSpec((DN_TB, 3 * DN_WIDTH), lambda s: (blk(s), 1))

    def ab_spec(blk):
        return pl.BlockSpec((DN_TB, AB_PAD), lambda s: (blk(s), 0))

    def abt_spec(blk):
        return pl.BlockSpec((AB_PAD, DN_TB), lambda s: (0, blk(s)))

    r = np.arange(DN_TB)
    same = (r[:, None] // DN_C) == (r[None, :] // DN_C)
    ltri = jnp.asarray(same & (r[None, :] <= r[:, None]), BF16)
    utri = jnp.asarray(same & (r[:, None] <= r[None, :]), BF16)
    item = lambda *shape: (2, DN_ITEMS) + shape
    resident = dict(pipeline_mode=pl.Buffered(1))
    return pl.pallas_call(
        functools.partial(_gdn_ffn_kernel, steps_per_seq=spq),
        grid=(steps + 1,),
        in_specs=[
            x_spec(first), x_spec(odd), x_spec(nxt),
            ab_spec(first), ab_spec(odd), ab_spec(nxt),
            abt_spec(first), abt_spec(odd), abt_spec(nxt),
            pl.BlockSpec((2 * DN_TB, DN_WIDTH), lambda s: (tile(s), 6)),
            pl.BlockSpec((DN_TB, DN_TB), const),
            pl.BlockSpec((DN_TB, DN_TB), const),
            pl.BlockSpec((1, AB_PAD), const),
            pl.BlockSpec((1, AB_PAD), const),
            pl.BlockSpec((8, 1), const),
            pl.BlockSpec((8, 1), const),
            pl.BlockSpec((1, DN_HEAD_DIM), const),
            pl.BlockSpec((FFN_TM, d), lambda s: (prev(s), 0)),
            pl.BlockSpec((FFN_TM, ATT_WIDTH), lambda s: (prev(s), 0)),
            pl.BlockSpec((d, d), const, **resident),
            pl.BlockSpec((1, d), const),
            pl.BlockSpec((d, dff), const, **resident),
            pl.BlockSpec((dff, d), const, **resident),
        ],
        out_specs=pl.BlockSpec((FFN_TM, d), lambda s: (prev(s), 0)),
        out_shape=jax.ShapeDtypeStruct((t, d), F32),
        scratch_shapes=[
            pltpu.VMEM((DN_HEADS, DN_HEAD_DIM, DN_HEAD_DIM), F32),
            pltpu.VMEM((2, FFN_TM, DN_WIDTH), BF16),
            pltpu.VMEM(item(DN_C, DN_C), F32),
            pltpu.VMEM(item(DN_C, DN_C), BF16),
            pltpu.VMEM(item(DN_C, 2 * DN_HEAD_DIM), BF16),
            pltpu.VMEM(item(DN_C, DN_HEAD_DIM), BF16),
            pltpu.VMEM(item(DN_C, DN_HEAD_DIM), BF16),
            pltpu.VMEM(item(8, DN_HEAD_DIM), F32),
        ],
        compiler_params=pltpu.CompilerParams(
            dimension_semantics=("arbitrary",), vmem_limit_bytes=VMEM_LIMIT),
        name="deltanet_ffn",
    )(main, main, main, ab, ab, ab, abt, abt, abt, main, ltri, utri, alog, dtb, alogt, dtbt, dn_og,
      xf, o_att, w_out, fg, w1, w2)


def _layer(h, p, batch, seq):
    w_in = p["w_in"]
    w_main = w_in.astype(BF16)
    w_ab = jnp.pad(w_in[:, MAIN_COLS:], ((0, 0), (0, AB_PAD - 2 * DN_HEADS))).astype(BF16)
    head_of = np.arange(ATT_WIDTH) // ATT_HEAD_DIM
    seg = jnp.asarray((head_of[:, None] == head_of[None, :]) / ATT_HEAD_DIM, BF16)
    qg = (jnp.tile(p["att_q_gain"].astype(F32), ATT_HEADS) * (ATT_HEAD_DIM ** -0.5 * LOG2_E))[None]
    kg = jnp.tile(p["att_k_gain"].astype(F32), ATT_HEADS)[None]
    (main, kt, ab, abt), (w_out, w_ff1, w_ff2) = _inproj(
        h, p["mix_norm_gain"].astype(F32)[None], w_main, w_ab, qg, kg, seg, p["dn_conv_w"].astype(F32), seq,
        [p["w_out"].astype(F32), p["w_ff1"].astype(F32), p["w_ff2"].astype(F32)])

    o_att = _attention(main, kt, _bias_signal(p["rel_bias"]),
                       p["att_out_gain"].astype(F32)[None], batch, seq)

    pad_row = lambda v: jnp.pad(v.astype(F32), (0, AB_PAD - DN_HEADS))[None]
    pad_col = lambda v: jnp.pad(v.astype(F32), (0, 8 - DN_HEADS))[:, None]
    return _deltanet_ffn(h, main, ab, abt, o_att, pad_row(p["dn_a_log"]), pad_row(p["dn_dt_bias"]),
                         pad_col(p["dn_a_log"]), pad_col(p["dn_dt_bias"]),
                         p["dn_out_gain"].astype(F32)[None], w_out, p["ffn_norm_gain"].astype(F32)[None],
                         w_ff1, w_ff2, seq)


def kernel(x, mix_norm_gain, w_in, att_q_gain, att_k_gain, rel_bias, att_out_gain, dn_conv_w,
           dn_a_log, dn_dt_bias, dn_out_gain, w_out, ffn_norm_gain, w_ff1, w_ff2):
    batch, seq, d = x.shape
    assert seq % max(ATT_TQ, 2 * DN_TB) == 0 and (batch * seq) % max(IN_TM, FFN_TM) == 0
    params = dict(mix_norm_gain=mix_norm_gain, w_in=w_in, att_q_gain=att_q_gain,
                  att_k_gain=att_k_gain, rel_bias=rel_bias, att_out_gain=att_out_gain,
                  dn_conv_w=dn_conv_w, dn_a_log=dn_a_log, dn_dt_bias=dn_dt_bias,
                  dn_out_gain=dn_out_gain, w_out=w_out, ffn_norm_gain=ffn_norm_gain,
                  w_ff1=w_ff1, w_ff2=w_ff2)
    h = x.reshape(batch * seq, d)
    for layer in range(w_in.shape[0]):
        h = _layer(h, {name: v[layer] for name, v in params.items()}, batch, seq)
    return h.reshape(batch, seq, d)
```

```python
import functools

import numpy as np
import jax
import jax.numpy as jnp
from jax import lax
from jax.experimental import pallas as pl
from jax.experimental.pallas import tpu as pltpu

F32 = jnp.float32
BF16 = jnp.bfloat16

CHUNK = 64
N_PAST_CHUNKS = 8
ATT_HEADS = 8
ATT_HEAD_DIM = 64
ATT_WIDTH = ATT_HEADS * ATT_HEAD_DIM
MAX_REL = 256
DN_HEADS = 4
DN_HEAD_DIM = 128
DN_WIDTH = DN_HEADS * DN_HEAD_DIM
CONV_K = 4
EPS = 1e-6

MAIN_COLS = 3 * ATT_WIDTH + 4 * DN_WIDTH
AB_PAD = 128

LANES = 128
VMEM_LIMIT = 56 * 1024 * 1024

IN_TM = 512
IN_TN = 256
ATT_TQ = 4 * CHUNK
ATT_WIN = ATT_TQ + N_PAST_CHUNKS * CHUNK
ATT_TILES = 2
DN_C = 128
DN_TB = 2 * DN_C
DN_TAIL = 8
DN_ITEMS = (DN_TB // DN_C) * DN_HEADS
DN_CONV_FIRST = 3 * ATT_WIDTH // IN_TN
DN_CONV_CHUNKS = 3 * DN_WIDTH // IN_TN
DN_STATE_PER_LOCAL = 4
FFN_TM = 512
FFN_TF = 1024
FFN_TN = 256
NEG_BIG = -1e30
LOG2_E = 1.4426950408889634


def _dot(a, b):
    return jnp.dot(a, b, preferred_element_type=F32)


def _dot_nt(a, b):
    return lax.dot_general(a, b, (((1,), (1,)), ((), ())), preferred_element_type=F32)


def _dot_tn(a, b):
    return lax.dot_general(a, b, (((0,), (0,)), ((), ())), preferred_element_type=F32)


def _split3(a):
    p1 = a.astype(BF16)
    r1 = a - p1.astype(F32)
    p2 = r1.astype(BF16)
    p3 = (r1 - p2.astype(F32)).astype(BF16)
    return p1, p2, p3


def _sigmoid(x):
    return 1.0 / (1.0 + jnp.exp2(x * -LOG2_E))


def _softplus(x):
    return jnp.maximum(x, 0.0) + jnp.log(1.0 + jnp.exp(-jnp.abs(x)))


def _inproj_kernel(x_ref, g_ref, wm_ref, wab_ref, qg_ref, kg_ref, seg_ref, cw_ref, *rest, tiles_per_seq):
    n_cast = (len(rest) - 5) // 2
    cast_in, (main_ref, kt_ref, ab_ref, abt_ref) = rest[:n_cast], rest[n_cast:n_cast + 4]
    cast_out, stage_ref = rest[n_cast + 4:2 * n_cast + 4], rest[2 * n_cast + 4]
    i = pl.program_id(0)
    for src, dst in zip(cast_in, cast_out):
        dst[...] = src[...].astype(BF16)

    @pl.when(i % tiles_per_seq == 0)
    def _():
        stage_ref[...] = jnp.zeros_like(stage_ref)

    x = x_ref[...]
    ms = jnp.mean(x * x, axis=-1, keepdims=True)
    u = (x * lax.rsqrt(ms + EPS) * g_ref[...]).astype(BF16)
    conv = list(range(DN_CONV_FIRST, DN_CONV_FIRST + DN_CONV_CHUNKS))
    plain = [c for c in range(MAIN_COLS // IN_TN) if c not in conv]
    order = [c for pair in zip(plain, conv) for c in pair] + plain[len(conv):]
    project = lambda c: _dot(u, wm_ref[:, c * IN_TN:(c + 1) * IN_TN])
    ab = _dot(u, wab_ref[...])
    y_next = project(order[0])
    for n, c in enumerate(order):
        cols = slice(c * IN_TN, (c + 1) * IN_TN)
        y = y_next
        if n + 1 < len(order):
            y_next = project(order[n + 1])
        if n == 0:
            ab_ref[...] = ab
            abt_ref[...] = ab.T
        if c < 2 * ATT_WIDTH // IN_TN:
            gain = qg_ref if c < ATT_WIDTH // IN_TN else kg_ref
            g0 = c * IN_TN % ATT_WIDTH
            m = _dot((y * y).astype(BF16), seg_ref[0:IN_TN, 0:IN_TN])
            y = y * lax.rsqrt(m + EPS) * gain[:, g0:g0 + IN_TN]
            if c >= ATT_WIDTH // IN_TN:
                kt_ref[g0:g0 + IN_TN, :] = y.T.astype(BF16)
        elif DN_CONV_FIRST <= c < DN_CONV_FIRST + DN_CONV_CHUNKS:
            assert CONV_K == 4
            j = c - DN_CONV_FIRST
            wcols = slice(j * IN_TN, (j + 1) * IN_TN)
            ext = jnp.concatenate([stage_ref[j], y], axis=0)
            tap = lambda t: 0.5 * cw_ref[t:t + 1, wcols]
            prev = pltpu.roll(ext, 1, axis=0)
            older = tap(1) * ext + tap(0) * prev
            h = tap(3) * ext + tap(2) * prev + pltpu.roll(older, 2, axis=0)
            h = h[DN_TAIL:]
            stage_ref[j] = y[IN_TM - DN_TAIL:IN_TM, :]
            y = h + h * jnp.tanh(h)
        main_ref[:, cols] = y.astype(BF16)


def _inproj(xf, gain, w_main, w_ab, qg, kg, seg, conv_w, seq, to_cast):
    t, d = xf.shape
    steps = t // IN_TM
    const = lambda i: (0, 0)
    slab = lambda w: pl.BlockSpec((w.shape[0] // steps, w.shape[1]), lambda i: (i, 0))
    outs = pl.pallas_call(
        functools.partial(_inproj_kernel, tiles_per_seq=seq // IN_TM),
        grid=(steps,),
        in_specs=[
            pl.BlockSpec((IN_TM, d), lambda i: (i, 0)),
            pl.BlockSpec((1, d), const),
            pl.BlockSpec((d, MAIN_COLS), const),
            pl.BlockSpec((d, AB_PAD), const),
            pl.BlockSpec((1, ATT_WIDTH), const),
            pl.BlockSpec((1, ATT_WIDTH), const),
            pl.BlockSpec((ATT_WIDTH, ATT_WIDTH), const),
            pl.BlockSpec((CONV_K, 3 * DN_WIDTH), const),
        ] + [slab(w) for w in to_cast],
        out_specs=[
            pl.BlockSpec((IN_TM, MAIN_COLS), lambda i: (i, 0)),
            pl.BlockSpec((ATT_WIDTH, IN_TM), lambda i: (0, i)),
            pl.BlockSpec((IN_TM, AB_PAD), lambda i: (i, 0)),
            pl.BlockSpec((AB_PAD, IN_TM), lambda i: (0, i)),
        ] + [slab(w) for w in to_cast],
        out_shape=[
            jax.ShapeDtypeStruct((t, MAIN_COLS), BF16),
            jax.ShapeDtypeStruct((ATT_WIDTH, t), BF16),
            jax.ShapeDtypeStruct((t, AB_PAD), F32),
            jax.ShapeDtypeStruct((AB_PAD, t), F32),
        ] + [jax.ShapeDtypeStruct(w.shape, BF16) for w in to_cast],
        scratch_shapes=[pltpu.VMEM((DN_CONV_CHUNKS, DN_TAIL, IN_TN), F32)],
        compiler_params=pltpu.CompilerParams(
            dimension_semantics=("arbitrary",), vmem_limit_bytes=VMEM_LIMIT),
        name="inproj",
    )(xf, gain, w_main, w_ab, qg, kg, seg, conv_w, *to_cast)
    return outs[:4], outs[4:]


def _attn_tile(q_ref, row0, k_refs, v_refs, bias_ref, start_mask, og_ref, o_ref):
    lane = lax.broadcasted_iota(jnp.int32, (1, LANES), 1)
    per_tile = LANES // ATT_HEAD_DIM
    rows = slice(row0, row0 + ATT_TQ)

    def mine(h):
        e = h % per_tile
        return (lane >= e * ATT_HEAD_DIM) & (lane < (e + 1) * ATT_HEAD_DIM)

    def tile_lanes(h):
        return slice(h // per_tile * LANES, (h // per_tile + 1) * LANES)

    def scores(h):
        qp = q_ref[rows, tile_lanes(h)]
        q_e = jnp.where(mine(h), qp, jnp.zeros_like(qp))
        bias = bias_ref[h] if start_mask is None else bias_ref[h] + start_mask
        kw = jnp.concatenate([r[tile_lanes(h), :] for r in k_refs], axis=1)
        return _dot(q_e, kw) + bias

    def attend(h, s):
        mx = jnp.max(s, axis=-1, keepdims=True)
        pe = jnp.exp2(s - mx).astype(BF16)
        vw = jnp.concatenate([r[:, tile_lanes(h)] for r in v_refs], axis=0)
        o_e = _dot(pe, jnp.where(mine(h), vw, jnp.ones_like(vw)))
        return o_e / pltpu.roll(o_e, ATT_HEAD_DIM, axis=1)

    normed = []
    s_next = scores(0)
    for h in range(ATT_HEADS):
        s_cur = s_next
        if h + 1 < ATT_HEADS:
            s_next = scores(h + 1)
        normed.append(attend(h, s_cur))
        yield
    outs = [jnp.where(lane < ATT_HEAD_DIM, normed[t * per_tile], normed[t * per_tile + 1])
            for t in range(ATT_WIDTH // LANES)]
    o = jnp.concatenate(outs, axis=1)
    ms = jnp.mean(o * o, axis=-1, keepdims=True)
    o_ref[rows, :] = (o * lax.rsqrt(ms + EPS) * og_ref[...]).astype(BF16)


def _attn_kernel(q_ref, k0_ref, k1_ref, k2_ref, k3_ref, v0_ref, v1_ref, v2_ref, v3_ref, sig_ref, og_ref, o_ref,
                 bias_ref):
    i = pl.program_id(1)

    @pl.when((pl.program_id(0) == 0) & (i == 0))
    def _():
        r = lax.broadcasted_iota(jnp.int32, (ATT_TQ, ATT_WIN), 0)
        s = lax.broadcasted_iota(jnp.int32, (ATT_TQ, ATT_WIN), 1)
        back = (s >> 6) - (r >> 6)
        in_band = (back >= 0) & (back <= N_PAST_CHUNKS)
        for h in range(ATT_HEADS):
            rows = jnp.broadcast_to(sig_ref[h:h + 1, :], (ATT_TQ, sig_ref.shape[1]))
            toep = pltpu.roll(rows, 0, 1, stride=1, stride_axis=0)
            bias_ref[h] = jnp.where(in_band, toep[:, 0:ATT_WIN], NEG_BIG)

    def tiles(masks):
        _run(_weave(
            _attn_tile(q_ref, 0, (k0_ref, k1_ref, k2_ref), (v0_ref, v1_ref, v2_ref), bias_ref, masks[0],
                       og_ref, o_ref),
            _attn_tile(q_ref, ATT_TQ, (k1_ref, k2_ref, k3_ref), (v1_ref, v2_ref, v3_ref), bias_ref, masks[1],
                       og_ref, o_ref), 1))

    steps_in_past = N_PAST_CHUNKS * CHUNK // (ATT_TILES * ATT_TQ)

    @pl.when(i < steps_in_past)
    def _():
        jchunk = lax.broadcasted_iota(jnp.int32, (1, ATT_WIN), 1) >> 6
        tiles([jnp.where(jchunk + (ATT_TQ // CHUNK) * (ATT_TILES * i + n) - N_PAST_CHUNKS >= 0, 0.0, NEG_BIG)
               for n in range(ATT_TILES)])

    @pl.when(i >= steps_in_past)
    def _():
        tiles([None] * ATT_TILES)


def _attention(main, kt, signal, og, batch, seq):
    nb = seq // ATT_TQ
    steps = nb // ATT_TILES
    blk = (ATT_TQ, ATT_WIDTH)

    window_blk = lambda b, i, j: b * nb + jnp.maximum(ATT_TILES * i - 2 + j, 0)

    def k_spec(j):
        return pl.BlockSpec((ATT_WIDTH, ATT_TQ), lambda b, i: (0, window_blk(b, i, j)))

    def v_spec(j):
        return pl.BlockSpec(blk, lambda b, i: (window_blk(b, i, j), 2))

    step_rows = pl.BlockSpec((ATT_TILES * ATT_TQ, ATT_WIDTH), lambda b, i: (b * steps + i, 0))
    return pl.pallas_call(
        _attn_kernel,
        grid=(batch, steps),
        in_specs=[
            step_rows,
            k_spec(0), k_spec(1), k_spec(2), k_spec(3),
            v_spec(0), v_spec(1), v_spec(2), v_spec(3),
            pl.BlockSpec(signal.shape, lambda b, i: (0, 0)),
            pl.BlockSpec((1, ATT_WIDTH), lambda b, i: (0, 0)),
        ],
        out_specs=step_rows,
        out_shape=jax.ShapeDtypeStruct((batch * seq, ATT_WIDTH), BF16),
        scratch_shapes=[pltpu.VMEM((ATT_HEADS, ATT_TQ, ATT_WIN), F32)],
        compiler_params=pltpu.CompilerParams(
            dimension_semantics=("arbitrary", "arbitrary"), vmem_limit_bytes=VMEM_LIMIT),
        name="band_attention",
    )(main, kt, kt, kt, kt, main, main, main, main, signal, og)


def _bias_signal(rel_bias):
    nh = rel_bias.shape[0]
    lo = N_PAST_CHUNKS * CHUNK - (ATT_WIN - 1) + MAX_REL
    n_dist = ATT_TQ + ATT_WIN - 1
    n_tail = n_dist - (2 * MAX_REL + 1 - lo)
    rb = rel_bias.astype(F32) * LOG2_E
    by_dist = jnp.concatenate([rb[:, lo:], jnp.broadcast_to(rb[:, -1:], (nh, n_tail))], axis=1)
    period = ATT_TQ + ATT_WIN
    return jnp.roll(jnp.pad(by_dist[:, ::-1], ((0, 0), (0, period - n_dist))), -(ATT_TQ - 1), axis=1)


def _weave(main, side, every):
    n = 0
    done = object()
    while True:
        if n % every == 0:
            next(side, done)
        if next(main, done) is done:
            break
        n += 1
        yield
    _run(side)


def _chain(*stages):
    for g in stages:
        yield from g


def _run(stage):
    for _ in stage:
        pass


def _inv_unit_lower(mats, ri, ci, out):
    eye = jnp.where(ri == ci, 1.0, 0.0)
    diag8 = (ri >> 3) == (ci >> 3)
    n1 = [jnp.where(diag8, -a, 0.0) for a in mats]
    n1b = [n.astype(BF16) for n in n1]
    n2b = [_dot(n, n).astype(BF16) for n in n1b]
    yield
    n4b = [_dot(n, n).astype(BF16) for n in n2b]
    ts = [eye + n for n in n1]
    yield
    ts = [t + _dot(t.astype(BF16), n) for t, n in zip(ts, n2b)]
    yield
    ts = [t + _dot(t.astype(BF16), n) for t, n in zip(ts, n4b)]
    yield
    half = DN_C // 2
    r_h = lax.broadcasted_iota(jnp.int32, (half, DN_C), 0)
    c_o = lax.broadcasted_iota(jnp.int32, (half, DN_C), 1)
    for sh in range(3, 7):
        blk = 1 << sh
        odd = [slice(s0, s0 + blk) for s0 in range(blk, DN_C, 2 * blk)]
        even = [slice(s0, s0 + blk) for s0 in range(0, DN_C, 2 * blk)]
        take = lambda m, parts: jnp.concatenate([m[p] for p in parts], axis=0)
        r_o = r_h + (((r_h >> sh) + 1) << sh)
        pair =((r_o >> (sh + 1)) == (c_o >> (sh + 1))) & ((r_o >> sh) != (c_o >> sh))
        tb = [t.astype(BF16) for t in ts]
        lt = [_dot(jnp.where(pair, take(a, odd), 0.0).astype(BF16), t) for a, t in zip(mats, tb)]
        yield
        zero = jnp.zeros((blk, DN_C), F32)
        lt = [jnp.concatenate([p for m in range(half // blk) for p in (zero, x[m * blk:(m + 1) * blk])],
                              axis=0).astype(BF16) for x in lt]
        t_odd = [take(t, odd) for t in ts]
        new = [t - _dot(t.astype(BF16), x) for t, x in zip(t_odd, lt)]
        ts = [jnp.concatenate([p for m in range(half // blk)
                               for p in (t[even[m]], n[m * blk:(m + 1) * blk])], axis=0)
              for t, n in zip(ts, new)]
        yield
    out.extend(ts)


def _gdn_local(x_ref, ab_ref, abt_ref, ltri_ref, utri_ref, alog_ref, dtb_ref, alogt_ref, dtbt_ref, slot,
               a_scr, att_scr, rhs_scr, qd_scr, kt_scr, edl_scr):
    ab = ab_ref[...]
    g_col = -jnp.exp(alog_ref[...]) * _softplus(ab + dtb_ref[...])
    beta_col = _sigmoid(ab)
    g_row = -jnp.exp(alogt_ref[...]) * _softplus(abt_ref[0:8, :] + dtbt_ref[...])
    gc1, gc2, gc3 = _split3(g_col)
    gr1, gr2, gr3 = _split3(g_row)

    y = x_ref[...].astype(F32)

    ones = jnp.ones((DN_HEAD_DIM, DN_HEAD_DIM), BF16)
    sq = [(t * t).astype(BF16) for t in (y[:, g * DN_HEAD_DIM:(g + 1) * DN_HEAD_DIM]
                                         for g in range(2 * DN_HEADS))]
    ssq = [_dot(t, ones) for t in sq]
    yield
    ltri = ltri_ref[...]
    utri = utri_ref[...]
    d_col = _dot(ltri, gc1) + (_dot(ltri, gc2) + _dot(ltri, gc3))
    d_row = _dot(gr1, utri) + (_dot(gr2, utri) + _dot(gr3, utri))
    qk = []
    for g, tot in enumerate(ssq):
        inv = lax.rsqrt(tot + EPS)
        qk.append(y[:, g * DN_HEAD_DIM:(g + 1) * DN_HEAD_DIM]
                  * (inv * DN_HEAD_DIM ** -0.5 if g < DN_HEADS else inv))

    ri = lax.broadcasted_iota(jnp.int32, (DN_C, DN_C), 0)
    ci = lax.broadcasted_iota(jnp.int32, (DN_C, DN_C), 1)
    lmasks, kbqs, kbfs = [], [], []
    for n in range(DN_ITEMS):
        c, h = divmod(n, DN_HEADS)
        rows = slice(c * DN_C, (c + 1) * DN_C)
        q = qk[h][rows]
        k = qk[DN_HEADS + h][rows]
        v = y[rows, 2 * DN_WIDTH + h * DN_HEAD_DIM:2 * DN_WIDTH + (h + 1) * DN_HEAD_DIM]
        dc = d_col[rows, h:h + 1]
        dr = d_row[h:h + 1, rows]
        dl = dr[:, DN_C - 1:DN_C]
        bc = beta_col[rows, DN_HEADS + h:DN_HEADS + h + 1]
        assert DN_HEAD_DIM == DN_C
        bc = jnp.broadcast_to(bc, (DN_C, DN_HEAD_DIM))
        dc = jnp.broadcast_to(dc, (DN_C, DN_HEAD_DIM))
        kb = k * bc
        edc = jnp.exp(dc)
        rhs_scr[slot, n] = jnp.concatenate([v * bc, kb * edc], axis=1).astype(BF16)
        qd_scr[slot, n] = (q * edc).astype(BF16)
        kt_scr[slot, n] = (k * jnp.exp(dl - dc)).astype(BF16)
        edl_scr[slot, n] = jnp.broadcast_to(jnp.exp(dl), (8, DN_HEAD_DIM))
        lmasks.append(jnp.exp(jnp.where(ci <= ri, dc - dr, -jnp.inf)))
        kbqs.append(jnp.concatenate([kb, q], axis=0).astype(BF16))
        kbfs.append(k.astype(BF16))
    yield
    ps =[_dot_nt(kbq, kbf) for kbq, kbf in zip(kbqs, kbfs)]
    for n, (p, lmask) in enumerate(zip(ps, lmasks)):
        a_scr[slot, n] = jnp.where(ci < ri, p[0:DN_C] * lmask, 0.0)
        att_scr[slot, n] = (p[DN_C:2 * DN_C] * lmask).astype(BF16)


def _gdn_state(slot, gate_ref, og_ref, o_ref, out_row0, s_ref, a_scr, att_scr, rhs_scr, qd_scr, kt_scr,
               edl_scr):
    ri = lax.broadcasted_iota(jnp.int32, (DN_C, DN_C), 0)
    ci = lax.broadcasted_iota(jnp.int32, (DN_C, DN_C), 1)
    og = og_ref[...]
    tinvs = []
    yield from _inv_unit_lower([a_scr[slot, n] for n in range(DN_ITEMS)], ri, ci, tinvs)
    ws = [_dot(t.astype(BF16), rhs_scr[slot, n]) for n, t in enumerate(tinvs)]
    yield
    for c in range(DN_TB // DN_C):
        idx = [c * DN_HEADS + h for h in range(DN_HEADS)]
        s_old = [s_ref[h] for h in range(DN_HEADS)]
        rs = [_dot(jnp.concatenate([ws[n][:, DN_HEAD_DIM:].astype(BF16), qd_scr[slot, n]], axis=0),
                   s.astype(BF16)) for n, s in zip(idx, s_old)]
        yield
        vnb = [(ws[n][:, 0:DN_HEAD_DIM] - r[0:DN_C]).astype(BF16) for n, r in zip(idx, rs)]
        outs = [r[DN_C:2 * DN_C] + _dot(att_scr[slot, n], vn) for n, r, vn in zip(idx, rs, vnb)]
        yield
        for h, (n, vn) in enumerate(zip(idx, vnb)):
            s_ref[h] = s_old[h] * edl_scr[slot, n, 0:1, :] + _dot_tn(kt_scr[slot, n], vn)
        yield
        rows = slice(out_row0 + c * DN_C, out_row0 + (c + 1) * DN_C)
        for h, o in enumerate(outs):
            hs = slice(h * DN_HEAD_DIM, (h + 1) * DN_HEAD_DIM)
            ms = jnp.mean(o * o, axis=-1, keepdims=True)
            gt = gate_ref[rows, hs].astype(F32)
            o_ref[rows, hs] = (o * lax.rsqrt(ms + EPS) * og * (gt * _sigmoid(gt))).astype(BF16)


def _ffn_stages(x_ref, oa_ref, od_ref, wo_ref, fg_ref, w1_ref, w2_ref, o_ref):
    d = x_ref.shape[1]
    oa = oa_ref[...]
    od = od_ref[...]
    mixed = []
    for j in range(d // FFN_TN):
        cols = slice(j * FFN_TN, (j + 1) * FFN_TN)
        mixed.append(_dot(oa, wo_ref[0:ATT_WIDTH, cols]) + _dot(od, wo_ref[ATT_WIDTH:, cols]))
        yield
    h1 = x_ref[...] + jnp.concatenate(mixed, axis=1)
    ms = jnp.mean(h1 * h1, axis=-1, keepdims=True)
    n = (h1 * lax.rsqrt(ms + EPS) * fg_ref[...]).astype(BF16)

    def up(c, out):
        for j in range(FFN_TF // FFN_TN):
            col0 = c * FFN_TF + j * FFN_TN
            out.append(_dot(n, w1_ref[:, col0:col0 + FFN_TN]))
            yield

    n_chunks = w1_ref.shape[1] // FFN_TF
    acc = h1
    z_next = []
    yield from up(0, z_next)
    for c in range(n_chunks):
        z = z_next
        if c + 1 < n_chunks:
            z_next = []
            yield from up(c + 1, z_next)
        act = jnp.concatenate([jnp.square(jnp.maximum(piece, 0.0)).astype(BF16) for piece in z], axis=1)
        down = []
        for j in range(d // FFN_TN):
            down.append(_dot(act, w2_ref[c * FFN_TF:(c + 1) * FFN_TF, j * FFN_TN:(j + 1) * FFN_TN]))
            yield
        acc = acc + jnp.concatenate(down, axis=1)
    o_ref[...] = acc


def _gdn_ffn_kernel(x0_ref, x1_ref, x2_ref, ab0_ref, ab1_ref, ab2_ref, abt0_ref, abt1_ref, abt2_ref,
                    gate_ref, ltri_ref, utri_ref, alog_ref, dtb_ref, alogt_ref, dtbt_ref, dn_og_ref,
                    xres_ref, oa_ref, wo_ref, fg_ref, w1_ref, w2_ref, o_ref, s_ref, od_scr, *local_scr,
                    steps_per_seq):
    s = pl.program_id(0)
    cur = s % 2
    consts = (ltri_ref, utri_ref, alog_ref, dtb_ref, alogt_ref, dtbt_ref)

    def local(x_ref, ab_ref, abt_ref, slot):
        return _gdn_local(x_ref, ab_ref, abt_ref, *consts, slot, *local_scr)

    def state(slot, out_row0):
        return _gdn_state(slot, gate_ref, dn_og_ref, od_scr.at[cur], out_row0, s_ref, *local_scr)

    @pl.when(s == 0)
    def _():
        od_scr[1] = jnp.zeros(od_scr.shape[1:], od_scr.dtype)

    @pl.when(s % steps_per_seq == 0)
    def _():
        s_ref[...] = jnp.zeros_like(s_ref)
        _run(local(x0_ref, ab0_ref, abt0_ref, 0))

    deltanet = _chain(
        _weave(state(0, 0), local(x1_ref, ab1_ref, abt1_ref, 1), DN_STATE_PER_LOCAL),
        _weave(state(1, DN_TB), local(x2_ref, ab2_ref, abt2_ref, 0), DN_STATE_PER_LOCAL))
    ffn = _ffn_stages(xres_ref, oa_ref, od_scr.at[1 - cur], wo_ref, fg_ref, w1_ref, w2_ref, o_ref)
    _run(_weave(deltanet, ffn, 1))


def _deltanet_ffn(xf, main, ab, abt, o_att, alog, dtb, alogt, dtbt, dn_og, w_out, fg, w1, w2, seq):
    t, d = xf.shape
    dff = w1.shape[1]
    assert FFN_TM == 2 * DN_TB
    nb = seq // DN_TB
    spq = nb // 2
    steps = t // FFN_TM
    tile = lambda s: jnp.minimum(s, steps - 1)
    prev = lambda s: jnp.maximum(s - 1, 0)
    seq_blk0 = lambda s: tile(s) // spq * nb
    first = seq_blk0
    odd = lambda s: 2 * tile(s) + 1
    nxt = lambda s: jnp.minimum(2 * tile(s) + 2, seq_blk0(s) + nb - 1)
    const = lambda s: (0, 0)

    def x_spec(blk):
        return pl.BlockSpec((DN_TB, 3 * DN_WIDTH), lambda s: (blk(s), 1))

    def ab_spec(blk):
        return pl.BlockSpec((DN_TB, AB_PAD), lambda s: (blk(s), 0))

    def abt_spec(blk):
        return pl.BlockSpec((AB_PAD, DN_TB), lambda s: (0, blk(s)))

    r = np.arange(DN_TB)
    same = (r[:, None] // DN_C) == (r[None, :] // DN_C)
    ltri = jnp.asarray(same & (r[None, :] <= r[:, None]), BF16)
    utri = jnp.asarray(same & (r[:, None] <= r[None, :]), BF16)
    item = lambda *shape: (2, DN_ITEMS) + shape
    resident = dict(pipeline_mode=pl.Buffered(1))
    return pl.pallas_call(
        functools.partial(_gdn_ffn_kernel, steps_per_seq=spq),
        grid=(steps + 1,),
        in_specs=[
            x_spec(first), x_spec(odd), x_spec(nxt),
            ab_spec(first), ab_spec(odd), ab_spec(nxt),
            abt_spec(first), abt_spec(odd), abt_spec(nxt),
            pl.BlockSpec((2 * DN_TB, DN_WIDTH), lambda s: (tile(s), 6)),
            pl.BlockSpec((DN_TB, DN_TB), const),
            pl.BlockSpec((DN_TB, DN_TB), const),
            pl.BlockSpec((1, AB_PAD), const),
            pl.BlockSpec((1, AB_PAD), const),
            pl.BlockSpec((8, 1), const),
            pl.BlockSpec((8, 1), const),
            pl.BlockSpec((1, DN_HEAD_DIM), const),
            pl.BlockSpec((FFN_TM, d), lambda s: (prev(s), 0)),
            pl.BlockSpec((FFN_TM, ATT_WIDTH), lambda s: (prev(s), 0)),
            pl.BlockSpec((d, d), const, **resident),
            pl.BlockSpec((1, d), const),
            pl.BlockSpec((d, dff), const, **resident),
            pl.BlockSpec((dff, d), const, **resident),
        ],
        out_specs=pl.BlockSpec((FFN_TM, d), lambda s: (prev(s), 0)),
        out_shape=jax.ShapeDtypeStruct((t, d), F32),
        scratch_shapes=[
            pltpu.VMEM((DN_HEADS, DN_HEAD_DIM, DN_HEAD_DIM), F32),
            pltpu.VMEM((2, FFN_TM, DN_WIDTH), BF16),
            pltpu.VMEM(item(DN_C, DN_C), F32),
            pltpu.VMEM(item(DN_C, DN_C), BF16),
            pltpu.VMEM(item(DN_C, 2 * DN_HEAD_DIM), BF16),
            pltpu.VMEM(item(DN_C, DN_HEAD_DIM), BF16),
            pltpu.VMEM(item(DN_C, DN_HEAD_DIM), BF16),
            pltpu.VMEM(item(8, DN_HEAD_DIM), F32),
        ],
        compiler_params=pltpu.CompilerParams(
            dimension_semantics=("arbitrary",), vmem_limit_bytes=VMEM_LIMIT),
        name="deltanet_ffn",
    )(main, main, main, ab, ab, ab, abt, abt, abt, main, ltri, utri, alog, dtb, alogt, dtbt, dn_og,
      xf, o_att, w_out, fg, w1, w2)


def _layer(h, p, batch, seq):
    w_in = p["w_in"]
    w_main = w_in.astype(BF16)
    w_ab = jnp.pad(w_in[:, MAIN_COLS:], ((0, 0), (0, AB_PAD - 2 * DN_HEADS))).astype(BF16)
    head_of = np.arange(ATT_WIDTH) // ATT_HEAD_DIM
    seg = jnp.asarray((head_of[:, None] == head_of[None, :]) / ATT_HEAD_DIM, BF16)
    qg = (jnp.tile(p["att_q_gain"].astype(F32), ATT_HEADS) * (ATT_HEAD_DIM ** -0.5 * LOG2_E))[None]
    kg = jnp.tile(p["att_k_gain"].astype(F32), ATT_HEADS)[None]
    (main, kt, ab, abt), (w_out, w_ff1, w_ff2) = _inproj(
        h, p["mix_norm_gain"].astype(F32)[None], w_main, w_ab, qg, kg, seg, p["dn_conv_w"].astype(F32), seq,
        [p["w_out"].astype(F32), p["w_ff1"].astype(F32), p["w_ff2"].astype(F32)])

    o_att = _attention(main, kt, _bias_signal(p["rel_bias"]),
                       p["att_out_gain"].astype(F32)[None], batch, seq)

    pad_row = lambda v: jnp.pad(v.astype(F32), (0, AB_PAD - DN_HEADS))[None]
    pad_col = lambda v: jnp.pad(v.astype(F32), (0, 8 - DN_HEADS))[:, None]
    return _deltanet_ffn(h, main, ab, abt, o_att, pad_row(p["dn_a_log"]), pad_row(p["dn_dt_bias"]),
                         pad_col(p["dn_a_log"]), pad_col(p["dn_dt_bias"]),
                         p["dn_out_gain"].astype(F32)[None], w_out, p["ffn_norm_gain"].astype(F32)[None],
                         w_ff1, w_ff2, seq)


def kernel(x, mix_norm_gain, w_in, att_q_gain, att_k_gain, rel_bias, att_out_gain, dn_conv_w,
           dn_a_log, dn_dt_bias, dn_out_gain, w_out, ffn_norm_gain, w_ff1, w_ff2):
    batch, seq, d = x.shape
    assert seq % max(ATT_TQ, 2 * DN_TB) == 0 and (batch * seq) % max(IN_TM, FFN_TM) == 0
    params = dict(mix_norm_gain=mix_norm_gain, w_in=w_in, att_q_gain=att_q_gain,
                  att_k_gain=att_k_gain, rel_bias=rel_bias, att_out_gain=att_out_gain,
                  dn_conv_w=dn_conv_w, dn_a_log=dn_a_log, dn_dt_bias=dn_dt_bias,
                  dn_out_gain=dn_out_gain, w_out=w_out, ffn_norm_gain=ffn_norm_gain,
                  w_ff1=w_ff1, w_ff2=w_ff2)
    h = x.reshape(batch * seq, d)
    for layer in range(w_in.shape[0]):
        h = _layer(h, {name: v[layer] for name, v in params.items()}, batch, seq)
    return h.reshape(batch, seq, d)
```

```python
import functools

import numpy as np
import jax
import jax.numpy as jnp
from jax import lax
from jax.experimental import pallas as pl
from jax.experimental.pallas import tpu as pltpu

F32 = jnp.float32
BF16 = jnp.bfloat16

CHUNK = 64
N_PAST_CHUNKS = 8
ATT_HEADS = 8
ATT_HEAD_DIM = 64
ATT_WIDTH = ATT_HEADS * ATT_HEAD_DIM
MAX_REL = 256
DN_HEADS = 4
DN_HEAD_DIM = 128
DN_WIDTH = DN_HEADS * DN_HEAD_DIM
CONV_K = 4
EPS = 1e-6

MAIN_COLS = 3 * ATT_WIDTH + 4 * DN_WIDTH
AB_PAD = 128

LANES = 128
VMEM_LIMIT = 56 * 1024 * 1024

IN_TM = 512
IN_TN = 256
ATT_TQ = 4 * CHUNK
ATT_WIN = ATT_TQ + N_PAST_CHUNKS * CHUNK
ATT_TILES = 4
DN_C = 128
DN_TB = 2 * DN_C
DN_TAIL = 8
DN_ITEMS = (DN_TB // DN_C) * DN_HEADS
DN_CONV_FIRST = 3 * ATT_WIDTH // IN_TN
DN_CONV_CHUNKS = 3 * DN_WIDTH // IN_TN
DN_STATE_PER_LOCAL = 4
FFN_TM = 512
FFN_TF = 1024
FFN_TN = 256
NEG_BIG = -1e30
LOG2_E = 1.4426950408889634


def _dot(a, b):
    return jnp.dot(a, b, preferred_element_type=F32)


def _dot_nt(a, b):
    return lax.dot_general(a, b, (((1,), (1,)), ((), ())), preferred_element_type=F32)


def _dot_tn(a, b):
    return lax.dot_general(a, b, (((0,), (0,)), ((), ())), preferred_element_type=F32)


def _split3(a):
    p1 = a.astype(BF16)
    r1 = a - p1.astype(F32)
    p2 = r1.astype(BF16)
    p3 = (r1 - p2.astype(F32)).astype(BF16)
    return p1, p2, p3


def _sigmoid(x):
    return 1.0 / (1.0 + jnp.exp2(x * -LOG2_E))


def _softplus(x):
    return jnp.maximum(x, 0.0) + jnp.log(1.0 + jnp.exp(-jnp.abs(x)))


def _inproj_kernel(x_ref, g_ref, wm_ref, wab_ref, qg_ref, kg_ref, seg_ref, cw_ref, *rest, tiles_per_seq):
    n_cast = (len(rest) - 5) // 2
    cast_in, (main_ref, kt_ref, ab_ref, abt_ref) = rest[:n_cast], rest[n_cast:n_cast + 4]
    cast_out, stage_ref = rest[n_cast + 4:2 * n_cast + 4], rest[2 * n_cast + 4]
    i = pl.program_id(0)
    for src, dst in zip(cast_in, cast_out):
        dst[...] = src[...].astype(BF16)

    @pl.when(i % tiles_per_seq == 0)
    def _():
        stage_ref[...] = jnp.zeros_like(stage_ref)

    x = x_ref[...]
    ms = jnp.mean(x * x, axis=-1, keepdims=True)
    u = (x * lax.rsqrt(ms + EPS) * g_ref[...]).astype(BF16)
    conv = list(range(DN_CONV_FIRST, DN_CONV_FIRST + DN_CONV_CHUNKS))
    plain = [c for c in range(MAIN_COLS // IN_TN) if c not in conv]
    order = [c for pair in zip(plain, conv) for c in pair] + plain[len(conv):]
    project = lambda c: _dot(u, wm_ref[:, c * IN_TN:(c + 1) * IN_TN])
    ab = _dot(u, wab_ref[...])
    y_next = project(order[0])
    for n, c in enumerate(order):
        cols = slice(c * IN_TN, (c + 1) * IN_TN)
        y = y_next
        if n + 1 < len(order):
            y_next = project(order[n + 1])
        if n == 0:
            ab_ref[...] = ab
            abt_ref[...] = ab.T
        if c < 2 * ATT_WIDTH // IN_TN:
            gain = qg_ref if c < ATT_WIDTH // IN_TN else kg_ref
            g0 = c * IN_TN % ATT_WIDTH
            m = _dot((y * y).astype(BF16), seg_ref[0:IN_TN, 0:IN_TN])
            y = y * lax.rsqrt(m + EPS) * gain[:, g0:g0 + IN_TN]
            if c >= ATT_WIDTH // IN_TN:
                kt_ref[g0:g0 + IN_TN, :] = y.T.astype(BF16)
        elif DN_CONV_FIRST <= c < DN_CONV_FIRST + DN_CONV_CHUNKS:
            assert CONV_K == 4
            j = c - DN_CONV_FIRST
            wcols = slice(j * IN_TN, (j + 1) * IN_TN)
            ext = jnp.concatenate([stage_ref[j], y], axis=0)
            tap = lambda t: 0.5 * cw_ref[t:t + 1, wcols]
            prev = pltpu.roll(ext, 1, axis=0)
            older = tap(1) * ext + tap(0) * prev
            h = tap(3) * ext + tap(2) * prev + pltpu.roll(older, 2, axis=0)
            h = h[DN_TAIL:]
            stage_ref[j] = y[IN_TM - DN_TAIL:IN_TM, :]
            y = h + h * jnp.tanh(h)
        main_ref[:, cols] = y.astype(BF16)


def _inproj(xf, gain, w_main, w_ab, qg, kg, seg, conv_w, seq, to_cast):
    t, d = xf.shape
    steps = t // IN_TM
    const = lambda i: (0, 0)
    slab = lambda w: pl.BlockSpec((w.shape[0] // steps, w.shape[1]), lambda i: (i, 0))
    outs = pl.pallas_call(
        functools.partial(_inproj_kernel, tiles_per_seq=seq // IN_TM),
        grid=(steps,),
        in_specs=[
            pl.BlockSpec((IN_TM, d), lambda i: (i, 0)),
            pl.BlockSpec((1, d), const),
            pl.BlockSpec((d, MAIN_COLS), const),
            pl.BlockSpec((d, AB_PAD), const),
            pl.BlockSpec((1, ATT_WIDTH), const),
            pl.BlockSpec((1, ATT_WIDTH), const),
            pl.BlockSpec((ATT_WIDTH, ATT_WIDTH), const),
            pl.BlockSpec((CONV_K, 3 * DN_WIDTH), const),
        ] + [slab(w) for w in to_cast],
        out_specs=[
            pl.BlockSpec((IN_TM, MAIN_COLS), lambda i: (i, 0)),
            pl.BlockSpec((ATT_WIDTH, IN_TM), lambda i: (0, i)),
            pl.BlockSpec((IN_TM, AB_PAD), lambda i: (i, 0)),
            pl.BlockSpec((AB_PAD, IN_TM), lambda i: (0, i)),
        ] + [slab(w) for w in to_cast],
        out_shape=[
            jax.ShapeDtypeStruct((t, MAIN_COLS), BF16),
            jax.ShapeDtypeStruct((ATT_WIDTH, t), BF16),
            jax.ShapeDtypeStruct((t, AB_PAD), F32),
            jax.ShapeDtypeStruct((AB_PAD, t), F32),
        ] + [jax.ShapeDtypeStruct(w.shape, BF16) for w in to_cast],
        scratch_shapes=[pltpu.VMEM((DN_CONV_CHUNKS, DN_TAIL, IN_TN), F32)],
        compiler_params=pltpu.CompilerParams(
            dimension_semantics=("arbitrary",), vmem_limit_bytes=VMEM_LIMIT),
        name="inproj",
    )(xf, gain, w_main, w_ab, qg, kg, seg, conv_w, *to_cast)
    return outs[:4], outs[4:]


def _attn_tile(q_ref, row0, k_refs, v_refs, bias_ref, start_mask, og_ref, o_ref):
    lane = lax.broadcasted_iota(jnp.int32, (1, LANES), 1)
    per_tile = LANES // ATT_HEAD_DIM
    rows = slice(row0, row0 + ATT_TQ)

    def mine(h):
        e = h % per_tile
        return (lane >= e * ATT_HEAD_DIM) & (lane < (e + 1) * ATT_HEAD_DIM)

    def tile_lanes(h):
        return slice(h // per_tile * LANES, (h // per_tile + 1) * LANES)

    def scores(h):
        qp = q_ref[rows, tile_lanes(h)]
        q_e = jnp.where(mine(h), qp, jnp.zeros_like(qp))
        bias = bias_ref[h] if start_mask is None else bias_ref[h] + start_mask
        kw = jnp.concatenate([r[tile_lanes(h), :] for r in k_refs], axis=1)
        return _dot(q_e, kw) + bias

    def attend(h, s):
        mx = jnp.max(s, axis=-1, keepdims=True)
        pe = jnp.exp2(s - mx).astype(BF16)
        vw = jnp.concatenate([r[:, tile_lanes(h)] for r in v_refs], axis=0)
        o_e = _dot(pe, jnp.where(mine(h), vw, jnp.ones_like(vw)))
        return o_e / pltpu.roll(o_e, ATT_HEAD_DIM, axis=1)

    normed = []
    s_next = scores(0)
    for h in range(ATT_HEADS):
        s_cur = s_next
        if h + 1 < ATT_HEADS:
            s_next = scores(h + 1)
        normed.append(attend(h, s_cur))
        yield
    outs = [jnp.where(lane < ATT_HEAD_DIM, normed[t * per_tile], normed[t * per_tile + 1])
            for t in range(ATT_WIDTH // LANES)]
    o = jnp.concatenate(outs, axis=1)
    ms = jnp.mean(o * o, axis=-1, keepdims=True)
    o_ref[rows, :] = (o * lax.rsqrt(ms + EPS) * og_ref[...]).astype(BF16)


def _attn_kernel(q_ref, *refs):
    n_blk = ATT_TILES + 2
    k_refs, v_refs = refs[:n_blk], refs[n_blk:2 * n_blk]
    sig_ref, og_ref, o_ref, bias_ref = refs[2 * n_blk:]
    i = pl.program_id(1)

    @pl.when((pl.program_id(0) == 0) & (i == 0))
    def _():
        r = lax.broadcasted_iota(jnp.int32, (ATT_TQ, ATT_WIN), 0)
        s = lax.broadcasted_iota(jnp.int32, (ATT_TQ, ATT_WIN), 1)
        back = (s >> 6) - (r >> 6)
        in_band = (back >= 0) & (back <= N_PAST_CHUNKS)
        for h in range(ATT_HEADS):
            rows = jnp.broadcast_to(sig_ref[h:h + 1, :], (ATT_TQ, sig_ref.shape[1]))
            toep = pltpu.roll(rows, 0, 1, stride=1, stride_axis=0)
            bias_ref[h] = jnp.where(in_band, toep[:, 0:ATT_WIN], NEG_BIG)

    def tiles(masks):
        gens = [_attn_tile(q_ref, n * ATT_TQ, k_refs[n:n + 3], v_refs[n:n + 3], bias_ref, masks[n], og_ref, o_ref)
                for n in range(ATT_TILES)]
        before = []
        for p in range(0, ATT_TILES, 2):
            pair = gens[p:p + 2]
            for head in range(ATT_HEADS):
                for g in pair:
                    next(g)
                if head == 1:
                    for g in before:
                        _run(g)
            before = pair
        for g in before:
            _run(g)

    steps_in_past = -(-N_PAST_CHUNKS * CHUNK // (ATT_TILES * ATT_TQ))

    @pl.when(i < steps_in_past)
    def _():
        jchunk = lax.broadcasted_iota(jnp.int32, (1, ATT_WIN), 1) >> 6
        tiles([jnp.where(jchunk + (ATT_TQ // CHUNK) * (ATT_TILES * i + n) - N_PAST_CHUNKS >= 0, 0.0, NEG_BIG)
               for n in range(ATT_TILES)])

    @pl.when(i >= steps_in_past)
    def _():
        tiles([None] * ATT_TILES)


def _attention(main, kt, signal, og, batch, seq):
    nb = seq // ATT_TQ
    steps = nb // ATT_TILES
    blk = (ATT_TQ, ATT_WIDTH)

    window_blk = lambda b, i, j: b * nb + jnp.maximum(ATT_TILES * i - 2 + j, 0)

    def k_spec(j):
        return pl.BlockSpec((ATT_WIDTH, ATT_TQ), lambda b, i: (0, window_blk(b, i, j)))

    def v_spec(j):
        return pl.BlockSpec(blk, lambda b, i: (window_blk(b, i, j), 2))

    step_rows = pl.BlockSpec((ATT_TILES * ATT_TQ, ATT_WIDTH), lambda b, i: (b * steps + i, 0))
    return pl.pallas_call(
        _attn_kernel,
        grid=(batch, steps),
        in_specs=[
            step_rows,
            *[k_spec(j) for j in range(ATT_TILES + 2)],
            *[v_spec(j) for j in range(ATT_TILES + 2)],
            pl.BlockSpec(signal.shape, lambda b, i: (0, 0)),
            pl.BlockSpec((1, ATT_WIDTH), lambda b, i: (0, 0)),
        ],
        out_specs=step_rows,
        out_shape=jax.ShapeDtypeStruct((batch * seq, ATT_WIDTH), BF16),
        scratch_shapes=[pltpu.VMEM((ATT_HEADS, ATT_TQ, ATT_WIN), F32)],
        compiler_params=pltpu.CompilerParams(
            dimension_semantics=("arbitrary", "arbitrary"), vmem_limit_bytes=VMEM_LIMIT),
        name="band_attention",
    )(main, *[kt] * (ATT_TILES + 2), *[main] * (ATT_TILES + 2), signal, og)


def _bias_signal(rel_bias):
    nh = rel_bias.shape[0]
    lo = N_PAST_CHUNKS * CHUNK - (ATT_WIN - 1) + MAX_REL
    n_dist = ATT_TQ + ATT_WIN - 1
    n_tail = n_dist - (2 * MAX_REL + 1 - lo)
    rb = rel_bias.astype(F32) * LOG2_E
    by_dist = jnp.concatenate([rb[:, lo:], jnp.broadcast_to(rb[:, -1:], (nh, n_tail))], axis=1)
    period = ATT_TQ + ATT_WIN
    return jnp.roll(jnp.pad(by_dist[:, ::-1], ((0, 0), (0, period - n_dist))), -(ATT_TQ - 1), axis=1)


def _weave(main, side, every):
    n = 0
    done = object()
    while True:
        if n % every == 0:
            next(side, done)
        if next(main, done) is done:
            break
        n += 1
        yield
    _run(side)


def _chain(*stages):
    for g in stages:
        yield from g


def _run(stage):
    for _ in stage:
        pass


def _inv_unit_lower(mats, ri, ci, out):
    eye = jnp.where(ri == ci, 1.0, 0.0)
    diag8 = (ri >> 3) == (ci >> 3)
    n1 = [jnp.where(diag8, -a, 0.0) for a in mats]
    n1b = [n.astype(BF16) for n in n1]
    n2b = [_dot(n, n).astype(BF16) for n in n1b]
    yield
    n4b = [_dot(n, n).astype(BF16) for n in n2b]
    ts = [eye + n for n in n1]
    yield
    ts = [t + _dot(t.astype(BF16), n) for t, n in zip(ts, n2b)]
    yield
    ts = [t + _dot(t.astype(BF16), n) for t, n in zip(ts, n4b)]
    yield
    half = DN_C // 2
    r_h = lax.broadcasted_iota(jnp.int32, (half, DN_C), 0)
    c_o = lax.broadcasted_iota(jnp.int32, (half, DN_C), 1)
    for sh in range(3, 7):
        blk = 1 << sh
        odd = [slice(s0, s0 + blk) for s0 in range(blk, DN_C, 2 * blk)]
        even = [slice(s0, s0 + blk) for s0 in range(0, DN_C, 2 * blk)]
        take = lambda m, parts: jnp.concatenate([m[p] for p in parts], axis=0)
        r_o = r_h + (((r_h >> sh) + 1) << sh)
        pair =((r_o >> (sh + 1)) == (c_o >> (sh + 1))) & ((r_o >> sh) != (c_o >> sh))
        tb = [t.astype(BF16) for t in ts]
        lt = [_dot(jnp.where(pair, take(a, odd), 0.0).astype(BF16), t) for a, t in zip(mats, tb)]
        yield
        zero = jnp.zeros((blk, DN_C), F32)
        lt = [jnp.concatenate([p for m in range(half // blk) for p in (zero, x[m * blk:(m + 1) * blk])],
                              axis=0).astype(BF16) for x in lt]
        t_odd = [take(t, odd) for t in ts]
        new = [t - _dot(t.astype(BF16), x) for t, x in zip(t_odd, lt)]
        ts = [jnp.concatenate([p for m in range(half // blk)
                               for p in (t[even[m]], n[m * blk:(m + 1) * blk])], axis=0)
              for t, n in zip(ts, new)]
        yield
    out.extend(ts)


def _gdn_local(x_ref, ab_ref, abt_ref, ltri_ref, utri_ref, alog_ref, dtb_ref, alogt_ref, dtbt_ref, slot,
               a_scr, att_scr, rhs_scr, qd_scr, kt_scr, edl_scr):
    ab = ab_ref[...]
    g_col = -jnp.exp(alog_ref[...]) * _softplus(ab + dtb_ref[...])
    beta_col = _sigmoid(ab)
    g_row = -jnp.exp(alogt_ref[...]) * _softplus(abt_ref[0:8, :] + dtbt_ref[...])
    gc1, gc2, gc3 = _split3(g_col)
    gr1, gr2, gr3 = _split3(g_row)

    y = x_ref[...].astype(F32)

    ones = jnp.ones((DN_HEAD_DIM, DN_HEAD_DIM), BF16)
    sq = [(t * t).astype(BF16) for t in (y[:, g * DN_HEAD_DIM:(g + 1) * DN_HEAD_DIM]
                                         for g in range(2 * DN_HEADS))]
    ssq = [_dot(t, ones) for t in sq]
    yield
    ltri = ltri_ref[...]
    utri = utri_ref[...]
    d_col = _dot(ltri, gc1) + (_dot(ltri, gc2) + _dot(ltri, gc3))
    d_row = _dot(gr1, utri) + (_dot(gr2, utri) + _dot(gr3, utri))
    qk = []
    for g, tot in enumerate(ssq):
        inv = lax.rsqrt(tot + EPS)
        qk.append(y[:, g * DN_HEAD_DIM:(g + 1) * DN_HEAD_DIM]
                  * (inv * DN_HEAD_DIM ** -0.5 if g < DN_HEADS else inv))

    ri = lax.broadcasted_iota(jnp.int32, (DN_C, DN_C), 0)
    ci = lax.broadcasted_iota(jnp.int32, (DN_C, DN_C), 1)
    lmasks, kbqs, kbfs = [], [], []
    for n in range(DN_ITEMS):
        c, h = divmod(n, DN_HEADS)
        rows = slice(c * DN_C, (c + 1) * DN_C)
        q = qk[h][rows]
        k = qk[DN_HEADS + h][rows]
        v = y[rows, 2 * DN_WIDTH + h * DN_HEAD_DIM:2 * DN_WIDTH + (h + 1) * DN_HEAD_DIM]
        dc = d_col[rows, h:h + 1]
        dr = d_row[h:h + 1, rows]
        dl = dr[:, DN_C - 1:DN_C]
        bc = beta_col[rows, DN_HEADS + h:DN_HEADS + h + 1]
        assert DN_HEAD_DIM == DN_C
        bc = jnp.broadcast_to(bc, (DN_C, DN_HEAD_DIM))
        dc = jnp.broadcast_to(dc, (DN_C, DN_HEAD_DIM))
        kb = k * bc
        edc = jnp.exp(dc)
        rhs_scr[slot, n] = jnp.concatenate([v * bc, kb * edc], axis=1).astype(BF16)
        qd_scr[slot, n] = (q * edc).astype(BF16)
        kt_scr[slot, n] = (k * jnp.exp(dl - dc)).astype(BF16)
        edl_scr[slot, n] = jnp.broadcast_to(jnp.exp(dl), (8, DN_HEAD_DIM))
        lmasks.append(jnp.exp(jnp.where(ci <= ri, dc - dr, -jnp.inf)))
        kbqs.append(jnp.concatenate([kb, q], axis=0).astype(BF16))
        kbfs.append(k.astype(BF16))
    yield
    ps =[_dot_nt(kbq, kbf) for kbq, kbf in zip(kbqs, kbfs)]
    for n, (p, lmask) in enumerate(zip(ps, lmasks)):
        a_scr[slot, n] = jnp.where(ci < ri, p[0:DN_C] * lmask, 0.0)
        att_scr[slot, n] = (p[DN_C:2 * DN_C] * lmask).astype(BF16)


def _gdn_state(slot, gate_ref, og_ref, o_ref, out_row0, s_ref, a_scr, att_scr, rhs_scr, qd_scr, kt_scr,
               edl_scr):
    ri = lax.broadcasted_iota(jnp.int32, (DN_C, DN_C), 0)
    ci = lax.broadcasted_iota(jnp.int32, (DN_C, DN_C), 1)
    og = og_ref[...]
    tinvs = []
    yield from _inv_unit_lower([a_scr[slot, n] for n in range(DN_ITEMS)], ri, ci, tinvs)
    ws = [_dot(t.astype(BF16), rhs_scr[slot, n]) for n, t in enumerate(tinvs)]
    yield
    for c in range(DN_TB // DN_C):
        idx = [c * DN_HEADS + h for h in range(DN_HEADS)]
        s_old = [s_ref[h] for h in range(DN_HEADS)]
        rs = [_dot(jnp.concatenate([ws[n][:, DN_HEAD_DIM:].astype(BF16), qd_scr[slot, n]], axis=0),
                   s.astype(BF16)) for n, s in zip(idx, s_old)]
        yield
        vnb = [(ws[n][:, 0:DN_HEAD_DIM] - r[0:DN_C]).astype(BF16) for n, r in zip(idx, rs)]
        outs = [r[DN_C:2 * DN_C] + _dot(att_scr[slot, n], vn) for n, r, vn in zip(idx, rs, vnb)]
        yield
        for h, (n, vn) in enumerate(zip(idx, vnb)):
            s_ref[h] = s_old[h] * edl_scr[slot, n, 0:1, :] + _dot_tn(kt_scr[slot, n], vn)
        yield
        rows = slice(out_row0 + c * DN_C, out_row0 + (c + 1) * DN_C)
        for h, o in enumerate(outs):
            hs = slice(h * DN_HEAD_DIM, (h + 1) * DN_HEAD_DIM)
            ms = jnp.mean(o * o, axis=-1, keepdims=True)
            gt = gate_ref[rows, hs].astype(F32)
            o_ref[rows, hs] = (o * lax.rsqrt(ms + EPS) * og * (gt * _sigmoid(gt))).astype(BF16)


def _ffn_stages(x_ref, oa_ref, od_ref, wo_ref, fg_ref, w1_ref, w2_ref, o_ref):
    d = x_ref.shape[1]
    oa = oa_ref[...]
    od = od_ref[...]
    mixed = []
    for j in range(d // FFN_TN):
        cols = slice(j * FFN_TN, (j + 1) * FFN_TN)
        mixed.append(_dot(oa, wo_ref[0:ATT_WIDTH, cols]) + _dot(od, wo_ref[ATT_WIDTH:, cols]))
        yield
    h1 = x_ref[...] + jnp.concatenate(mixed, axis=1)
    ms = jnp.mean(h1 * h1, axis=-1, keepdims=True)
    n = (h1 * lax.rsqrt(ms + EPS) * fg_ref[...]).astype(BF16)

    def up(c, out):
        for j in range(FFN_TF // FFN_TN):
            col0 = c * FFN_TF + j * FFN_TN
            out.append(_dot(n, w1_ref[:, col0:col0 + FFN_TN]))
            yield

    n_chunks = w1_ref.shape[1] // FFN_TF
    acc = h1
    z_next = []
    yield from up(0, z_next)
    for c in range(n_chunks):
        z = z_next
        if c + 1 < n_chunks:
            z_next = []
            yield from up(c + 1, z_next)
        act = jnp.concatenate([jnp.square(jnp.maximum(piece, 0.0)).astype(BF16) for piece in z], axis=1)
        down = []
        for j in range(d // FFN_TN):
            down.append(_dot(act, w2_ref[c * FFN_TF:(c + 1) * FFN_TF, j * FFN_TN:(j + 1) * FFN_TN]))
            yield
        acc = acc + jnp.concatenate(down, axis=1)
    o_ref[...] = acc


def _gdn_ffn_kernel(x0_ref, x1_ref, x2_ref, ab0_ref, ab1_ref, ab2_ref, abt0_ref, abt1_ref, abt2_ref,
                    gate_ref, ltri_ref, utri_ref, alog_ref, dtb_ref, alogt_ref, dtbt_ref, dn_og_ref,
                    xres_ref, oa_ref, wo_ref, fg_ref, w1_ref, w2_ref, o_ref, s_ref, od_scr, *local_scr,
                    steps_per_seq):
    s = pl.program_id(0)
    cur = s % 2
    consts = (ltri_ref, utri_ref, alog_ref, dtb_ref, alogt_ref, dtbt_ref)

    def local(x_ref, ab_ref, abt_ref, slot):
        return _gdn_local(x_ref, ab_ref, abt_ref, *consts, slot, *local_scr)

    def state(slot, out_row0):
        return _gdn_state(slot, gate_ref, dn_og_ref, od_scr.at[cur], out_row0, s_ref, *local_scr)

    @pl.when(s == 0)
    def _():
        od_scr[1] = jnp.zeros(od_scr.shape[1:], od_scr.dtype)

    @pl.when(s % steps_per_seq == 0)
    def _():
        s_ref[...] = jnp.zeros_like(s_ref)
        _run(local(x0_ref, ab0_ref, abt0_ref, 0))

    deltanet = _chain(
        _weave(state(0, 0), local(x1_ref, ab1_ref, abt1_ref, 1), DN_STATE_PER_LOCAL),
        _weave(state(1, DN_TB), local(x2_ref, ab2_ref, abt2_ref, 0), DN_STATE_PER_LOCAL))
    ffn = _ffn_stages(xres_ref, oa_ref, od_scr.at[1 - cur], wo_ref, fg_ref, w1_ref, w2_ref, o_ref)
    _run(_weave(deltanet, ffn, 1))


def _deltanet_ffn(xf, main, ab, abt, o_att, alog, dtb, alogt, dtbt, dn_og, w_out, fg, w1, w2, seq):
    t, d = xf.shape
    dff = w1.shape[1]
    assert FFN_TM == 2 * DN_TB
    nb = seq // DN_TB
    spq = nb // 2
    steps = t // FFN_TM
    tile = lambda s: jnp.minimum(s, steps - 1)
    prev = lambda s: jnp.maximum(s - 1, 0)
    seq_blk0 = lambda s: tile(s) // spq * nb
    first = seq_blk0
    odd = lambda s: 2 * tile(s) + 1
    nxt = lambda s: jnp.minimum(2 * tile(s) + 2, seq_blk0(s) + nb - 1)
    const = lambda s: (0, 0)

    def x_spec(blk):
        return pl.BlockSpec((DN_TB, 3 * DN_WIDTH), lambda s: (blk(s), 1))

    def ab_spec(blk):
        return pl.BlockSpec((DN_TB, AB_PAD), lambda s: (blk(s), 0))

    def abt_spec(blk):
        return pl.BlockSpec((AB_PAD, DN_TB), lambda s: (0, blk(s)))

    r = np.arange(DN_TB)
    same = (r[:, None] // DN_C) == (r[None, :] // DN_C)
    ltri = jnp.asarray(same & (r[None, :] <= r[:, None]), BF16)
    utri = jnp.asarray(same & (r[:, None] <= r[None, :]), BF16)
    item = lambda *shape: (2, DN_ITEMS) + shape
    resident = dict(pipeline_mode=pl.Buffered(1))
    return pl.pallas_call(
        functools.partial(_gdn_ffn_kernel, steps_per_seq=spq),
        grid=(steps + 1,),
        in_specs=[
            x_spec(first), x_spec(odd), x_spec(nxt),
            ab_spec(first), ab_spec(odd), ab_spec(nxt),
            abt_spec(first), abt_spec(odd), abt_spec(nxt),
            pl.BlockSpec((2 * DN_TB, DN_WIDTH), lambda s: (tile(s), 6)),
            pl.BlockSpec((DN_TB, DN_TB), const),
            pl.BlockSpec((DN_TB, DN_TB), const),
            pl.BlockSpec((1, AB_PAD), const),
            pl.BlockSpec((1, AB_PAD), const),
            pl.BlockSpec((8, 1), const),
            pl.BlockSpec((8, 1), const),
            pl.BlockSpec((1, DN_HEAD_DIM), const),
            pl.BlockSpec((FFN_TM, d), lambda s: (prev(s), 0)),
            pl.BlockSpec((FFN_TM, ATT_WIDTH), lambda s: (prev(s), 0)),
            pl.BlockSpec((d, d), const, **resident),
            pl.BlockSpec((1, d), const),
            pl.BlockSpec((d, dff), const, **resident),
            pl.BlockSpec((dff, d), const, **resident),
        ],
        out_specs=pl.BlockSpec((FFN_TM, d), lambda s: (prev(s), 0)),
        out_shape=jax.ShapeDtypeStruct((t, d), F32),
        scratch_shapes=[
            pltpu.VMEM((DN_HEADS, DN_HEAD_DIM, DN_HEAD_DIM), F32),
            pltpu.VMEM((2, FFN_TM, DN_WIDTH), BF16),
            pltpu.VMEM(item(DN_C, DN_C), F32),
            pltpu.VMEM(item(DN_C, DN_C), BF16),
            pltpu.VMEM(item(DN_C, 2 * DN_HEAD_DIM), BF16),
            pltpu.VMEM(item(DN_C, DN_HEAD_DIM), BF16),
            pltpu.VMEM(item(DN_C, DN_HEAD_DIM), BF16),
            pltpu.VMEM(item(8, DN_HEAD_DIM), F32),
        ],
        compiler_params=pltpu.CompilerParams(
            dimension_semantics=("arbitrary",), vmem_limit_bytes=VMEM_LIMIT),
        name="deltanet_ffn",
    )(main, main, main, ab, ab, ab, abt, abt, abt, main, ltri, utri, alog, dtb, alogt, dtbt, dn_og,
      xf, o_att, w_out, fg, w1, w2)


def _layer(h, p, batch, seq):
    w_in = p["w_in"]
    w_main = w_in.astype(BF16)
    w_ab = jnp.pad(w_in[:, MAIN_COLS:], ((0, 0), (0, AB_PAD - 2 * DN_HEADS))).astype(BF16)
    head_of = np.arange(ATT_WIDTH) // ATT_HEAD_DIM
    seg = jnp.asarray((head_of[:, None] == head_of[None, :]) / ATT_HEAD_DIM, BF16)
    qg = (jnp.tile(p["att_q_gain"].astype(F32), ATT_HEADS) * (ATT_HEAD_DIM ** -0.5 * LOG2_E))[None]
    kg = jnp.tile(p["att_k_gain"].astype(F32), ATT_HEADS)[None]
    (main, kt, ab, abt), (w_out, w_ff1, w_ff2) = _inproj(
        h, p["mix_norm_gain"].astype(F32)[None], w_main, w_ab, qg, kg, seg, p["dn_conv_w"].astype(F32), seq,
        [p["w_out"].astype(F32), p["w_ff1"].astype(F32), p["w_ff2"].astype(F32)])

    o_att = _attention(main, kt, _bias_signal(p["rel_bias"]),
                       p["att_out_gain"].astype(F32)[None], batch, seq)

    pad_row = lambda v: jnp.pad(v.astype(F32), (0, AB_PAD - DN_HEADS))[None]
    pad_col = lambda v: jnp.pad(v.astype(F32), (0, 8 - DN_HEADS))[:, None]
    return _deltanet_ffn(h, main, ab, abt, o_att, pad_row(p["dn_a_log"]), pad_row(p["dn_dt_bias"]),
                         pad_col(p["dn_a_log"]), pad_col(p["dn_dt_bias"]),
                         p["dn_out_gain"].astype(F32)[None], w_out, p["ffn_norm_gain"].astype(F32)[None],
                         w_ff1, w_ff2, seq)


def kernel(x, mix_norm_gain, w_in, att_q_gain, att_k_gain, rel_bias, att_out_gain, dn_conv_w,
           dn_a_log, dn_dt_bias, dn_out_gain, w_out, ffn_norm_gain, w_ff1, w_ff2):
    batch, seq, d = x.shape
    assert seq % max(ATT_TQ, 2 * DN_TB) == 0 and (batch * seq) % max(IN_TM, FFN_TM) == 0
    params = dict(mix_norm_gain=mix_norm_gain, w_in=w_in, att_q_gain=att_q_gain,
                  att_k_gain=att_k_gain, rel_bias=rel_bias, att_out_gain=att_out_gain,
                  dn_conv_w=dn_conv_w, dn_a_log=dn_a_log, dn_dt_bias=dn_dt_bias,
                  dn_out_gain=dn_out_gain, w_out=w_out, ffn_norm_gain=ffn_norm_gain,
                  w_ff1=w_ff1, w_ff2=w_ff2)
    h = x.reshape(batch * seq, d)
    for layer in range(w_in.shape[0]):
        h = _layer(h, {name: v[layer] for name, v in params.items()}, batch, seq)
    return h.reshape(batch, seq, d)
```

```python
import functools

import numpy as np
import jax
import jax.numpy as jnp
from jax import lax
from jax.experimental import pallas as pl
from jax.experimental.pallas import tpu as pltpu

F32 = jnp.float32
BF16 = jnp.bfloat16

CHUNK = 64
N_PAST_CHUNKS = 8
ATT_HEADS = 8
ATT_HEAD_DIM = 64
ATT_WIDTH = ATT_HEADS * ATT_HEAD_DIM
MAX_REL = 256
DN_HEADS = 4
DN_HEAD_DIM = 128
DN_WIDTH = DN_HEADS * DN_HEAD_DIM
CONV_K = 4
EPS = 1e-6

MAIN_COLS = 3 * ATT_WIDTH + 4 * DN_WIDTH
AB_PAD = 128

LANES = 128
VMEM_LIMIT = 56 * 1024 * 1024

IN_TM = 1024
IN_TN = 256
ATT_TQ = 4 * CHUNK
ATT_WIN = ATT_TQ + N_PAST_CHUNKS * CHUNK
ATT_TILES = 2
DN_C = 128
DN_TB = 2 * DN_C
DN_TAIL = 8
DN_ITEMS = (DN_TB // DN_C) * DN_HEADS
DN_CONV_FIRST = 3 * ATT_WIDTH // IN_TN
DN_CONV_CHUNKS = 3 * DN_WIDTH // IN_TN
DN_STATE_PER_LOCAL = 4
FFN_TM = 512
FFN_TF = 1024
FFN_TN = 256
NEG_BIG = -1e30
LOG2_E = 1.4426950408889634


def _dot(a, b):
    return jnp.dot(a, b, preferred_element_type=F32)


def _dot_nt(a, b):
    return lax.dot_general(a, b, (((1,), (1,)), ((), ())), preferred_element_type=F32)


def _dot_tn(a, b):
    return lax.dot_general(a, b, (((0,), (0,)), ((), ())), preferred_element_type=F32)


def _split3(a):
    p1 = a.astype(BF16)
    r1 = a - p1.astype(F32)
    p2 = r1.astype(BF16)
    p3 = (r1 - p2.astype(F32)).astype(BF16)
    return p1, p2, p3


def _sigmoid(x):
    return 1.0 / (1.0 + jnp.exp2(x * -LOG2_E))


def _softplus(x):
    return jnp.maximum(x, 0.0) + jnp.log(1.0 + jnp.exp(-jnp.abs(x)))


def _inproj_kernel(x_ref, g_ref, wm_ref, wab_ref, qg_ref, kg_ref, seg_ref, cw_ref, *rest, tiles_per_seq):
    n_cast = (len(rest) - 5) // 2
    cast_in, (main_ref, kt_ref, ab_ref, abt_ref) = rest[:n_cast], rest[n_cast:n_cast + 4]
    cast_out, stage_ref = rest[n_cast + 4:2 * n_cast + 4], rest[2 * n_cast + 4]
    i = pl.program_id(0)
    for src, dst in zip(cast_in, cast_out):
        dst[...] = src[...].astype(BF16)

    @pl.when(i % tiles_per_seq == 0)
    def _():
        stage_ref[...] = jnp.zeros_like(stage_ref)

    x = x_ref[...]
    ms = jnp.mean(x * x, axis=-1, keepdims=True)
    u = (x * lax.rsqrt(ms + EPS) * g_ref[...]).astype(BF16)
    conv = list(range(DN_CONV_FIRST, DN_CONV_FIRST + DN_CONV_CHUNKS))
    plain = [c for c in range(MAIN_COLS // IN_TN) if c not in conv]
    order = [c for pair in zip(plain, conv) for c in pair] + plain[len(conv):]
    project = lambda c: _dot(u, wm_ref[:, c * IN_TN:(c + 1) * IN_TN])
    ab = _dot(u, wab_ref[...])
    y_next = project(order[0])
    for n, c in enumerate(order):
        cols = slice(c * IN_TN, (c + 1) * IN_TN)
        y = y_next
        if n + 1 < len(order):
            y_next = project(order[n + 1])
        if n == 0:
            ab_ref[...] = ab
            abt_ref[...] = ab.T
        if c < 2 * ATT_WIDTH // IN_TN:
            gain = qg_ref if c < ATT_WIDTH // IN_TN else kg_ref
            g0 = c * IN_TN % ATT_WIDTH
            m = _dot((y * y).astype(BF16), seg_ref[0:IN_TN, 0:IN_TN])
            y = y * lax.rsqrt(m + EPS) * gain[:, g0:g0 + IN_TN]
            if c >= ATT_WIDTH // IN_TN:
                kt_ref[g0:g0 + IN_TN, :] = y.T.astype(BF16)
        elif DN_CONV_FIRST <= c < DN_CONV_FIRST + DN_CONV_CHUNKS:
            assert CONV_K == 4
            j = c - DN_CONV_FIRST
            wcols = slice(j * IN_TN, (j + 1) * IN_TN)
            ext = jnp.concatenate([stage_ref[j], y], axis=0)
            tap = lambda t: 0.5 * cw_ref[t:t + 1, wcols]
            prev = pltpu.roll(ext, 1, axis=0)
            older = tap(1) * ext + tap(0) * prev
            h = tap(3) * ext + tap(2) * prev + pltpu.roll(older, 2, axis=0)
            h = h[DN_TAIL:]
            stage_ref[j] = y[IN_TM - DN_TAIL:IN_TM, :]
            y = h + h * jnp.tanh(h)
        main_ref[:, cols] = y.astype(BF16)


def _inproj(xf, gain, w_main, w_ab, qg, kg, seg, conv_w, seq, to_cast):
    t, d = xf.shape
    steps = t // IN_TM
    const = lambda i: (0, 0)
    slab = lambda w: pl.BlockSpec((w.shape[0] // steps, w.shape[1]), lambda i: (i, 0))
    outs = pl.pallas_call(
        functools.partial(_inproj_kernel, tiles_per_seq=seq // IN_TM),
        grid=(steps,),
        in_specs=[
            pl.BlockSpec((IN_TM, d), lambda i: (i, 0)),
            pl.BlockSpec((1, d), const),
            pl.BlockSpec((d, MAIN_COLS), const),
            pl.BlockSpec((d, AB_PAD), const),
            pl.BlockSpec((1, ATT_WIDTH), const),
            pl.BlockSpec((1, ATT_WIDTH), const),
            pl.BlockSpec((ATT_WIDTH, ATT_WIDTH), const),
            pl.BlockSpec((CONV_K, 3 * DN_WIDTH), const),
        ] + [slab(w) for w in to_cast],
        out_specs=[
            pl.BlockSpec((IN_TM, MAIN_COLS), lambda i: (i, 0)),
            pl.BlockSpec((ATT_WIDTH, IN_TM), lambda i: (0, i)),
            pl.BlockSpec((IN_TM, AB_PAD), lambda i: (i, 0)),
            pl.BlockSpec((AB_PAD, IN_TM), lambda i: (0, i)),
        ] + [slab(w) for w in to_cast],
        out_shape=[
            jax.ShapeDtypeStruct((t, MAIN_COLS), BF16),
            jax.ShapeDtypeStruct((ATT_WIDTH, t), BF16),
            jax.ShapeDtypeStruct((t, AB_PAD), F32),
            jax.ShapeDtypeStruct((AB_PAD, t), F32),
        ] + [jax.ShapeDtypeStruct(w.shape, BF16) for w in to_cast],
        scratch_shapes=[pltpu.VMEM((DN_CONV_CHUNKS, DN_TAIL, IN_TN), F32)],
        compiler_params=pltpu.CompilerParams(
            dimension_semantics=("arbitrary",), vmem_limit_bytes=VMEM_LIMIT),
        name="inproj",
    )(xf, gain, w_main, w_ab, qg, kg, seg, conv_w, *to_cast)
    return outs[:4], outs[4:]


def _attn_tile(q_ref, row0, k_refs, v_refs, bias_ref, start_mask, og_ref, o_ref):
    lane = lax.broadcasted_iota(jnp.int32, (1, LANES), 1)
    per_tile = LANES // ATT_HEAD_DIM
    rows = slice(row0, row0 + ATT_TQ)

    def mine(h):
        e = h % per_tile
        return (lane >= e * ATT_HEAD_DIM) & (lane < (e + 1) * ATT_HEAD_DIM)

    def tile_lanes(h):
        return slice(h // per_tile * LANES, (h // per_tile + 1) * LANES)

    def scores(h):
        qp = q_ref[rows, tile_lanes(h)]
        q_e = jnp.where(mine(h), qp, jnp.zeros_like(qp))
        bias = bias_ref[h] if start_mask is None else bias_ref[h] + start_mask
        kw = jnp.concatenate([r[tile_lanes(h), :] for r in k_refs], axis=1)
        return _dot(q_e, kw) + bias

    def attend(h, s):
        mx = jnp.max(s, axis=-1, keepdims=True)
        pe = jnp.exp2(s - mx).astype(BF16)
        vw = jnp.concatenate([r[:, tile_lanes(h)] for r in v_refs], axis=0)
        o_e = _dot(pe, jnp.where(mine(h), vw, jnp.ones_like(vw)))
        return o_e / pltpu.roll(o_e, ATT_HEAD_DIM, axis=1)

    normed = []
    s_next = scores(0)
    for h in range(ATT_HEADS):
        s_cur = s_next
        if h + 1 < ATT_HEADS:
            s_next = scores(h + 1)
        normed.append(attend(h, s_cur))
        yield
    outs = [jnp.where(lane < ATT_HEAD_DIM, normed[t * per_tile], normed[t * per_tile + 1])
            for t in range(ATT_WIDTH // LANES)]
    o = jnp.concatenate(outs, axis=1)
    ms = jnp.mean(o * o, axis=-1, keepdims=True)
    o_ref[rows, :] = (o * lax.rsqrt(ms + EPS) * og_ref[...]).astype(BF16)


def _attn_kernel(q_ref, k0_ref, k1_ref, k2_ref, k3_ref, v0_ref, v1_ref, v2_ref, v3_ref, sig_ref, og_ref, o_ref,
                 bias_ref):
    i = pl.program_id(1)

    @pl.when((pl.program_id(0) == 0) & (i == 0))
    def _():
        r = lax.broadcasted_iota(jnp.int32, (ATT_TQ, ATT_WIN), 0)
        s = lax.broadcasted_iota(jnp.int32, (ATT_TQ, ATT_WIN), 1)
        back = (s >> 6) - (r >> 6)
        in_band = (back >= 0) & (back <= N_PAST_CHUNKS)
        for h in range(ATT_HEADS):
            rows = jnp.broadcast_to(sig_ref[h:h + 1, :], (ATT_TQ, sig_ref.shape[1]))
            toep = pltpu.roll(rows, 0, 1, stride=1, stride_axis=0)
            bias_ref[h] = jnp.where(in_band, toep[:, 0:ATT_WIN], NEG_BIG)

    def tiles(masks):
        _run(_weave(
            _attn_tile(q_ref, 0, (k0_ref, k1_ref, k2_ref), (v0_ref, v1_ref, v2_ref), bias_ref, masks[0],
                       og_ref, o_ref),
            _attn_tile(q_ref, ATT_TQ, (k1_ref, k2_ref, k3_ref), (v1_ref, v2_ref, v3_ref), bias_ref, masks[1],
                       og_ref, o_ref), 1))

    steps_in_past = N_PAST_CHUNKS * CHUNK // (ATT_TILES * ATT_TQ)

    @pl.when(i < steps_in_past)
    def _():
        jchunk = lax.broadcasted_iota(jnp.int32, (1, ATT_WIN), 1) >> 6
        tiles([jnp.where(jchunk + (ATT_TQ // CHUNK) * (ATT_TILES * i + n) - N_PAST_CHUNKS >= 0, 0.0, NEG_BIG)
               for n in range(ATT_TILES)])

    @pl.when(i >= steps_in_past)
    def _():
        tiles([None] * ATT_TILES)


def _attention(main, kt, signal, og, batch, seq):
    nb = seq // ATT_TQ
    steps = nb // ATT_TILES
    blk = (ATT_TQ, ATT_WIDTH)

    window_blk = lambda b, i, j: b * nb + jnp.maximum(ATT_TILES * i - 2 + j, 0)

    def k_spec(j):
        return pl.BlockSpec((ATT_WIDTH, ATT_TQ), lambda b, i: (0, window_blk(b, i, j)))

    def v_spec(j):
        return pl.BlockSpec(blk, lambda b, i: (window_blk(b, i, j), 2))

    step_rows = pl.BlockSpec((ATT_TILES * ATT_TQ, ATT_WIDTH), lambda b, i: (b * steps + i, 0))
    return pl.pallas_call(
        _attn_kernel,
        grid=(batch, steps),
        in_specs=[
            step_rows,
            k_spec(0), k_spec(1), k_spec(2), k_spec(3),
            v_spec(0), v_spec(1), v_spec(2), v_spec(3),
            pl.BlockSpec(signal.shape, lambda b, i: (0, 0)),
            pl.BlockSpec((1, ATT_WIDTH), lambda b, i: (0, 0)),
        ],
        out_specs=step_rows,
        out_shape=jax.ShapeDtypeStruct((batch * seq, ATT_WIDTH), BF16),
        scratch_shapes=[pltpu.VMEM((ATT_HEADS, ATT_TQ, ATT_WIN), F32)],
        compiler_params=pltpu.CompilerParams(
            dimension_semantics=("arbitrary", "arbitrary"), vmem_limit_bytes=VMEM_LIMIT),
        name="band_attention",
    )(main, kt, kt, kt, kt, main, main, main, main, signal, og)


def _bias_signal(rel_bias):
    nh = rel_bias.shape[0]
    lo = N_PAST_CHUNKS * CHUNK - (ATT_WIN - 1) + MAX_REL
    n_dist = ATT_TQ + ATT_WIN - 1
    n_tail = n_dist - (2 * MAX_REL + 1 - lo)
    rb = rel_bias.astype(F32) * LOG2_E
    by_dist = jnp.concatenate([rb[:, lo:], jnp.broadcast_to(rb[:, -1:], (nh, n_tail))], axis=1)
    period = ATT_TQ + ATT_WIN
    return jnp.roll(jnp.pad(by_dist[:, ::-1], ((0, 0), (0, period - n_dist))), -(ATT_TQ - 1), axis=1)


def _weave(main, side, every):
    n = 0
    done = object()
    while True:
        if n % every == 0:
            next(side, done)
        if next(main, done) is done:
            break
        n += 1
        yield
    _run(side)


def _chain(*stages):
    for g in stages:
        yield from g


def _run(stage):
    for _ in stage:
        pass


def _inv_unit_lower(mats, ri, ci, out):
    eye = jnp.where(ri == ci, 1.0, 0.0)
    diag8 = (ri >> 3) == (ci >> 3)
    n1 = [jnp.where(diag8, -a, 0.0) for a in mats]
    n1b = [n.astype(BF16) for n in n1]
    n2b = [_dot(n, n).astype(BF16) for n in n1b]
    yield
    n4b = [_dot(n, n).astype(BF16) for n in n2b]
    ts = [eye + n for n in n1]
    yield
    ts = [t + _dot(t.astype(BF16), n) for t, n in zip(ts, n2b)]
    yield
    ts = [t + _dot(t.astype(BF16), n) for t, n in zip(ts, n4b)]
    yield
    half = DN_C // 2
    r_h = lax.broadcasted_iota(jnp.int32, (half, DN_C), 0)
    c_o = lax.broadcasted_iota(jnp.int32, (half, DN_C), 1)
    for sh in range(3, 7):
        blk = 1 << sh
        odd = [slice(s0, s0 + blk) for s0 in range(blk, DN_C, 2 * blk)]
        even = [slice(s0, s0 + blk) for s0 in range(0, DN_C, 2 * blk)]
        take = lambda m, parts: jnp.concatenate([m[p] for p in parts], axis=0)
        r_o = r_h + (((r_h >> sh) + 1) << sh)
        pair =((r_o >> (sh + 1)) == (c_o >> (sh + 1))) & ((r_o >> sh) != (c_o >> sh))
        tb = [t.astype(BF16) for t in ts]
        lt = [_dot(jnp.where(pair, take(a, odd), 0.0).astype(BF16), t) for a, t in zip(mats, tb)]
        yield
        zero = jnp.zeros((blk, DN_C), F32)
        lt = [jnp.concatenate([p for m in range(half // blk) for p in (zero, x[m * blk:(m + 1) * blk])],
                              axis=0).astype(BF16) for x in lt]
        t_odd = [take(t, odd) for t in ts]
        new = [t - _dot(t.astype(BF16), x) for t, x in zip(t_odd, lt)]
        ts = [jnp.concatenate([p for m in range(half // blk)
                               for p in (t[even[m]], n[m * blk:(m + 1) * blk])], axis=0)
              for t, n in zip(ts, new)]
        yield
    out.extend(ts)


def _gdn_local(x_ref, ab_ref, abt_ref, ltri_ref, utri_ref, alog_ref, dtb_ref, alogt_ref, dtbt_ref, slot,
               a_scr, att_scr, rhs_scr, qd_scr, kt_scr, edl_scr):
    ab = ab_ref[...]
    g_col = -jnp.exp(alog_ref[...]) * _softplus(ab + dtb_ref[...])
    beta_col = _sigmoid(ab)
    g_row = -jnp.exp(alogt_ref[...]) * _softplus(abt_ref[0:8, :] + dtbt_ref[...])
    gc1, gc2, gc3 = _split3(g_col)
    gr1, gr2, gr3 = _split3(g_row)

    y = x_ref[...].astype(F32)

    ones = jnp.ones((DN_HEAD_DIM, DN_HEAD_DIM), BF16)
    sq = [(t * t).astype(BF16) for t in (y[:, g * DN_HEAD_DIM:(g + 1) * DN_HEAD_DIM]
                                         for g in range(2 * DN_HEADS))]
    ssq = [_dot(t, ones) for t in sq]
    yield
    ltri = ltri_ref[...]
    utri = utri_ref[...]
    d_col = _dot(ltri, gc1) + (_dot(ltri, gc2) + _dot(ltri, gc3))
    d_row = _dot(gr1, utri) + (_dot(gr2, utri) + _dot(gr3, utri))
    qk = []
    for g, tot in enumerate(ssq):
        inv = lax.rsqrt(tot + EPS)
        qk.append(y[:, g * DN_HEAD_DIM:(g + 1) * DN_HEAD_DIM]
                  * (inv * DN_HEAD_DIM ** -0.5 if g < DN_HEADS else inv))

    ri = lax.broadcasted_iota(jnp.int32, (DN_C, DN_C), 0)
    ci = lax.broadcasted_iota(jnp.int32, (DN_C, DN_C), 1)
    lmasks, kbqs, kbfs = [], [], []
    for n in range(DN_ITEMS):
        c, h = divmod(n, DN_HEADS)
        rows = slice(c * DN_C, (c + 1) * DN_C)
        q = qk[h][rows]
        k = qk[DN_HEADS + h][rows]
        v = y[rows, 2 * DN_WIDTH + h * DN_HEAD_DIM:2 * DN_WIDTH + (h + 1) * DN_HEAD_DIM]
        dc = d_col[rows, h:h + 1]
        dr = d_row[h:h + 1, rows]
        dl = dr[:, DN_C - 1:DN_C]
        bc = beta_col[rows, DN_HEADS + h:DN_HEADS + h + 1]
        assert DN_HEAD_DIM == DN_C
        bc = jnp.broadcast_to(bc, (DN_C, DN_HEAD_DIM))
        dc = jnp.broadcast_to(dc, (DN_C, DN_HEAD_DIM))
        kb = k * bc
        edc = jnp.exp(dc)
        rhs_scr[slot, n] = jnp.concatenate([v * bc, kb * edc], axis=1).astype(BF16)
        qd_scr[slot, n] = (q * edc).astype(BF16)
        kt_scr[slot, n] = (k * jnp.exp(dl - dc)).astype(BF16)
        edl_scr[slot, n] = jnp.broadcast_to(jnp.exp(dl), (8, DN_HEAD_DIM))
        lmasks.append(jnp.exp(jnp.where(ci <= ri, dc - dr, -jnp.inf)))
        kbqs.append(jnp.concatenate([kb, q], axis=0).astype(BF16))
        kbfs.append(k.astype(BF16))
    yield
    ps =[_dot_nt(kbq, kbf) for kbq, kbf in zip(kbqs, kbfs)]
    for n, (p, lmask) in enumerate(zip(ps, lmasks)):
        a_scr[slot, n] = jnp.where(ci < ri, p[0:DN_C] * lmask, 0.0)
        att_scr[slot, n] = (p[DN_C:2 * DN_C] * lmask).astype(BF16)


def _gdn_state(slot, gate_ref, og_ref, o_ref, out_row0, s_ref, a_scr, att_scr, rhs_scr, qd_scr, kt_scr,
               edl_scr):
    ri = lax.broadcasted_iota(jnp.int32, (DN_C, DN_C), 0)
    ci = lax.broadcasted_iota(jnp.int32, (DN_C, DN_C), 1)
    og = og_ref[...]
    tinvs = []
    yield from _inv_unit_lower([a_scr[slot, n] for n in range(DN_ITEMS)], ri, ci, tinvs)
    ws = [_dot(t.astype(BF16), rhs_scr[slot, n]) for n, t in enumerate(tinvs)]
    yield
    for c in range(DN_TB // DN_C):
        idx = [c * DN_HEADS + h for h in range(DN_HEADS)]
        s_old = [s_ref[h] for h in range(DN_HEADS)]
        rs = [_dot(jnp.concatenate([ws[n][:, DN_HEAD_DIM:].astype(BF16), qd_scr[slot, n]], axis=0),
                   s.astype(BF16)) for n, s in zip(idx, s_old)]
        yield
        vnb = [(ws[n][:, 0:DN_HEAD_DIM] - r[0:DN_C]).astype(BF16) for n, r in zip(idx, rs)]
        outs = [r[DN_C:2 * DN_C] + _dot(att_scr[slot, n], vn) for n, r, vn in zip(idx, rs, vnb)]
        yield
        for h, (n, vn) in enumerate(zip(idx, vnb)):
            s_ref[h] = s_old[h] * edl_scr[slot, n, 0:1, :] + _dot_tn(kt_scr[slot, n], vn)
        yield
        rows = slice(out_row0 + c * DN_C, out_row0 + (c + 1) * DN_C)
        for h, o in enumerate(outs):
            hs = slice(h * DN_HEAD_DIM, (h + 1) * DN_HEAD_DIM)
            ms = jnp.mean(o * o, axis=-1, keepdims=True)
            gt = gate_ref[rows, hs].astype(F32)
            o_ref[rows, hs] = (o * lax.rsqrt(ms + EPS) * og * (gt * _sigmoid(gt))).astype(BF16)


def _ffn_stages(x_ref, oa_ref, od_ref, wo_ref, fg_ref, w1_ref, w2_ref, o_ref):
    d = x_ref.shape[1]
    oa = oa_ref[...]
    od = od_ref[...]
    mixed = []
    for j in range(d // FFN_TN):
        cols = slice(j * FFN_TN, (j + 1) * FFN_TN)
        mixed.append(_dot(oa, wo_ref[0:ATT_WIDTH, cols]) + _dot(od, wo_ref[ATT_WIDTH:, cols]))
        yield
    h1 = x_ref[...] + jnp.concatenate(mixed, axis=1)
    ms = jnp.mean(h1 * h1, axis=-1, keepdims=True)
    n = (h1 * lax.rsqrt(ms + EPS) * fg_ref[...]).astype(BF16)

    def up(c, out):
        for j in range(FFN_TF // FFN_TN):
            col0 = c * FFN_TF + j * FFN_TN
            out.append(_dot(n, w1_ref[:, col0:col0 + FFN_TN]))
            yield

    n_chunks = w1_ref.shape[1] // FFN_TF
    acc = h1
    z_next = []
    yield from up(0, z_next)
    for c in range(n_chunks):
        z = z_next
        if c + 1 < n_chunks:
            z_next = []
            yield from up(c + 1, z_next)
        act = jnp.concatenate([jnp.square(jnp.maximum(piece, 0.0)).astype(BF16) for piece in z], axis=1)
        down = []
        for j in range(d // FFN_TN):
            down.append(_dot(act, w2_ref[c * FFN_TF:(c + 1) * FFN_TF, j * FFN_TN:(j + 1) * FFN_TN]))
            yield
        acc = acc + jnp.concatenate(down, axis=1)
    o_ref[...] = acc


def _gdn_ffn_kernel(x0_ref, x1_ref, x2_ref, ab0_ref, ab1_ref, ab2_ref, abt0_ref, abt1_ref, abt2_ref,
                    gate_ref, ltri_ref, utri_ref, alog_ref, dtb_ref, alogt_ref, dtbt_ref, dn_og_ref,
                    xres_ref, oa_ref, wo_ref, fg_ref, w1_ref, w2_ref, o_ref, s_ref, od_scr, *local_scr,
                    steps_per_seq):
    s = pl.program_id(0)
    cur = s % 2
    consts = (ltri_ref, utri_ref, alog_ref, dtb_ref, alogt_ref, dtbt_ref)

    def local(x_ref, ab_ref, abt_ref, slot):
        return _gdn_local(x_ref, ab_ref, abt_ref, *consts, slot, *local_scr)

    def state(slot, out_row0):
        return _gdn_state(slot, gate_ref, dn_og_ref, od_scr.at[cur], out_row0, s_ref, *local_scr)

    @pl.when(s == 0)
    def _():
        od_scr[1] = jnp.zeros(od_scr.shape[1:], od_scr.dtype)

    @pl.when(s % steps_per_seq == 0)
    def _():
        s_ref[...] = jnp.zeros_like(s_ref)
        _run(local(x0_ref, ab0_ref, abt0_ref, 0))

    deltanet = _chain(
        _weave(state(0, 0), local(x1_ref, ab1_ref, abt1_ref, 1), DN_STATE_PER_LOCAL),
        _weave(state(1, DN_TB), local(x2_ref, ab2_ref, abt2_ref, 0), DN_STATE_PER_LOCAL))
    ffn = _ffn_stages(xres_ref, oa_ref, od_scr.at[1 - cur], wo_ref, fg_ref, w1_ref, w2_ref, o_ref)
    _run(_weave(deltanet, ffn, 1))


def _deltanet_ffn(xf, main, ab, abt, o_att, alog, dtb, alogt, dtbt, dn_og, w_out, fg, w1, w2, seq):
    t, d = xf.shape
    dff = w1.shape[1]
    assert FFN_TM == 2 * DN_TB
    nb = seq // DN_TB
    spq = nb // 2
    steps = t // FFN_TM
    tile = lambda s: jnp.minimum(s, steps - 1)
    prev = lambda s: jnp.maximum(s - 1, 0)
    seq_blk0 = lambda s: tile(s) // spq * nb
    first = seq_blk0
    odd = lambda s: 2 * tile(s) + 1
    nxt = lambda s: jnp.minimum(2 * tile(s) + 2, seq_blk0(s) + nb - 1)
    const = lambda s: (0, 0)

    def x_spec(blk):
        return pl.BlockSpec((DN_TB, 3 * DN_WIDTH), lambda s: (blk(s), 1))

    def ab_spec(blk):
        return pl.BlockSpec((DN_TB, AB_PAD), lambda s: (blk(s), 0))

    def abt_spec(blk):
        return pl.BlockSpec((AB_PAD, DN_TB), lambda s: (0, blk(s)))

    r = np.arange(DN_TB)
    same = (r[:, None] // DN_C) == (r[None, :] // DN_C)
    ltri = jnp.asarray(same & (r[None, :] <= r[:, None]), BF16)
    utri = jnp.asarray(same & (r[:, None] <= r[None, :]), BF16)
    item = lambda *shape: (2, DN_ITEMS) + shape
    resident = dict(pipeline_mode=pl.Buffered(1))
    return pl.pallas_call(
        functools.partial(_gdn_ffn_kernel, steps_per_seq=spq),
        grid=(steps + 1,),
        in_specs=[
            x_spec(first), x_spec(odd), x_spec(nxt),
            ab_spec(first), ab_spec(odd), ab_spec(nxt),
            abt_spec(first), abt_spec(odd), abt_spec(nxt),
            pl.BlockSpec((2 * DN_TB, DN_WIDTH), lambda s: (tile(s), 6)),
            pl.BlockSpec((DN_TB, DN_TB), const),
            pl.BlockSpec((DN_TB, DN_TB), const),
            pl.BlockSpec((1, AB_PAD), const),
            pl.BlockSpec((1, AB_PAD), const),
            pl.BlockSpec((8, 1), const),
            pl.BlockSpec((8, 1), const),
            pl.BlockSpec((1, DN_HEAD_DIM), const),
            pl.BlockSpec((FFN_TM, d), lambda s: (prev(s), 0)),
            pl.BlockSpec((FFN_TM, ATT_WIDTH), lambda s: (prev(s), 0)),
            pl.BlockSpec((d, d), const, **resident),
            pl.BlockSpec((1, d), const),
            pl.BlockSpec((d, dff), const, **resident),
            pl.BlockSpec((dff, d), const, **resident),
        ],
        out_specs=pl.BlockSpec((FFN_TM, d), lambda s: (prev(s), 0)),
        out_shape=jax.ShapeDtypeStruct((t, d), F32),
        scratch_shapes=[
            pltpu.VMEM((DN_HEADS, DN_HEAD_DIM, DN_HEAD_DIM), F32),
            pltpu.VMEM((2, FFN_TM, DN_WIDTH), BF16),
            pltpu.VMEM(item(DN_C, DN_C), F32),
            pltpu.VMEM(item(DN_C, DN_C), BF16),
            pltpu.VMEM(item(DN_C, 2 * DN_HEAD_DIM), BF16),
            pltpu.VMEM(item(DN_C, DN_HEAD_DIM), BF16),
            pltpu.VMEM(item(DN_C, DN_HEAD_DIM), BF16),
            pltpu.VMEM(item(8, DN_HEAD_DIM), F32),
        ],
        compiler_params=pltpu.CompilerParams(
            dimension_semantics=("arbitrary",), vmem_limit_bytes=VMEM_LIMIT),
        name="deltanet_ffn",
    )(main, main, main, ab, ab, ab, abt, abt, abt, main, ltri, utri, alog, dtb, alogt, dtbt, dn_og,
      xf, o_att, w_out, fg, w1, w2)


def _layer(h, p, batch, seq):
    w_in = p["w_in"]
    w_main = w_in.astype(BF16)
    w_ab = jnp.pad(w_in[:, MAIN_COLS:], ((0, 0), (0, AB_PAD - 2 * DN_HEADS))).astype(BF16)
    head_of = np.arange(ATT_WIDTH) // ATT_HEAD_DIM
    seg = jnp.asarray((head_of[:, None] == head_of[None, :]) / ATT_HEAD_DIM, BF16)
    qg = (jnp.tile(p["att_q_gain"].astype(F32), ATT_HEADS) * (ATT_HEAD_DIM ** -0.5 * LOG2_E))[None]
    kg = jnp.tile(p["att_k_gain"].astype(F32), ATT_HEADS)[None]
    (main, kt, ab, abt), (w_out, w_ff1, w_ff2) = _inproj(
        h, p["mix_norm_gain"].astype(F32)[None], w_main, w_ab, qg, kg, seg, p["dn_conv_w"].astype(F32), seq,
        [p["w_out"].astype(F32), p["w_ff1"].astype(F32), p["w_ff2"].astype(F32)])

    o_att = _attention(main, kt, _bias_signal(p["rel_bias"]),
                       p["att_out_gain"].astype(F32)[None], batch, seq)

    pad_row = lambda v: jnp.pad(v.astype(F32), (0, AB_PAD - DN_HEADS))[None]
    pad_col = lambda v: jnp.pad(v.astype(F32), (0, 8 - DN_HEADS))[:, None]
    return _deltanet_ffn(h, main, ab, abt, o_att, pad_row(p["dn_a_log"]), pad_row(p["dn_dt_bias"]),
                         pad_col(p["dn_a_log"]), pad_col(p["dn_dt_bias"]),
                         p["dn_out_gain"].astype(F32)[None], w_out, p["ffn_norm_gain"].astype(F32)[None],
                         w_ff1, w_ff2, seq)


def kernel(x, mix_norm_gain, w_in, att_q_gain, att_k_gain, rel_bias, att_out_gain, dn_conv_w,
           dn_a_log, dn_dt_bias, dn_out_gain, w_out, ffn_norm_gain, w_ff1, w_ff2):
    batch, seq, d = x.shape
    assert seq % max(ATT_TQ, 2 * DN_TB) == 0 and (batch * seq) % max(IN_TM, FFN_TM) == 0
    params = dict(mix_norm_gain=mix_norm_gain, w_in=w_in, att_q_gain=att_q_gain,
                  att_k_gain=att_k_gain, rel_bias=rel_bias, att_out_gain=att_out_gain,
                  dn_conv_w=dn_conv_w, dn_a_log=dn_a_log, dn_dt_bias=dn_dt_bias,
                  dn_out_gain=dn_out_gain, w_out=w_out, ffn_norm_gain=ffn_norm_gain,
                  w_ff1=w_ff1, w_ff2=w_ff2)
    h = x.reshape(batch * seq, d)
    for layer in range(w_in.shape[0]):
        h = _layer(h, {name: v[layer] for name, v in params.items()}, batch, seq)
    return h.reshape(batch, seq, d)
```

```python
import functools

import numpy as np
import jax
import jax.numpy as jnp
from jax import lax
from jax.experimental import pallas as pl
from jax.experimental.pallas import tpu as pltpu

F32 = jnp.float32
BF16 = jnp.bfloat16

CHUNK = 64
N_PAST_CHUNKS = 8
ATT_HEADS = 8
ATT_HEAD_DIM = 64
ATT_WIDTH = ATT_HEADS * ATT_HEAD_DIM
MAX_REL = 256
DN_HEADS = 4
DN_HEAD_DIM = 128
DN_WIDTH = DN_HEADS * DN_HEAD_DIM
CONV_K = 4
EPS = 1e-6

MAIN_COLS = 3 * ATT_WIDTH + 4 * DN_WIDTH
AB_PAD = 128

LANES = 128
VMEM_LIMIT = 56 * 1024 * 1024

IN_TM = 1024
IN_TN = 256
ATT_TQ = 4 * CHUNK
ATT_WIN = ATT_TQ + N_PAST_CHUNKS * CHUNK
ATT_TILES = 2
DN_C = 128
DN_TB = 2 * DN_C
DN_TAIL = 8
DN_ITEMS = (DN_TB // DN_C) * DN_HEADS
DN_CONV_FIRST = 3 * ATT_WIDTH // IN_TN
DN_CONV_CHUNKS = 3 * DN_WIDTH // IN_TN
DN_STATE_PER_LOCAL = 7
FFN_TM = 512
FFN_TF = 1024
FFN_TN = 256
NEG_BIG = -1e30
LOG2_E = 1.4426950408889634


def _dot(a, b):
    return jnp.dot(a, b, preferred_element_type=F32)


def _dot_nt(a, b):
    return lax.dot_general(a, b, (((1,), (1,)), ((), ())), preferred_element_type=F32)


def _dot_tn(a, b):
    return lax.dot_general(a, b, (((0,), (0,)), ((), ())), preferred_element_type=F32)


def _split3(a):
    p1 = a.astype(BF16)
    r1 = a - p1.astype(F32)
    p2 = r1.astype(BF16)
    p3 = (r1 - p2.astype(F32)).astype(BF16)
    return p1, p2, p3


def _sigmoid(x):
    return 1.0 / (1.0 + jnp.exp2(x * -LOG2_E))


def _softplus(x):
    return jnp.maximum(x, 0.0) + jnp.log(1.0 + jnp.exp(-jnp.abs(x)))


def _inproj_kernel(x_ref, g_ref, wm_ref, wab_ref, qg_ref, kg_ref, seg_ref, cw_ref, *rest, tiles_per_seq):
    n_cast = (len(rest) - 5) // 2
    cast_in, (main_ref, kt_ref, ab_ref, abt_ref) = rest[:n_cast], rest[n_cast:n_cast + 4]
    cast_out, stage_ref = rest[n_cast + 4:2 * n_cast + 4], rest[2 * n_cast + 4]
    i = pl.program_id(0)

    @pl.when(i % tiles_per_seq == 0)
    def _():
        stage_ref[...] = jnp.zeros_like(stage_ref)

    x = x_ref[...]
    ms = jnp.mean(x * x, axis=-1, keepdims=True)
    u = (x * lax.rsqrt(ms + EPS) * g_ref[...]).astype(BF16)
    conv = list(range(DN_CONV_FIRST, DN_CONV_FIRST + DN_CONV_CHUNKS))
    plain = [c for c in range(MAIN_COLS // IN_TN) if c not in conv]
    order = [c for pair in zip(plain, conv) for c in pair] + plain[len(conv):]
    project = lambda c: _dot(u, wm_ref[:, c * IN_TN:(c + 1) * IN_TN])
    ab = _dot(u, wab_ref[...])
    y_next = project(order[0])
    for n, c in enumerate(order):
        cols = slice(c * IN_TN, (c + 1) * IN_TN)
        y = y_next
        if n + 1 < len(order):
            y_next = project(order[n + 1])
        if n == 0:
            ab_ref[...] = ab
            abt_ref[...] = ab.T
        if n == 1:
            for src, dst in zip(cast_in, cast_out):
                dst[...] = src[...].astype(BF16)
        if c < 2 * ATT_WIDTH // IN_TN:
            gain = qg_ref if c < ATT_WIDTH // IN_TN else kg_ref
            g0 = c * IN_TN % ATT_WIDTH
            m = _dot((y * y).astype(BF16), seg_ref[0:IN_TN, 0:IN_TN])
            y = y * lax.rsqrt(m + EPS) * gain[:, g0:g0 + IN_TN]
            if c >= ATT_WIDTH // IN_TN:
                kt_ref[g0:g0 + IN_TN, :] = y.T.astype(BF16)
        elif DN_CONV_FIRST <= c < DN_CONV_FIRST + DN_CONV_CHUNKS:
            assert CONV_K == 4
            j = c - DN_CONV_FIRST
            wcols = slice(j * IN_TN, (j + 1) * IN_TN)
            ext = jnp.concatenate([stage_ref[j], y], axis=0)
            tap = lambda t: 0.5 * cw_ref[t:t + 1, wcols]
            prev = pltpu.roll(ext, 1, axis=0)
            older = tap(1) * ext + tap(0) * prev
            h = tap(3) * ext + tap(2) * prev + pltpu.roll(older, 2, axis=0)
            h = h[DN_TAIL:]
            stage_ref[j] = y[IN_TM - DN_TAIL:IN_TM, :]
            y = h + h * jnp.tanh(h)
        main_ref[:, cols] = y.astype(BF16)


def _inproj(xf, gain, w_main, w_ab, qg, kg, seg, conv_w, seq, to_cast):
    t, d = xf.shape
    steps = t // IN_TM
    const = lambda i: (0, 0)
    slab = lambda w: pl.BlockSpec((w.shape[0] // steps, w.shape[1]), lambda i: (i, 0))
    outs = pl.pallas_call(
        functools.partial(_inproj_kernel, tiles_per_seq=seq // IN_TM),
        grid=(steps,),
        in_specs=[
            pl.BlockSpec((IN_TM, d), lambda i: (i, 0)),
            pl.BlockSpec((1, d), const),
            pl.BlockSpec((d, MAIN_COLS), const),
            pl.BlockSpec((d, AB_PAD), const),
            pl.BlockSpec((1, ATT_WIDTH), const),
            pl.BlockSpec((1, ATT_WIDTH), const),
            pl.BlockSpec((ATT_WIDTH, ATT_WIDTH), const),
            pl.BlockSpec((CONV_K, 3 * DN_WIDTH), const),
        ] + [slab(w) for w in to_cast],
        out_specs=[
            pl.BlockSpec((IN_TM, MAIN_COLS), lambda i: (i, 0)),
            pl.BlockSpec((ATT_WIDTH, IN_TM), lambda i: (0, i)),
            pl.BlockSpec((IN_TM, AB_PAD), lambda i: (i, 0)),
            pl.BlockSpec((AB_PAD, IN_TM), lambda i: (0, i)),
        ] + [slab(w) for w in to_cast],
        out_shape=[
            jax.ShapeDtypeStruct((t, MAIN_COLS), BF16),
            jax.ShapeDtypeStruct((ATT_WIDTH, t), BF16),
            jax.ShapeDtypeStruct((t, AB_PAD), F32),
            jax.ShapeDtypeStruct((AB_PAD, t), F32),
        ] + [jax.ShapeDtypeStruct(w.shape, BF16) for w in to_cast],
        scratch_shapes=[pltpu.VMEM((DN_CONV_CHUNKS, DN_TAIL, IN_TN), F32)],
        compiler_params=pltpu.CompilerParams(
            dimension_semantics=("arbitrary",), vmem_limit_bytes=VMEM_LIMIT),
        name="inproj",
    )(xf, gain, w_main, w_ab, qg, kg, seg, conv_w, *to_cast)
    return outs[:4], outs[4:]


def _attn_tile(q_ref, row0, k_refs, v_refs, bias_ref, start_mask, og_ref, o_ref):
    lane = lax.broadcasted_iota(jnp.int32, (1, LANES), 1)
    per_tile = LANES // ATT_HEAD_DIM
    rows = slice(row0, row0 + ATT_TQ)

    def mine(h):
        e = h % per_tile
        return (lane >= e * ATT_HEAD_DIM) & (lane < (e + 1) * ATT_HEAD_DIM)

    def tile_lanes(h):
        return slice(h // per_tile * LANES, (h // per_tile + 1) * LANES)

    def scores(h):
        qp = q_ref[rows, tile_lanes(h)]
        q_e = jnp.where(mine(h), qp, jnp.zeros_like(qp))
        bias = bias_ref[h] if start_mask is None else bias_ref[h] + start_mask
        kw = jnp.concatenate([r[tile_lanes(h), :] for r in k_refs], axis=1)
        return _dot(q_e, kw) + bias

    def attend(h, s):
        mx = jnp.max(s, axis=-1, keepdims=True)
        pe = jnp.exp2(s - mx).astype(BF16)
        vw = jnp.concatenate([r[:, tile_lanes(h)] for r in v_refs], axis=0)
        o_e = _dot(pe, jnp.where(mine(h), vw, jnp.ones_like(vw)))
        return o_e / pltpu.roll(o_e, ATT_HEAD_DIM, axis=1)

    normed = []
    s_next = scores(0)
    for h in range(ATT_HEADS):
        s_cur = s_next
        if h + 1 < ATT_HEADS:
            s_next = scores(h + 1)
        normed.append(attend(h, s_cur))
        yield
    outs = [jnp.where(lane < ATT_HEAD_DIM, normed[t * per_tile], normed[t * per_tile + 1])
            for t in range(ATT_WIDTH // LANES)]
    o = jnp.concatenate(outs, axis=1)
    ms = jnp.mean(o * o, axis=-1, keepdims=True)
    o_ref[rows, :] = (o * lax.rsqrt(ms + EPS) * og_ref[...]).astype(BF16)


def _attn_kernel(q_ref, k0_ref, k1_ref, k2_ref, k3_ref, v0_ref, v1_ref, v2_ref, v3_ref, sig_ref, og_ref, o_ref,
                 bias_ref):
    i = pl.program_id(1)

    @pl.when((pl.program_id(0) == 0) & (i == 0))
    def _():
        r = lax.broadcasted_iota(jnp.int32, (ATT_TQ, ATT_WIN), 0)
        s = lax.broadcasted_iota(jnp.int32, (ATT_TQ, ATT_WIN), 1)
        back = (s >> 6) - (r >> 6)
        in_band = (back >= 0) & (back <= N_PAST_CHUNKS)
        for h in range(ATT_HEADS):
            rows = jnp.broadcast_to(sig_ref[h:h + 1, :], (ATT_TQ, sig_ref.shape[1]))
            toep = pltpu.roll(rows, 0, 1, stride=1, stride_axis=0)
            bias_ref[h] = jnp.where(in_band, toep[:, 0:ATT_WIN], NEG_BIG)

    def tiles(masks):
        _run(_weave(
            _attn_tile(q_ref, 0, (k0_ref, k1_ref, k2_ref), (v0_ref, v1_ref, v2_ref), bias_ref, masks[0],
                       og_ref, o_ref),
            _attn_tile(q_ref, ATT_TQ, (k1_ref, k2_ref, k3_ref), (v1_ref, v2_ref, v3_ref), bias_ref, masks[1],
                       og_ref, o_ref), 1))

    steps_in_past = N_PAST_CHUNKS * CHUNK // (ATT_TILES * ATT_TQ)

    @pl.when(i < steps_in_past)
    def _():
        jchunk = lax.broadcasted_iota(jnp.int32, (1, ATT_WIN), 1) >> 6
        tiles([jnp.where(jchunk + (ATT_TQ // CHUNK) * (ATT_TILES * i + n) - N_PAST_CHUNKS >= 0, 0.0, NEG_BIG)
               for n in range(ATT_TILES)])

    @pl.when(i >= steps_in_past)
    def _():
        tiles([None] * ATT_TILES)


def _attention(main, kt, signal, og, batch, seq):
    nb = seq // ATT_TQ
    steps = nb // ATT_TILES
    blk = (ATT_TQ, ATT_WIDTH)

    window_blk = lambda b, i, j: b * nb + jnp.maximum(ATT_TILES * i - 2 + j, 0)

    def k_spec(j):
        return pl.BlockSpec((ATT_WIDTH, ATT_TQ), lambda b, i: (0, window_blk(b, i, j)))

    def v_spec(j):
        return pl.BlockSpec(blk, lambda b, i: (window_blk(b, i, j), 2))

    step_rows = pl.BlockSpec((ATT_TILES * ATT_TQ, ATT_WIDTH), lambda b, i: (b * steps + i, 0))
    return pl.pallas_call(
        _attn_kernel,
        grid=(batch, steps),
        in_specs=[
            step_rows,
            k_spec(0), k_spec(1), k_spec(2), k_spec(3),
            v_spec(0), v_spec(1), v_spec(2), v_spec(3),
            pl.BlockSpec(signal.shape, lambda b, i: (0, 0)),
            pl.BlockSpec((1, ATT_WIDTH), lambda b, i: (0, 0)),
        ],
        out_specs=step_rows,
        out_shape=jax.ShapeDtypeStruct((batch * seq, ATT_WIDTH), BF16),
        scratch_shapes=[pltpu.VMEM((ATT_HEADS, ATT_TQ, ATT_WIN), F32)],
        compiler_params=pltpu.CompilerParams(
            dimension_semantics=("arbitrary", "arbitrary"), vmem_limit_bytes=VMEM_LIMIT),
        name="band_attention",
    )(main, kt, kt, kt, kt, main, main, main, main, signal, og)


def _bias_signal(rel_bias):
    nh = rel_bias.shape[0]
    lo = N_PAST_CHUNKS * CHUNK - (ATT_WIN - 1) + MAX_REL
    n_dist = ATT_TQ + ATT_WIN - 1
    n_tail = n_dist - (2 * MAX_REL + 1 - lo)
    rb = rel_bias.astype(F32) * LOG2_E
    by_dist = jnp.concatenate([rb[:, lo:], jnp.broadcast_to(rb[:, -1:], (nh, n_tail))], axis=1)
    period = ATT_TQ + ATT_WIN
    return jnp.roll(jnp.pad(by_dist[:, ::-1], ((0, 0), (0, period - n_dist))), -(ATT_TQ - 1), axis=1)


def _weave(main, side, every):
    n = 0
    done = object()
    while True:
        if n % every == 0:
            next(side, done)
        if next(main, done) is done:
            break
        n += 1
        yield
    _run(side)


def _chain(*stages):
    for g in stages:
        yield from g


def _run(stage):
    for _ in stage:
        pass


def _inv_unit_lower(mats, ri, ci, out):
    eye = jnp.where(ri == ci, 1.0, 0.0)
    diag8 = (ri >> 3) == (ci >> 3)
    n1 = [jnp.where(diag8, -a, 0.0) for a in mats]
    n1b = [n.astype(BF16) for n in n1]
    n2b = [_dot(n, n).astype(BF16) for n in n1b]
    yield
    n4b = [_dot(n, n).astype(BF16) for n in n2b]
    ts = [eye + n for n in n1]
    yield
    ts = [t + _dot(t.astype(BF16), n) for t, n in zip(ts, n2b)]
    yield
    ts = [t + _dot(t.astype(BF16), n) for t, n in zip(ts, n4b)]
    yield
    half = DN_C // 2
    r_h = lax.broadcasted_iota(jnp.int32, (half, DN_C), 0)
    c_o = lax.broadcasted_iota(jnp.int32, (half, DN_C), 1)
    for sh in range(3, 7):
        blk = 1 << sh
        odd = [slice(s0, s0 + blk) for s0 in range(blk, DN_C, 2 * blk)]
        even = [slice(s0, s0 + blk) for s0 in range(0, DN_C, 2 * blk)]
        take = lambda m, parts: jnp.concatenate([m[p] for p in parts], axis=0)
        r_o = r_h + (((r_h >> sh) + 1) << sh)
        pair =((r_o >> (sh + 1)) == (c_o >> (sh + 1))) & ((r_o >> sh) != (c_o >> sh))
        tb = [t.astype(BF16) for t in ts]
        lt = [_dot(jnp.where(pair, take(a, odd), 0.0).astype(BF16), t) for a, t in zip(mats, tb)]
        yield
        zero = jnp.zeros((blk, DN_C), F32)
        lt = [jnp.concatenate([p for m in range(half // blk) for p in (zero, x[m * blk:(m + 1) * blk])],
                              axis=0).astype(BF16) for x in lt]
        t_odd = [take(t, odd) for t in ts]
        new = [t - _dot(t.astype(BF16), x) for t, x in zip(t_odd, lt)]
        ts = [jnp.concatenate([p for m in range(half // blk)
                               for p in (t[even[m]], n[m * blk:(m + 1) * blk])], axis=0)
              for t, n in zip(ts, new)]
        yield
    out.extend(ts)


def _gdn_local(x_ref, ab_ref, abt_ref, ltri_ref, utri_ref, alog_ref, dtb_ref, alogt_ref, dtbt_ref, slot,
               a_scr, att_scr, rhs_scr, qd_scr, kt_scr, edl_scr):
    ab = ab_ref[...]
    g_col = -jnp.exp(alog_ref[...]) * _softplus(ab + dtb_ref[...])
    beta_col = _sigmoid(ab)
    g_row = -jnp.exp(alogt_ref[...]) * _softplus(abt_ref[0:8, :] + dtbt_ref[...])
    gc1, gc2, gc3 = _split3(g_col)
    gr1, gr2, gr3 = _split3(g_row)

    y = x_ref[...].astype(F32)

    ones = jnp.ones((DN_HEAD_DIM, DN_HEAD_DIM), BF16)
    sq = [(t * t).astype(BF16) for t in (y[:, g * DN_HEAD_DIM:(g + 1) * DN_HEAD_DIM]
                                         for g in range(2 * DN_HEADS))]
    ssq = [_dot(t, ones) for t in sq]
    yield
    ltri = ltri_ref[...]
    utri = utri_ref[...]
    d_col = _dot(ltri, gc1) + (_dot(ltri, gc2) + _dot(ltri, gc3))
    d_row = _dot(gr1, utri) + (_dot(gr2, utri) + _dot(gr3, utri))
    qk = []
    for g, tot in enumerate(ssq):
        inv = lax.rsqrt(tot + EPS)
        qk.append(y[:, g * DN_HEAD_DIM:(g + 1) * DN_HEAD_DIM]
                  * (inv * DN_HEAD_DIM ** -0.5 if g < DN_HEADS else inv))

    ri = lax.broadcasted_iota(jnp.int32, (DN_C, DN_C), 0)
    ci = lax.broadcasted_iota(jnp.int32, (DN_C, DN_C), 1)
    lmasks, kbqs, kbfs = [], [], []
    for n in range(DN_ITEMS):
        c, h = divmod(n, DN_HEADS)
        rows = slice(c * DN_C, (c + 1) * DN_C)
        q = qk[h][rows]
        k = qk[DN_HEADS + h][rows]
        v = y[rows, 2 * DN_WIDTH + h * DN_HEAD_DIM:2 * DN_WIDTH + (h + 1) * DN_HEAD_DIM]
        dc = d_col[rows, h:h + 1]
        dr = d_row[h:h + 1, rows]
        dl = dr[:, DN_C - 1:DN_C]
        bc = beta_col[rows, DN_HEADS + h:DN_HEADS + h + 1]
        assert DN_HEAD_DIM == DN_C
        bc = jnp.broadcast_to(bc, (DN_C, DN_HEAD_DIM))
        dc = jnp.broadcast_to(dc, (DN_C, DN_HEAD_DIM))
        kb = k * bc
        edc = jnp.exp(dc)
        rhs_scr[slot, n] = jnp.concatenate([v * bc, kb * edc], axis=1).astype(BF16)
        qd_scr[slot, n] = (q * edc).astype(BF16)
        kt_scr[slot, n] = (k * jnp.exp(dl - dc)).astype(BF16)
        edl_scr[slot, n] = jnp.broadcast_to(jnp.exp(dl), (8, DN_HEAD_DIM))
        lmasks.append(jnp.exp(jnp.where(ci <= ri, dc - dr, -jnp.inf)))
        kbqs.append(jnp.concatenate([kb, q], axis=0).astype(BF16))
        kbfs.append(k.astype(BF16))
    yield
    ps =[_dot_nt(kbq, kbf) for kbq, kbf in zip(kbqs, kbfs)]
    for n, (p, lmask) in enumerate(zip(ps, lmasks)):
        a_scr[slot, n] = jnp.where(ci < ri, p[0:DN_C] * lmask, 0.0)
        att_scr[slot, n] = (p[DN_C:2 * DN_C] * lmask).astype(BF16)


def _gdn_state(slot, gate_ref, og_ref, o_ref, out_row0, s_ref, a_scr, att_scr, rhs_scr, qd_scr, kt_scr,
               edl_scr):
    ri = lax.broadcasted_iota(jnp.int32, (DN_C, DN_C), 0)
    ci = lax.broadcasted_iota(jnp.int32, (DN_C, DN_C), 1)
    og = og_ref[...]
    tinvs = []
    yield from _inv_unit_lower([a_scr[slot, n] for n in range(DN_ITEMS)], ri, ci, tinvs)
    ws = [_dot(t.astype(BF16), rhs_scr[slot, n]) for n, t in enumerate(tinvs)]
    yield
    for c in range(DN_TB // DN_C):
        idx = [c * DN_HEADS + h for h in range(DN_HEADS)]
        s_old = [s_ref[h] for h in range(DN_HEADS)]
        rs = [_dot(jnp.concatenate([ws[n][:, DN_HEAD_DIM:].astype(BF16), qd_scr[slot, n]], axis=0),
                   s.astype(BF16)) for n, s in zip(idx, s_old)]
        yield
        vnb = [(ws[n][:, 0:DN_HEAD_DIM] - r[0:DN_C]).astype(BF16) for n, r in zip(idx, rs)]
        outs = [r[DN_C:2 * DN_C] + _dot(att_scr[slot, n], vn) for n, r, vn in zip(idx, rs, vnb)]
        yield
        for h, (n, vn) in enumerate(zip(idx, vnb)):
            s_ref[h] = s_old[h] * edl_scr[slot, n, 0:1, :] + _dot_tn(kt_scr[slot, n], vn)
        yield
        rows = slice(out_row0 + c * DN_C, out_row0 + (c + 1) * DN_C)
        for h, o in enumerate(outs):
            hs = slice(h * DN_HEAD_DIM, (h + 1) * DN_HEAD_DIM)
            ms = jnp.mean(o * o, axis=-1, keepdims=True)
            gt = gate_ref[rows, hs].astype(F32)
            o_ref[rows, hs] = (o * lax.rsqrt(ms + EPS) * og * (gt * _sigmoid(gt))).astype(BF16)


def _ffn_stages(x_ref, oa_ref, od_ref, wo_ref, fg_ref, w1_ref, w2_ref, o_ref):
    d = x_ref.shape[1]
    oa = oa_ref[...]
    od = od_ref[...]
    mixed = []
    for j in range(d // FFN_TN):
        cols = slice(j * FFN_TN, (j + 1) * FFN_TN)
        mixed.append(_dot(oa, wo_ref[0:ATT_WIDTH, cols]) + _dot(od, wo_ref[ATT_WIDTH:, cols]))
        yield
    h1 = x_ref[...] + jnp.concatenate(mixed, axis=1)
    ms = jnp.mean(h1 * h1, axis=-1, keepdims=True)
    n = (h1 * lax.rsqrt(ms + EPS) * fg_ref[...]).astype(BF16)

    def up(c, out):
        for j in range(FFN_TF // FFN_TN):
            col0 = c * FFN_TF + j * FFN_TN
            out.append(_dot(n, w1_ref[:, col0:col0 + FFN_TN]))
            yield

    n_chunks = w1_ref.shape[1] // FFN_TF
    acc = h1
    z_next = []
    yield from up(0, z_next)
    for c in range(n_chunks):
        z = z_next
        if c + 1 < n_chunks:
            z_next = []
            yield from up(c + 1, z_next)
        act = jnp.concatenate([jnp.square(jnp.maximum(piece, 0.0)).astype(BF16) for piece in z], axis=1)
        down = []
        for j in range(d // FFN_TN):
            down.append(_dot(act, w2_ref[c * FFN_TF:(c + 1) * FFN_TF, j * FFN_TN:(j + 1) * FFN_TN]))
            yield
        acc = acc + jnp.concatenate(down, axis=1)
    o_ref[...] = acc


def _gdn_ffn_kernel(x0_ref, x1_ref, x2_ref, ab0_ref, ab1_ref, ab2_ref, abt0_ref, abt1_ref, abt2_ref,
                    gate_ref, ltri_ref, utri_ref, alog_ref, dtb_ref, alogt_ref, dtbt_ref, dn_og_ref,
                    xres_ref, oa_ref, wo_ref, fg_ref, w1_ref, w2_ref, o_ref, s_ref, od_scr, *local_scr,
                    steps_per_seq):
    s = pl.program_id(0)
    cur = s % 2
    consts = (ltri_ref, utri_ref, alog_ref, dtb_ref, alogt_ref, dtbt_ref)

    def local(x_ref, ab_ref, abt_ref, slot):
        return _gdn_local(x_ref, ab_ref, abt_ref, *consts, slot, *local_scr)

    def state(slot, out_row0):
        return _gdn_state(slot, gate_ref, dn_og_ref, od_scr.at[cur], out_row0, s_ref, *local_scr)

    @pl.when(s == 0)
    def _():
        od_scr[1] = jnp.zeros(od_scr.shape[1:], od_scr.dtype)

    @pl.when(s % steps_per_seq == 0)
    def _():
        s_ref[...] = jnp.zeros_like(s_ref)
        _run(local(x0_ref, ab0_ref, abt0_ref, 0))

    deltanet = _chain(
        _weave(state(0, 0), local(x1_ref, ab1_ref, abt1_ref, 1), DN_STATE_PER_LOCAL),
        _weave(state(1, DN_TB), local(x2_ref, ab2_ref, abt2_ref, 0), DN_STATE_PER_LOCAL))
    ffn = _ffn_stages(xres_ref, oa_ref, od_scr.at[1 - cur], wo_ref, fg_ref, w1_ref, w2_ref, o_ref)
    _run(_weave(deltanet, ffn, 1))


def _deltanet_ffn(xf, main, ab, abt, o_att, alog, dtb, alogt, dtbt, dn_og, w_out, fg, w1, w2, seq):
    t, d = xf.shape
    dff = w1.shape[1]
    assert FFN_TM == 2 * DN_TB
    nb = seq // DN_TB
    spq = nb // 2
    steps = t // FFN_TM
    tile = lambda s: jnp.minimum(s, steps - 1)
    prev = lambda s: jnp.maximum(s - 1, 0)
    seq_blk0 = lambda s: tile(s) // spq * nb
    first = seq_blk0
    odd = lambda s: 2 * tile(s) + 1
    nxt = lambda s: jnp.minimum(2 * tile(s) + 2, seq_blk0(s) + nb - 1)
    const = lambda s: (0, 0)

    def x_spec(blk):
        return pl.BlockSpec((DN_TB, 3 * DN_WIDTH), lambda s: (blk(s), 1))

    def ab_spec(blk):
        return pl.BlockSpec((DN_TB, AB_PAD), lambda s: (blk(s), 0))

    def abt_spec(blk):
        return pl.BlockSpec((AB_PAD, DN_TB), lambda s: (0, blk(s)))

    r = np.arange(DN_TB)
    same = (r[:, None] // DN_C) == (r[None, :] // DN_C)
    ltri = jnp.asarray(same & (r[None, :] <= r[:, None]), BF16)
    utri = jnp.asarray(same & (r[:, None] <= r[None, :]), BF16)
    item = lambda *shape: (2, DN_ITEMS) + shape
    resident = dict(pipeline_mode=pl.Buffered(1))
    return pl.pallas_call(
        functools.partial(_gdn_ffn_kernel, steps_per_seq=spq),
        grid=(steps + 1,),
        in_specs=[
            x_spec(first), x_spec(odd), x_spec(nxt),
            ab_spec(first), ab_spec(odd), ab_spec(nxt),
            abt_spec(first), abt_spec(odd), abt_spec(nxt),
            pl.BlockSpec((2 * DN_TB, DN_WIDTH), lambda s: (tile(s), 6)),
            pl.BlockSpec((DN_TB, DN_TB), const),
            pl.BlockSpec((DN_TB, DN_TB), const),
            pl.BlockSpec((1, AB_PAD), const),
            pl.BlockSpec((1, AB_PAD), const),
            pl.BlockSpec((8, 1), const),
            pl.BlockSpec((8, 1), const),
            pl.BlockSpec((1, DN_HEAD_DIM), const),
            pl.BlockSpec((FFN_TM, d), lambda s: (prev(s), 0)),
            pl.BlockSpec((FFN_TM, ATT_WIDTH), lambda s: (prev(s), 0)),
            pl.BlockSpec((d, d), const, **resident),
            pl.BlockSpec((1, d), const),
            pl.BlockSpec((d, dff), const, **resident),
            pl.BlockSpec((dff, d), const, **resident),
        ],
        out_specs=pl.BlockSpec((FFN_TM, d), lambda s: (prev(s), 0)),
        out_shape=jax.ShapeDtypeStruct((t, d), F32),
        scratch_shapes=[
            pltpu.VMEM((DN_HEADS, DN_HEAD_DIM, DN_HEAD_DIM), F32),
            pltpu.VMEM((2, FFN_TM, DN_WIDTH), BF16),
            pltpu.VMEM(item(DN_C, DN_C), F32),
            pltpu.VMEM(item(DN_C, DN_C), BF16),
            pltpu.VMEM(item(DN_C, 2 * DN_HEAD_DIM), BF16),
            pltpu.VMEM(item(DN_C, DN_HEAD_DIM), BF16),
            pltpu.VMEM(item(DN_C, DN_HEAD_DIM), BF16),
            pltpu.VMEM(item(8, DN_HEAD_DIM), F32),
        ],
        compiler_params=pltpu.CompilerParams(
            dimension_semantics=("arbitrary",), vmem_limit_bytes=VMEM_LIMIT),
        name="deltanet_ffn",
    )(main, main, main, ab, ab, ab, abt, abt, abt, main, ltri, utri, alog, dtb, alogt, dtbt, dn_og,
      xf, o_att, w_out, fg, w1, w2)


def _layer(h, p, batch, seq):
    w_in = p["w_in"]
    w_main = w_in.astype(BF16)
    w_ab = jnp.pad(w_in[:, MAIN_COLS:], ((0, 0), (0, AB_PAD - 2 * DN_HEADS))).astype(BF16)
    head_of = np.arange(ATT_WIDTH) // ATT_HEAD_DIM
    seg = jnp.asarray((head_of[:, None] == head_of[None, :]) / ATT_HEAD_DIM, BF16)
    qg = (jnp.tile(p["att_q_gain"].astype(F32), ATT_HEADS) * (ATT_HEAD_DIM ** -0.5 * LOG2_E))[None]
    kg = jnp.tile(p["att_k_gain"].astype(F32), ATT_HEADS)[None]
    (main, kt, ab, abt), (w_out, w_ff1, w_ff2) = _inproj(
        h, p["mix_norm_gain"].astype(F32)[None], w_main, w_ab, qg, kg, seg, p["dn_conv_w"].astype(F32), seq,
        [p["w_out"].astype(F32), p["w_ff1"].astype(F32), p["w_ff2"].astype(F32)])

    o_att = _attention(main, kt, _bias_signal(p["rel_bias"]),
                       p["att_out_gain"].astype(F32)[None], batch, seq)

    pad_row = lambda v: jnp.pad(v.astype(F32), (0, AB_PAD - DN_HEADS))[None]
    pad_col = lambda v: jnp.pad(v.astype(F32), (0, 8 - DN_HEADS))[:, None]
    return _deltanet_ffn(h, main, ab, abt, o_att, pad_row(p["dn_a_log"]), pad_row(p["dn_dt_bias"]),
                         pad_col(p["dn_a_log"]), pad_col(p["dn_dt_bias"]),
                         p["dn_out_gain"].astype(F32)[None], w_out, p["ffn_norm_gain"].astype(F32)[None],
                         w_ff1, w_ff2, seq)


def kernel(x, mix_norm_gain, w_in, att_q_gain, att_k_gain, rel_bias, att_out_gain, dn_conv_w,
           dn_a_log, dn_dt_bias, dn_out_gain, w_out, ffn_norm_gain, w_ff1, w_ff2):
    batch, seq, d = x.shape
    assert seq % max(ATT_TQ, 2 * DN_TB) == 0 and (batch * seq) % max(IN_TM, FFN_TM) == 0
    params = dict(mix_norm_gain=mix_norm_gain, w_in=w_in, att_q_gain=att_q_gain,
                  att_k_gain=att_k_gain, rel_bias=rel_bias, att_out_gain=att_out_gain,
                  dn_conv_w=dn_conv_w, dn_a_log=dn_a_log, dn_dt_bias=dn_dt_bias,
                  dn_out_gain=dn_out_gain, w_out=w_out, ffn_norm_gain=ffn_norm_gain,
                  w_ff1=w_ff1, w_ff2=w_ff2)
    h = x.reshape(batch * seq, d)
    for layer in range(w_in.shape[0]):
        h = _layer(h, {name: v[layer] for name, v in params.items()}, batch, seq)
    return h.reshape(batch, seq, d)
```

```python
import functools

import numpy as np
import jax
import jax.numpy as jnp
from jax import lax
from jax.experimental import pallas as pl
from jax.experimental.pallas import tpu as pltpu

F32 = jnp.float32
BF16 = jnp.bfloat16

CHUNK = 64
N_PAST_CHUNKS = 8
ATT_HEADS = 8
ATT_HEAD_DIM = 64
ATT_WIDTH = ATT_HEADS * ATT_HEAD_DIM
MAX_REL = 256
DN_HEADS = 4
DN_HEAD_DIM = 128
DN_WIDTH = DN_HEADS * DN_HEAD_DIM
CONV_K = 4
EPS = 1e-6

MAIN_COLS = 3 * ATT_WIDTH + 4 * DN_WIDTH
AB_PAD = 128

LANES = 128
VMEM_LIMIT = 56 * 1024 * 1024

IN_TM = 1024
IN_TN = 256
ATT_TQ = 4 * CHUNK
ATT_WIN = ATT_TQ + N_PAST_CHUNKS * CHUNK
ATT_TILES = 2
DN_C = 128
DN_TB = 2 * DN_C
DN_TAIL = 8
DN_ITEMS = (DN_TB // DN_C) * DN_HEADS
DN_CONV_FIRST = 3 * ATT_WIDTH // IN_TN
DN_CONV_CHUNKS = 3 * DN_WIDTH // IN_TN
DN_STATE_PER_LOCAL = 7
FFN_TM = 512
FFN_TF = 1024
FFN_TN = 256
NEG_BIG = -1e30
LOG2_E = 1.4426950408889634


def _dot(a, b):
    return jnp.dot(a, b, preferred_element_type=F32)


def _dot_nt(a, b):
    return lax.dot_general(a, b, (((1,), (1,)), ((), ())), preferred_element_type=F32)


def _dot_tn(a, b):
    return lax.dot_general(a, b, (((0,), (0,)), ((), ())), preferred_element_type=F32)


def _split3(a):
    p1 = a.astype(BF16)
    r1 = a - p1.astype(F32)
    p2 = r1.astype(BF16)
    p3 = (r1 - p2.astype(F32)).astype(BF16)
    return p1, p2, p3


def _sigmoid(x):
    return 1.0 / (1.0 + jnp.exp2(x * -LOG2_E))


def _softplus(x):
    return jnp.maximum(x, 0.0) + jnp.log(1.0 + jnp.exp(-jnp.abs(x)))


def _inproj_kernel(x_ref, g_ref, wm_ref, wab_ref, qg_ref, kg_ref, seg_ref, cw_ref, *rest, tiles_per_seq):
    n_cast = (len(rest) - 5) // 2
    cast_in, (main_ref, kt_ref, ab_ref, abt_ref) = rest[:n_cast], rest[n_cast:n_cast + 4]
    cast_out, stage_ref = rest[n_cast + 4:2 * n_cast + 4], rest[2 * n_cast + 4]
    i = pl.program_id(0)

    @pl.when(i % tiles_per_seq == 0)
    def _():
        stage_ref[...] = jnp.zeros_like(stage_ref)

    x = x_ref[...]
    ms = jnp.mean(x * x, axis=-1, keepdims=True)
    u = (x * lax.rsqrt(ms + EPS) * g_ref[...]).astype(BF16)
    conv = list(range(DN_CONV_FIRST, DN_CONV_FIRST + DN_CONV_CHUNKS))
    per = ATT_WIDTH // IN_TN
    q, k, v = (list(range(g * per, (g + 1) * per)) for g in range(3))
    gate = list(range(DN_CONV_FIRST + DN_CONV_CHUNKS, MAIN_COLS // IN_TN))
    plain = k[:1] + v + gate + k[1:] + q
    order = [c for pair in zip(plain, conv) for c in pair] + plain[len(conv):]
    project = lambda c: _dot(u, wm_ref[:, c * IN_TN:(c + 1) * IN_TN])
    ab = _dot(u, wab_ref[...])
    y_next = project(order[0])
    for n, c in enumerate(order):
        cols = slice(c * IN_TN, (c + 1) * IN_TN)
        y = y_next
        if n + 1 < len(order):
            y_next = project(order[n + 1])
        if n == 0:
            ab_ref[...] = ab
            abt_ref[...] = ab.T
        if n == 1:
            for src, dst in zip(cast_in, cast_out):
                dst[...] = src[...].astype(BF16)
        if c < 2 * ATT_WIDTH // IN_TN:
            gain = qg_ref if c < ATT_WIDTH // IN_TN else kg_ref
            g0 = c * IN_TN % ATT_WIDTH
            m = _dot((y * y).astype(BF16), seg_ref[0:IN_TN, 0:IN_TN])
            y = y * lax.rsqrt(m + EPS) * gain[:, g0:g0 + IN_TN]
            if c >= ATT_WIDTH // IN_TN:
                kt_ref[g0:g0 + IN_TN, :] = y.T.astype(BF16)
        elif DN_CONV_FIRST <= c < DN_CONV_FIRST + DN_CONV_CHUNKS:
            assert CONV_K == 4
            j = c - DN_CONV_FIRST
            wcols = slice(j * IN_TN, (j + 1) * IN_TN)
            ext = jnp.concatenate([stage_ref[j], y], axis=0)
            tap = lambda t: 0.5 * cw_ref[t:t + 1, wcols]
            prev = pltpu.roll(ext, 1, axis=0)
            older = tap(1) * ext + tap(0) * prev
            h = tap(3) * ext + tap(2) * prev + pltpu.roll(older, 2, axis=0)
            h = h[DN_TAIL:]
            stage_ref[j] = y[IN_TM - DN_TAIL:IN_TM, :]
            y = h + h * jnp.tanh(h)
        main_ref[:, cols] = y.astype(BF16)


def _inproj(xf, gain, w_main, w_ab, qg, kg, seg, conv_w, seq, to_cast):
    t, d = xf.shape
    steps = t // IN_TM
    const = lambda i: (0, 0)
    slab = lambda w: pl.BlockSpec((w.shape[0] // steps, w.shape[1]), lambda i: (i, 0))
    outs = pl.pallas_call(
        functools.partial(_inproj_kernel, tiles_per_seq=seq // IN_TM),
        grid=(steps,),
        in_specs=[
            pl.BlockSpec((IN_TM, d), lambda i: (i, 0)),
            pl.BlockSpec((1, d), const),
            pl.BlockSpec((d, MAIN_COLS), const),
            pl.BlockSpec((d, AB_PAD), const),
            pl.BlockSpec((1, ATT_WIDTH), const),
            pl.BlockSpec((1, ATT_WIDTH), const),
            pl.BlockSpec((ATT_WIDTH, ATT_WIDTH), const),
            pl.BlockSpec((CONV_K, 3 * DN_WIDTH), const),
        ] + [slab(w) for w in to_cast],
        out_specs=[
            pl.BlockSpec((IN_TM, MAIN_COLS), lambda i: (i, 0)),
            pl.BlockSpec((ATT_WIDTH, IN_TM), lambda i: (0, i)),
            pl.BlockSpec((IN_TM, AB_PAD), lambda i: (i, 0)),
            pl.BlockSpec((AB_PAD, IN_TM), lambda i: (0, i)),
        ] + [slab(w) for w in to_cast],
        out_shape=[
            jax.ShapeDtypeStruct((t, MAIN_COLS), BF16),
            jax.ShapeDtypeStruct((ATT_WIDTH, t), BF16),
            jax.ShapeDtypeStruct((t, AB_PAD), F32),
            jax.ShapeDtypeStruct((AB_PAD, t), F32),
        ] + [jax.ShapeDtypeStruct(w.shape, BF16) for w in to_cast],
        scratch_shapes=[pltpu.VMEM((DN_CONV_CHUNKS, DN_TAIL, IN_TN), F32)],
        compiler_params=pltpu.CompilerParams(
            dimension_semantics=("arbitrary",), vmem_limit_bytes=VMEM_LIMIT),
        name="inproj",
    )(xf, gain, w_main, w_ab, qg, kg, seg, conv_w, *to_cast)
    return outs[:4], outs[4:]


def _attn_tile(q_ref, row0, k_refs, v_refs, bias_ref, start_mask, og_ref, o_ref):
    lane = lax.broadcasted_iota(jnp.int32, (1, LANES), 1)
    per_tile = LANES // ATT_HEAD_DIM
    rows = slice(row0, row0 + ATT_TQ)

    def mine(h):
        e = h % per_tile
        return (lane >= e * ATT_HEAD_DIM) & (lane < (e + 1) * ATT_HEAD_DIM)

    def tile_lanes(h):
        return slice(h // per_tile * LANES, (h // per_tile + 1) * LANES)

    def scores(h):
        qp = q_ref[rows, tile_lanes(h)]
        q_e = jnp.where(mine(h), qp, jnp.zeros_like(qp))
        bias = bias_ref[h] if start_mask is None else bias_ref[h] + start_mask
        kw = jnp.concatenate([r[tile_lanes(h), :] for r in k_refs], axis=1)
        return _dot(q_e, kw) + bias

    def attend(h, s):
        mx = jnp.max(s, axis=-1, keepdims=True)
        pe = jnp.exp2(s - mx).astype(BF16)
        vw = jnp.concatenate([r[:, tile_lanes(h)] for r in v_refs], axis=0)
        o_e = _dot(pe, jnp.where(mine(h), vw, jnp.ones_like(vw)))
        return o_e / pltpu.roll(o_e, ATT_HEAD_DIM, axis=1)

    normed = []
    s_next = scores(0)
    for h in range(ATT_HEADS):
        s_cur = s_next
        if h + 1 < ATT_HEADS:
            s_next = scores(h + 1)
        normed.append(attend(h, s_cur))
        yield
    outs = [jnp.where(lane < ATT_HEAD_DIM, normed[t * per_tile], normed[t * per_tile + 1])
            for t in range(ATT_WIDTH // LANES)]
    o = jnp.concatenate(outs, axis=1)
    ms = jnp.mean(o * o, axis=-1, keepdims=True)
    o_ref[rows, :] = (o * lax.rsqrt(ms + EPS) * og_ref[...]).astype(BF16)


def _attn_kernel(q_ref, k0_ref, k1_ref, k2_ref, k3_ref, v0_ref, v1_ref, v2_ref, v3_ref, sig_ref, og_ref, o_ref,
                 bias_ref):
    i = pl.program_id(1)

    @pl.when((pl.program_id(0) == 0) & (i == 0))
    def _():
        r = lax.broadcasted_iota(jnp.int32, (ATT_TQ, ATT_WIN), 0)
        s = lax.broadcasted_iota(jnp.int32, (ATT_TQ, ATT_WIN), 1)
        back = (s >> 6) - (r >> 6)
        in_band = (back >= 0) & (back <= N_PAST_CHUNKS)
        for h in range(ATT_HEADS):
            rows = jnp.broadcast_to(sig_ref[h:h + 1, :], (ATT_TQ, sig_ref.shape[1]))
            toep = pltpu.roll(rows, 0, 1, stride=1, stride_axis=0)
            bias_ref[h] = jnp.where(in_band, toep[:, 0:ATT_WIN], NEG_BIG)

    def tiles(masks):
        _run(_weave(
            _attn_tile(q_ref, 0, (k0_ref, k1_ref, k2_ref), (v0_ref, v1_ref, v2_ref), bias_ref, masks[0],
                       og_ref, o_ref),
            _attn_tile(q_ref, ATT_TQ, (k1_ref, k2_ref, k3_ref), (v1_ref, v2_ref, v3_ref), bias_ref, masks[1],
                       og_ref, o_ref), 1))

    steps_in_past = N_PAST_CHUNKS * CHUNK // (ATT_TILES * ATT_TQ)

    @pl.when(i < steps_in_past)
    def _():
        jchunk = lax.broadcasted_iota(jnp.int32, (1, ATT_WIN), 1) >> 6
        tiles([jnp.where(jchunk + (ATT_TQ // CHUNK) * (ATT_TILES * i + n) - N_PAST_CHUNKS >= 0, 0.0, NEG_BIG)
               for n in range(ATT_TILES)])

    @pl.when(i >= steps_in_past)
    def _():
        tiles([None] * ATT_TILES)


def _attention(main, kt, signal, og, batch, seq):
    nb = seq // ATT_TQ
    steps = nb // ATT_TILES
    blk = (ATT_TQ, ATT_WIDTH)

    window_blk = lambda b, i, j: b * nb + jnp.maximum(ATT_TILES * i - 2 + j, 0)

    def k_spec(j):
        return pl.BlockSpec((ATT_WIDTH, ATT_TQ), lambda b, i: (0, window_blk(b, i, j)))

    def v_spec(j):
        return pl.BlockSpec(blk, lambda b, i: (window_blk(b, i, j), 2))

    step_rows = pl.BlockSpec((ATT_TILES * ATT_TQ, ATT_WIDTH), lambda b, i: (b * steps + i, 0))
    return pl.pallas_call(
        _attn_kernel,
        grid=(batch, steps),
        in_specs=[
            step_rows,
            k_spec(0), k_spec(1), k_spec(2), k_spec(3),
            v_spec(0), v_spec(1), v_spec(2), v_spec(3),
            pl.BlockSpec(signal.shape, lambda b, i: (0, 0)),
            pl.BlockSpec((1, ATT_WIDTH), lambda b, i: (0, 0)),
        ],
        out_specs=step_rows,
        out_shape=jax.ShapeDtypeStruct((batch * seq, ATT_WIDTH), BF16),
        scratch_shapes=[pltpu.VMEM((ATT_HEADS, ATT_TQ, ATT_WIN), F32)],
        compiler_params=pltpu.CompilerParams(
            dimension_semantics=("arbitrary", "arbitrary"), vmem_limit_bytes=VMEM_LIMIT),
        name="band_attention",
    )(main, kt, kt, kt, kt, main, main, main, main, signal, og)


def _bias_signal(rel_bias):
    nh = rel_bias.shape[0]
    lo = N_PAST_CHUNKS * CHUNK - (ATT_WIN - 1) + MAX_REL
    n_dist = ATT_TQ + ATT_WIN - 1
    n_tail = n_dist - (2 * MAX_REL + 1 - lo)
    rb = rel_bias.astype(F32) * LOG2_E
    by_dist = jnp.concatenate([rb[:, lo:], jnp.broadcast_to(rb[:, -1:], (nh, n_tail))], axis=1)
    period = ATT_TQ + ATT_WIN
    return jnp.roll(jnp.pad(by_dist[:, ::-1], ((0, 0), (0, period - n_dist))), -(ATT_TQ - 1), axis=1)


def _weave(main, side, every):
    n = 0
    done = object()
    while True:
        if n % every == 0:
            next(side, done)
        if next(main, done) is done:
            break
        n += 1
        yield
    _run(side)


def _chain(*stages):
    for g in stages:
        yield from g


def _run(stage):
    for _ in stage:
        pass


def _inv_unit_lower(mats, ri, ci, out):
    eye = jnp.where(ri == ci, 1.0, 0.0)
    diag8 = (ri >> 3) == (ci >> 3)
    n1 = [jnp.where(diag8, -a, 0.0) for a in mats]
    n1b = [n.astype(BF16) for n in n1]
    n2b = [_dot(n, n).astype(BF16) for n in n1b]
    yield
    n4b = [_dot(n, n).astype(BF16) for n in n2b]
    ts = [eye + n for n in n1]
    yield
    ts = [t + _dot(t.astype(BF16), n) for t, n in zip(ts, n2b)]
    yield
    ts = [t + _dot(t.astype(BF16), n) for t, n in zip(ts, n4b)]
    yield
    half = DN_C // 2
    r_h = lax.broadcasted_iota(jnp.int32, (half, DN_C), 0)
    c_o = lax.broadcasted_iota(jnp.int32, (half, DN_C), 1)
    for sh in range(3, 7):
        blk = 1 << sh
        odd = [slice(s0, s0 + blk) for s0 in range(blk, DN_C, 2 * blk)]
        even = [slice(s0, s0 + blk) for s0 in range(0, DN_C, 2 * blk)]
        take = lambda m, parts: jnp.concatenate([m[p] for p in parts], axis=0)
        r_o = r_h + (((r_h >> sh) + 1) << sh)
        pair =((r_o >> (sh + 1)) == (c_o >> (sh + 1))) & ((r_o >> sh) != (c_o >> sh))
        tb = [t.astype(BF16) for t in ts]
        lt = [_dot(jnp.where(pair, take(a, odd), 0.0).astype(BF16), t) for a, t in zip(mats, tb)]
        yield
        zero = jnp.zeros((blk, DN_C), F32)
        lt = [jnp.concatenate([p for m in range(half // blk) for p in (zero, x[m * blk:(m + 1) * blk])],
                              axis=0).astype(BF16) for x in lt]
        t_odd = [take(t, odd) for t in ts]
        new = [t - _dot(t.astype(BF16), x) for t, x in zip(t_odd, lt)]
        ts = [jnp.concatenate([p for m in range(half // blk)
                               for p in (t[even[m]], n[m * blk:(m + 1) * blk])], axis=0)
              for t, n in zip(ts, new)]
        yield
    out.extend(ts)


def _gdn_local(x_ref, ab_ref, abt_ref, ltri_ref, utri_ref, alog_ref, dtb_ref, alogt_ref, dtbt_ref, slot,
               a_scr, att_scr, rhs_scr, qd_scr, kt_scr, edl_scr):
    ab = ab_ref[...]
    g_col = -jnp.exp(alog_ref[...]) * _softplus(ab + dtb_ref[...])
    beta_col = _sigmoid(ab)
    g_row = -jnp.exp(alogt_ref[...]) * _softplus(abt_ref[0:8, :] + dtbt_ref[...])
    gc1, gc2, gc3 = _split3(g_col)
    gr1, gr2, gr3 = _split3(g_row)

    y = x_ref[...].astype(F32)

    ones = jnp.ones((DN_HEAD_DIM, DN_HEAD_DIM), BF16)
    sq = [(t * t).astype(BF16) for t in (y[:, g * DN_HEAD_DIM:(g + 1) * DN_HEAD_DIM]
                                         for g in range(2 * DN_HEADS))]
    ssq = [_dot(t, ones) for t in sq]
    yield
    ltri = ltri_ref[...]
    utri = utri_ref[...]
    d_col = _dot(ltri, gc1) + (_dot(ltri, gc2) + _dot(ltri, gc3))
    d_row = _dot(gr1, utri) + (_dot(gr2, utri) + _dot(gr3, utri))
    qk = []
    for g, tot in enumerate(ssq):
        inv = lax.rsqrt(tot + EPS)
        qk.append(y[:, g * DN_HEAD_DIM:(g + 1) * DN_HEAD_DIM]
                  * (inv * DN_HEAD_DIM ** -0.5 if g < DN_HEADS else inv))

    ri = lax.broadcasted_iota(jnp.int32, (DN_C, DN_C), 0)
    ci = lax.broadcasted_iota(jnp.int32, (DN_C, DN_C), 1)
    lmasks, kbqs, kbfs = [], [], []
    for n in range(DN_ITEMS):
        c, h = divmod(n, DN_HEADS)
        rows = slice(c * DN_C, (c + 1) * DN_C)
        q = qk[h][rows]
        k = qk[DN_HEADS + h][rows]
        v = y[rows, 2 * DN_WIDTH + h * DN_HEAD_DIM:2 * DN_WIDTH + (h + 1) * DN_HEAD_DIM]
        dc = d_col[rows, h:h + 1]
        dr = d_row[h:h + 1, rows]
        dl = dr[:, DN_C - 1:DN_C]
        bc = beta_col[rows, DN_HEADS + h:DN_HEADS + h + 1]
        assert DN_HEAD_DIM == DN_C
        bc = jnp.broadcast_to(bc, (DN_C, DN_HEAD_DIM))
        dc = jnp.broadcast_to(dc, (DN_C, DN_HEAD_DIM))
        kb = k * bc
        edc = jnp.exp(dc)
        rhs_scr[slot, n] = jnp.concatenate([v * bc, kb * edc], axis=1).astype(BF16)
        qd_scr[slot, n] = (q * edc).astype(BF16)
        kt_scr[slot, n] = (k * jnp.exp(dl - dc)).astype(BF16)
        edl_scr[slot, n] = jnp.broadcast_to(jnp.exp(dl), (8, DN_HEAD_DIM))
        lmasks.append(jnp.exp(jnp.where(ci <= ri, dc - dr, -jnp.inf)))
        kbqs.append(jnp.concatenate([kb, q], axis=0).astype(BF16))
        kbfs.append(k.astype(BF16))
    yield
    ps =[_dot_nt(kbq, kbf) for kbq, kbf in zip(kbqs, kbfs)]
    for n, (p, lmask) in enumerate(zip(ps, lmasks)):
        a_scr[slot, n] = jnp.where(ci < ri, p[0:DN_C] * lmask, 0.0)
        att_scr[slot, n] = (p[DN_C:2 * DN_C] * lmask).astype(BF16)


def _gdn_state(slot, gate_ref, og_ref, o_ref, out_row0, s_ref, a_scr, att_scr, rhs_scr, qd_scr, kt_scr,
               edl_scr):
    ri = lax.broadcasted_iota(jnp.int32, (DN_C, DN_C), 0)
    ci = lax.broadcasted_iota(jnp.int32, (DN_C, DN_C), 1)
    og = og_ref[...]
    tinvs = []
    yield from _inv_unit_lower([a_scr[slot, n] for n in range(DN_ITEMS)], ri, ci, tinvs)
    ws = [_dot(t.astype(BF16), rhs_scr[slot, n]) for n, t in enumerate(tinvs)]
    yield
    for c in range(DN_TB // DN_C):
        idx = [c * DN_HEADS + h for h in range(DN_HEADS)]
        s_old = [s_ref[h] for h in range(DN_HEADS)]
        rs = [_dot(jnp.concatenate([ws[n][:, DN_HEAD_DIM:].astype(BF16), qd_scr[slot, n]], axis=0),
                   s.astype(BF16)) for n, s in zip(idx, s_old)]
        yield
        vnb = [(ws[n][:, 0:DN_HEAD_DIM] - r[0:DN_C]).astype(BF16) for n, r in zip(idx, rs)]
        outs = [r[DN_C:2 * DN_C] + _dot(att_scr[slot, n], vn) for n, r, vn in zip(idx, rs, vnb)]
        yield
        for h, (n, vn) in enumerate(zip(idx, vnb)):
            s_ref[h] = s_old[h] * edl_scr[slot, n, 0:1, :] + _dot_tn(kt_scr[slot, n], vn)
        yield
        rows = slice(out_row0 + c * DN_C, out_row0 + (c + 1) * DN_C)
        for h, o in enumerate(outs):
            hs = slice(h * DN_HEAD_DIM, (h + 1) * DN_HEAD_DIM)
            ms = jnp.mean(o * o, axis=-1, keepdims=True)
            gt = gate_ref[rows, hs].astype(F32)
            o_ref[rows, hs] = (o * lax.rsqrt(ms + EPS) * og * (gt * _sigmoid(gt))).astype(BF16)


def _ffn_stages(x_ref, oa_ref, od_ref, wo_ref, fg_ref, w1_ref, w2_ref, o_ref):
    d = x_ref.shape[1]
    oa = oa_ref[...]
    od = od_ref[...]
    mixed = []
    for j in range(d // FFN_TN):
        cols = slice(j * FFN_TN, (j + 1) * FFN_TN)
        mixed.append(_dot(oa, wo_ref[0:ATT_WIDTH, cols]) + _dot(od, wo_ref[ATT_WIDTH:, cols]))
        yield
    h1 = x_ref[...] + jnp.concatenate(mixed, axis=1)
    ms = jnp.mean(h1 * h1, axis=-1, keepdims=True)
    n = (h1 * lax.rsqrt(ms + EPS) * fg_ref[...]).astype(BF16)

    def up(c, out):
        for j in range(FFN_TF // FFN_TN):
            col0 = c * FFN_TF + j * FFN_TN
            out.append(_dot(n, w1_ref[:, col0:col0 + FFN_TN]))
            yield

    n_chunks = w1_ref.shape[1] // FFN_TF
    acc = h1
    z_next = []
    yield from up(0, z_next)
    for c in range(n_chunks):
        z = z_next
        if c + 1 < n_chunks:
            z_next = []
            yield from up(c + 1, z_next)
        act = jnp.concatenate([jnp.square(jnp.maximum(piece, 0.0)).astype(BF16) for piece in z], axis=1)
        down = []
        for j in range(d // FFN_TN):
            down.append(_dot(act, w2_ref[c * FFN_TF:(c + 1) * FFN_TF, j * FFN_TN:(j + 1) * FFN_TN]))
            yield
        acc = acc + jnp.concatenate(down, axis=1)
    o_ref[...] = acc


def _gdn_ffn_kernel(x0_ref, x1_ref, x2_ref, ab0_ref, ab1_ref, ab2_ref, abt0_ref, abt1_ref, abt2_ref,
                    gate_ref, ltri_ref, utri_ref, alog_ref, dtb_ref, alogt_ref, dtbt_ref, dn_og_ref,
                    xres_ref, oa_ref, wo_ref, fg_ref, w1_ref, w2_ref, o_ref, s_ref, od_scr, *local_scr,
                    steps_per_seq):
    s = pl.program_id(0)
    cur = s % 2
    consts = (ltri_ref, utri_ref, alog_ref, dtb_ref, alogt_ref, dtbt_ref)

    def local(x_ref, ab_ref, abt_ref, slot):
        return _gdn_local(x_ref, ab_ref, abt_ref, *consts, slot, *local_scr)

    def state(slot, out_row0):
        return _gdn_state(slot, gate_ref, dn_og_ref, od_scr.at[cur], out_row0, s_ref, *local_scr)

    @pl.when(s == 0)
    def _():
        od_scr[1] = jnp.zeros(od_scr.shape[1:], od_scr.dtype)

    @pl.when(s % steps_per_seq == 0)
    def _():
        s_ref[...] = jnp.zeros_like(s_ref)
        _run(local(x0_ref, ab0_ref, abt0_ref, 0))

    deltanet = _chain(
        _weave(state(0, 0), local(x1_ref, ab1_ref, abt1_ref, 1), DN_STATE_PER_LOCAL),
        _weave(state(1, DN_TB), local(x2_ref, ab2_ref, abt2_ref, 0), DN_STATE_PER_LOCAL))
    ffn = _ffn_stages(xres_ref, oa_ref, od_scr.at[1 - cur], wo_ref, fg_ref, w1_ref, w2_ref, o_ref)
    _run(_weave(deltanet, ffn, 1))


def _deltanet_ffn(xf, main, ab, abt, o_att, alog, dtb, alogt, dtbt, dn_og, w_out, fg, w1, w2, seq):
    t, d = xf.shape
    dff = w1.shape[1]
    assert FFN_TM == 2 * DN_TB
    nb = seq // DN_TB
    spq = nb // 2
    steps = t // FFN_TM
    tile = lambda s: jnp.minimum(s, steps - 1)
    prev = lambda s: jnp.maximum(s - 1, 0)
    seq_blk0 = lambda s: tile(s) // spq * nb
    first = seq_blk0
    odd = lambda s: 2 * tile(s) + 1
    nxt = lambda s: jnp.minimum(2 * tile(s) + 2, seq_blk0(s) + nb - 1)
    const = lambda s: (0, 0)

    def x_spec(blk):
        return pl.BlockSpec((DN_TB, 3 * DN_WIDTH), lambda s: (blk(s), 1))

    def ab_spec(blk):
        return pl.BlockSpec((DN_TB, AB_PAD), lambda s: (blk(s), 0))

    def abt_spec(blk):
        return pl.BlockSpec((AB_PAD, DN_TB), lambda s: (0, blk(s)))

    r = np.arange(DN_TB)
    same = (r[:, None] // DN_C) == (r[None, :] // DN_C)
    ltri = jnp.asarray(same & (r[None, :] <= r[:, None]), BF16)
    utri = jnp.asarray(same & (r[:, None] <= r[None, :]), BF16)
    item = lambda *shape: (2, DN_ITEMS) + shape
    resident = dict(pipeline_mode=pl.Buffered(1))
    return pl.pallas_call(
        functools.partial(_gdn_ffn_kernel, steps_per_seq=spq),
        grid=(steps + 1,),
        in_specs=[
            x_spec(first), x_spec(odd), x_spec(nxt),
            ab_spec(first), ab_spec(odd), ab_spec(nxt),
            abt_spec(first), abt_spec(odd), abt_spec(nxt),
            pl.BlockSpec((2 * DN_TB, DN_WIDTH), lambda s: (tile(s), 6)),
            pl.BlockSpec((DN_TB, DN_TB), const),
            pl.BlockSpec((DN_TB, DN_TB), const),
            pl.BlockSpec((1, AB_PAD), const),
            pl.BlockSpec((1, AB_PAD), const),
            pl.BlockSpec((8, 1), const),
            pl.BlockSpec((8, 1), const),
            pl.BlockSpec((1, DN_HEAD_DIM), const),
            pl.BlockSpec((FFN_TM, d), lambda s: (prev(s), 0)),
            pl.BlockSpec((FFN_TM, ATT_WIDTH), lambda s: (prev(s), 0)),
            pl.BlockSpec((d, d), const, **resident),
            pl.BlockSpec((1, d), const),
            pl.BlockSpec((d, dff), const, **resident),
            pl.BlockSpec((dff, d), const, **resident),
        ],
        out_specs=pl.BlockSpec((FFN_TM, d), lambda s: (prev(s), 0)),
        out_shape=jax.ShapeDtypeStruct((t, d), F32),
        scratch_shapes=[
            pltpu.VMEM((DN_HEADS, DN_HEAD_DIM, DN_HEAD_DIM), F32),
            pltpu.VMEM((2, FFN_TM, DN_WIDTH), BF16),
            pltpu.VMEM(item(DN_C, DN_C), F32),
            pltpu.VMEM(item(DN_C, DN_C), BF16),
            pltpu.VMEM(item(DN_C, 2 * DN_HEAD_DIM), BF16),
            pltpu.VMEM(item(DN_C, DN_HEAD_DIM), BF16),
            pltpu.VMEM(item(DN_C, DN_HEAD_DIM), BF16),
            pltpu.VMEM(item(8, DN_HEAD_DIM), F32),
        ],
        compiler_params=pltpu.CompilerParams(
            dimension_semantics=("arbitrary",), vmem_limit_bytes=VMEM_LIMIT),
        name="deltanet_ffn",
    )(main, main, main, ab, ab, ab, abt, abt, abt, main, ltri, utri, alog, dtb, alogt, dtbt, dn_og,
      xf, o_att, w_out, fg, w1, w2)


def _layer(h, p, batch, seq):
    w_in = p["w_in"]
    w_main = w_in.astype(BF16)
    w_ab = jnp.pad(w_in[:, MAIN_COLS:], ((0, 0), (0, AB_PAD - 2 * DN_HEADS))).astype(BF16)
    head_of = np.arange(ATT_WIDTH) // ATT_HEAD_DIM
    seg = jnp.asarray((head_of[:, None] == head_of[None, :]) / ATT_HEAD_DIM, BF16)
    qg = (jnp.tile(p["att_q_gain"].astype(F32), ATT_HEADS) * (ATT_HEAD_DIM ** -0.5 * LOG2_E))[None]
    kg = jnp.tile(p["att_k_gain"].astype(F32), ATT_HEADS)[None]
    (main, kt, ab, abt), (w_out, w_ff1, w_ff2) = _inproj(
        h, p["mix_norm_gain"].astype(F32)[None], w_main, w_ab, qg, kg, seg, p["dn_conv_w"].astype(F32), seq,
        [p["w_out"].astype(F32), p["w_ff1"].astype(F32), p["w_ff2"].astype(F32)])

    o_att = _attention(main, kt, _bias_signal(p["rel_bias"]),
                       p["att_out_gain"].astype(F32)[None], batch, seq)

    pad_row = lambda v: jnp.pad(v.astype(F32), (0, AB_PAD - DN_HEADS))[None]
    pad_col = lambda v: jnp.pad(v.astype(F32), (0, 8 - DN_HEADS))[:, None]
    return _deltanet_ffn(h, main, ab, abt, o_att, pad_row(p["dn_a_log"]), pad_row(p["dn_dt_bias"]),
                         pad_col(p["dn_a_log"]), pad_col(p["dn_dt_bias"]),
                         p["dn_out_gain"].astype(F32)[None], w_out, p["ffn_norm_gain"].astype(F32)[None],
                         w_ff1, w_ff2, seq)


def kernel(x, mix_norm_gain, w_in, att_q_gain, att_k_gain, rel_bias, att_out_gain, dn_conv_w,
           dn_a_log, dn_dt_bias, dn_out_gain, w_out, ffn_norm_gain, w_ff1, w_ff2):
    batch, seq, d = x.shape
    assert seq % max(ATT_TQ, 2 * DN_TB) == 0 and (batch * seq) % max(IN_TM, FFN_TM) == 0
    params = dict(mix_norm_gain=mix_norm_gain, w_in=w_in, att_q_gain=att_q_gain,
                  att_k_gain=att_k_gain, rel_bias=rel_bias, att_out_gain=att_out_gain,
                  dn_conv_w=dn_conv_w, dn_a_log=dn_a_log, dn_dt_bias=dn_dt_bias,
                  dn_out_gain=dn_out_gain, w_out=w_out, ffn_norm_gain=ffn_norm_gain,
                  w_ff1=w_ff1, w_ff2=w_ff2)
    h = x.reshape(batch * seq, d)
    for layer in range(w_in.shape[0]):
        h = _layer(h, {name: v[layer] for name, v in params.items()}, batch, seq)
    return h.reshape(batch, seq, d)
```

```python
import functools

import numpy as np
import jax
import jax.numpy as jnp
from jax import lax
from jax.experimental import pallas as pl
from jax.experimental.pallas import tpu as pltpu

F32 = jnp.float32
BF16 = jnp.bfloat16

CHUNK = 64
N_PAST_CHUNKS = 8
ATT_HEADS = 8
ATT_HEAD_DIM = 64
ATT_WIDTH = ATT_HEADS * ATT_HEAD_DIM
MAX_REL = 256
DN_HEADS = 4
DN_HEAD_DIM = 128
DN_WIDTH = DN_HEADS * DN_HEAD_DIM
CONV_K = 4
EPS = 1e-6

MAIN_COLS = 3 * ATT_WIDTH + 4 * DN_WIDTH
AB_PAD = 128

LANES = 128
VMEM_LIMIT = 56 * 1024 * 1024

IN_TM = 1024
IN_TN = 256
ATT_TQ = 4 * CHUNK
ATT_WIN = ATT_TQ + N_PAST_CHUNKS * CHUNK
ATT_TILES = 2
DN_C = 128
DN_TB = 2 * DN_C
DN_TAIL = 8
DN_ITEMS = (DN_TB // DN_C) * DN_HEADS
DN_CONV_FIRST = 3 * ATT_WIDTH // IN_TN
DN_CONV_CHUNKS = 3 * DN_WIDTH // IN_TN
DN_STATE_PER_LOCAL = 7
FFN_TM = 512
FFN_TF = 1024
FFN_TN = 256
NEG_BIG = -1e30
LOG2_E = 1.4426950408889634


def _dot(a, b):
    return jnp.dot(a, b, preferred_element_type=F32)


def _dot_nt(a, b):
    return lax.dot_general(a, b, (((1,), (1,)), ((), ())), preferred_element_type=F32)


def _dot_tn(a, b):
    return lax.dot_general(a, b, (((0,), (0,)), ((), ())), preferred_element_type=F32)


def _split3(a):
    p1 = a.astype(BF16)
    r1 = a - p1.astype(F32)
    p2 = r1.astype(BF16)
    p3 = (r1 - p2.astype(F32)).astype(BF16)
    return p1, p2, p3


def _sigmoid(x):
    return 1.0 / (1.0 + jnp.exp2(x * -LOG2_E))


def _softplus(x):
    return jnp.maximum(x, 0.0) + jnp.log(1.0 + jnp.exp(-jnp.abs(x)))


def _inproj_kernel(x_ref, g_ref, wm_ref, wab_ref, qg_ref, kg_ref, seg_ref, cw_ref, *rest, tiles_per_seq):
    n_cast = (len(rest) - 5) // 2
    cast_in, (main_ref, kt_ref, ab_ref, abt_ref) = rest[:n_cast], rest[n_cast:n_cast + 4]
    cast_out, stage_ref = rest[n_cast + 4:2 * n_cast + 4], rest[2 * n_cast + 4]
    i = pl.program_id(0)

    @pl.when(i % tiles_per_seq == 0)
    def _():
        stage_ref[...] = jnp.zeros_like(stage_ref)

    x = x_ref[...]
    ms = jnp.mean(x * x, axis=-1, keepdims=True)
    u = (x * lax.rsqrt(ms + EPS) * g_ref[...]).astype(BF16)
    conv = list(range(DN_CONV_FIRST, DN_CONV_FIRST + DN_CONV_CHUNKS))
    per = ATT_WIDTH // IN_TN
    q, k, v = (list(range(g * per, (g + 1) * per)) for g in range(3))
    gate = list(range(DN_CONV_FIRST + DN_CONV_CHUNKS, MAIN_COLS // IN_TN))
    plain = k[:1] + v + gate + k[1:] + q
    order = [c for pair in zip(plain, conv) for c in pair] + plain[len(conv):]
    project = lambda c: _dot(u, wm_ref[:, c * IN_TN:(c + 1) * IN_TN])
    ab = _dot(u, wab_ref[...])
    y_next = project(order[0])
    for n, c in enumerate(order):
        cols = slice(c * IN_TN, (c + 1) * IN_TN)
        y = y_next
        if n + 1 < len(order):
            y_next = project(order[n + 1])
        if n == 0:
            ab_ref[...] = ab
            abt_ref[...] = ab.T
        if n == 1:
            for src, dst in zip(cast_in, cast_out):
                dst[...] = src[...].astype(BF16)
        if c < 2 * ATT_WIDTH // IN_TN:
            gain = qg_ref if c < ATT_WIDTH // IN_TN else kg_ref
            g0 = c * IN_TN % ATT_WIDTH
            m = _dot((y * y).astype(BF16), seg_ref[0:IN_TN, 0:IN_TN])
            y = y * lax.rsqrt(m + EPS) * gain[:, g0:g0 + IN_TN]
            if c >= ATT_WIDTH // IN_TN:
                kt_ref[g0:g0 + IN_TN, :] = y.T.astype(BF16)
        elif DN_CONV_FIRST <= c < DN_CONV_FIRST + DN_CONV_CHUNKS:
            assert CONV_K == 4
            j = c - DN_CONV_FIRST
            wcols = slice(j * IN_TN, (j + 1) * IN_TN)
            ext = jnp.concatenate([stage_ref[j], y], axis=0)
            tap = lambda t: 0.5 * cw_ref[t:t + 1, wcols]
            prev = pltpu.roll(ext, 1, axis=0)
            older = tap(1) * ext + tap(0) * prev
            h = tap(3) * ext + tap(2) * prev + pltpu.roll(older, 2, axis=0)
            h = h[DN_TAIL:]
            stage_ref[j] = y[IN_TM - DN_TAIL:IN_TM, :]
            y = h + h * jnp.tanh(h)
        main_ref[:, cols] = y.astype(BF16)


def _inproj(xf, gain, w_main, w_ab, qg, kg, seg, conv_w, seq, to_cast):
    t, d = xf.shape
    steps = t // IN_TM
    const = lambda i: (0, 0)
    slab = lambda w: pl.BlockSpec((w.shape[0] // steps, w.shape[1]), lambda i: (i, 0))
    outs = pl.pallas_call(
        functools.partial(_inproj_kernel, tiles_per_seq=seq // IN_TM),
        grid=(steps,),
        in_specs=[
            pl.BlockSpec((IN_TM, d), lambda i: (i, 0)),
            pl.BlockSpec((1, d), const),
            pl.BlockSpec((d, MAIN_COLS), const),
            pl.BlockSpec((d, AB_PAD), const),
            pl.BlockSpec((1, ATT_WIDTH), const),
            pl.BlockSpec((1, ATT_WIDTH), const),
            pl.BlockSpec((ATT_WIDTH, ATT_WIDTH), const),
            pl.BlockSpec((CONV_K, 3 * DN_WIDTH), const),
        ] + [slab(w) for w in to_cast],
        out_specs=[
            pl.BlockSpec((IN_TM, MAIN_COLS), lambda i: (i, 0)),
            pl.BlockSpec((ATT_WIDTH, IN_TM), lambda i: (0, i)),
            pl.BlockSpec((IN_TM, AB_PAD), lambda i: (i, 0)),
            pl.BlockSpec((AB_PAD, IN_TM), lambda i: (0, i)),
        ] + [slab(w) for w in to_cast],
        out_shape=[
            jax.ShapeDtypeStruct((t, MAIN_COLS), BF16),
            jax.ShapeDtypeStruct((ATT_WIDTH, t), BF16),
            jax.ShapeDtypeStruct((t, AB_PAD), F32),
            jax.ShapeDtypeStruct((AB_PAD, t), F32),
        ] + [jax.ShapeDtypeStruct(w.shape, BF16) for w in to_cast],
        scratch_shapes=[pltpu.VMEM((DN_CONV_CHUNKS, DN_TAIL, IN_TN), F32)],
        compiler_params=pltpu.CompilerParams(
            dimension_semantics=("arbitrary",), vmem_limit_bytes=VMEM_LIMIT),
        name="inproj",
    )(xf, gain, w_main, w_ab, qg, kg, seg, conv_w, *to_cast)
    return outs[:4], outs[4:]


def _attn_tile(q_ref, row0, k_refs, v_refs, bias_ref, start_mask, og_ref, o_ref):
    lane = lax.broadcasted_iota(jnp.int32, (1, LANES), 1)
    per_tile = LANES // ATT_HEAD_DIM
    rows = slice(row0, row0 + ATT_TQ)

    def mine(h):
        e = h % per_tile
        return (lane >= e * ATT_HEAD_DIM) & (lane < (e + 1) * ATT_HEAD_DIM)

    def tile_lanes(h):
        return slice(h // per_tile * LANES, (h // per_tile + 1) * LANES)

    def scores(h):
        qp = q_ref[rows, tile_lanes(h)]
        q_e = jnp.where(mine(h), qp, jnp.zeros_like(qp))
        pieces = []
        for j, r in enumerate(k_refs):
            keys = slice(j * ATT_TQ, (j + 1) * ATT_TQ)
            bias = bias_ref[h, :, keys] if start_mask is None else bias_ref[h, :, keys] + start_mask[:, keys]
            pieces.append(_dot(q_e, r[tile_lanes(h), :]) + bias)
        return pieces

    def attend(h, pieces):
        mx = jnp.max(functools.reduce(jnp.maximum, pieces), axis=-1, keepdims=True)
        pe = jnp.concatenate([jnp.exp2(s - mx).astype(BF16) for s in pieces], axis=1)
        vw = jnp.concatenate([r[:, tile_lanes(h)] for r in v_refs], axis=0)
        o_e = _dot(pe, jnp.where(mine(h), vw, jnp.ones_like(vw)))
        return o_e / pltpu.roll(o_e, ATT_HEAD_DIM, axis=1)

    normed = []
    s_next = scores(0)
    for h in range(ATT_HEADS):
        s_cur = s_next
        if h + 1 < ATT_HEADS:
            s_next = scores(h + 1)
        normed.append(attend(h, s_cur))
        yield
    outs = [jnp.where(lane < ATT_HEAD_DIM, normed[t * per_tile], normed[t * per_tile + 1])
            for t in range(ATT_WIDTH // LANES)]
    o = jnp.concatenate(outs, axis=1)
    ms = jnp.mean(o * o, axis=-1, keepdims=True)
    o_ref[rows, :] = (o * lax.rsqrt(ms + EPS) * og_ref[...]).astype(BF16)


def _attn_kernel(q_ref, k0_ref, k1_ref, k2_ref, k3_ref, v0_ref, v1_ref, v2_ref, v3_ref, sig_ref, og_ref, o_ref,
                 bias_ref):
    i = pl.program_id(1)

    @pl.when((pl.program_id(0) == 0) & (i == 0))
    def _():
        r = lax.broadcasted_iota(jnp.int32, (ATT_TQ, ATT_WIN), 0)
        s = lax.broadcasted_iota(jnp.int32, (ATT_TQ, ATT_WIN), 1)
        back = (s >> 6) - (r >> 6)
        in_band = (back >= 0) & (back <= N_PAST_CHUNKS)
        for h in range(ATT_HEADS):
            rows = jnp.broadcast_to(sig_ref[h:h + 1, :], (ATT_TQ, sig_ref.shape[1]))
            toep = pltpu.roll(rows, 0, 1, stride=1, stride_axis=0)
            bias_ref[h] = jnp.where(in_band, toep[:, 0:ATT_WIN], NEG_BIG)

    def tiles(masks):
        _run(_weave(
            _attn_tile(q_ref, 0, (k0_ref, k1_ref, k2_ref), (v0_ref, v1_ref, v2_ref), bias_ref, masks[0],
                       og_ref, o_ref),
            _attn_tile(q_ref, ATT_TQ, (k1_ref, k2_ref, k3_ref), (v1_ref, v2_ref, v3_ref), bias_ref, masks[1],
                       og_ref, o_ref), 1))

    steps_in_past = N_PAST_CHUNKS * CHUNK // (ATT_TILES * ATT_TQ)

    @pl.when(i < steps_in_past)
    def _():
        jchunk = lax.broadcasted_iota(jnp.int32, (1, ATT_WIN), 1) >> 6
        tiles([jnp.where(jchunk + (ATT_TQ // CHUNK) * (ATT_TILES * i + n) - N_PAST_CHUNKS >= 0, 0.0, NEG_BIG)
               for n in range(ATT_TILES)])

    @pl.when(i >= steps_in_past)
    def _():
        tiles([None] * ATT_TILES)


def _attention(main, kt, signal, og, batch, seq):
    nb = seq // ATT_TQ
    steps = nb // ATT_TILES
    blk = (ATT_TQ, ATT_WIDTH)

    window_blk = lambda b, i, j: b * nb + jnp.maximum(ATT_TILES * i - 2 + j, 0)

    def k_spec(j):
        return pl.BlockSpec((ATT_WIDTH, ATT_TQ), lambda b, i: (0, window_blk(b, i, j)))

    def v_spec(j):
        return pl.BlockSpec(blk, lambda b, i: (window_blk(b, i, j), 2))

    step_rows = pl.BlockSpec((ATT_TILES * ATT_TQ, ATT_WIDTH), lambda b, i: (b * steps + i, 0))
    return pl.pallas_call(
        _attn_kernel,
        grid=(batch, steps),
        in_specs=[
            step_rows,
            k_spec(0), k_spec(1), k_spec(2), k_spec(3),
            v_spec(0), v_spec(1), v_spec(2), v_spec(3),
            pl.BlockSpec(signal.shape, lambda b, i: (0, 0)),
            pl.BlockSpec((1, ATT_WIDTH), lambda b, i: (0, 0)),
        ],
        out_specs=step_rows,
        out_shape=jax.ShapeDtypeStruct((batch * seq, ATT_WIDTH), BF16),
        scratch_shapes=[pltpu.VMEM((ATT_HEADS, ATT_TQ, ATT_WIN), F32)],
        compiler_params=pltpu.CompilerParams(
            dimension_semantics=("arbitrary", "arbitrary"), vmem_limit_bytes=VMEM_LIMIT),
        name="band_attention",
    )(main, kt, kt, kt, kt, main, main, main, main, signal, og)


def _bias_signal(rel_bias):
    nh = rel_bias.shape[0]
    lo = N_PAST_CHUNKS * CHUNK - (ATT_WIN - 1) + MAX_REL
    n_dist = ATT_TQ + ATT_WIN - 1
    n_tail = n_dist - (2 * MAX_REL + 1 - lo)
    rb = rel_bias.astype(F32) * LOG2_E
    by_dist = jnp.concatenate([rb[:, lo:], jnp.broadcast_to(rb[:, -1:], (nh, n_tail))], axis=1)
    period = ATT_TQ + ATT_WIN
    return jnp.roll(jnp.pad(by_dist[:, ::-1], ((0, 0), (0, period - n_dist))), -(ATT_TQ - 1), axis=1)


def _weave(main, side, every):
    n = 0
    done = object()
    while True:
        if n % every == 0:
            next(side, done)
        if next(main, done) is done:
            break
        n += 1
        yield
    _run(side)


def _chain(*stages):
    for g in stages:
        yield from g


def _run(stage):
    for _ in stage:
        pass


def _inv_unit_lower(mats, ri, ci, out):
    eye = jnp.where(ri == ci, 1.0, 0.0)
    diag8 = (ri >> 3) == (ci >> 3)
    n1 = [jnp.where(diag8, -a, 0.0) for a in mats]
    n1b = [n.astype(BF16) for n in n1]
    n2b = [_dot(n, n).astype(BF16) for n in n1b]
    yield
    n4b = [_dot(n, n).astype(BF16) for n in n2b]
    ts = [eye + n for n in n1]
    yield
    ts = [t + _dot(t.astype(BF16), n) for t, n in zip(ts, n2b)]
    yield
    ts = [t + _dot(t.astype(BF16), n) for t, n in zip(ts, n4b)]
    yield
    half = DN_C // 2
    r_h = lax.broadcasted_iota(jnp.int32, (half, DN_C), 0)
    c_o = lax.broadcasted_iota(jnp.int32, (half, DN_C), 1)
    for sh in range(3, 7):
        blk = 1 << sh
        odd = [slice(s0, s0 + blk) for s0 in range(blk, DN_C, 2 * blk)]
        even = [slice(s0, s0 + blk) for s0 in range(0, DN_C, 2 * blk)]
        take = lambda m, parts: jnp.concatenate([m[p] for p in parts], axis=0)
        r_o = r_h + (((r_h >> sh) + 1) << sh)
        pair =((r_o >> (sh + 1)) == (c_o >> (sh + 1))) & ((r_o >> sh) != (c_o >> sh))
        tb = [t.astype(BF16) for t in ts]
        lt = [_dot(jnp.where(pair, take(a, odd), 0.0).astype(BF16), t) for a, t in zip(mats, tb)]
        yield
        zero = jnp.zeros((blk, DN_C), F32)
        lt = [jnp.concatenate([p for m in range(half // blk) for p in (zero, x[m * blk:(m + 1) * blk])],
                              axis=0).astype(BF16) for x in lt]
        t_odd = [take(t, odd) for t in ts]
        new = [t - _dot(t.astype(BF16), x) for t, x in zip(t_odd, lt)]
        ts = [jnp.concatenate([p for m in range(half // blk)
                               for p in (t[even[m]], n[m * blk:(m + 1) * blk])], axis=0)
              for t, n in zip(ts, new)]
        yield
    out.extend(ts)


def _gdn_local(x_ref, ab_ref, abt_ref, ltri_ref, utri_ref, alog_ref, dtb_ref, alogt_ref, dtbt_ref, slot,
               a_scr, att_scr, rhs_scr, qd_scr, kt_scr, edl_scr):
    ab = ab_ref[...]
    g_col = -jnp.exp(alog_ref[...]) * _softplus(ab + dtb_ref[...])
    beta_col = _sigmoid(ab)
    g_row = -jnp.exp(alogt_ref[...]) * _softplus(abt_ref[0:8, :] + dtbt_ref[...])
    gc1, gc2, gc3 = _split3(g_col)
    gr1, gr2, gr3 = _split3(g_row)

    y = x_ref[...].astype(F32)

    ones = jnp.ones((DN_HEAD_DIM, DN_HEAD_DIM), BF16)
    sq = [(t * t).astype(BF16) for t in (y[:, g * DN_HEAD_DIM:(g + 1) * DN_HEAD_DIM]
                                         for g in range(2 * DN_HEADS))]
    ssq = [_dot(t, ones) for t in sq]
    yield
    ltri = ltri_ref[...]
    utri = utri_ref[...]
    d_col = _dot(ltri, gc1) + (_dot(ltri, gc2) + _dot(ltri, gc3))
    d_row = _dot(gr1, utri) + (_dot(gr2, utri) + _dot(gr3, utri))
    qk = []
    for g, tot in enumerate(ssq):
        inv = lax.rsqrt(tot + EPS)
        qk.append(y[:, g * DN_HEAD_DIM:(g + 1) * DN_HEAD_DIM]
                  * (inv * DN_HEAD_DIM ** -0.5 if g < DN_HEADS else inv))

    ri = lax.broadcasted_iota(jnp.int32, (DN_C, DN_C), 0)
    ci = lax.broadcasted_iota(jnp.int32, (DN_C, DN_C), 1)
    lmasks, kbqs, kbfs = [], [], []
    for n in range(DN_ITEMS):
        c, h = divmod(n, DN_HEADS)
        rows = slice(c * DN_C, (c + 1) * DN_C)
        q = qk[h][rows]
        k = qk[DN_HEADS + h][rows]
        v = y[rows, 2 * DN_WIDTH + h * DN_HEAD_DIM:2 * DN_WIDTH + (h + 1) * DN_HEAD_DIM]
        dc = d_col[rows, h:h + 1]
        dr = d_row[h:h + 1, rows]
        dl = dr[:, DN_C - 1:DN_C]
        bc = beta_col[rows, DN_HEADS + h:DN_HEADS + h + 1]
        assert DN_HEAD_DIM == DN_C
        bc = jnp.broadcast_to(bc, (DN_C, DN_HEAD_DIM))
        dc = jnp.broadcast_to(dc, (DN_C, DN_HEAD_DIM))
        kb = k * bc
        edc = jnp.exp(dc)
        rhs_scr[slot, n] = jnp.concatenate([v * bc, kb * edc], axis=1).astype(BF16)
        qd_scr[slot, n] = (q * edc).astype(BF16)
        kt_scr[slot, n] = (k * jnp.exp(dl - dc)).astype(BF16)
        edl_scr[slot, n] = jnp.broadcast_to(jnp.exp(dl), (8, DN_HEAD_DIM))
        lmasks.append(jnp.exp(jnp.where(ci <= ri, dc - dr, -jnp.inf)))
        kbqs.append(jnp.concatenate([kb, q], axis=0).astype(BF16))
        kbfs.append(k.astype(BF16))
    yield
    ps =[_dot_nt(kbq, kbf) for kbq, kbf in zip(kbqs, kbfs)]
    for n, (p, lmask) in enumerate(zip(ps, lmasks)):
        a_scr[slot, n] = jnp.where(ci < ri, p[0:DN_C] * lmask, 0.0)
        att_scr[slot, n] = (p[DN_C:2 * DN_C] * lmask).astype(BF16)


def _gdn_state(slot, gate_ref, og_ref, o_ref, out_row0, s_ref, a_scr, att_scr, rhs_scr, qd_scr, kt_scr,
               edl_scr):
    ri = lax.broadcasted_iota(jnp.int32, (DN_C, DN_C), 0)
    ci = lax.broadcasted_iota(jnp.int32, (DN_C, DN_C), 1)
    og = og_ref[...]
    tinvs = []
    yield from _inv_unit_lower([a_scr[slot, n] for n in range(DN_ITEMS)], ri, ci, tinvs)
    ws = [_dot(t.astype(BF16), rhs_scr[slot, n]) for n, t in enumerate(tinvs)]
    yield
    for c in range(DN_TB // DN_C):
        idx = [c * DN_HEADS + h for h in range(DN_HEADS)]
        s_old = [s_ref[h] for h in range(DN_HEADS)]
        rs = [_dot(jnp.concatenate([ws[n][:, DN_HEAD_DIM:].astype(BF16), qd_scr[slot, n]], axis=0),
                   s.astype(BF16)) for n, s in zip(idx, s_old)]
        yield
        vnb = [(ws[n][:, 0:DN_HEAD_DIM] - r[0:DN_C]).astype(BF16) for n, r in zip(idx, rs)]
        outs = [r[DN_C:2 * DN_C] + _dot(att_scr[slot, n], vn) for n, r, vn in zip(idx, rs, vnb)]
        yield
        for h, (n, vn) in enumerate(zip(idx, vnb)):
            s_ref[h] = s_old[h] * edl_scr[slot, n, 0:1, :] + _dot_tn(kt_scr[slot, n], vn)
        yield
        rows = slice(out_row0 + c * DN_C, out_row0 + (c + 1) * DN_C)
        for h, o in enumerate(outs):
            hs = slice(h * DN_HEAD_DIM, (h + 1) * DN_HEAD_DIM)
            ms = jnp.mean(o * o, axis=-1, keepdims=True)
            gt = gate_ref[rows, hs].astype(F32)
            o_ref[rows, hs] = (o * lax.rsqrt(ms + EPS) * og * (gt * _sigmoid(gt))).astype(BF16)


def _ffn_stages(x_ref, oa_ref, od_ref, wo_ref, fg_ref, w1_ref, w2_ref, o_ref):
    d = x_ref.shape[1]
    oa = oa_ref[...]
    od = od_ref[...]
    mixed = []
    for j in range(d // FFN_TN):
        cols = slice(j * FFN_TN, (j + 1) * FFN_TN)
        mixed.append(_dot(oa, wo_ref[0:ATT_WIDTH, cols]) + _dot(od, wo_ref[ATT_WIDTH:, cols]))
        yield
    h1 = x_ref[...] + jnp.concatenate(mixed, axis=1)
    ms = jnp.mean(h1 * h1, axis=-1, keepdims=True)
    n = (h1 * lax.rsqrt(ms + EPS) * fg_ref[...]).astype(BF16)

    def up(c, out):
        for j in range(FFN_TF // FFN_TN):
            col0 = c * FFN_TF + j * FFN_TN
            out.append(_dot(n, w1_ref[:, col0:col0 + FFN_TN]))
            yield

    n_chunks = w1_ref.shape[1] // FFN_TF
    acc = h1
    z_next = []
    yield from up(0, z_next)
    for c in range(n_chunks):
        z = z_next
        if c + 1 < n_chunks:
            z_next = []
            yield from up(c + 1, z_next)
        act = jnp.concatenate([jnp.square(jnp.maximum(piece, 0.0)).astype(BF16) for piece in z], axis=1)
        down = []
        for j in range(d // FFN_TN):
            down.append(_dot(act, w2_ref[c * FFN_TF:(c + 1) * FFN_TF, j * FFN_TN:(j + 1) * FFN_TN]))
            yield
        acc = acc + jnp.concatenate(down, axis=1)
    o_ref[...] = acc


def _gdn_ffn_kernel(x0_ref, x1_ref, x2_ref, ab0_ref, ab1_ref, ab2_ref, abt0_ref, abt1_ref, abt2_ref,
                    gate_ref, ltri_ref, utri_ref, alog_ref, dtb_ref, alogt_ref, dtbt_ref, dn_og_ref,
                    xres_ref, oa_ref, wo_ref, fg_ref, w1_ref, w2_ref, o_ref, s_ref, od_scr, *local_scr,
                    steps_per_seq):
    s = pl.program_id(0)
    cur = s % 2
    consts = (ltri_ref, utri_ref, alog_ref, dtb_ref, alogt_ref, dtbt_ref)

    def local(x_ref, ab_ref, abt_ref, slot):
        return _gdn_local(x_ref, ab_ref, abt_ref, *consts, slot, *local_scr)

    def state(slot, out_row0):
        return _gdn_state(slot, gate_ref, dn_og_ref, od_scr.at[cur], out_row0, s_ref, *local_scr)

    @pl.when(s == 0)
    def _():
        od_scr[1] = jnp.zeros(od_scr.shape[1:], od_scr.dtype)

    @pl.when(s % steps_per_seq == 0)
    def _():
        s_ref[...] = jnp.zeros_like(s_ref)
        _run(local(x0_ref, ab0_ref, abt0_ref, 0))

    deltanet = _chain(
        _weave(state(0, 0), local(x1_ref, ab1_ref, abt1_ref, 1), DN_STATE_PER_LOCAL),
        _weave(state(1, DN_TB), local(x2_ref, ab2_ref, abt2_ref, 0), DN_STATE_PER_LOCAL))
    ffn = _ffn_stages(xres_ref, oa_ref, od_scr.at[1 - cur], wo_ref, fg_ref, w1_ref, w2_ref, o_ref)
    _run(_weave(deltanet, ffn, 1))


def _deltanet_ffn(xf, main, ab, abt, o_att, alog, dtb, alogt, dtbt, dn_og, w_out, fg, w1, w2, seq):
    t, d = xf.shape
    dff = w1.shape[1]
    assert FFN_TM == 2 * DN_TB
    nb = seq // DN_TB
    spq = nb // 2
    steps = t // FFN_TM
    tile = lambda s: jnp.minimum(s, steps - 1)
    prev = lambda s: jnp.maximum(s - 1, 0)
    seq_blk0 = lambda s: tile(s) // spq * nb
    first = seq_blk0
    odd = lambda s: 2 * tile(s) + 1
    nxt = lambda s: jnp.minimum(2 * tile(s) + 2, seq_blk0(s) + nb - 1)
    const = lambda s: (0, 0)

    def x_spec(blk):
        return pl.BlockSpec((DN_TB, 3 * DN_WIDTH), lambda s: (blk(s), 1))

    def ab_spec(blk):
        return pl.BlockSpec((DN_TB, AB_PAD), lambda s: (blk(s), 0))

    def abt_spec(blk):
        return pl.BlockSpec((AB_PAD, DN_TB), lambda s: (0, blk(s)))

    r = np.arange(DN_TB)
    same = (r[:, None] // DN_C) == (r[None, :] // DN_C)
    ltri = jnp.asarray(same & (r[None, :] <= r[:, None]), BF16)
    utri = jnp.asarray(same & (r[:, None] <= r[None, :]), BF16)
    item = lambda *shape: (2, DN_ITEMS) + shape
    resident = dict(pipeline_mode=pl.Buffered(1))
    return pl.pallas_call(
        functools.partial(_gdn_ffn_kernel, steps_per_seq=spq),
        grid=(steps + 1,),
        in_specs=[
            x_spec(first), x_spec(odd), x_spec(nxt),
            ab_spec(first), ab_spec(odd), ab_spec(nxt),
            abt_spec(first), abt_spec(odd), abt_spec(nxt),
            pl.BlockSpec((2 * DN_TB, DN_WIDTH), lambda s: (tile(s), 6)),
            pl.BlockSpec((DN_TB, DN_TB), const),
            pl.BlockSpec((DN_TB, DN_TB), const),
            pl.BlockSpec((1, AB_PAD), const),
            pl.BlockSpec((1, AB_PAD), const),
            pl.BlockSpec((8, 1), const),
            pl.BlockSpec((8, 1), const),
            pl.BlockSpec((1, DN_HEAD_DIM), const),
            pl.BlockSpec((FFN_TM, d), lambda s: (prev(s), 0)),
            pl.BlockSpec((FFN_TM, ATT_WIDTH), lambda s: (prev(s), 0)),
            pl.BlockSpec((d, d), const, **resident),
            pl.BlockSpec((1, d), const),
            pl.BlockSpec((d, dff), const, **resident),
            pl.BlockSpec((dff, d), const, **resident),
        ],
        out_specs=pl.BlockSpec((FFN_TM, d), lambda s: (prev(s), 0)),
        out_shape=jax.ShapeDtypeStruct((t, d), F32),
        scratch_shapes=[
            pltpu.VMEM((DN_HEADS, DN_HEAD_DIM, DN_HEAD_DIM), F32),
            pltpu.VMEM((2, FFN_TM, DN_WIDTH), BF16),
            pltpu.VMEM(item(DN_C, DN_C), F32),
            pltpu.VMEM(item(DN_C, DN_C), BF16),
            pltpu.VMEM(item(DN_C, 2 * DN_HEAD_DIM), BF16),
            pltpu.VMEM(item(DN_C, DN_HEAD_DIM), BF16),
            pltpu.VMEM(item(DN_C, DN_HEAD_DIM), BF16),
            pltpu.VMEM(item(8, DN_HEAD_DIM), F32),
        ],
        compiler_params=pltpu.CompilerParams(
            dimension_semantics=("arbitrary",), vmem_limit_bytes=VMEM_LIMIT),
        name="deltanet_ffn",
    )(main, main, main, ab, ab, ab, abt, abt, abt, main, ltri, utri, alog, dtb, alogt, dtbt, dn_og,
      xf, o_att, w_out, fg, w1, w2)


def _layer(h, p, batch, seq):
    w_in = p["w_in"]
    w_main = w_in.astype(BF16)
    w_ab = jnp.pad(w_in[:, MAIN_COLS:], ((0, 0), (0, AB_PAD - 2 * DN_HEADS))).astype(BF16)
    head_of = np.arange(ATT_WIDTH) // ATT_HEAD_DIM
    seg = jnp.asarray((head_of[:, None] == head_of[None, :]) / ATT_HEAD_DIM, BF16)
    qg = (jnp.tile(p["att_q_gain"].astype(F32), ATT_HEADS) * (ATT_HEAD_DIM ** -0.5 * LOG2_E))[None]
    kg = jnp.tile(p["att_k_gain"].astype(F32), ATT_HEADS)[None]
    (main, kt, ab, abt), (w_out, w_ff1, w_ff2) = _inproj(
        h, p["mix_norm_gain"].astype(F32)[None], w_main, w_ab, qg, kg, seg, p["dn_conv_w"].astype(F32), seq,
        [p["w_out"].astype(F32), p["w_ff1"].astype(F32), p["w_ff2"].astype(F32)])

    o_att = _attention(main, kt, _bias_signal(p["rel_bias"]),
                       p["att_out_gain"].astype(F32)[None], batch, seq)

    pad_row = lambda v: jnp.pad(v.astype(F32), (0, AB_PAD - DN_HEADS))[None]
    pad_col = lambda v: jnp.pad(v.astype(F32), (0, 8 - DN_HEADS))[:, None]
    return _deltanet_ffn(h, main, ab, abt, o_att, pad_row(p["dn_a_log"]), pad_row(p["dn_dt_bias"]),
                         pad_col(p["dn_a_log"]), pad_col(p["dn_dt_bias"]),
                         p["dn_out_gain"].astype(F32)[None], w_out, p["ffn_norm_gain"].astype(F32)[None],
                         w_ff1, w_ff2, seq)


def kernel(x, mix_norm_gain, w_in, att_q_gain, att_k_gain, rel_bias, att_out_gain, dn_conv_w,
           dn_a_log, dn_dt_bias, dn_out_gain, w_out, ffn_norm_gain, w_ff1, w_ff2):
    batch, seq, d = x.shape
    assert seq % max(ATT_TQ, 2 * DN_TB) == 0 and (batch * seq) % max(IN_TM, FFN_TM) == 0
    params = dict(mix_norm_gain=mix_norm_gain, w_in=w_in, att_q_gain=att_q_gain,
                  att_k_gain=att_k_gain, rel_bias=rel_bias, att_out_gain=att_out_gain,
                  dn_conv_w=dn_conv_w, dn_a_log=dn_a_log, dn_dt_bias=dn_dt_bias,
                  dn_out_gain=dn_out_gain, w_out=w_out, ffn_norm_gain=ffn_norm_gain,
                  w_ff1=w_ff1, w_ff2=w_ff2)
    h = x.reshape(batch * seq, d)
    for layer in range(w_in.shape[0]):
        h = _layer(h, {name: v[layer] for name, v in params.items()}, batch, seq)
    return h.reshape(batch, seq, d)
```
